```python
import jax, jax.numpy as jnp
from jax import lax
import numpy as np

D_MODEL = 1024
BATCH = 8
SEQ = 2048
DEPTH = 1
DEC_BATCH = 128
DEC_SEQ = 1
PAST_LEN = 16384
PAGE_SIZE = 128

D_CONV = D_MODEL // 2
CONV_GROUPS = 4
CONV_WIDTH = 3
D_REC = D_MODEL // 2
N_REC_HEADS = 4
HEAD_K = D_REC // N_REC_HEADS
HEAD_V = D_REC // N_REC_HEADS
D_MIX = D_CONV + D_REC
CHUNK = 64
N_META = 16
N_EXPERTS = 32
TOP_K = 4
D_FF = D_MODEL
SWIGLU_LIMIT = 7.0
SWIGLU_ALPHA = 1.702
EPS = 1e-5
SPLIT_SIZES = (D_CONV, D_CONV, D_CONV, D_REC, D_REC, D_REC, D_REC)
SPLIT_POINTS = tuple(int(p) for p in np.cumsum(SPLIT_SIZES)[:-1])
D_IN = sum(SPLIT_SIZES)

kernel_name = "hymba_conv_hgrn2_moe_step"


def _rmsnorm(x, g):
    xf = x.astype(jnp.float32)
    out = xf * lax.rsqrt(jnp.mean(xf * xf, axis=-1, keepdims=True) + EPS)
    return (out * g.astype(jnp.float32)).astype(x.dtype)


def _gla_chunk(S0, q, k, v, logf):
    b = jnp.cumsum(logf.astype(jnp.float32), axis=2)
    C = q.shape[2]
    causal = jnp.tril(jnp.ones((C, C), dtype=bool))
    diff = b[:, :, :, None, :] - b[:, :, None, :, :]
    decay = jnp.exp(jnp.where(causal[:, :, None], diff, -jnp.inf))
    scores = jnp.einsum('bhtk,bhsk,bhtsk->bhts', q, k, decay)
    o = (jnp.einsum('bhts,bhsv->bhtv', scores, v)
         + jnp.einsum('bhtk,bhkv->bhtv', q * jnp.exp(b), S0))
    b_last = b[:, :, -1:, :]
    S_new = (jnp.exp(b_last[:, :, 0, :])[..., None] * S0
             + jnp.einsum('bhsk,bhsv->bhkv', k * jnp.exp(b_last - b), v))
    return S_new.astype(S0.dtype), o.astype(q.dtype)


def _gla_scan(S0, q, k, v, logf, chunk):
    Bn, H, T, _ = q.shape
    n = T // chunk

    def split(a):
        return jnp.moveaxis(a.reshape(Bn, H, n, chunk, a.shape[-1]), 2, 0)

    def step(S, xs):
        return _gla_chunk(S, *xs)

    S, o = lax.scan(step, S0, (split(q), split(k), split(v), split(logf)))
    return S, jnp.moveaxis(o, 0, 2).reshape(Bn, H, T, -1)


def _token_mixers(u, conv_buf, S0, conv_w, lb, rec_gain, segments):
    Bn, T, _ = u.shape
    bg, cg, hv, q, fx, iv, og = jnp.split(u, SPLIT_POINTS, axis=-1)
    bx = bg * hv
    xp = jnp.concatenate([conv_buf.astype(bx.dtype), bx], axis=1)
    conv = sum(conv_w[j] * xp[:, j:j + T] for j in range(CONV_WIDTH))
    y_conv = cg * conv
    new_buf = xp[:, T:]
    f = lb.astype(u.dtype) + (1 - lb.astype(u.dtype)) * jax.nn.sigmoid(fx)
    logf = jnp.log(f)
    kk = 1 - f

    def heads(a, d):
        return a.reshape(Bn, T, N_REC_HEADS, d).transpose(0, 2, 1, 3)

    qh, kh, vh, lfh = heads(q, HEAD_K), heads(kk, HEAD_K), heads(iv, HEAD_V), heads(logf, HEAD_K)
    S = S0
    outs = []
    start = 0
    for length, chunk in segments:
        sl = slice(start, start + length)
        S, o = _gla_scan(S, qh[:, :, sl], kh[:, :, sl], vh[:, :, sl], lfh[:, :, sl], chunk)
        outs.append(o)
        start += length
    o = jnp.concatenate(outs, axis=2)
    of = o.astype(jnp.float32)
    o = (of * lax.rsqrt(jnp.mean(of * of, axis=-1, keepdims=True) + EPS)).astype(u.dtype)
    o = o.transpose(0, 2, 1, 3).reshape(Bn, T, D_REC)
    y_rec = o * rec_gain * jax.nn.silu(og)
    return jnp.concatenate([y_conv, y_rec], axis=-1), new_buf, S


def _moe(x, router_w, router_b, w1, b1, w2, b2):
    logits = x.astype(jnp.float32) @ router_w.astype(jnp.float32) + router_b.astype(jnp.float32)
    top_logits, top_idx = lax.top_k(logits, TOP_K)
    gates = jax.nn.softmax(top_logits, axis=-1)
    combine = jnp.einsum('tk,tke->te', gates,
                         jax.nn.one_hot(top_idx, N_EXPERTS, dtype=jnp.float32)).astype(x.dtype)
    out = jnp.zeros_like(x)
    for e in range(N_EXPERTS):
        a = x @ w1[e] + b1[e]
        glu = jnp.minimum(a[:, ::2], SWIGLU_LIMIT)
        lin = jnp.clip(a[:, 1::2], -SWIGLU_LIMIT, SWIGLU_LIMIT)
        act = glu * jax.nn.sigmoid(SWIGLU_ALPHA * glu) * (lin + 1)
        out = out + combine[:, e:e + 1] * (act @ w2[e] + b2[e])
    return out


def setup_inputs(seed: int = 0) -> dict:
    key = jax.random.key(seed)
    ks = jax.random.split(key, 20)
    f32 = jnp.float32
    nrm = lambda k, shape, s: jax.random.normal(k, shape, f32) * s
    return {
        "x_prompt": nrm(ks[0], (BATCH, SEQ, D_MODEL), 1.0),
        "x_sample": nrm(ks[1], (DEC_BATCH, DEC_SEQ, D_MODEL), 1.0),
        "state_conv": nrm(ks[2], (DEPTH, DEC_BATCH, CONV_WIDTH - 1, D_CONV), 1.0),
        "state_rec": nrm(ks[3], (DEPTH, DEC_BATCH, N_REC_HEADS, HEAD_K, HEAD_V), 0.5),
        "meta_tokens": nrm(ks[4], (N_META, D_MODEL), 1.0),
        "norm_mix": 1.0 + nrm(ks[5], (DEPTH, D_MODEL), 0.02),
        "w_in": nrm(ks[6], (DEPTH, D_MODEL, D_IN), D_MODEL ** -0.5),
        "conv_w": nrm(ks[7], (DEPTH, CONV_WIDTH, D_CONV), CONV_WIDTH ** -0.5),
        "rec_lower_bound": nrm(ks[8], (DEPTH + 1, D_REC), 0.1),
        "rec_norm": 1.0 + nrm(ks[9], (DEPTH, D_REC), 0.02),
        "w_out": nrm(ks[10], (DEPTH, D_MIX, D_MODEL), D_MIX ** -0.5),
        "norm_ffn": 1.0 + nrm(ks[11], (DEPTH, D_MODEL), 0.02),
        "router_w": nrm(ks[12], (DEPTH, D_MODEL, N_EXPERTS), D_MODEL ** -0.5),
        "router_b": nrm(ks[13], (DEPTH, N_EXPERTS), 0.01),
        "expert_w1": nrm(ks[14], (DEPTH, N_EXPERTS, D_MODEL, 2 * D_FF), D_MODEL ** -0.5),
        "expert_b1": nrm(ks[15], (DEPTH, N_EXPERTS, 2 * D_FF), 0.01),
        "expert_w2": nrm(ks[16], (DEPTH, N_EXPERTS, D_FF, D_MODEL), D_FF ** -0.5),
        "expert_b2": nrm(ks[17], (DEPTH, N_EXPERTS, D_MODEL), 0.01),
        "norm_final": 1.0 + nrm(ks[18], (D_MODEL,), 0.02),
    }


def reference(x_prompt, x_sample, state_conv, state_rec, meta_tokens, norm_mix, w_in,
              conv_w, rec_lower_bound, rec_norm, w_out, norm_ffn, router_w, router_b,
              expert_w1, expert_b1, expert_w2, expert_b2, norm_final):
    Bp = x_prompt.shape[0]
    meta = jnp.broadcast_to(meta_tokens.astype(x_prompt.dtype), (Bp, N_META, D_MODEL))
    h_p = jnp.concatenate([meta, x_prompt], axis=1)
    h_s = x_sample
    lb_all = jnp.cumsum(jax.nn.softmax(rec_lower_bound.astype(jnp.float32), axis=0), axis=0)
    conv_p_l, rec_p_l, conv_s_l, rec_s_l = [], [], [], []
    for l in range(DEPTH):
        Lp = h_p.shape[1]
        Bs, Ls = h_s.shape[0], h_s.shape[1]
        n_p = Bp * Lp
        tok = jnp.concatenate([h_p.reshape(-1, D_MODEL), h_s.reshape(-1, D_MODEL)], axis=0)
        u = _rmsnorm(tok, norm_mix[l]) @ w_in[l]
        u_p = u[:n_p].reshape(Bp, Lp, D_IN)
        u_s = u[n_p:].reshape(Bs, Ls, D_IN)
        zero_buf = jnp.zeros((Bp, CONV_WIDTH - 1, D_CONV), u.dtype)
        zero_S = jnp.zeros((Bp, N_REC_HEADS, HEAD_K, HEAD_V), u.dtype)
        y_p, cb_p, S_p = _token_mixers(u_p, zero_buf, zero_S, conv_w[l], lb_all[l], rec_norm[l],
                                       ((N_META, N_META), (Lp - N_META, CHUNK)))
        y_s, cb_s, S_s = _token_mixers(u_s, state_conv[l], state_rec[l], conv_w[l], lb_all[l],
                                       rec_norm[l], ((Ls, Ls),))
        conv_p_l.append(cb_p)
        rec_p_l.append(S_p)
        conv_s_l.append(cb_s)
        rec_s_l.append(S_s)
        mix = jnp.concatenate([y_p.reshape(-1, D_MIX), y_s.reshape(-1, D_MIX)], axis=0) @ w_out[l]
        h_p = h_p + mix[:n_p].reshape(Bp, Lp, D_MODEL)
        h_s = h_s + mix[n_p:].reshape(Bs, Ls, D_MODEL)
        if l == DEPTH - 1:
            h_p = h_p[:, N_META:]
        n_p = h_p.shape[0] * h_p.shape[1]
        tok = jnp.concatenate([h_p.reshape(-1, D_MODEL), h_s.reshape(-1, D_MODEL)], axis=0)
        ff = _moe(_rmsnorm(tok, norm_ffn[l]), router_w[l], router_b[l],
                  expert_w1[l], expert_b1[l], expert_w2[l], expert_b2[l])
        h_p = h_p + ff[:n_p].reshape(h_p.shape)
        h_s = h_s + ff[n_p:].reshape(h_s.shape)
    y_prompt = _rmsnorm(h_p, norm_final)
    y_sample = _rmsnorm(h_s, norm_final)
    new_conv_prompt = jnp.stack(conv_p_l, axis=0)
    new_rec_prompt = jnp.stack(rec_p_l, axis=0)
    new_conv_sample = jnp.stack(conv_s_l, axis=0)
    new_rec_sample = jnp.stack(rec_s_l, axis=0)
    return (y_prompt, y_sample, new_conv_prompt, new_rec_prompt, new_conv_sample, new_rec_sample)
```

```python
import functools

import jax
import jax.numpy as jnp
from jax import lax
from jax.experimental import pallas as pl
from jax.experimental.pallas import tpu as pltpu

F32 = jnp.float32
BF16 = jnp.bfloat16

N_HEADS = 4
HEAD = 128
N_META = 16
CHUNK = 64
TOP_K = 4
SWIGLU_LIMIT = 7.0
SWIGLU_ALPHA = 1.702
EPS = 1e-5

LANES = 128
MXU_N = 256
PROMPT_TILE = 256
TOKEN_TILE = 128
RANK_TILE = 384
MOE_TILE = 256
VMEM_LIMIT = 56 * 1024 * 1024


def _dot(a, b):
    return jnp.dot(a, b, preferred_element_type=F32)


def _dot_nt(a, b):
    return lax.dot_general(a, b, (((1,), (1,)), ((), ())), preferred_element_type=F32)


def _dot_tn(a, b):
    return lax.dot_general(a, b, (((0,), (0,)), ((), ())), preferred_element_type=F32)


def _split3(x):
    hi = x.astype(BF16)
    r = x - hi.astype(F32)
    mid = r.astype(BF16)
    lo = (r - mid.astype(F32)).astype(BF16)
    return hi, mid, lo


def _sigmoid(x):
    return 1.0 / (1.0 + jnp.exp(-x))


def _rms(x, g):
    ms = jnp.mean(x * x, axis=-1, keepdims=True)
    return x * lax.rsqrt(ms + EPS) * g


def _project(x, nm_ref, win_ref):
    return _dot(_rms(x, nm_ref[...]).astype(BF16), win_ref[...])


def _split_u(u, d_conv, d_rec):
    pts = [0, d_conv, 2 * d_conv, 3 * d_conv, 3 * d_conv + d_rec, 3 * d_conv + 2 * d_rec,
           3 * d_conv + 3 * d_rec, 3 * d_conv + 4 * d_rec]
    return [u[:, pts[i]:pts[i + 1]] for i in range(7)]


def _lower_bound(rlb_ref, layer):
    r = rlb_ref[...]
    e = jnp.exp(r - jnp.max(r, axis=0, keepdims=True))
    return jnp.sum(e[0:layer + 1], axis=0, keepdims=True) / jnp.sum(e, axis=0, keepdims=True)


def _forget(fx, lb):
    f = lb + (1.0 - lb) * _sigmoid(fx)
    return f, 1.0 - f


def _rec_out(o, og, rg):
    parts = []
    for h in range(N_HEADS):
        oh = o[:, h * HEAD:(h + 1) * HEAD]
        parts.append(oh * lax.rsqrt(jnp.mean(oh * oh, axis=-1, keepdims=True) + EPS))
    return jnp.concatenate(parts, axis=-1) * rg * (og * _sigmoid(og))


def _route(logits):
    n = logits.shape[-1]
    lane = lax.broadcasted_iota(jnp.int32, logits.shape, 1).astype(F32)
    work = logits
    tops, hots = [], []
    for _ in range(TOP_K):
        m = jnp.max(work, axis=-1, keepdims=True)
        first = jnp.min(jnp.where(work == m, lane, float(n)), axis=-1, keepdims=True)
        hot = lane == first
        tops.append(m)
        hots.append(hot)
        work = jnp.where(hot, -jnp.inf, work)
    es = [jnp.exp(t - tops[0]) for t in tops]
    den = es[0]
    for e in es[1:]:
        den = den + e
    sel = jnp.zeros_like(logits)
    gm = jnp.zeros_like(logits)
    for hot, e in zip(hots, es):
        sel = sel + jnp.where(hot, 1.0, 0.0)
        gm = gm + jnp.where(hot, e / den, 0.0)
    return sel, gm


def _tail(x, y, wout_ref, nf_ref, rw_ref, rb_ref):
    h = x + _dot(y.astype(BF16), wout_ref[...])
    xn = _rms(h, nf_ref[...])
    xh = xn.astype(BF16)
    xl = (xn - xh.astype(F32)).astype(BF16)
    rw = rw_ref[...]
    wh = rw.astype(BF16)
    wl = (rw - wh.astype(F32)).astype(BF16)
    logits = _dot(xh, wh) + _dot(xh, wl) + _dot(xl, wh) + rb_ref[...]
    sel, gm = _route(logits)
    return h, xn, sel, gm


def _rec_chunk(q, kk, v, lf, st_ref, tri, causal):
    c = q.shape[0]
    hi, mid, lo = _split3(lf)
    b = _dot(tri, hi) + _dot(tri, mid) + _dot(tri, lo)
    eb = jnp.exp(b)
    qe = (q * eb).astype(BF16)
    ke = (kk * jnp.exp(-b)).astype(BF16)
    vb = v.astype(BF16)
    eb_last = eb[c - 1:c]
    outs = []
    for h in range(N_HEADS):
        sl = slice(h * HEAD, (h + 1) * HEAD)
        st = st_ref[h]
        sc = jnp.where(causal, _dot_nt(qe[:, sl], ke[:, sl]), 0.0)
        outs.append(_dot(sc.astype(BF16), vb[:, sl]) + _dot_nt(qe[:, sl], st.astype(BF16)))
        st_ref[h] = (st + _dot_tn(vb[:, sl], ke[:, sl])) * eb_last[:, sl]
    return jnp.concatenate(outs, axis=-1)


def _mixer_kernel(layer, tm, chunk, nt, n_live, emit, *refs):
    if emit:
        (x_ref, s0_ref, c0_ref, nm_ref, win_ref, cw_ref, rlb_ref, rg_ref, wout_ref, nf_ref, rw_ref,
         rb_ref, hs_ref, xns_ref, sels_ref, gms_ref,
         h_ref, xn_ref, sel_ref, gm_ref, st_out, cv_out, convbuf, st) = refs
    else:
        (x_ref, s0_ref, c0_ref, nm_ref, win_ref, cw_ref, rlb_ref, st_out, cv_out, convbuf, st) = refs
    step = pl.program_id(0)
    t = lax.rem(step, nt)
    live = step < n_live
    d_conv = cw_ref.shape[-1]
    d_rec = rlb_ref.shape[-1]

    @pl.when(jnp.logical_and(t == 0, live))
    def _():
        for h in range(N_HEADS):
            st[h] = s0_ref[h].T
        convbuf[6:8, :] = c0_ref[...]

    @pl.when(live)
    def _():
        x = x_ref[...]
        u = _project(x, nm_ref, win_ref)
        bg, cg, hv, q, fx, iv, og = _split_u(u, d_conv, d_rec)

        bx = bg * hv
        convbuf[8:8 + tm, :] = bx
        cw = cw_ref[...]
        conv = cw[0:1] * convbuf[6:6 + tm, :] + cw[1:2] * convbuf[7:7 + tm, :] + cw[2:3] * bx
        y_conv = cg * conv
        convbuf[6:8, :] = bx[tm - 2:tm]
        cv_out[...] = bx[tm - 2:tm]

        f, kk = _forget(fx, _lower_bound(rlb_ref, layer))
        lf = jnp.log(f)
        row = lax.broadcasted_iota(jnp.int32, (chunk, chunk), 0)
        col = lax.broadcasted_iota(jnp.int32, (chunk, chunk), 1)
        causal = row >= col
        tri = jnp.where(causal, 1.0, 0.0).astype(BF16)
        outs = []
        for c in range(tm // chunk):
            rs = slice(c * chunk, (c + 1) * chunk)
            outs.append(_rec_chunk(q[rs], kk[rs], iv[rs], lf[rs], st, tri, causal))

        if emit:
            o = jnp.concatenate(outs, axis=0)
            y = jnp.concatenate([y_conv, _rec_out(o, og, rg_ref[...])], axis=-1)
            h, xn, sel, gm = _tail(x, y, wout_ref, nf_ref, rw_ref, rb_ref)
            h_ref[...] = h
            xn_ref[...] = xn
            sel_ref[...] = sel
            gm_ref[...] = gm

    @pl.when(jnp.logical_and(t == nt - 1, live))
    def _():
        for h in range(N_HEADS):
            st_out[h] = st[h].T

    if emit:
        @pl.when(step == n_live)
        def _():
            ns = hs_ref.shape[0]
            for dst, src in ((h_ref, hs_ref), (xn_ref, xns_ref), (sel_ref, sels_ref), (gm_ref, gms_ref)):
                dst[0:ns, :] = src[...]
                dst[ns:tm, :] = jnp.zeros((tm - ns, dst.shape[-1]), F32)


def _const_spec(shape):
    return pl.BlockSpec(shape, lambda *_: (0,) * len(shape))


def _mixer_call(x, s0, c0, weights, layer, tm, chunk, tail=None):
    nseq, length, d = x.shape
    nt = length // tm
    n_live = nseq * nt
    nm, win, cw, rlb = weights
    d_conv = cw.shape[-1]
    per_seq = s0.shape[0] == nseq

    def seq_of(s):
        return jnp.minimum(s // nt, nseq - 1)

    in_specs = [
        pl.BlockSpec((None, tm, d), lambda s: (seq_of(s), lax.rem(s, nt), 0)),
        pl.BlockSpec((None,) + s0.shape[1:], lambda s: (seq_of(s) if per_seq else 0, 0, 0, 0)),
        pl.BlockSpec((None,) + c0.shape[1:], lambda s: (seq_of(s) if per_seq else 0, 0, 0)),
        _const_spec(nm.shape), _const_spec(win.shape), _const_spec(cw.shape), _const_spec(rlb.shape),
    ]
    args = [x, s0, c0, nm, win, cw, rlb]
    out_shape = [jax.ShapeDtypeStruct((nseq,) + s0.shape[1:], F32),
                 jax.ShapeDtypeStruct((nseq,) + c0.shape[1:], F32)]
    out_specs = [pl.BlockSpec((None,) + s0.shape[1:], lambda s: (seq_of(s), 0, 0, 0)),
                 pl.BlockSpec((None,) + c0.shape[1:], lambda s: (seq_of(s), 0, 0))]
    n_steps = n_live
    if tail is not None:
        tail_w, decode = tail
        assert decode[0].shape[0] <= tm
        n_steps = n_live + 1
        extra = list(tail_w) + list(decode)
        args += extra
        in_specs += [_const_spec(a.shape) for a in extra]
        out_shape = [jax.ShapeDtypeStruct((n_steps * tm, a.shape[-1]), F32) for a in decode] + out_shape
        out_specs = [pl.BlockSpec((tm, a.shape[-1]), lambda s: (s, 0)) for a in decode] + out_specs
    return pl.pallas_call(
        functools.partial(_mixer_kernel, layer, tm, chunk, nt, n_live, tail is not None),
        grid=(n_steps,),
        in_specs=in_specs,
        out_specs=out_specs,
        out_shape=out_shape,
        scratch_shapes=[pltpu.VMEM((tm + 8, d_conv), F32), pltpu.VMEM(s0.shape[1:], F32)],
        compiler_params=pltpu.CompilerParams(dimension_semantics=("arbitrary",),
                                             vmem_limit_bytes=VMEM_LIMIT),
        name="mixer_prompt" if tail is not None else "mixer_meta",
    )(*args)


def _sample_in_kernel(layer, x_ref, sc_ref, nm_ref, win_ref, cw_ref, rlb_ref,
                      yc_ref, nc_ref, f_ref, k_ref, q_ref, v_ref, og_ref):
    d_conv = cw_ref.shape[-1]
    d_rec = rlb_ref.shape[-1]
    u = _project(x_ref[...], nm_ref, win_ref)
    bg, cg, hv, q, fx, iv, og = _split_u(u, d_conv, d_rec)
    bx = bg * hv
    sc = sc_ref[...]
    s0, s1 = sc[:, :d_conv], sc[:, d_conv:]
    cw = cw_ref[...]
    yc_ref[...] = cg * (cw[0:1] * s0 + cw[1:2] * s1 + cw[2:3] * bx)
    nc_ref[...] = jnp.concatenate([s1, bx], axis=-1)
    f, kk = _forget(fx, _lower_bound(rlb_ref, layer))
    f_ref[...] = f
    k_ref[...] = kk
    q_ref[...] = q
    v_ref[...] = iv
    og_ref[...] = og


def _sample_state_kernel(group, f_ref, k_ref, q_ref, v_ref, s_ref, sn_ref, o_ref):
    for j in range(group):
        for h in range(N_HEADS):
            rs = slice(h * HEAD, (h + 1) * HEAD)
            fcol = f_ref[rs, j:j + 1]
            kcol = k_ref[rs, j:j + 1]
            qcol = q_ref[rs, j:j + 1]
            vrow = v_ref[j:j + 1, rs]
            sn = fcol * s_ref[j, h] + kcol * vrow
            sn_ref[j, h] = sn
            o_ref[j:j + 1, rs] = jnp.sum(qcol * sn, axis=0, keepdims=True)


def _sample_tail_kernel(x_ref, yc_ref, o_ref, og_ref, rg_ref, wout_ref, nf_ref, rw_ref, rb_ref,
                        h_ref, xn_ref, sel_ref, gm_ref):
    y = jnp.concatenate([yc_ref[...], _rec_out(o_ref[...], og_ref[...], rg_ref[...])], axis=-1)
    h, xn, sel, gm = _tail(x_ref[...], y, wout_ref, nf_ref, rw_ref, rb_ref)
    h_ref[...] = h
    xn_ref[...] = xn
    sel_ref[...] = sel
    gm_ref[...] = gm


def _rank_kernel(sel_ref, gm_ref, eid_ref, rank_ref, gate_ref, cnt_ref, carry):
    i = pl.program_id(0)

    @pl.when(i == 0)
    def _():
        carry[...] = jnp.zeros_like(carry)

    sel = sel_ref[...]
    gm = gm_ref[...]
    tb, ne = sel.shape
    row = lax.broadcasted_iota(jnp.int32, (tb, tb), 0)
    col = lax.broadcasted_iota(jnp.int32, (tb, tb), 1)
    before = jnp.where(col < row, 1.0, 0.0).astype(BF16)
    selb = sel.astype(BF16)
    rank = _dot(before, selb) + carry[...]
    carry[...] = carry[...] + jnp.sum(sel, axis=0, keepdims=True)
    cnt_ref[...] = carry[...]
    er = lax.broadcasted_iota(jnp.int32, (ne, ne), 0)
    ec = lax.broadcasted_iota(jnp.int32, (ne, ne), 1)
    lower = jnp.where(er < ec, 1.0, 0.0).astype(BF16)
    order = _dot(selb, lower)
    lane_e = lax.broadcasted_iota(jnp.int32, (tb, ne), 1).astype(F32)
    lane = lax.broadcasted_iota(jnp.int32, (tb, LANES), 1)
    eid = jnp.zeros((tb, LANES), F32)
    rnk = jnp.zeros((tb, LANES), F32)
    gat = jnp.zeros((tb, LANES), F32)
    for k in range(TOP_K):
        pick = jnp.where(order == float(k), sel, 0.0)
        eid = jnp.where(lane == k, jnp.sum(pick * lane_e, axis=-1, keepdims=True), eid)
        rnk = jnp.where(lane == k, jnp.sum(pick * rank, axis=-1, keepdims=True), rnk)
        gat = jnp.where(lane == k, jnp.sum(pick * gm, axis=-1, keepdims=True), gat)
    eid_ref[...] = eid
    rank_ref[...] = rnk
    gate_ref[...] = gat


def _moe_kernel(tm, te_ref, nu_ref, tok_ref, xn_hbm, w1_ref, bg_ref, bl_ref, w2_ref, b2_ref,
                ys_ref, xg, sem, w1p, w2b, act):
    i = pl.program_id(0)
    n_tiles = pl.num_programs(0)
    n_used = nu_ref[0]
    d_ff2 = w1_ref.shape[-1]
    n_blk = d_ff2 // MXU_N

    def row_copy(tok, j, slot):
        return pltpu.make_async_copy(xn_hbm.at[pl.ds(tok, 1)], xg.at[slot, pl.ds(j, 1)], sem.at[slot])

    @pl.when(i == 0)
    def _():
        def body(j, carry):
            row_copy(tok_ref[j], j, 0).start()
            return carry
        lax.fori_loop(0, tm, body, 0)

    @pl.when(i >= n_used)
    def _():
        ys_ref[...] = jnp.zeros_like(ys_ref)

    prev = te_ref[jnp.maximum(i - 1, 0)]
    changed = jnp.logical_or(i == 0, te_ref[i] != prev)

    @pl.when(jnp.logical_and(changed, i < n_used))
    def _():
        r = lax.broadcasted_iota(jnp.int32, (MXU_N, MXU_N), 0)
        c = lax.broadcasted_iota(jnp.int32, (MXU_N, MXU_N), 1)
        src = jnp.where(c < MXU_N // 2, 2 * c, 2 * (c - MXU_N // 2) + 1)
        perm = jnp.where(r == src, 1.0, 0.0).astype(BF16)
        for blk in range(n_blk):
            cs = slice(blk * MXU_N, (blk + 1) * MXU_N)
            w1p[:, cs] = _dot(w1_ref[:, cs].astype(BF16), perm).astype(BF16)
        w2b[...] = w2_ref[...].astype(BF16)

    @pl.when(i < n_used)
    def _():
        slot = i % 2
        nxt = jnp.minimum(i + 1, n_tiles - 1)
        for j in range(tm):
            row_copy(tok_ref[nxt * tm + j], j, 1 - slot).start()
        for j in range(tm):
            row_copy(0, j, slot).wait()
        x = xg[slot].astype(BF16)
        for blk in range(n_blk):
            a = _dot(x, w1p[:, blk * MXU_N:(blk + 1) * MXU_N])
            half = MXU_N // 2
            hs = slice(blk * half, (blk + 1) * half)
            glu = jnp.minimum(a[:, :half] + bg_ref[:, hs], SWIGLU_LIMIT)
            lin = jnp.clip(a[:, half:] + bl_ref[:, hs], -SWIGLU_LIMIT, SWIGLU_LIMIT)
            act[:, hs] = (glu * _sigmoid(SWIGLU_ALPHA * glu) * (lin + 1.0)).astype(BF16)
        ys_ref[...] = _dot(act[...], w2b[...]) + b2_ref[...]

    @pl.when(i == n_used - 1)
    def _():
        for j in range(tm):
            row_copy(0, j, (i + 1) % 2).wait()


def _final_kernel(tf, pos_ref, h_ref, gate_ref, nfin_ref, ys_hbm, yp_ref, ysm_ref, buf, sem):
    i = pl.program_id(0)
    n = pl.num_programs(0)

    def row_copy(p, j, k, slot):
        return pltpu.make_async_copy(ys_hbm.at[pl.ds(p, 1)], buf.at[slot, k, pl.ds(j, 1)], sem.at[slot])

    @pl.when(i == 0)
    def _():
        def body(j, carry):
            for k in range(TOP_K):
                row_copy(pos_ref[j * TOP_K + k], j, k, 0).start()
            return carry
        lax.fori_loop(0, tf, body, 0)

    slot = i % 2

    @pl.when(i + 1 < n)
    def _():
        for j in range(tf):
            for k in range(TOP_K):
                row_copy(pos_ref[((i + 1) * tf + j) * TOP_K + k], j, k, 1 - slot).start()

    for j in range(tf):
        for k in range(TOP_K):
            row_copy(0, j, k, slot).wait()
    g = gate_ref[...]
    out = h_ref[...]
    for k in range(TOP_K):
        out = out + g[:, k:k + 1] * buf[slot, k]
    y = _rms(out, nfin_ref[...])

    @pl.when(i < n - 1)
    def _():
        yp_ref[...] = y

    @pl.when(i == n - 1)
    def _():
        ysm_ref[...] = y


def kernel(x_prompt, x_sample, state_conv, state_rec, meta_tokens, norm_mix, w_in, conv_w,
           rec_lower_bound, rec_norm, w_out, norm_ffn, router_w, router_b, expert_w1, expert_b1,
           expert_w2, expert_b2, norm_final):
    depth = norm_mix.shape[0]
    assert depth == 1, "single-layer step"
    layer = 0
    bp, seq, d = x_prompt.shape
    ns = x_sample.shape[0]
    assert x_sample.shape[1] == 1 and ns == TOKEN_TILE
    d_conv = conv_w.shape[-1]
    d_rec = rec_lower_bound.shape[-1]
    assert state_conv.shape[2] == 2 and d_rec == N_HEADS * HEAD
    n_exp = router_w.shape[-1]
    d_ff = expert_w2.shape[2]
    n_prompt = bp * seq
    n_tok = n_prompt + ns
    assert seq % PROMPT_TILE == 0 and n_prompt % TOKEN_TILE == 0 and n_tok % RANK_TILE == 0

    nm = norm_mix[layer][None]
    win = w_in[layer].astype(BF16)
    cw = conv_w[layer]
    rlb = rec_lower_bound
    rg = rec_norm[layer][None]
    wout = w_out[layer].astype(BF16)
    nf = norm_ffn[layer][None]
    rw = router_w[layer]
    rb = router_b[layer][None]
    mix_w = (nm, win, cw, rlb)
    tail_w = (rg, wout, nf, rw, rb)

    st_meta, cv_meta = _mixer_call(
        meta_tokens[None], jnp.zeros((1, N_HEADS, HEAD, HEAD), F32), jnp.zeros((1, 2, d_conv), F32),
        mix_w, layer, N_META, N_META)

    xs = x_sample.reshape(ns, d)
    wide = jax.ShapeDtypeStruct((ns, d_rec), F32)
    y_conv_s, new_conv_s, f_s, k_s, q_s, v_s, og_s = pl.pallas_call(
        functools.partial(_sample_in_kernel, layer),
        out_shape=[jax.ShapeDtypeStruct((ns, d_conv), F32), jax.ShapeDtypeStruct((ns, 2 * d_conv), F32),
                   wide, wide, wide, wide, wide],
        compiler_params=pltpu.CompilerParams(vmem_limit_bytes=VMEM_LIMIT),
        name="sample_in",
    )(xs, state_conv[layer].reshape(ns, 2 * d_conv), nm, win, cw, rlb)

    group = 8
    n_grp = ns // group

    def cols(a):
        return a.T.reshape(d_rec, n_grp, group).transpose(1, 0, 2)

    col_spec = pl.BlockSpec((None, d_rec, group), lambda g: (g, 0, 0))
    st_spec = pl.BlockSpec((group, N_HEADS, HEAD, HEAD), lambda g: (g, 0, 0, 0))
    row_spec = pl.BlockSpec((group, d_rec), lambda g: (g, 0))
    new_rec_s, o_s = pl.pallas_call(
        functools.partial(_sample_state_kernel, group),
        grid=(n_grp,),
        in_specs=[col_spec, col_spec, col_spec, row_spec, st_spec],
        out_specs=[st_spec, row_spec],
        out_shape=[jax.ShapeDtypeStruct(state_rec.shape[1:], F32), wide],
        compiler_params=pltpu.CompilerParams(dimension_semantics=("arbitrary",),
                                             vmem_limit_bytes=VMEM_LIMIT),
        name="sample_state",
    )(cols(f_s), cols(k_s), cols(q_s), v_s, state_rec[layer])

    decode = pl.pallas_call(
        _sample_tail_kernel,
        out_shape=[jax.ShapeDtypeStruct((ns, d), F32), jax.ShapeDtypeStruct((ns, d), F32),
                   jax.ShapeDtypeStruct((ns, n_exp), F32), jax.ShapeDtypeStruct((ns, n_exp), F32)],
        compiler_params=pltpu.CompilerParams(vmem_limit_bytes=VMEM_LIMIT),
        name="sample_tail",
    )(xs, y_conv_s, o_s, og_s, rg, wout, nf, rw, rb)

    h_all, xn_all, sel_all, gm_all, new_rec_p, new_conv_p = _mixer_call(
        x_prompt, st_meta, cv_meta, mix_w, layer, PROMPT_TILE, CHUNK, tail=(tail_w, decode))

    canvas = jax.ShapeDtypeStruct((n_tok, LANES), F32)
    tok_spec = pl.BlockSpec((RANK_TILE, n_exp), lambda i: (i, 0))
    can_spec = pl.BlockSpec((RANK_TILE, LANES), lambda i: (i, 0))
    eid, rnk, gate, counts = pl.pallas_call(
        _rank_kernel,
        grid=(n_tok // RANK_TILE,),
        in_specs=[tok_spec, tok_spec],
        out_specs=[can_spec, can_spec, can_spec, pl.BlockSpec((1, n_exp), lambda i: (0, 0))],
        out_shape=[canvas, canvas, canvas, jax.ShapeDtypeStruct((1, n_exp), F32)],
        scratch_shapes=[pltpu.VMEM((1, n_exp), F32)],
        compiler_params=pltpu.CompilerParams(dimension_semantics=("arbitrary",),
                                             vmem_limit_bytes=VMEM_LIMIT),
        name="rank",
    )(sel_all, gm_all)

    tm = MOE_TILE
    n_tiles = (n_tok * TOP_K) // tm + n_exp
    counts = counts[0].astype(jnp.int32)
    tiles_e = (counts + tm - 1) // tm
    tile_end = jnp.cumsum(tiles_e)
    n_used = tile_end[-1]
    offs = (tile_end - tiles_e) * tm
    eid4 = eid[:, :TOP_K].astype(jnp.int32)
    pos = offs[eid4] + rnk[:, :TOP_K].astype(jnp.int32)
    tile_ids = jnp.minimum(jnp.arange(n_tiles, dtype=jnp.int32), n_used - 1)
    tile_expert = jnp.minimum(jnp.searchsorted(tile_end, tile_ids, side="right"), n_exp - 1).astype(jnp.int32)
    tok_ids = jnp.broadcast_to(jnp.arange(n_tok, dtype=jnp.int32)[:, None], (n_tok, TOP_K))
    tok_of_slot = jnp.zeros((n_tiles * tm,), jnp.int32).at[pos.reshape(-1)].set(tok_ids.reshape(-1))

    w1 = expert_w1[layer]
    w2 = expert_w2[layer]
    b1 = expert_b1[layer]
    b1g = b1[:, 0::2][:, None, :]
    b1l = b1[:, 1::2][:, None, :]
    b2 = expert_b2[layer][:, None, :]
    ys = pl.pallas_call(
        functools.partial(_moe_kernel, tm),
        grid_spec=pltpu.PrefetchScalarGridSpec(
            num_scalar_prefetch=3,
            grid=(n_tiles,),
            in_specs=[
                pl.BlockSpec(memory_space=pl.ANY),
                pl.BlockSpec((None, d, 2 * d_ff), lambda i, te, nu, tk: (te[i], 0, 0)),
                pl.BlockSpec((None, 1, d_ff), lambda i, te, nu, tk: (te[i], 0, 0)),
                pl.BlockSpec((None, 1, d_ff), lambda i, te, nu, tk: (te[i], 0, 0)),
                pl.BlockSpec((None, d_ff, d), lambda i, te, nu, tk: (te[i], 0, 0)),
                pl.BlockSpec((None, 1, d), lambda i, te, nu, tk: (te[i], 0, 0)),
            ],
            out_specs=pl.BlockSpec((tm, d), lambda i, te, nu, tk: (i, 0)),
            scratch_shapes=[pltpu.VMEM((2, tm, d), F32), pltpu.SemaphoreType.DMA((2,)),
                            pltpu.VMEM((d, 2 * d_ff), BF16), pltpu.VMEM((d_ff, d), BF16),
                            pltpu.VMEM((tm, d_ff), BF16)],
        ),
        out_shape=jax.ShapeDtypeStruct((n_tiles * tm, d), F32),
        compiler_params=pltpu.CompilerParams(dimension_semantics=("arbitrary",),
                                             vmem_limit_bytes=VMEM_LIMIT),
        name="moe",
    )(tile_expert, n_used[None].astype(jnp.int32), tok_of_slot, xn_all, w1, b1g, b1l, w2, b2)

    tf = TOKEN_TILE
    n_fin = n_tok // tf
    y_p, y_s = pl.pallas_call(
        functools.partial(_final_kernel, tf),
        grid_spec=pltpu.PrefetchScalarGridSpec(
            num_scalar_prefetch=1,
            grid=(n_fin,),
            in_specs=[
                pl.BlockSpec((tf, d), lambda i, p: (i, 0)),
                pl.BlockSpec((tf, LANES), lambda i, p: (i, 0)),
                pl.BlockSpec((1, d), lambda i, p: (0, 0)),
                pl.BlockSpec(memory_space=pl.ANY),
            ],
            out_specs=[pl.BlockSpec((tf, d), lambda i, p: (jnp.minimum(i, n_fin - 2), 0)),
                       pl.BlockSpec((tf, d), lambda i, p: (0, 0))],
            scratch_shapes=[pltpu.VMEM((2, TOP_K, tf, d), F32), pltpu.SemaphoreType.DMA((2,))],
        ),
        out_shape=[jax.ShapeDtypeStruct((n_prompt, d), F32), jax.ShapeDtypeStruct((ns, d), F32)],
        compiler_params=pltpu.CompilerParams(dimension_semantics=("arbitrary",),
                                             vmem_limit_bytes=VMEM_LIMIT),
        name="final",
    )(pos.reshape(-1), h_all, gate, norm_final[None], ys)

    return (y_p.reshape(bp, seq, d), y_s.reshape(ns, 1, d),
            new_conv_p[None], new_rec_p[None],
            new_conv_s.reshape(1, ns, 2, d_conv), new_rec_s[None])
```

```python
import functools

import jax
import jax.numpy as jnp
from jax import lax
from jax.experimental import pallas as pl
from jax.experimental.pallas import tpu as pltpu

F32 = jnp.float32
BF16 = jnp.bfloat16

N_HEADS = 4
HEAD = 128
N_META = 16
CHUNK = 64
TOP_K = 4
SWIGLU_LIMIT = 7.0
SWIGLU_ALPHA = 1.702
EPS = 1e-5

LANES = 128
MXU_N = 256
PROMPT_TILE = 256
TOKEN_TILE = 128
RANK_TILE = 384
MOE_TILE = 256
GATHER_GROUP = 16
VMEM_LIMIT = 56 * 1024 * 1024


def _dot(a, b):
    return jnp.dot(a, b, preferred_element_type=F32)


def _dot_nt(a, b):
    return lax.dot_general(a, b, (((1,), (1,)), ((), ())), preferred_element_type=F32)


def _dot_tn(a, b):
    return lax.dot_general(a, b, (((0,), (0,)), ((), ())), preferred_element_type=F32)


def _split3(x):
    hi = x.astype(BF16)
    r = x - hi.astype(F32)
    mid = r.astype(BF16)
    lo = (r - mid.astype(F32)).astype(BF16)
    return hi, mid, lo


def _sigmoid(x):
    return 1.0 / (1.0 + jnp.exp(-x))


def _rms(x, g):
    ms = jnp.mean(x * x, axis=-1, keepdims=True)
    return x * lax.rsqrt(ms + EPS) * g


def _project(x, nm_ref, win_ref):
    return _dot(_rms(x, nm_ref[...]).astype(BF16), win_ref[...])


def _split_u(u, d_conv, d_rec):
    pts = [0, d_conv, 2 * d_conv, 3 * d_conv, 3 * d_conv + d_rec, 3 * d_conv + 2 * d_rec,
           3 * d_conv + 3 * d_rec, 3 * d_conv + 4 * d_rec]
    return [u[:, pts[i]:pts[i + 1]] for i in range(7)]


def _lower_bound(rlb_ref, layer):
    r = rlb_ref[...]
    e = jnp.exp(r - jnp.max(r, axis=0, keepdims=True))
    return jnp.sum(e[0:layer + 1], axis=0, keepdims=True) / jnp.sum(e, axis=0, keepdims=True)


def _forget(fx, lb):
    f = lb + (1.0 - lb) * _sigmoid(fx)
    return f, 1.0 - f


def _rec_out(o, og, rg):
    parts = []
    for h in range(N_HEADS):
        oh = o[:, h * HEAD:(h + 1) * HEAD]
        parts.append(oh * lax.rsqrt(jnp.mean(oh * oh, axis=-1, keepdims=True) + EPS))
    return jnp.concatenate(parts, axis=-1) * rg * (og * _sigmoid(og))


def _route(logits):
    n = logits.shape[-1]
    lane = lax.broadcasted_iota(jnp.int32, logits.shape, 1).astype(F32)
    work = logits
    tops, hots = [], []
    for _ in range(TOP_K):
        m = jnp.max(work, axis=-1, keepdims=True)
        first = jnp.min(jnp.where(work == m, lane, float(n)), axis=-1, keepdims=True)
        hot = lane == first
        tops.append(m)
        hots.append(hot)
        work = jnp.where(hot, -jnp.inf, work)
    es = [jnp.exp(t - tops[0]) for t in tops]
    den = es[0]
    for e in es[1:]:
        den = den + e
    sel = jnp.zeros_like(logits)
    gm = jnp.zeros_like(logits)
    for hot, e in zip(hots, es):
        sel = sel + jnp.where(hot, 1.0, 0.0)
        gm = gm + jnp.where(hot, e / den, 0.0)
    return sel, gm


def _tail(x, y, wout_ref, nf_ref, rw_ref, rb_ref):
    h = x + _dot(y.astype(BF16), wout_ref[...])
    xn = _rms(h, nf_ref[...])
    xh = xn.astype(BF16)
    xl = (xn - xh.astype(F32)).astype(BF16)
    rw = rw_ref[...]
    wh = rw.astype(BF16)
    wl = (rw - wh.astype(F32)).astype(BF16)
    logits = _dot(xh, wh) + _dot(xh, wl) + _dot(xl, wh) + rb_ref[...]
    sel, gm = _route(logits)
    return h, xn, sel, gm


def _rec_chunk(q, kk, v, lf, st_ref, tri, causal):
    c = q.shape[0]
    hi, mid, lo = _split3(lf)
    b = _dot(tri, hi) + _dot(tri, mid) + _dot(tri, lo)
    eb = jnp.exp(b)
    qe = (q * eb).astype(BF16)
    ke = (kk * jnp.exp(-b)).astype(BF16)
    vb = v.astype(BF16)
    eb_last = eb[c - 1:c]
    outs = []
    for h in range(N_HEADS):
        sl = slice(h * HEAD, (h + 1) * HEAD)
        st = st_ref[h]
        sc = jnp.where(causal, _dot_nt(qe[:, sl], ke[:, sl]), 0.0)
        outs.append(_dot(sc.astype(BF16), vb[:, sl]) + _dot_nt(qe[:, sl], st.astype(BF16)))
        st_ref[h] = (st + _dot_tn(vb[:, sl], ke[:, sl])) * eb_last[:, sl]
    return jnp.concatenate(outs, axis=-1)


def _mixer_kernel(layer, tm, chunk, nt, n_live, emit, *refs):
    if emit:
        (x_ref, s0_ref, c0_ref, nm_ref, win_ref, cw_ref, rlb_ref, rg_ref, wout_ref, nf_ref, rw_ref,
         rb_ref, hs_ref, xns_ref, sels_ref, gms_ref,
         h_ref, xn_ref, sel_ref, gm_ref, st_out, cv_out, convbuf, st) = refs
    else:
        (x_ref, s0_ref, c0_ref, nm_ref, win_ref, cw_ref, rlb_ref, st_out, cv_out, convbuf, st) = refs
    step = pl.program_id(0)
    t = lax.rem(step, nt)
    live = step < n_live
    d_conv = cw_ref.shape[-1]
    d_rec = rlb_ref.shape[-1]

    @pl.when(jnp.logical_and(t == 0, live))
    def _():
        for h in range(N_HEADS):
            st[h] = s0_ref[h].T
        convbuf[6:8, :] = c0_ref[...]

    @pl.when(live)
    def _():
        x = x_ref[...]
        u = _project(x, nm_ref, win_ref)
        bg, cg, hv, q, fx, iv, og = _split_u(u, d_conv, d_rec)

        bx = bg * hv
        convbuf[8:8 + tm, :] = bx
        cw = cw_ref[...]
        conv = cw[0:1] * convbuf[6:6 + tm, :] + cw[1:2] * convbuf[7:7 + tm, :] + cw[2:3] * bx
        y_conv = cg * conv
        convbuf[6:8, :] = bx[tm - 2:tm]
        cv_out[...] = bx[tm - 2:tm]

        f, kk = _forget(fx, _lower_bound(rlb_ref, layer))
        lf = jnp.log(f)
        row = lax.broadcasted_iota(jnp.int32, (chunk, chunk), 0)
        col = lax.broadcasted_iota(jnp.int32, (chunk, chunk), 1)
        causal = row >= col
        tri = jnp.where(causal, 1.0, 0.0).astype(BF16)
        outs = []
        for c in range(tm // chunk):
            rs = slice(c * chunk, (c + 1) * chunk)
            outs.append(_rec_chunk(q[rs], kk[rs], iv[rs], lf[rs], st, tri, causal))

        if emit:
            o = jnp.concatenate(outs, axis=0)
            y = jnp.concatenate([y_conv, _rec_out(o, og, rg_ref[...])], axis=-1)
            h, xn, sel, gm = _tail(x, y, wout_ref, nf_ref, rw_ref, rb_ref)
            h_ref[...] = h
            xn_ref[...] = xn
            sel_ref[...] = sel
            gm_ref[...] = gm

    @pl.when(jnp.logical_and(t == nt - 1, live))
    def _():
        for h in range(N_HEADS):
            st_out[h] = st[h].T

    if emit:
        @pl.when(step == n_live)
        def _():
            ns = hs_ref.shape[0]
            for dst, src in ((h_ref, hs_ref), (xn_ref, xns_ref), (sel_ref, sels_ref), (gm_ref, gms_ref)):
                dst[0:ns, :] = src[...]
                dst[ns:tm, :] = jnp.zeros((tm - ns, dst.shape[-1]), F32)


def _const_spec(shape):
    return pl.BlockSpec(shape, lambda *_: (0,) * len(shape))


def _mixer_call(x, s0, c0, weights, layer, tm, chunk, tail=None):
    nseq, length, d = x.shape
    nt = length // tm
    n_live = nseq * nt
    nm, win, cw, rlb = weights
    d_conv = cw.shape[-1]
    per_seq = s0.shape[0] == nseq

    def seq_of(s):
        return jnp.minimum(s // nt, nseq - 1)

    in_specs = [
        pl.BlockSpec((None, tm, d), lambda s: (seq_of(s), lax.rem(s, nt), 0)),
        pl.BlockSpec((None,) + s0.shape[1:], lambda s: (seq_of(s) if per_seq else 0, 0, 0, 0)),
        pl.BlockSpec((None,) + c0.shape[1:], lambda s: (seq_of(s) if per_seq else 0, 0, 0)),
        _const_spec(nm.shape), _const_spec(win.shape), _const_spec(cw.shape), _const_spec(rlb.shape),
    ]
    args = [x, s0, c0, nm, win, cw, rlb]
    out_shape = [jax.ShapeDtypeStruct((nseq,) + s0.shape[1:], F32),
                 jax.ShapeDtypeStruct((nseq,) + c0.shape[1:], F32)]
    out_specs = [pl.BlockSpec((None,) + s0.shape[1:], lambda s: (seq_of(s), 0, 0, 0)),
                 pl.BlockSpec((None,) + c0.shape[1:], lambda s: (seq_of(s), 0, 0))]
    n_steps = n_live
    if tail is not None:
        tail_w, decode = tail
        assert decode[0].shape[0] <= tm
        n_steps = n_live + 1
        extra = list(tail_w) + list(decode)
        args += extra
        in_specs += [_const_spec(a.shape) for a in extra]
        out_shape = [jax.ShapeDtypeStruct((n_steps * tm, a.shape[-1]), F32) for a in decode] + out_shape
        out_specs = [pl.BlockSpec((tm, a.shape[-1]), lambda s: (s, 0)) for a in decode] + out_specs
    return pl.pallas_call(
        functools.partial(_mixer_kernel, layer, tm, chunk, nt, n_live, tail is not None),
        grid=(n_steps,),
        in_specs=in_specs,
        out_specs=out_specs,
        out_shape=out_shape,
        scratch_shapes=[pltpu.VMEM((tm + 8, d_conv), F32), pltpu.VMEM(s0.shape[1:], F32)],
        compiler_params=pltpu.CompilerParams(dimension_semantics=("arbitrary",),
                                             vmem_limit_bytes=VMEM_LIMIT),
        name="mixer_prompt" if tail is not None else "mixer_meta",
    )(*args)


def _sample_in_kernel(layer, x_ref, sc_ref, nm_ref, win_ref, cw_ref, rlb_ref,
                      yc_ref, nc_ref, f_ref, k_ref, q_ref, v_ref, og_ref):
    d_conv = cw_ref.shape[-1]
    d_rec = rlb_ref.shape[-1]
    u = _project(x_ref[...], nm_ref, win_ref)
    bg, cg, hv, q, fx, iv, og = _split_u(u, d_conv, d_rec)
    bx = bg * hv
    sc = sc_ref[...]
    s0, s1 = sc[:, :d_conv], sc[:, d_conv:]
    cw = cw_ref[...]
    yc_ref[...] = cg * (cw[0:1] * s0 + cw[1:2] * s1 + cw[2:3] * bx)
    nc_ref[...] = jnp.concatenate([s1, bx], axis=-1)
    f, kk = _forget(fx, _lower_bound(rlb_ref, layer))
    f_ref[...] = f
    k_ref[...] = kk
    q_ref[...] = q
    v_ref[...] = iv
    og_ref[...] = og


def _sample_state_kernel(group, f_ref, k_ref, q_ref, v_ref, s_ref, sn_ref, o_ref):
    for j in range(group):
        for h in range(N_HEADS):
            rs = slice(h * HEAD, (h + 1) * HEAD)
            fcol = f_ref[rs, j:j + 1]
            kcol = k_ref[rs, j:j + 1]
            qcol = q_ref[rs, j:j + 1]
            vrow = v_ref[j:j + 1, rs]
            sn = fcol * s_ref[j, h] + kcol * vrow
            sn_ref[j, h] = sn
            o_ref[j:j + 1, rs] = jnp.sum(qcol * sn, axis=0, keepdims=True)


def _sample_tail_kernel(x_ref, yc_ref, o_ref, og_ref, rg_ref, wout_ref, nf_ref, rw_ref, rb_ref,
                        h_ref, xn_ref, sel_ref, gm_ref):
    y = jnp.concatenate([yc_ref[...], _rec_out(o_ref[...], og_ref[...], rg_ref[...])], axis=-1)
    h, xn, sel, gm = _tail(x_ref[...], y, wout_ref, nf_ref, rw_ref, rb_ref)
    h_ref[...] = h
    xn_ref[...] = xn
    sel_ref[...] = sel
    gm_ref[...] = gm


def _rank_kernel(sel_ref, gm_ref, eid_ref, rank_ref, gate_ref, cnt_ref, carry):
    i = pl.program_id(0)

    @pl.when(i == 0)
    def _():
        carry[...] = jnp.zeros_like(carry)

    sel = sel_ref[...]
    gm = gm_ref[...]
    tb, ne = sel.shape
    row = lax.broadcasted_iota(jnp.int32, (tb, tb), 0)
    col = lax.broadcasted_iota(jnp.int32, (tb, tb), 1)
    before = jnp.where(col < row, 1.0, 0.0).astype(BF16)
    selb = sel.astype(BF16)
    rank = _dot(before, selb) + carry[...]
    carry[...] = carry[...] + jnp.sum(sel, axis=0, keepdims=True)
    cnt_ref[...] = carry[...]
    er = lax.broadcasted_iota(jnp.int32, (ne, ne), 0)
    ec = lax.broadcasted_iota(jnp.int32, (ne, ne), 1)
    lower = jnp.where(er < ec, 1.0, 0.0).astype(BF16)
    order = _dot(selb, lower)
    lane_e = lax.broadcasted_iota(jnp.int32, (tb, ne), 1).astype(F32)
    lane = lax.broadcasted_iota(jnp.int32, (tb, LANES), 1)
    eid = jnp.zeros((tb, LANES), F32)
    rnk = jnp.zeros((tb, LANES), F32)
    gat = jnp.zeros((tb, LANES), F32)
    for k in range(TOP_K):
        pick = jnp.where(order == float(k), sel, 0.0)
        eid = jnp.where(lane == k, jnp.sum(pick * lane_e, axis=-1, keepdims=True), eid)
        rnk = jnp.where(lane == k, jnp.sum(pick * rank, axis=-1, keepdims=True), rnk)
        gat = jnp.where(lane == k, jnp.sum(pick * gm, axis=-1, keepdims=True), gat)
    eid_ref[...] = eid
    rank_ref[...] = rnk
    gate_ref[...] = gat


def _moe_kernel(tm, te_ref, nu_ref, tok_ref, xn_hbm, w1_ref, bg_ref, bl_ref, w2_ref, b2_ref,
                ys_ref, xg, sem, w1p, w2b, act):
    i = pl.program_id(0)
    n_tiles = pl.num_programs(0)
    n_used = nu_ref[0]
    d_ff2 = w1_ref.shape[-1]
    n_blk = d_ff2 // MXU_N

    def row_copy(tok, j, slot):
        return pltpu.make_async_copy(xn_hbm.at[pl.ds(tok, 1)], xg.at[slot, pl.ds(j, 1)], sem.at[slot])

    @pl.when(i == 0)
    def _():
        def body(j, carry):
            row_copy(tok_ref[j], j, 0).start()
            return carry
        lax.fori_loop(0, tm, body, 0)

    @pl.when(i >= n_used)
    def _():
        ys_ref[...] = jnp.zeros_like(ys_ref)

    prev = te_ref[jnp.maximum(i - 1, 0)]
    changed = jnp.logical_or(i == 0, te_ref[i] != prev)

    @pl.when(jnp.logical_and(changed, i < n_used))
    def _():
        r = lax.broadcasted_iota(jnp.int32, (MXU_N, MXU_N), 0)
        c = lax.broadcasted_iota(jnp.int32, (MXU_N, MXU_N), 1)
        src = jnp.where(c < MXU_N // 2, 2 * c, 2 * (c - MXU_N // 2) + 1)
        perm = jnp.where(r == src, 1.0, 0.0).astype(BF16)
        for blk in range(n_blk):
            cs = slice(blk * MXU_N, (blk + 1) * MXU_N)
            w1p[:, cs] = _dot(w1_ref[:, cs].astype(BF16), perm).astype(BF16)
        w2b[...] = w2_ref[...].astype(BF16)

    @pl.when(i < n_used)
    def _():
        slot = i % 2
        nxt = jnp.minimum(i + 1, n_tiles - 1)
        for j in range(tm):
            row_copy(0, j, slot).wait()
        rows = tm // n_blk
        for blk in range(n_blk):
            a = _dot(xg[slot].astype(BF16), w1p[:, blk * MXU_N:(blk + 1) * MXU_N])
            half = MXU_N // 2
            hs = slice(blk * half, (blk + 1) * half)
            glu = jnp.minimum(a[:, :half] + bg_ref[:, hs], SWIGLU_LIMIT)
            lin = jnp.clip(a[:, half:] + bl_ref[:, hs], -SWIGLU_LIMIT, SWIGLU_LIMIT)
            act[:, hs] = (glu * _sigmoid(SWIGLU_ALPHA * glu) * (lin + 1.0)).astype(BF16)
            for j0 in range(blk * rows, (blk + 1) * rows, GATHER_GROUP):
                toks = [tok_ref[nxt * tm + j] for j in range(j0, j0 + GATHER_GROUP)]
                for j, tok in zip(range(j0, j0 + GATHER_GROUP), toks):
                    row_copy(tok, j, 1 - slot).start()
        ys_ref[...] = _dot(act[...], w2b[...]) + b2_ref[...]

    @pl.when(i == n_used - 1)
    def _():
        for j in range(tm):
            row_copy(0, j, (i + 1) % 2).wait()


def _final_kernel(tf, pos_ref, h_ref, gate_ref, nfin_ref, ys_hbm, yp_ref, ysm_ref, buf, sem):
    i = pl.program_id(0)
    n = pl.num_programs(0)

    def row_copy(p, j, k, slot):
        return pltpu.make_async_copy(ys_hbm.at[pl.ds(p, 1)], buf.at[slot, k, pl.ds(j, 1)], sem.at[slot])

    @pl.when(i == 0)
    def _():
        def body(j, carry):
            for k in range(TOP_K):
                row_copy(pos_ref[j * TOP_K + k], j, k, 0).start()
            return carry
        lax.fori_loop(0, tf, body, 0)

    slot = i % 2

    @pl.when(i + 1 < n)
    def _():
        for j in range(tf):
            for k in range(TOP_K):
                row_copy(pos_ref[((i + 1) * tf + j) * TOP_K + k], j, k, 1 - slot).start()

    for j in range(tf):
        for k in range(TOP_K):
            row_copy(0, j, k, slot).wait()
    g = gate_ref[...]
    out = h_ref[...]
    for k in range(TOP_K):
        out = out + g[:, k:k + 1] * buf[slot, k]
    y = _rms(out, nfin_ref[...])

    @pl.when(i < n - 1)
    def _():
        yp_ref[...] = y

    @pl.when(i == n - 1)
    def _():
        ysm_ref[...] = y


def kernel(x_prompt, x_sample, state_conv, state_rec, meta_tokens, norm_mix, w_in, conv_w,
           rec_lower_bound, rec_norm, w_out, norm_ffn, router_w, router_b, expert_w1, expert_b1,
           expert_w2, expert_b2, norm_final):
    depth = norm_mix.shape[0]
    assert depth == 1, "single-layer step"
    layer = 0
    bp, seq, d = x_prompt.shape
    ns = x_sample.shape[0]
    assert x_sample.shape[1] == 1 and ns == TOKEN_TILE
    d_conv = conv_w.shape[-1]
    d_rec = rec_lower_bound.shape[-1]
    assert state_conv.shape[2] == 2 and d_rec == N_HEADS * HEAD
    n_exp = router_w.shape[-1]
    d_ff = expert_w2.shape[2]
    n_prompt = bp * seq
    n_tok = n_prompt + ns
    assert seq % PROMPT_TILE == 0 and n_prompt % TOKEN_TILE == 0 and n_tok % RANK_TILE == 0

    nm = norm_mix[layer][None]
    win = w_in[layer].astype(BF16)
    cw = conv_w[layer]
    rlb = rec_lower_bound
    rg = rec_norm[layer][None]
    wout = w_out[layer].astype(BF16)
    nf = norm_ffn[layer][None]
    rw = router_w[layer]
    rb = router_b[layer][None]
    mix_w = (nm, win, cw, rlb)
    tail_w = (rg, wout, nf, rw, rb)

    st_meta, cv_meta = _mixer_call(
        meta_tokens[None], jnp.zeros((1, N_HEADS, HEAD, HEAD), F32), jnp.zeros((1, 2, d_conv), F32),
        mix_w, layer, N_META, N_META)

    xs = x_sample.reshape(ns, d)
    wide = jax.ShapeDtypeStruct((ns, d_rec), F32)
    y_conv_s, new_conv_s, f_s, k_s, q_s, v_s, og_s = pl.pallas_call(
        functools.partial(_sample_in_kernel, layer),
        out_shape=[jax.ShapeDtypeStruct((ns, d_conv), F32), jax.ShapeDtypeStruct((ns, 2 * d_conv), F32),
                   wide, wide, wide, wide, wide],
        compiler_params=pltpu.CompilerParams(vmem_limit_bytes=VMEM_LIMIT),
        name="sample_in",
    )(xs, state_conv[layer].reshape(ns, 2 * d_conv), nm, win, cw, rlb)

    group = 8
    n_grp = ns // group

    def cols(a):
        return a.T.reshape(d_rec, n_grp, group).transpose(1, 0, 2)

    col_spec = pl.BlockSpec((None, d_rec, group), lambda g: (g, 0, 0))
    st_spec = pl.BlockSpec((group, N_HEADS, HEAD, HEAD), lambda g: (g, 0, 0, 0))
    row_spec = pl.BlockSpec((group, d_rec), lambda g: (g, 0))
    new_rec_s, o_s = pl.pallas_call(
        functools.partial(_sample_state_kernel, group),
        grid=(n_grp,),
        in_specs=[col_spec, col_spec, col_spec, row_spec, st_spec],
        out_specs=[st_spec, row_spec],
        out_shape=[jax.ShapeDtypeStruct(state_rec.shape[1:], F32), wide],
        compiler_params=pltpu.CompilerParams(dimension_semantics=("arbitrary",),
                                             vmem_limit_bytes=VMEM_LIMIT),
        name="sample_state",
    )(cols(f_s), cols(k_s), cols(q_s), v_s, state_rec[layer])

    decode = pl.pallas_call(
        _sample_tail_kernel,
        out_shape=[jax.ShapeDtypeStruct((ns, d), F32), jax.ShapeDtypeStruct((ns, d), F32),
                   jax.ShapeDtypeStruct((ns, n_exp), F32), jax.ShapeDtypeStruct((ns, n_exp), F32)],
        compiler_params=pltpu.CompilerParams(vmem_limit_bytes=VMEM_LIMIT),
        name="sample_tail",
    )(xs, y_conv_s, o_s, og_s, rg, wout, nf, rw, rb)

    h_all, xn_all, sel_all, gm_all, new_rec_p, new_conv_p = _mixer_call(
        x_prompt, st_meta, cv_meta, mix_w, layer, PROMPT_TILE, CHUNK, tail=(tail_w, decode))

    canvas = jax.ShapeDtypeStruct((n_tok, LANES), F32)
    tok_spec = pl.BlockSpec((RANK_TILE, n_exp), lambda i: (i, 0))
    can_spec = pl.BlockSpec((RANK_TILE, LANES), lambda i: (i, 0))
    eid, rnk, gate, counts = pl.pallas_call(
        _rank_kernel,
        grid=(n_tok // RANK_TILE,),
        in_specs=[tok_spec, tok_spec],
        out_specs=[can_spec, can_spec, can_spec, pl.BlockSpec((1, n_exp), lambda i: (0, 0))],
        out_shape=[canvas, canvas, canvas, jax.ShapeDtypeStruct((1, n_exp), F32)],
        scratch_shapes=[pltpu.VMEM((1, n_exp), F32)],
        compiler_params=pltpu.CompilerParams(dimension_semantics=("arbitrary",),
                                             vmem_limit_bytes=VMEM_LIMIT),
        name="rank",
    )(sel_all, gm_all)

    tm = MOE_TILE
    n_tiles = (n_tok * TOP_K) // tm + n_exp
    counts = counts[0].astype(jnp.int32)
    tiles_e = (counts + tm - 1) // tm
    tile_end = jnp.cumsum(tiles_e)
    n_used = tile_end[-1]
    offs = (tile_end - tiles_e) * tm
    eid4 = eid[:, :TOP_K].astype(jnp.int32)
    experts = jnp.arange(n_exp, dtype=jnp.int32)
    off4 = jnp.sum(jnp.where(eid4[..., None] == experts, offs, 0), axis=-1)
    pos = off4 + rnk[:, :TOP_K].astype(jnp.int32)
    tile_ids = jnp.minimum(jnp.arange(n_tiles, dtype=jnp.int32), n_used - 1)
    tile_expert = jnp.minimum(jnp.sum((tile_end[None, :] <= tile_ids[:, None]).astype(jnp.int32), axis=1),
                              n_exp - 1)
    tok_ids = jnp.broadcast_to(jnp.arange(n_tok, dtype=jnp.int32)[:, None], (n_tok, TOP_K))
    tok_of_slot = jnp.zeros((n_tiles * tm,), jnp.int32).at[pos.reshape(-1)].set(tok_ids.reshape(-1))

    w1 = expert_w1[layer]
    w2 = expert_w2[layer]
    b1 = expert_b1[layer]
    b1g = b1[:, 0::2][:, None, :]
    b1l = b1[:, 1::2][:, None, :]
    b2 = expert_b2[layer][:, None, :]
    ys = pl.pallas_call(
        functools.partial(_moe_kernel, tm),
        grid_spec=pltpu.PrefetchScalarGridSpec(
            num_scalar_prefetch=3,
            grid=(n_tiles,),
            in_specs=[
                pl.BlockSpec(memory_space=pl.ANY),
                pl.BlockSpec((None, d, 2 * d_ff), lambda i, te, nu, tk: (te[i], 0, 0)),
                pl.BlockSpec((None, 1, d_ff), lambda i, te, nu, tk: (te[i], 0, 0)),
                pl.BlockSpec((None, 1, d_ff), lambda i, te, nu, tk: (te[i], 0, 0)),
                pl.BlockSpec((None, d_ff, d), lambda i, te, nu, tk: (te[i], 0, 0)),
                pl.BlockSpec((None, 1, d), lambda i, te, nu, tk: (te[i], 0, 0)),
            ],
            out_specs=pl.BlockSpec((tm, d), lambda i, te, nu, tk: (i, 0)),
            scratch_shapes=[pltpu.VMEM((2, tm, d), F32), pltpu.SemaphoreType.DMA((2,)),
                            pltpu.VMEM((d, 2 * d_ff), BF16), pltpu.VMEM((d_ff, d), BF16),
                            pltpu.VMEM((tm, d_ff), BF16)],
        ),
        out_shape=jax.ShapeDtypeStruct((n_tiles * tm, d), F32),
        compiler_params=pltpu.CompilerParams(dimension_semantics=("arbitrary",),
                                             vmem_limit_bytes=VMEM_LIMIT),
        name="moe",
    )(tile_expert, n_used[None].astype(jnp.int32), tok_of_slot, xn_all, w1, b1g, b1l, w2, b2)

    tf = TOKEN_TILE
    n_fin = n_tok // tf
    y_p, y_s = pl.pallas_call(
        functools.partial(_final_kernel, tf),
        grid_spec=pltpu.PrefetchScalarGridSpec(
            num_scalar_prefetch=1,
            grid=(n_fin,),
            in_specs=[
                pl.BlockSpec((tf, d), lambda i, p: (i, 0)),
                pl.BlockSpec((tf, LANES), lambda i, p: (i, 0)),
                pl.BlockSpec((1, d), lambda i, p: (0, 0)),
                pl.BlockSpec(memory_space=pl.ANY),
            ],
            out_specs=[pl.BlockSpec((tf, d), lambda i, p: (jnp.minimum(i, n_fin - 2), 0)),
                       pl.BlockSpec((tf, d), lambda i, p: (0, 0))],
            scratch_shapes=[pltpu.VMEM((2, TOP_K, tf, d), F32), pltpu.SemaphoreType.DMA((2,))],
        ),
        out_shape=[jax.ShapeDtypeStruct((n_prompt, d), F32), jax.ShapeDtypeStruct((ns, d), F32)],
        compiler_params=pltpu.CompilerParams(dimension_semantics=("arbitrary",),
                                             vmem_limit_bytes=VMEM_LIMIT),
        name="final",
    )(pos.reshape(-1), h_all, gate, norm_final[None], ys)

    return (y_p.reshape(bp, seq, d), y_s.reshape(ns, 1, d),
            new_conv_p[None], new_rec_p[None],
            new_conv_s.reshape(1, ns, 2, d_conv), new_rec_s[None])
```

```python
import functools

import jax
import jax.numpy as jnp
from jax import lax
from jax.experimental import pallas as pl
from jax.experimental.pallas import tpu as pltpu

F32 = jnp.float32
BF16 = jnp.bfloat16

N_HEADS = 4
HEAD = 128
N_META = 16
CHUNK = 64
TOP_K = 4
SWIGLU_LIMIT = 7.0
SWIGLU_ALPHA = 1.702
EPS = 1e-5

LANES = 128
MXU_N = 256
PROMPT_TILE = 256
TOKEN_TILE = 128
RANK_TILE = 384
MOE_TILE = 256
VMEM_LIMIT = 56 * 1024 * 1024


def _dot(a, b):
    return jnp.dot(a, b, preferred_element_type=F32)


def _dot_nt(a, b):
    return lax.dot_general(a, b, (((1,), (1,)), ((), ())), preferred_element_type=F32)


def _dot_tn(a, b):
    return lax.dot_general(a, b, (((0,), (0,)), ((), ())), preferred_element_type=F32)


def _split3(x):
    hi = x.astype(BF16)
    r = x - hi.astype(F32)
    mid = r.astype(BF16)
    lo = (r - mid.astype(F32)).astype(BF16)
    return hi, mid, lo


def _sigmoid(x):
    return 1.0 / (1.0 + jnp.exp(-x))


def _rms(x, g):
    ms = jnp.mean(x * x, axis=-1, keepdims=True)
    return x * lax.rsqrt(ms + EPS) * g


def _project(x, nm_ref, win_ref):
    return _dot(_rms(x, nm_ref[...]).astype(BF16), win_ref[...])


def _split_u(u, d_conv, d_rec):
    pts = [0, d_conv, 2 * d_conv, 3 * d_conv, 3 * d_conv + d_rec, 3 * d_conv + 2 * d_rec,
           3 * d_conv + 3 * d_rec, 3 * d_conv + 4 * d_rec]
    return [u[:, pts[i]:pts[i + 1]] for i in range(7)]


def _lower_bound(rlb_ref, layer):
    r = rlb_ref[...]
    e = jnp.exp(r - jnp.max(r, axis=0, keepdims=True))
    return jnp.sum(e[0:layer + 1], axis=0, keepdims=True) / jnp.sum(e, axis=0, keepdims=True)


def _forget(fx, lb):
    f = lb + (1.0 - lb) * _sigmoid(fx)
    return f, 1.0 - f


def _rec_out(o, og, rg):
    parts = []
    for h in range(N_HEADS):
        oh = o[:, h * HEAD:(h + 1) * HEAD]
        parts.append(oh * lax.rsqrt(jnp.mean(oh * oh, axis=-1, keepdims=True) + EPS))
    return jnp.concatenate(parts, axis=-1) * rg * (og * _sigmoid(og))


def _route(logits):
    n = logits.shape[-1]
    lane = lax.broadcasted_iota(jnp.int32, logits.shape, 1).astype(F32)
    work = logits
    tops, hots = [], []
    for _ in range(TOP_K):
        m = jnp.max(work, axis=-1, keepdims=True)
        first = jnp.min(jnp.where(work == m, lane, float(n)), axis=-1, keepdims=True)
        hot = lane == first
        tops.append(m)
        hots.append(hot)
        work = jnp.where(hot, -jnp.inf, work)
    es = [jnp.exp(t - tops[0]) for t in tops]
    den = es[0]
    for e in es[1:]:
        den = den + e
    sel = jnp.zeros_like(logits)
    gm = jnp.zeros_like(logits)
    for hot, e in zip(hots, es):
        sel = sel + jnp.where(hot, 1.0, 0.0)
        gm = gm + jnp.where(hot, e / den, 0.0)
    return sel, gm


def _tail(x, y, wout_ref, nf_ref, rw_ref, rb_ref):
    h = x + _dot(y.astype(BF16), wout_ref[...])
    xn = _rms(h, nf_ref[...])
    xh = xn.astype(BF16)
    xl = (xn - xh.astype(F32)).astype(BF16)
    rw = rw_ref[...]
    wh = rw.astype(BF16)
    wl = (rw - wh.astype(F32)).astype(BF16)
    logits = _dot(xh, wh) + _dot(xh, wl) + _dot(xl, wh) + rb_ref[...]
    sel, gm = _route(logits)
    return h, xn, sel, gm


def _rec_chunk(q, kk, v, lf, st_ref, tri, causal):
    c = q.shape[0]
    hi, mid, lo = _split3(lf)
    b = _dot(tri, hi) + _dot(tri, mid) + _dot(tri, lo)
    eb = jnp.exp(b)
    qe = (q * eb).astype(BF16)
    ke = (kk * jnp.exp(-b)).astype(BF16)
    vb = v.astype(BF16)
    eb_last = eb[c - 1:c]
    outs = []
    for h in range(N_HEADS):
        sl = slice(h * HEAD, (h + 1) * HEAD)
        st = st_ref[h]
        sc = jnp.where(causal, _dot_nt(qe[:, sl], ke[:, sl]), 0.0)
        outs.append(_dot(sc.astype(BF16), vb[:, sl]) + _dot_nt(qe[:, sl], st.astype(BF16)))
        st_ref[h] = (st + _dot_tn(vb[:, sl], ke[:, sl])) * eb_last[:, sl]
    return jnp.concatenate(outs, axis=-1)


def _mixer_kernel(layer, tm, chunk, nt, n_live, emit, *refs):
    if emit:
        (x_ref, s0_ref, c0_ref, nm_ref, win_ref, cw_ref, rlb_ref, rg_ref, wout_ref, nf_ref, rw_ref,
         rb_ref, hs_ref, xns_ref, sels_ref, gms_ref,
         h_ref, xn_ref, sel_ref, gm_ref, st_out, cv_out, convbuf, st) = refs
    else:
        (x_ref, s0_ref, c0_ref, nm_ref, win_ref, cw_ref, rlb_ref, st_out, cv_out, convbuf, st) = refs
    step = pl.program_id(0)
    t = lax.rem(step, nt)
    live = step < n_live
    d_conv = cw_ref.shape[-1]
    d_rec = rlb_ref.shape[-1]

    @pl.when(jnp.logical_and(t == 0, live))
    def _():
        for h in range(N_HEADS):
            st[h] = s0_ref[h].T
        convbuf[6:8, :] = c0_ref[...]

    @pl.when(live)
    def _():
        x = x_ref[...]
        u = _project(x, nm_ref, win_ref)
        bg, cg, hv, q, fx, iv, og = _split_u(u, d_conv, d_rec)

        bx = bg * hv
        convbuf[8:8 + tm, :] = bx
        cw = cw_ref[...]
        conv = cw[0:1] * convbuf[6:6 + tm, :] + cw[1:2] * convbuf[7:7 + tm, :] + cw[2:3] * bx
        y_conv = cg * conv
        convbuf[6:8, :] = bx[tm - 2:tm]
        cv_out[...] = bx[tm - 2:tm]

        f, kk = _forget(fx, _lower_bound(rlb_ref, layer))
        lf = jnp.log(f)
        row = lax.broadcasted_iota(jnp.int32, (chunk, chunk), 0)
        col = lax.broadcasted_iota(jnp.int32, (chunk, chunk), 1)
        causal = row >= col
        tri = jnp.where(causal, 1.0, 0.0).astype(BF16)
        outs = []
        for c in range(tm // chunk):
            rs = slice(c * chunk, (c + 1) * chunk)
            outs.append(_rec_chunk(q[rs], kk[rs], iv[rs], lf[rs], st, tri, causal))

        if emit:
            o = jnp.concatenate(outs, axis=0)
            y = jnp.concatenate([y_conv, _rec_out(o, og, rg_ref[...])], axis=-1)
            h, xn, sel, gm = _tail(x, y, wout_ref, nf_ref, rw_ref, rb_ref)
            h_ref[...] = h
            xn_ref[...] = xn.reshape(xn_ref.shape)
            sel_ref[...] = sel
            gm_ref[...] = gm

    @pl.when(jnp.logical_and(t == nt - 1, live))
    def _():
        for h in range(N_HEADS):
            st_out[h] = st[h].T

    if emit:
        @pl.when(step == n_live)
        def _():
            ns = hs_ref.shape[0]
            for dst, src in ((h_ref, hs_ref), (xn_ref, xns_ref), (sel_ref, sels_ref), (gm_ref, gms_ref)):
                dst[0:ns] = src[...].reshape((ns,) + dst.shape[1:])
                dst[ns:tm] = jnp.zeros((tm - ns,) + dst.shape[1:], F32)


def _const_spec(shape):
    return pl.BlockSpec(shape, lambda *_: (0,) * len(shape))


def _mixer_call(x, s0, c0, weights, layer, tm, chunk, tail=None):
    nseq, length, d = x.shape
    nt = length // tm
    n_live = nseq * nt
    nm, win, cw, rlb = weights
    d_conv = cw.shape[-1]
    per_seq = s0.shape[0] == nseq

    def seq_of(s):
        return jnp.minimum(s // nt, nseq - 1)

    in_specs = [
        pl.BlockSpec((None, tm, d), lambda s: (seq_of(s), lax.rem(s, nt), 0)),
        pl.BlockSpec((None,) + s0.shape[1:], lambda s: (seq_of(s) if per_seq else 0, 0, 0, 0)),
        pl.BlockSpec((None,) + c0.shape[1:], lambda s: (seq_of(s) if per_seq else 0, 0, 0)),
        _const_spec(nm.shape), _const_spec(win.shape), _const_spec(cw.shape), _const_spec(rlb.shape),
    ]
    args = [x, s0, c0, nm, win, cw, rlb]
    out_shape = [jax.ShapeDtypeStruct((nseq,) + s0.shape[1:], F32),
                 jax.ShapeDtypeStruct((nseq,) + c0.shape[1:], F32)]
    out_specs = [pl.BlockSpec((None,) + s0.shape[1:], lambda s: (seq_of(s), 0, 0, 0)),
                 pl.BlockSpec((None,) + c0.shape[1:], lambda s: (seq_of(s), 0, 0))]
    n_steps = n_live
    if tail is not None:
        tail_w, decode = tail
        assert decode[0].shape[0] <= tm
        n_steps = n_live + 1
        extra = list(tail_w) + list(decode)
        args += extra
        in_specs += [_const_spec(a.shape) for a in extra]
        mids = [(), (1,), (), ()]
        out_shape = [jax.ShapeDtypeStruct((n_steps * tm,) + m + (a.shape[-1],), F32)
                     for a, m in zip(decode, mids)] + out_shape
        out_specs = [pl.BlockSpec((tm,) + m + (a.shape[-1],), lambda s, m=m: (s,) + (0,) * (len(m) + 1))
                     for a, m in zip(decode, mids)] + out_specs
    return pl.pallas_call(
        functools.partial(_mixer_kernel, layer, tm, chunk, nt, n_live, tail is not None),
        grid=(n_steps,),
        in_specs=in_specs,
        out_specs=out_specs,
        out_shape=out_shape,
        scratch_shapes=[pltpu.VMEM((tm + 8, d_conv), F32), pltpu.VMEM(s0.shape[1:], F32)],
        compiler_params=pltpu.CompilerParams(dimension_semantics=("arbitrary",),
                                             vmem_limit_bytes=VMEM_LIMIT),
        name="mixer_prompt" if tail is not None else "mixer_meta",
    )(*args)


def _sample_in_kernel(layer, x_ref, sc_ref, nm_ref, win_ref, cw_ref, rlb_ref,
                      yc_ref, nc_ref, f_ref, k_ref, q_ref, v_ref, og_ref):
    d_conv = cw_ref.shape[-1]
    d_rec = rlb_ref.shape[-1]
    u = _project(x_ref[...], nm_ref, win_ref)
    bg, cg, hv, q, fx, iv, og = _split_u(u, d_conv, d_rec)
    bx = bg * hv
    sc = sc_ref[...]
    s0, s1 = sc[:, :d_conv], sc[:, d_conv:]
    cw = cw_ref[...]
    yc_ref[...] = cg * (cw[0:1] * s0 + cw[1:2] * s1 + cw[2:3] * bx)
    nc_ref[...] = jnp.concatenate([s1, bx], axis=-1)
    f, kk = _forget(fx, _lower_bound(rlb_ref, layer))
    f_ref[...] = f
    k_ref[...] = kk
    q_ref[...] = q
    v_ref[...] = iv
    og_ref[...] = og


def _sample_state_kernel(group, f_ref, k_ref, q_ref, v_ref, s_ref, sn_ref, o_ref):
    for j in range(group):
        for h in range(N_HEADS):
            rs = slice(h * HEAD, (h + 1) * HEAD)
            fcol = f_ref[rs, j:j + 1]
            kcol = k_ref[rs, j:j + 1]
            qcol = q_ref[rs, j:j + 1]
            vrow = v_ref[j:j + 1, rs]
            sn = fcol * s_ref[j, h] + kcol * vrow
            sn_ref[j, h] = sn
            o_ref[j:j + 1, rs] = jnp.sum(qcol * sn, axis=0, keepdims=True)


def _sample_tail_kernel(x_ref, yc_ref, o_ref, og_ref, rg_ref, wout_ref, nf_ref, rw_ref, rb_ref,
                        h_ref, xn_ref, sel_ref, gm_ref):
    y = jnp.concatenate([yc_ref[...], _rec_out(o_ref[...], og_ref[...], rg_ref[...])], axis=-1)
    h, xn, sel, gm = _tail(x_ref[...], y, wout_ref, nf_ref, rw_ref, rb_ref)
    h_ref[...] = h
    xn_ref[...] = xn
    sel_ref[...] = sel
    gm_ref[...] = gm


def _rank_kernel(sel_ref, gm_ref, eid_ref, rank_ref, gate_ref, cnt_ref, carry):
    i = pl.program_id(0)

    @pl.when(i == 0)
    def _():
        carry[...] = jnp.zeros_like(carry)

    sel = sel_ref[...]
    gm = gm_ref[...]
    tb, ne = sel.shape
    row = lax.broadcasted_iota(jnp.int32, (tb, tb), 0)
    col = lax.broadcasted_iota(jnp.int32, (tb, tb), 1)
    before = jnp.where(col < row, 1.0, 0.0).astype(BF16)
    selb = sel.astype(BF16)
    rank = _dot(before, selb) + carry[...]
    carry[...] = carry[...] + jnp.sum(sel, axis=0, keepdims=True)
    cnt_ref[...] = carry[...]
    er = lax.broadcasted_iota(jnp.int32, (ne, ne), 0)
    ec = lax.broadcasted_iota(jnp.int32, (ne, ne), 1)
    lower = jnp.where(er < ec, 1.0, 0.0).astype(BF16)
    order = _dot(selb, lower)
    lane_e = lax.broadcasted_iota(jnp.int32, (tb, ne), 1).astype(F32)
    lane = lax.broadcasted_iota(jnp.int32, (tb, LANES), 1)
    eid = jnp.zeros((tb, LANES), F32)
    rnk = jnp.zeros((tb, LANES), F32)
    gat = jnp.zeros((tb, LANES), F32)
    for k in range(TOP_K):
        pick = jnp.where(order == float(k), sel, 0.0)
        eid = jnp.where(lane == k, jnp.sum(pick * lane_e, axis=-1, keepdims=True), eid)
        rnk = jnp.where(lane == k, jnp.sum(pick * rank, axis=-1, keepdims=True), rnk)
        gat = jnp.where(lane == k, jnp.sum(pick * gm, axis=-1, keepdims=True), gat)
    eid_ref[...] = eid
    rank_ref[...] = rnk
    gate_ref[...] = gat


def _moe_kernel(tm, te_ref, nu_ref, tok_ref, xn_hbm, w1_ref, bg_ref, bl_ref, w2_ref, b2_ref,
                ys_ref, xg, sem, w1p, w2b, act):
    i = pl.program_id(0)
    n_tiles = pl.num_programs(0)
    n_used = nu_ref[0]
    d_ff2 = w1_ref.shape[-1]
    n_blk = d_ff2 // MXU_N

    def row_copy(tok, j, slot):
        return pltpu.make_async_copy(xn_hbm.at[tok], xg.at[slot, pl.ds(j, 1)], sem.at[slot])

    @pl.when(i == 0)
    def _():
        def body(j, carry):
            row_copy(tok_ref[j], j, 0).start()
            return carry
        lax.fori_loop(0, tm, body, 0)

    @pl.when(i >= n_used)
    def _():
        ys_ref[...] = jnp.zeros_like(ys_ref)

    prev = te_ref[jnp.maximum(i - 1, 0)]
    changed = jnp.logical_or(i == 0, te_ref[i] != prev)

    @pl.when(jnp.logical_and(changed, i < n_used))
    def _():
        r = lax.broadcasted_iota(jnp.int32, (MXU_N, MXU_N), 0)
        c = lax.broadcasted_iota(jnp.int32, (MXU_N, MXU_N), 1)
        src = jnp.where(c < MXU_N // 2, 2 * c, 2 * (c - MXU_N // 2) + 1)
        perm = jnp.where(r == src, 1.0, 0.0).astype(BF16)
        for blk in range(n_blk):
            cs = slice(blk * MXU_N, (blk + 1) * MXU_N)
            w1p[:, cs] = _dot(w1_ref[:, cs].astype(BF16), perm).astype(BF16)
        w2b[...] = w2_ref[...].astype(BF16)

    @pl.when(i < n_used)
    def _():
        slot = i % 2
        nxt = jnp.minimum(i + 1, n_tiles - 1)
        for j in range(tm):
            row_copy(tok_ref[nxt * tm + j], j, 1 - slot).start()
        for j in range(tm):
            row_copy(0, j, slot).wait()
        x = xg[slot].astype(BF16)
        for blk in range(n_blk):
            a = _dot(x, w1p[:, blk * MXU_N:(blk + 1) * MXU_N])
            half = MXU_N // 2
            hs = slice(blk * half, (blk + 1) * half)
            glu = jnp.minimum(a[:, :half] + bg_ref[:, hs], SWIGLU_LIMIT)
            lin = jnp.clip(a[:, half:] + bl_ref[:, hs], -SWIGLU_LIMIT, SWIGLU_LIMIT)
            act[:, hs] = (glu * _sigmoid(SWIGLU_ALPHA * glu) * (lin + 1.0)).astype(BF16)
        ys_ref[...] = _dot(act[...], w2b[...]) + b2_ref[...]

    @pl.when(i == n_used - 1)
    def _():
        for j in range(tm):
            row_copy(0, j, (i + 1) % 2).wait()


def _final_kernel(tf, pos_ref, h_ref, gate_ref, nfin_ref, ys_hbm, yp_ref, ysm_ref, buf, sem):
    i = pl.program_id(0)
    n = pl.num_programs(0)

    def row_copy(p, j, k, slot):
        return pltpu.make_async_copy(ys_hbm.at[pl.ds(p, 1)], buf.at[slot, k, pl.ds(j, 1)], sem.at[slot])

    @pl.when(i == 0)
    def _():
        def body(j, carry):
            for k in range(TOP_K):
                row_copy(pos_ref[j * TOP_K + k], j, k, 0).start()
            return carry
        lax.fori_loop(0, tf, body, 0)

    slot = i % 2

    @pl.when(i + 1 < n)
    def _():
        for j in range(tf):
            for k in range(TOP_K):
                row_copy(pos_ref[((i + 1) * tf + j) * TOP_K + k], j, k, 1 - slot).start()

    for j in range(tf):
        for k in range(TOP_K):
            row_copy(0, j, k, slot).wait()
    g = gate_ref[...]
    out = h_ref[...]
    for k in range(TOP_K):
        out = out + g[:, k:k + 1] * buf[slot, k]
    y = _rms(out, nfin_ref[...])

    @pl.when(i < n - 1)
    def _():
        yp_ref[...] = y

    @pl.when(i == n - 1)
    def _():
        ysm_ref[...] = y


def kernel(x_prompt, x_sample, state_conv, state_rec, meta_tokens, norm_mix, w_in, conv_w,
           rec_lower_bound, rec_norm, w_out, norm_ffn, router_w, router_b, expert_w1, expert_b1,
           expert_w2, expert_b2, norm_final):
    depth = norm_mix.shape[0]
    assert depth == 1, "single-layer step"
    layer = 0
    bp, seq, d = x_prompt.shape
    ns = x_sample.shape[0]
    assert x_sample.shape[1] == 1 and ns == TOKEN_TILE
    d_conv = conv_w.shape[-1]
    d_rec = rec_lower_bound.shape[-1]
    assert state_conv.shape[2] == 2 and d_rec == N_HEADS * HEAD
    n_exp = router_w.shape[-1]
    d_ff = expert_w2.shape[2]
    n_prompt = bp * seq
    n_tok = n_prompt + ns
    assert seq % PROMPT_TILE == 0 and n_prompt % TOKEN_TILE == 0 and n_tok % RANK_TILE == 0

    nm = norm_mix[layer][None]
    win = w_in[layer].astype(BF16)
    cw = conv_w[layer]
    rlb = rec_lower_bound
    rg = rec_norm[layer][None]
    wout = w_out[layer].astype(BF16)
    nf = norm_ffn[layer][None]
    rw = router_w[layer]
    rb = router_b[layer][None]
    mix_w = (nm, win, cw, rlb)
    tail_w = (rg, wout, nf, rw, rb)

    st_meta, cv_meta = _mixer_call(
        meta_tokens[None], jnp.zeros((1, N_HEADS, HEAD, HEAD), F32), jnp.zeros((1, 2, d_conv), F32),
        mix_w, layer, N_META, N_META)

    xs = x_sample.reshape(ns, d)
    wide = jax.ShapeDtypeStruct((ns, d_rec), F32)
    y_conv_s, new_conv_s, f_s, k_s, q_s, v_s, og_s = pl.pallas_call(
        functools.partial(_sample_in_kernel, layer),
        out_shape=[jax.ShapeDtypeStruct((ns, d_conv), F32), jax.ShapeDtypeStruct((ns, 2 * d_conv), F32),
                   wide, wide, wide, wide, wide],
        compiler_params=pltpu.CompilerParams(vmem_limit_bytes=VMEM_LIMIT),
        name="sample_in",
    )(xs, state_conv[layer].reshape(ns, 2 * d_conv), nm, win, cw, rlb)

    group = 8
    n_grp = ns // group

    def cols(a):
        return a.T.reshape(d_rec, n_grp, group).transpose(1, 0, 2)

    col_spec = pl.BlockSpec((None, d_rec, group), lambda g: (g, 0, 0))
    st_spec = pl.BlockSpec((group, N_HEADS, HEAD, HEAD), lambda g: (g, 0, 0, 0))
    row_spec = pl.BlockSpec((group, d_rec), lambda g: (g, 0))
    new_rec_s, o_s = pl.pallas_call(
        functools.partial(_sample_state_kernel, group),
        grid=(n_grp,),
        in_specs=[col_spec, col_spec, col_spec, row_spec, st_spec],
        out_specs=[st_spec, row_spec],
        out_shape=[jax.ShapeDtypeStruct(state_rec.shape[1:], F32), wide],
        compiler_params=pltpu.CompilerParams(dimension_semantics=("arbitrary",),
                                             vmem_limit_bytes=VMEM_LIMIT),
        name="sample_state",
    )(cols(f_s), cols(k_s), cols(q_s), v_s, state_rec[layer])

    decode = pl.pallas_call(
        _sample_tail_kernel,
        out_shape=[jax.ShapeDtypeStruct((ns, d), F32), jax.ShapeDtypeStruct((ns, d), F32),
                   jax.ShapeDtypeStruct((ns, n_exp), F32), jax.ShapeDtypeStruct((ns, n_exp), F32)],
        compiler_params=pltpu.CompilerParams(vmem_limit_bytes=VMEM_LIMIT),
        name="sample_tail",
    )(xs, y_conv_s, o_s, og_s, rg, wout, nf, rw, rb)

    h_all, xn_all, sel_all, gm_all, new_rec_p, new_conv_p = _mixer_call(
        x_prompt, st_meta, cv_meta, mix_w, layer, PROMPT_TILE, CHUNK, tail=(tail_w, decode))

    canvas = jax.ShapeDtypeStruct((n_tok, LANES), F32)
    tok_spec = pl.BlockSpec((RANK_TILE, n_exp), lambda i: (i, 0))
    can_spec = pl.BlockSpec((RANK_TILE, LANES), lambda i: (i, 0))
    eid, rnk, gate, counts = pl.pallas_call(
        _rank_kernel,
        grid=(n_tok // RANK_TILE,),
        in_specs=[tok_spec, tok_spec],
        out_specs=[can_spec, can_spec, can_spec, pl.BlockSpec((1, n_exp), lambda i: (0, 0))],
        out_shape=[canvas, canvas, canvas, jax.ShapeDtypeStruct((1, n_exp), F32)],
        scratch_shapes=[pltpu.VMEM((1, n_exp), F32)],
        compiler_params=pltpu.CompilerParams(dimension_semantics=("arbitrary",),
                                             vmem_limit_bytes=VMEM_LIMIT),
        name="rank",
    )(sel_all, gm_all)

    tm = MOE_TILE
    n_tiles = (n_tok * TOP_K) // tm + n_exp
    counts = counts[0].astype(jnp.int32)
    tiles_e = (counts + tm - 1) // tm
    tile_end = jnp.cumsum(tiles_e)
    n_used = tile_end[-1]
    offs = (tile_end - tiles_e) * tm
    eid4 = eid[:, :TOP_K].astype(jnp.int32)
    experts = jnp.arange(n_exp, dtype=jnp.int32)
    off4 = jnp.sum(jnp.where(eid4[..., None] == experts, offs, 0), axis=-1)
    pos = off4 + rnk[:, :TOP_K].astype(jnp.int32)
    tile_ids = jnp.minimum(jnp.arange(n_tiles, dtype=jnp.int32), n_used - 1)
    tile_expert = jnp.minimum(jnp.sum((tile_end[None, :] <= tile_ids[:, None]).astype(jnp.int32), axis=1),
                              n_exp - 1)
    tok_ids = jnp.broadcast_to(jnp.arange(n_tok, dtype=jnp.int32)[:, None], (n_tok, TOP_K))
    tok_of_slot = jnp.zeros((n_tiles * tm,), jnp.int32).at[pos.reshape(-1)].set(tok_ids.reshape(-1))

    w1 = expert_w1[layer]
    w2 = expert_w2[layer]
    b1 = expert_b1[layer]
    b1g = b1[:, 0::2][:, None, :]
    b1l = b1[:, 1::2][:, None, :]
    b2 = expert_b2[layer][:, None, :]
    ys = pl.pallas_call(
        functools.partial(_moe_kernel, tm),
        grid_spec=pltpu.PrefetchScalarGridSpec(
            num_scalar_prefetch=3,
            grid=(n_tiles,),
            in_specs=[
                pl.BlockSpec(memory_space=pl.ANY),
                pl.BlockSpec((None, d, 2 * d_ff), lambda i, te, nu, tk: (te[i], 0, 0)),
                pl.BlockSpec((None, 1, d_ff), lambda i, te, nu, tk: (te[i], 0, 0)),
                pl.BlockSpec((None, 1, d_ff), lambda i, te, nu, tk: (te[i], 0, 0)),
                pl.BlockSpec((None, d_ff, d), lambda i, te, nu, tk: (te[i], 0, 0)),
                pl.BlockSpec((None, 1, d), lambda i, te, nu, tk: (te[i], 0, 0)),
            ],
            out_specs=pl.BlockSpec((tm, d), lambda i, te, nu, tk: (i, 0)),
            scratch_shapes=[pltpu.VMEM((2, tm, d), F32), pltpu.SemaphoreType.DMA((2,)),
                            pltpu.VMEM((d, 2 * d_ff), BF16), pltpu.VMEM((d_ff, d), BF16),
                            pltpu.VMEM((tm, d_ff), BF16)],
        ),
        out_shape=jax.ShapeDtypeStruct((n_tiles * tm, d), F32),
        compiler_params=pltpu.CompilerParams(dimension_semantics=("arbitrary",),
                                             vmem_limit_bytes=VMEM_LIMIT),
        name="moe",
    )(tile_expert, n_used[None].astype(jnp.int32), tok_of_slot, xn_all, w1, b1g, b1l, w2, b2)

    tf = TOKEN_TILE
    n_fin = n_tok // tf
    y_p, y_s = pl.pallas_call(
        functools.partial(_final_kernel, tf),
        grid_spec=pltpu.PrefetchScalarGridSpec(
            num_scalar_prefetch=1,
            grid=(n_fin,),
            in_specs=[
                pl.BlockSpec((tf, d), lambda i, p: (i, 0)),
                pl.BlockSpec((tf, LANES), lambda i, p: (i, 0)),
                pl.BlockSpec((1, d), lambda i, p: (0, 0)),
                pl.BlockSpec(memory_space=pl.ANY),
            ],
            out_specs=[pl.BlockSpec((tf, d), lambda i, p: (jnp.minimum(i, n_fin - 2), 0)),
                       pl.BlockSpec((tf, d), lambda i, p: (0, 0))],
            scratch_shapes=[pltpu.VMEM((2, TOP_K, tf, d), F32), pltpu.SemaphoreType.DMA((2,))],
        ),
        out_shape=[jax.ShapeDtypeStruct((n_prompt, d), F32), jax.ShapeDtypeStruct((ns, d), F32)],
        compiler_params=pltpu.CompilerParams(dimension_semantics=("arbitrary",),
                                             vmem_limit_bytes=VMEM_LIMIT),
        name="final",
    )(pos.reshape(-1), h_all, gate, norm_final[None], ys)

    return (y_p.reshape(bp, seq, d), y_s.reshape(ns, 1, d),
            new_conv_p[None], new_rec_p[None],
            new_conv_s.reshape(1, ns, 2, d_conv), new_rec_s[None])
```

```python
import functools

import jax
import jax.numpy as jnp
from jax import lax
from jax.experimental import pallas as pl
from jax.experimental.pallas import tpu as pltpu
from jax.experimental.pallas import tpu_sc as plsc

F32 = jnp.float32
BF16 = jnp.bfloat16

N_HEADS = 4
HEAD = 128
N_META = 16
CHUNK = 64
TOP_K = 4
SWIGLU_LIMIT = 7.0
SWIGLU_ALPHA = 1.702
EPS = 1e-5

LANES = 128
MXU_N = 256
PROMPT_TILE = 256
TOKEN_TILE = 128
RANK_TILE = 384
MOE_TILE = 256
SC_CORES = 2
SC_SUBCORES = 16
DISPATCH_CHUNK = 40
VMEM_LIMIT = 56 * 1024 * 1024


def _dot(a, b):
    return jnp.dot(a, b, preferred_element_type=F32)


def _dot_nt(a, b):
    return lax.dot_general(a, b, (((1,), (1,)), ((), ())), preferred_element_type=F32)


def _dot_tn(a, b):
    return lax.dot_general(a, b, (((0,), (0,)), ((), ())), preferred_element_type=F32)


def _split3(x):
    hi = x.astype(BF16)
    r = x - hi.astype(F32)
    mid = r.astype(BF16)
    lo = (r - mid.astype(F32)).astype(BF16)
    return hi, mid, lo


def _sigmoid(x):
    return 1.0 / (1.0 + jnp.exp(-x))


def _rms(x, g):
    ms = jnp.mean(x * x, axis=-1, keepdims=True)
    return x * lax.rsqrt(ms + EPS) * g


def _project(x, nm_ref, win_ref):
    return _dot(_rms(x, nm_ref[...]).astype(BF16), win_ref[...])


def _split_u(u, d_conv, d_rec):
    pts = [0, d_conv, 2 * d_conv, 3 * d_conv, 3 * d_conv + d_rec, 3 * d_conv + 2 * d_rec,
           3 * d_conv + 3 * d_rec, 3 * d_conv + 4 * d_rec]
    return [u[:, pts[i]:pts[i + 1]] for i in range(7)]


def _lower_bound(rlb_ref, layer):
    r = rlb_ref[...]
    e = jnp.exp(r - jnp.max(r, axis=0, keepdims=True))
    return jnp.sum(e[0:layer + 1], axis=0, keepdims=True) / jnp.sum(e, axis=0, keepdims=True)


def _forget(fx, lb):
    f = lb + (1.0 - lb) * _sigmoid(fx)
    return f, 1.0 - f


def _rec_out(o, og, rg):
    parts = []
    for h in range(N_HEADS):
        oh = o[:, h * HEAD:(h + 1) * HEAD]
        parts.append(oh * lax.rsqrt(jnp.mean(oh * oh, axis=-1, keepdims=True) + EPS))
    return jnp.concatenate(parts, axis=-1) * rg * (og * _sigmoid(og))


def _route(logits):
    n = logits.shape[-1]
    lane = lax.broadcasted_iota(jnp.int32, logits.shape, 1).astype(F32)
    work = logits
    tops, hots = [], []
    for _ in range(TOP_K):
        m = jnp.max(work, axis=-1, keepdims=True)
        first = jnp.min(jnp.where(work == m, lane, float(n)), axis=-1, keepdims=True)
        hot = lane == first
        tops.append(m)
        hots.append(hot)
        work = jnp.where(hot, -jnp.inf, work)
    es = [jnp.exp(t - tops[0]) for t in tops]
    den = es[0]
    for e in es[1:]:
        den = den + e
    sel = jnp.zeros_like(logits)
    gm = jnp.zeros_like(logits)
    for hot, e in zip(hots, es):
        sel = sel + jnp.where(hot, 1.0, 0.0)
        gm = gm + jnp.where(hot, e / den, 0.0)
    return sel, gm


def _tail(x, y, wout_ref, nf_ref, rw_ref, rb_ref):
    h = x + _dot(y.astype(BF16), wout_ref[...])
    xn = _rms(h, nf_ref[...])
    xh = xn.astype(BF16)
    xl = (xn - xh.astype(F32)).astype(BF16)
    rw = rw_ref[...]
    wh = rw.astype(BF16)
    wl = (rw - wh.astype(F32)).astype(BF16)
    logits = _dot(xh, wh) + _dot(xh, wl) + _dot(xl, wh) + rb_ref[...]
    sel, gm = _route(logits)
    return h, xn, sel, gm


def _rec_chunk(q, kk, v, lf, st_ref, tri, causal):
    c = q.shape[0]
    hi, mid, lo = _split3(lf)
    b = _dot(tri, hi) + _dot(tri, mid) + _dot(tri, lo)
    eb = jnp.exp(b)
    qe = (q * eb).astype(BF16)
    ke = (kk * jnp.exp(-b)).astype(BF16)
    vb = v.astype(BF16)
    eb_last = eb[c - 1:c]
    outs = []
    for h in range(N_HEADS):
        sl = slice(h * HEAD, (h + 1) * HEAD)
        st = st_ref[h]
        sc = jnp.where(causal, _dot_nt(qe[:, sl], ke[:, sl]), 0.0)
        outs.append(_dot(sc.astype(BF16), vb[:, sl]) + _dot_nt(qe[:, sl], st.astype(BF16)))
        st_ref[h] = (st + _dot_tn(vb[:, sl], ke[:, sl])) * eb_last[:, sl]
    return jnp.concatenate(outs, axis=-1)


def _mixer_kernel(layer, tm, chunk, nt, n_live, emit, *refs):
    if emit:
        (x_ref, s0_ref, c0_ref, nm_ref, win_ref, cw_ref, rlb_ref, rg_ref, wout_ref, nf_ref, rw_ref,
         rb_ref, hs_ref, xns_ref, sels_ref, gms_ref,
         h_ref, xn_ref, sel_ref, gm_ref, st_out, cv_out, convbuf, st) = refs
    else:
        (x_ref, s0_ref, c0_ref, nm_ref, win_ref, cw_ref, rlb_ref, st_out, cv_out, convbuf, st) = refs
    step = pl.program_id(0)
    t = lax.rem(step, nt)
    live = step < n_live
    d_conv = cw_ref.shape[-1]
    d_rec = rlb_ref.shape[-1]

    @pl.when(jnp.logical_and(t == 0, live))
    def _():
        for h in range(N_HEADS):
            st[h] = s0_ref[h].T
        convbuf[6:8, :] = c0_ref[...]

    @pl.when(live)
    def _():
        x = x_ref[...]
        u = _project(x, nm_ref, win_ref)
        bg, cg, hv, q, fx, iv, og = _split_u(u, d_conv, d_rec)

        bx = bg * hv
        convbuf[8:8 + tm, :] = bx
        cw = cw_ref[...]
        conv = cw[0:1] * convbuf[6:6 + tm, :] + cw[1:2] * convbuf[7:7 + tm, :] + cw[2:3] * bx
        y_conv = cg * conv
        convbuf[6:8, :] = bx[tm - 2:tm]
        cv_out[...] = bx[tm - 2:tm]

        f, kk = _forget(fx, _lower_bound(rlb_ref, layer))
        lf = jnp.log(f)
        row = lax.broadcasted_iota(jnp.int32, (chunk, chunk), 0)
        col = lax.broadcasted_iota(jnp.int32, (chunk, chunk), 1)
        causal = row >= col
        tri = jnp.where(causal, 1.0, 0.0).astype(BF16)
        outs = []
        for c in range(tm // chunk):
            rs = slice(c * chunk, (c + 1) * chunk)
            outs.append(_rec_chunk(q[rs], kk[rs], iv[rs], lf[rs], st, tri, causal))

        if emit:
            o = jnp.concatenate(outs, axis=0)
            y = jnp.concatenate([y_conv, _rec_out(o, og, rg_ref[...])], axis=-1)
            h, xn, sel, gm = _tail(x, y, wout_ref, nf_ref, rw_ref, rb_ref)
            h_ref[...] = h
            xn_ref[...] = xn.reshape(xn_ref.shape)
            sel_ref[...] = sel
            gm_ref[...] = gm

    @pl.when(jnp.logical_and(t == nt - 1, live))
    def _():
        for h in range(N_HEADS):
            st_out[h] = st[h].T

    if emit:
        @pl.when(step == n_live)
        def _():
            ns = hs_ref.shape[0]
            for dst, src in ((h_ref, hs_ref), (xn_ref, xns_ref), (sel_ref, sels_ref), (gm_ref, gms_ref)):
                dst[0:ns] = src[...].reshape((ns,) + dst.shape[1:])
                dst[ns:tm] = jnp.zeros((tm - ns,) + dst.shape[1:], F32)


def _const_spec(shape):
    return pl.BlockSpec(shape, lambda *_: (0,) * len(shape))


def _mixer_call(x, s0, c0, weights, layer, tm, chunk, tail=None):
    nseq, length, d = x.shape
    nt = length // tm
    n_live = nseq * nt
    nm, win, cw, rlb = weights
    d_conv = cw.shape[-1]
    per_seq = s0.shape[0] == nseq

    def seq_of(s):
        return jnp.minimum(s // nt, nseq - 1)

    in_specs = [
        pl.BlockSpec((None, tm, d), lambda s: (seq_of(s), lax.rem(s, nt), 0)),
        pl.BlockSpec((None,) + s0.shape[1:], lambda s: (seq_of(s) if per_seq else 0, 0, 0, 0)),
        pl.BlockSpec((None,) + c0.shape[1:], lambda s: (seq_of(s) if per_seq else 0, 0, 0)),
        _const_spec(nm.shape), _const_spec(win.shape), _const_spec(cw.shape), _const_spec(rlb.shape),
    ]
    args = [x, s0, c0, nm, win, cw, rlb]
    out_shape = [jax.ShapeDtypeStruct((nseq,) + s0.shape[1:], F32),
                 jax.ShapeDtypeStruct((nseq,) + c0.shape[1:], F32)]
    out_specs = [pl.BlockSpec((None,) + s0.shape[1:], lambda s: (seq_of(s), 0, 0, 0)),
                 pl.BlockSpec((None,) + c0.shape[1:], lambda s: (seq_of(s), 0, 0))]
    n_steps = n_live
    if tail is not None:
        tail_w, decode = tail
        assert decode[0].shape[0] <= tm
        n_steps = n_live + 1
        extra = list(tail_w) + list(decode)
        args += extra
        in_specs += [_const_spec(a.shape) for a in extra]
        out_shape = [jax.ShapeDtypeStruct((n_steps * tm, a.shape[-1]), F32) for a in decode] + out_shape
        out_specs = [pl.BlockSpec((tm, a.shape[-1]), lambda s: (s, 0)) for a in decode] + out_specs
    return pl.pallas_call(
        functools.partial(_mixer_kernel, layer, tm, chunk, nt, n_live, tail is not None),
        grid=(n_steps,),
        in_specs=in_specs,
        out_specs=out_specs,
        out_shape=out_shape,
        scratch_shapes=[pltpu.VMEM((tm + 8, d_conv), F32), pltpu.VMEM(s0.shape[1:], F32)],
        compiler_params=pltpu.CompilerParams(dimension_semantics=("arbitrary",),
                                             vmem_limit_bytes=VMEM_LIMIT),
        name="mixer_prompt" if tail is not None else "mixer_meta",
    )(*args)


def _sample_in_kernel(layer, x_ref, sc_ref, nm_ref, win_ref, cw_ref, rlb_ref,
                      yc_ref, nc_ref, f_ref, k_ref, q_ref, v_ref, og_ref):
    d_conv = cw_ref.shape[-1]
    d_rec = rlb_ref.shape[-1]
    u = _project(x_ref[...], nm_ref, win_ref)
    bg, cg, hv, q, fx, iv, og = _split_u(u, d_conv, d_rec)
    bx = bg * hv
    sc = sc_ref[...]
    s0, s1 = sc[:, :d_conv], sc[:, d_conv:]
    cw = cw_ref[...]
    yc_ref[...] = cg * (cw[0:1] * s0 + cw[1:2] * s1 + cw[2:3] * bx)
    nc_ref[...] = jnp.concatenate([s1, bx], axis=-1)
    f, kk = _forget(fx, _lower_bound(rlb_ref, layer))
    f_ref[...] = f
    k_ref[...] = kk
    q_ref[...] = q
    v_ref[...] = iv
    og_ref[...] = og


def _sample_state_kernel(group, f_ref, k_ref, q_ref, v_ref, s_ref, sn_ref, o_ref):
    for j in range(group):
        for h in range(N_HEADS):
            rs = slice(h * HEAD, (h + 1) * HEAD)
            fcol = f_ref[rs, j:j + 1]
            kcol = k_ref[rs, j:j + 1]
            qcol = q_ref[rs, j:j + 1]
            vrow = v_ref[j:j + 1, rs]
            sn = fcol * s_ref[j, h] + kcol * vrow
            sn_ref[j, h] = sn
            o_ref[j:j + 1, rs] = jnp.sum(qcol * sn, axis=0, keepdims=True)


def _sample_tail_kernel(x_ref, yc_ref, o_ref, og_ref, rg_ref, wout_ref, nf_ref, rw_ref, rb_ref,
                        h_ref, xn_ref, sel_ref, gm_ref):
    y = jnp.concatenate([yc_ref[...], _rec_out(o_ref[...], og_ref[...], rg_ref[...])], axis=-1)
    h, xn, sel, gm = _tail(x_ref[...], y, wout_ref, nf_ref, rw_ref, rb_ref)
    h_ref[...] = h
    xn_ref[...] = xn
    sel_ref[...] = sel
    gm_ref[...] = gm


def _rank_kernel(sel_ref, gm_ref, eid_ref, rank_ref, gate_ref, cnt_ref, carry):
    i = pl.program_id(0)

    @pl.when(i == 0)
    def _():
        carry[...] = jnp.zeros_like(carry)

    sel = sel_ref[...]
    gm = gm_ref[...]
    tb, ne = sel.shape
    row = lax.broadcasted_iota(jnp.int32, (tb, tb), 0)
    col = lax.broadcasted_iota(jnp.int32, (tb, tb), 1)
    before = jnp.where(col < row, 1.0, 0.0).astype(BF16)
    selb = sel.astype(BF16)
    rank = _dot(before, selb) + carry[...]
    carry[...] = carry[...] + jnp.sum(sel, axis=0, keepdims=True)
    cnt_ref[...] = carry[...]
    er = lax.broadcasted_iota(jnp.int32, (ne, ne), 0)
    ec = lax.broadcasted_iota(jnp.int32, (ne, ne), 1)
    lower = jnp.where(er < ec, 1.0, 0.0).astype(BF16)
    order = _dot(selb, lower)
    lane_e = lax.broadcasted_iota(jnp.int32, (tb, ne), 1).astype(F32)
    lane = lax.broadcasted_iota(jnp.int32, (tb, LANES), 1)
    eid = jnp.zeros((tb, LANES), F32)
    rnk = jnp.zeros((tb, LANES), F32)
    gat = jnp.zeros((tb, LANES), F32)
    for k in range(TOP_K):
        pick = jnp.where(order == float(k), sel, 0.0)
        eid = jnp.where(lane == k, jnp.sum(pick * lane_e, axis=-1, keepdims=True), eid)
        rnk = jnp.where(lane == k, jnp.sum(pick * rank, axis=-1, keepdims=True), rnk)
        gat = jnp.where(lane == k, jnp.sum(pick * gm, axis=-1, keepdims=True), gat)
    eid_ref[...] = eid
    rank_ref[...] = rnk
    gate_ref[...] = gat


def _dispatch(xn, pos_w, n_slots):
    n_workers, n_chunks, top_k, ch = pos_w.shape
    assert n_workers == SC_CORES * SC_SUBCORES and ch % 8 == 0 and ch <= LANES
    d = xn.shape[1]
    mesh = plsc.VectorSubcoreMesh(core_axis_name="c", subcore_axis_name="s")

    @functools.partial(
        pl.kernel, mesh=mesh,
        out_type=jax.ShapeDtypeStruct((n_slots, d), xn.dtype),
        scratch_types=[pltpu.VMEM((top_k, ch), jnp.int32), pltpu.VMEM((ch, d), xn.dtype),
                       pltpu.SemaphoreType.DMA],
        name="dispatch",
    )
    def run(xn_hbm, pos_hbm, xs_hbm, idx_v, rows_v, sem):
        wid = lax.axis_index("s") * SC_CORES + lax.axis_index("c")

        @pl.loop(0, n_chunks)
        def _(c):
            base = (wid * n_chunks + c) * ch
            pltpu.sync_copy(pos_hbm.at[wid, c], idx_v)
            pltpu.sync_copy(xn_hbm.at[pl.ds(base, ch)], rows_v)
            for k in range(top_k):
                pltpu.async_copy(rows_v, xs_hbm.at[idx_v.at[k]], sem).wait()

    return run(xn, pos_w)


def _moe_kernel(tm, te_ref, nu_ref, x_ref, w1_ref, bg_ref, bl_ref, w2_ref, b2_ref,
                ys_ref, w1p, w2b, act):
    i = pl.program_id(0)
    n_used = nu_ref[0]
    d_ff2 = w1_ref.shape[-1]
    n_blk = d_ff2 // MXU_N

    @pl.when(i >= n_used)
    def _():
        ys_ref[...] = jnp.zeros_like(ys_ref)

    prev = te_ref[jnp.maximum(i - 1, 0)]
    changed = jnp.logical_or(i == 0, te_ref[i] != prev)

    @pl.when(jnp.logical_and(changed, i < n_used))
    def _():
        r = lax.broadcasted_iota(jnp.int32, (MXU_N, MXU_N), 0)
        c = lax.broadcasted_iota(jnp.int32, (MXU_N, MXU_N), 1)
        src = jnp.where(c < MXU_N // 2, 2 * c, 2 * (c - MXU_N // 2) + 1)
        perm = jnp.where(r == src, 1.0, 0.0).astype(BF16)
        for blk in range(n_blk):
            cs = slice(blk * MXU_N, (blk + 1) * MXU_N)
            w1p[:, cs] = _dot(w1_ref[:, cs].astype(BF16), perm).astype(BF16)
        w2b[...] = w2_ref[...].astype(BF16)

    @pl.when(i < n_used)
    def _():
        x = x_ref[...].astype(BF16)
        for blk in range(n_blk):
            a = _dot(x, w1p[:, blk * MXU_N:(blk + 1) * MXU_N])
            half = MXU_N // 2
            hs = slice(blk * half, (blk + 1) * half)
            glu = jnp.minimum(a[:, :half] + bg_ref[:, hs], SWIGLU_LIMIT)
            lin = jnp.clip(a[:, half:] + bl_ref[:, hs], -SWIGLU_LIMIT, SWIGLU_LIMIT)
            act[:, hs] = (glu * _sigmoid(SWIGLU_ALPHA * glu) * (lin + 1.0)).astype(BF16)
        ys_ref[...] = _dot(act[...], w2b[...]) + b2_ref[...]


def _final_kernel(tf, pos_ref, h_ref, gate_ref, nfin_ref, ys_hbm, yp_ref, ysm_ref, buf, sem):
    i = pl.program_id(0)
    n = pl.num_programs(0)

    def row_copy(p, j, k, slot):
        return pltpu.make_async_copy(ys_hbm.at[pl.ds(p, 1)], buf.at[slot, k, pl.ds(j, 1)], sem.at[slot])

    @pl.when(i == 0)
    def _():
        def body(j, carry):
            for k in range(TOP_K):
                row_copy(pos_ref[j * TOP_K + k], j, k, 0).start()
            return carry
        lax.fori_loop(0, tf, body, 0)

    slot = i % 2

    @pl.when(i + 1 < n)
    def _():
        for j in range(tf):
            for k in range(TOP_K):
                row_copy(pos_ref[((i + 1) * tf + j) * TOP_K + k], j, k, 1 - slot).start()

    for j in range(tf):
        for k in range(TOP_K):
            row_copy(0, j, k, slot).wait()
    g = gate_ref[...]
    out = h_ref[...]
    for k in range(TOP_K):
        out = out + g[:, k:k + 1] * buf[slot, k]
    y = _rms(out, nfin_ref[...])

    @pl.when(i < n - 1)
    def _():
        yp_ref[...] = y

    @pl.when(i == n - 1)
    def _():
        ysm_ref[...] = y


def kernel(x_prompt, x_sample, state_conv, state_rec, meta_tokens, norm_mix, w_in, conv_w,
           rec_lower_bound, rec_norm, w_out, norm_ffn, router_w, router_b, expert_w1, expert_b1,
           expert_w2, expert_b2, norm_final):
    depth = norm_mix.shape[0]
    assert depth == 1, "single-layer step"
    layer = 0
    bp, seq, d = x_prompt.shape
    ns = x_sample.shape[0]
    assert x_sample.shape[1] == 1 and ns == TOKEN_TILE
    d_conv = conv_w.shape[-1]
    d_rec = rec_lower_bound.shape[-1]
    assert state_conv.shape[2] == 2 and d_rec == N_HEADS * HEAD
    n_exp = router_w.shape[-1]
    d_ff = expert_w2.shape[2]
    n_prompt = bp * seq
    n_tok = n_prompt + ns
    assert seq % PROMPT_TILE == 0 and n_prompt % TOKEN_TILE == 0 and n_tok % RANK_TILE == 0

    nm = norm_mix[layer][None]
    win = w_in[layer].astype(BF16)
    cw = conv_w[layer]
    rlb = rec_lower_bound
    rg = rec_norm[layer][None]
    wout = w_out[layer].astype(BF16)
    nf = norm_ffn[layer][None]
    rw = router_w[layer]
    rb = router_b[layer][None]
    mix_w = (nm, win, cw, rlb)
    tail_w = (rg, wout, nf, rw, rb)

    st_meta, cv_meta = _mixer_call(
        meta_tokens[None], jnp.zeros((1, N_HEADS, HEAD, HEAD), F32), jnp.zeros((1, 2, d_conv), F32),
        mix_w, layer, N_META, N_META)

    xs = x_sample.reshape(ns, d)
    wide = jax.ShapeDtypeStruct((ns, d_rec), F32)
    y_conv_s, new_conv_s, f_s, k_s, q_s, v_s, og_s = pl.pallas_call(
        functools.partial(_sample_in_kernel, layer),
        out_shape=[jax.ShapeDtypeStruct((ns, d_conv), F32), jax.ShapeDtypeStruct((ns, 2 * d_conv), F32),
                   wide, wide, wide, wide, wide],
        compiler_params=pltpu.CompilerParams(vmem_limit_bytes=VMEM_LIMIT),
        name="sample_in",
    )(xs, state_conv[layer].reshape(ns, 2 * d_conv), nm, win, cw, rlb)

    group = 8
    n_grp = ns // group

    def cols(a):
        return a.T.reshape(d_rec, n_grp, group).transpose(1, 0, 2)

    col_spec = pl.BlockSpec((None, d_rec, group), lambda g: (g, 0, 0))
    st_spec = pl.BlockSpec((group, N_HEADS, HEAD, HEAD), lambda g: (g, 0, 0, 0))
    row_spec = pl.BlockSpec((group, d_rec), lambda g: (g, 0))
    new_rec_s, o_s = pl.pallas_call(
        functools.partial(_sample_state_kernel, group),
        grid=(n_grp,),
        in_specs=[col_spec, col_spec, col_spec, row_spec, st_spec],
        out_specs=[st_spec, row_spec],
        out_shape=[jax.ShapeDtypeStruct(state_rec.shape[1:], F32), wide],
        compiler_params=pltpu.CompilerParams(dimension_semantics=("arbitrary",),
                                             vmem_limit_bytes=VMEM_LIMIT),
        name="sample_state",
    )(cols(f_s), cols(k_s), cols(q_s), v_s, state_rec[layer])

    decode = pl.pallas_call(
        _sample_tail_kernel,
        out_shape=[jax.ShapeDtypeStruct((ns, d), F32), jax.ShapeDtypeStruct((ns, d), F32),
                   jax.ShapeDtypeStruct((ns, n_exp), F32), jax.ShapeDtypeStruct((ns, n_exp), F32)],
        compiler_params=pltpu.CompilerParams(vmem_limit_bytes=VMEM_LIMIT),
        name="sample_tail",
    )(xs, y_conv_s, o_s, og_s, rg, wout, nf, rw, rb)

    h_all, xn_all, sel_all, gm_all, new_rec_p, new_conv_p = _mixer_call(
        x_prompt, st_meta, cv_meta, mix_w, layer, PROMPT_TILE, CHUNK, tail=(tail_w, decode))

    canvas = jax.ShapeDtypeStruct((n_tok, LANES), F32)
    tok_spec = pl.BlockSpec((RANK_TILE, n_exp), lambda i: (i, 0))
    can_spec = pl.BlockSpec((RANK_TILE, LANES), lambda i: (i, 0))
    eid, rnk, gate, counts = pl.pallas_call(
        _rank_kernel,
        grid=(n_tok // RANK_TILE,),
        in_specs=[tok_spec, tok_spec],
        out_specs=[can_spec, can_spec, can_spec, pl.BlockSpec((1, n_exp), lambda i: (0, 0))],
        out_shape=[canvas, canvas, canvas, jax.ShapeDtypeStruct((1, n_exp), F32)],
        scratch_shapes=[pltpu.VMEM((1, n_exp), F32)],
        compiler_params=pltpu.CompilerParams(dimension_semantics=("arbitrary",),
                                             vmem_limit_bytes=VMEM_LIMIT),
        name="rank",
    )(sel_all, gm_all)

    tm = MOE_TILE
    n_tiles = (n_tok * TOP_K) // tm + n_exp
    counts = counts[0].astype(jnp.int32)
    tiles_e = (counts + tm - 1) // tm
    tile_end = jnp.cumsum(tiles_e)
    n_used = tile_end[-1]
    offs = (tile_end - tiles_e) * tm
    eid4 = eid[:, :TOP_K].astype(jnp.int32)
    experts = jnp.arange(n_exp, dtype=jnp.int32)
    off4 = jnp.sum(jnp.where(eid4[..., None] == experts, offs, 0), axis=-1)
    pos = off4 + rnk[:, :TOP_K].astype(jnp.int32)
    tile_ids = jnp.minimum(jnp.arange(n_tiles, dtype=jnp.int32), n_used - 1)
    tile_expert = jnp.minimum(jnp.sum((tile_end[None, :] <= tile_ids[:, None]).astype(jnp.int32), axis=1),
                              n_exp - 1)

    n_rows = xn_all.shape[0]
    n_workers = SC_CORES * SC_SUBCORES
    assert n_rows % (n_workers * DISPATCH_CHUNK) == 0
    spare = jnp.full((n_rows - n_tok, TOP_K), n_tiles * tm, jnp.int32)
    pos_w = jnp.concatenate([pos, spare], axis=0).reshape(n_workers, -1, DISPATCH_CHUNK, TOP_K)
    xs = _dispatch(xn_all, pos_w.transpose(0, 1, 3, 2), n_tiles * tm + 8)

    w1 = expert_w1[layer]
    w2 = expert_w2[layer]
    b1 = expert_b1[layer]
    b1g = b1[:, 0::2][:, None, :]
    b1l = b1[:, 1::2][:, None, :]
    b2 = expert_b2[layer][:, None, :]
    ys = pl.pallas_call(
        functools.partial(_moe_kernel, tm),
        grid_spec=pltpu.PrefetchScalarGridSpec(
            num_scalar_prefetch=2,
            grid=(n_tiles,),
            in_specs=[
                pl.BlockSpec((tm, d), lambda i, te, nu: (jnp.minimum(i, nu[0] - 1), 0)),
                pl.BlockSpec((None, d, 2 * d_ff), lambda i, te, nu: (te[i], 0, 0)),
                pl.BlockSpec((None, 1, d_ff), lambda i, te, nu: (te[i], 0, 0)),
                pl.BlockSpec((None, 1, d_ff), lambda i, te, nu: (te[i], 0, 0)),
                pl.BlockSpec((None, d_ff, d), lambda i, te, nu: (te[i], 0, 0)),
                pl.BlockSpec((None, 1, d), lambda i, te, nu: (te[i], 0, 0)),
            ],
            out_specs=pl.BlockSpec((tm, d), lambda i, te, nu: (i, 0)),
            scratch_shapes=[pltpu.VMEM((d, 2 * d_ff), BF16), pltpu.VMEM((d_ff, d), BF16),
                            pltpu.VMEM((tm, d_ff), BF16)],
        ),
        out_shape=jax.ShapeDtypeStruct((n_tiles * tm, d), F32),
        compiler_params=pltpu.CompilerParams(dimension_semantics=("arbitrary",),
                                             vmem_limit_bytes=VMEM_LIMIT),
        name="moe",
    )(tile_expert, n_used[None].astype(jnp.int32), xs, w1, b1g, b1l, w2, b2)

    tf = TOKEN_TILE
    n_fin = n_tok // tf
    y_p, y_s = pl.pallas_call(
        functools.partial(_final_kernel, tf),
        grid_spec=pltpu.PrefetchScalarGridSpec(
            num_scalar_prefetch=1,
            grid=(n_fin,),
            in_specs=[
                pl.BlockSpec((tf, d), lambda i, p: (i, 0)),
                pl.BlockSpec((tf, LANES), lambda i, p: (i, 0)),
                pl.BlockSpec((1, d), lambda i, p: (0, 0)),
                pl.BlockSpec(memory_space=pl.ANY),
            ],
            out_specs=[pl.BlockSpec((tf, d), lambda i, p: (jnp.minimum(i, n_fin - 2), 0)),
                       pl.BlockSpec((tf, d), lambda i, p: (0, 0))],
            scratch_shapes=[pltpu.VMEM((2, TOP_K, tf, d), F32), pltpu.SemaphoreType.DMA((2,))],
        ),
        out_shape=[jax.ShapeDtypeStruct((n_prompt, d), F32), jax.ShapeDtypeStruct((ns, d), F32)],
        compiler_params=pltpu.CompilerParams(dimension_semantics=("arbitrary",),
                                             vmem_limit_bytes=VMEM_LIMIT),
        name="final",
    )(pos.reshape(-1), h_all, gate, norm_final[None], ys)

    return (y_p.reshape(bp, seq, d), y_s.reshape(ns, 1, d),
            new_conv_p[None], new_rec_p[None],
            new_conv_s.reshape(1, ns, 2, d_conv), new_rec_s[None])
```

```python
import functools

import jax
import jax.numpy as jnp
from jax import lax
from jax.experimental import pallas as pl
from jax.experimental.pallas import tpu as pltpu
from jax.experimental.pallas import tpu_sc as plsc

F32 = jnp.float32
BF16 = jnp.bfloat16

N_HEADS = 4
HEAD = 128
N_META = 16
CHUNK = 64
TOP_K = 4
SWIGLU_LIMIT = 7.0
SWIGLU_ALPHA = 1.702
EPS = 1e-5

LANES = 128
MXU_N = 256
PROMPT_TILE = 256
TOKEN_TILE = 128
RANK_TILE = 384
MOE_TILE = 256
SC_CORES = 2
SC_SUBCORES = 16
DISPATCH_CHUNK = 40
VMEM_LIMIT = 56 * 1024 * 1024


def _dot(a, b):
    return jnp.dot(a, b, preferred_element_type=F32)


def _dot_nt(a, b):
    return lax.dot_general(a, b, (((1,), (1,)), ((), ())), preferred_element_type=F32)


def _dot_tn(a, b):
    return lax.dot_general(a, b, (((0,), (0,)), ((), ())), preferred_element_type=F32)


def _split3(x):
    hi = x.astype(BF16)
    r = x - hi.astype(F32)
    mid = r.astype(BF16)
    lo = (r - mid.astype(F32)).astype(BF16)
    return hi, mid, lo


def _sigmoid(x):
    return 1.0 / (1.0 + jnp.exp(-x))


def _rms(x, g):
    ms = jnp.mean(x * x, axis=-1, keepdims=True)
    return x * lax.rsqrt(ms + EPS) * g


def _project(x, nm_ref, win_ref):
    return _dot(_rms(x, nm_ref[...]).astype(BF16), win_ref[...])


def _split_u(u, d_conv, d_rec):
    pts = [0, d_conv, 2 * d_conv, 3 * d_conv, 3 * d_conv + d_rec, 3 * d_conv + 2 * d_rec,
           3 * d_conv + 3 * d_rec, 3 * d_conv + 4 * d_rec]
    return [u[:, pts[i]:pts[i + 1]] for i in range(7)]


def _lower_bound(rlb_ref, layer):
    r = rlb_ref[...]
    e = jnp.exp(r - jnp.max(r, axis=0, keepdims=True))
    return jnp.sum(e[0:layer + 1], axis=0, keepdims=True) / jnp.sum(e, axis=0, keepdims=True)


def _forget(fx, lb):
    f = lb + (1.0 - lb) * _sigmoid(fx)
    return f, 1.0 - f


def _rec_out(o, og, rg):
    parts = []
    for h in range(N_HEADS):
        oh = o[:, h * HEAD:(h + 1) * HEAD]
        parts.append(oh * lax.rsqrt(jnp.mean(oh * oh, axis=-1, keepdims=True) + EPS))
    return jnp.concatenate(parts, axis=-1) * rg * (og * _sigmoid(og))


def _route(logits):
    n = logits.shape[-1]
    lane = lax.broadcasted_iota(jnp.int32, logits.shape, 1).astype(F32)
    work = logits
    tops, hots = [], []
    for _ in range(TOP_K):
        m = jnp.max(work, axis=-1, keepdims=True)
        first = jnp.min(jnp.where(work == m, lane, float(n)), axis=-1, keepdims=True)
        hot = lane == first
        tops.append(m)
        hots.append(hot)
        work = jnp.where(hot, -jnp.inf, work)
    es = [jnp.exp(t - tops[0]) for t in tops]
    den = es[0]
    for e in es[1:]:
        den = den + e
    sel = jnp.zeros_like(logits)
    gm = jnp.zeros_like(logits)
    for hot, e in zip(hots, es):
        sel = sel + jnp.where(hot, 1.0, 0.0)
        gm = gm + jnp.where(hot, e / den, 0.0)
    return sel, gm


def _tail(x, y, wout_ref, nf_ref, rw_ref, rb_ref):
    h = x + _dot(y.astype(BF16), wout_ref[...])
    xn = _rms(h, nf_ref[...])
    xh = xn.astype(BF16)
    xl = (xn - xh.astype(F32)).astype(BF16)
    rw = rw_ref[...]
    wh = rw.astype(BF16)
    wl = (rw - wh.astype(F32)).astype(BF16)
    logits = _dot(xh, wh) + _dot(xh, wl) + _dot(xl, wh) + rb_ref[...]
    sel, gm = _route(logits)
    return h, xn, sel, gm


def _rec_chunk(q, kk, v, lf, st_ref, tri, causal):
    c = q.shape[0]
    hi, mid, lo = _split3(lf)
    b = _dot(tri, hi) + _dot(tri, mid) + _dot(tri, lo)
    eb = jnp.exp(b)
    qe = (q * eb).astype(BF16)
    ke = (kk * jnp.exp(-b)).astype(BF16)
    vb = v.astype(BF16)
    eb_last = eb[c - 1:c]
    outs = []
    for h in range(N_HEADS):
        sl = slice(h * HEAD, (h + 1) * HEAD)
        st = st_ref[h]
        sc = jnp.where(causal, _dot_nt(qe[:, sl], ke[:, sl]), 0.0)
        outs.append(_dot(sc.astype(BF16), vb[:, sl]) + _dot_nt(qe[:, sl], st.astype(BF16)))
        st_ref[h] = (st + _dot_tn(vb[:, sl], ke[:, sl])) * eb_last[:, sl]
    return jnp.concatenate(outs, axis=-1)


def _mixer_kernel(layer, tm, chunk, nt, n_live, emit, *refs):
    if emit:
        (x_ref, s0_ref, c0_ref, nm_ref, win_ref, cw_ref, rlb_ref, rg_ref, wout_ref, nf_ref, rw_ref,
         rb_ref, hs_ref, xns_ref, sels_ref, gms_ref,
         h_ref, xn_ref, sel_ref, gm_ref, st_out, cv_out, convbuf, st) = refs
    else:
        (x_ref, s0_ref, c0_ref, nm_ref, win_ref, cw_ref, rlb_ref, st_out, cv_out, convbuf, st) = refs
    step = pl.program_id(0)
    t = lax.rem(step, nt)
    live = step < n_live
    d_conv = cw_ref.shape[-1]
    d_rec = rlb_ref.shape[-1]

    @pl.when(jnp.logical_and(t == 0, live))
    def _():
        for h in range(N_HEADS):
            st[h] = s0_ref[h].T
        convbuf[6:8, :] = c0_ref[...]

    @pl.when(live)
    def _():
        x = x_ref[...]
        u = _project(x, nm_ref, win_ref)
        bg, cg, hv, q, fx, iv, og = _split_u(u, d_conv, d_rec)

        bx = bg * hv
        convbuf[8:8 + tm, :] = bx
        cw = cw_ref[...]
        conv = cw[0:1] * convbuf[6:6 + tm, :] + cw[1:2] * convbuf[7:7 + tm, :] + cw[2:3] * bx
        y_conv = cg * conv
        convbuf[6:8, :] = bx[tm - 2:tm]
        cv_out[...] = bx[tm - 2:tm]

        f, kk = _forget(fx, _lower_bound(rlb_ref, layer))
        lf = jnp.log(f)
        row = lax.broadcasted_iota(jnp.int32, (chunk, chunk), 0)
        col = lax.broadcasted_iota(jnp.int32, (chunk, chunk), 1)
        causal = row >= col
        tri = jnp.where(causal, 1.0, 0.0).astype(BF16)
        outs = []
        for c in range(tm // chunk):
            rs = slice(c * chunk, (c + 1) * chunk)
            outs.append(_rec_chunk(q[rs], kk[rs], iv[rs], lf[rs], st, tri, causal))

        if emit:
            o = jnp.concatenate(outs, axis=0)
            y = jnp.concatenate([y_conv, _rec_out(o, og, rg_ref[...])], axis=-1)
            h, xn, sel, gm = _tail(x, y, wout_ref, nf_ref, rw_ref, rb_ref)
            h_ref[...] = h
            xn_ref[...] = xn.reshape(xn_ref.shape)
            sel_ref[...] = sel
            gm_ref[...] = gm

    @pl.when(jnp.logical_and(t == nt - 1, live))
    def _():
        for h in range(N_HEADS):
            st_out[h] = st[h].T

    if emit:
        @pl.when(step == n_live)
        def _():
            ns = hs_ref.shape[0]
            for dst, src in ((h_ref, hs_ref), (xn_ref, xns_ref), (sel_ref, sels_ref), (gm_ref, gms_ref)):
                dst[0:ns] = src[...].reshape((ns,) + dst.shape[1:])
                dst[ns:tm] = jnp.zeros((tm - ns,) + dst.shape[1:], F32)


def _const_spec(shape):
    return pl.BlockSpec(shape, lambda *_: (0,) * len(shape))


def _mixer_call(x, s0, c0, weights, layer, tm, chunk, tail=None):
    nseq, length, d = x.shape
    nt = length // tm
    n_live = nseq * nt
    nm, win, cw, rlb = weights
    d_conv = cw.shape[-1]
    per_seq = s0.shape[0] == nseq

    def seq_of(s):
        return jnp.minimum(s // nt, nseq - 1)

    in_specs = [
        pl.BlockSpec((None, tm, d), lambda s: (seq_of(s), lax.rem(s, nt), 0)),
        pl.BlockSpec((None,) + s0.shape[1:], lambda s: (seq_of(s) if per_seq else 0, 0, 0, 0)),
        pl.BlockSpec((None,) + c0.shape[1:], lambda s: (seq_of(s) if per_seq else 0, 0, 0)),
        _const_spec(nm.shape), _const_spec(win.shape), _const_spec(cw.shape), _const_spec(rlb.shape),
    ]
    args = [x, s0, c0, nm, win, cw, rlb]
    out_shape = [jax.ShapeDtypeStruct((nseq,) + s0.shape[1:], F32),
                 jax.ShapeDtypeStruct((nseq,) + c0.shape[1:], F32)]
    out_specs = [pl.BlockSpec((None,) + s0.shape[1:], lambda s: (seq_of(s), 0, 0, 0)),
                 pl.BlockSpec((None,) + c0.shape[1:], lambda s: (seq_of(s), 0, 0))]
    n_steps = n_live
    if tail is not None:
        tail_w, decode = tail
        assert decode[0].shape[0] <= tm
        n_steps = n_live + 1
        extra = list(tail_w) + list(decode)
        args += extra
        in_specs += [_const_spec(a.shape) for a in extra]
        out_shape = [jax.ShapeDtypeStruct((n_steps * tm, a.shape[-1]), F32) for a in decode] + out_shape
        out_specs = [pl.BlockSpec((tm, a.shape[-1]), lambda s: (s, 0)) for a in decode] + out_specs
    return pl.pallas_call(
        functools.partial(_mixer_kernel, layer, tm, chunk, nt, n_live, tail is not None),
        grid=(n_steps,),
        in_specs=in_specs,
        out_specs=out_specs,
        out_shape=out_shape,
        scratch_shapes=[pltpu.VMEM((tm + 8, d_conv), F32), pltpu.VMEM(s0.shape[1:], F32)],
        compiler_params=pltpu.CompilerParams(dimension_semantics=("arbitrary",),
                                             vmem_limit_bytes=VMEM_LIMIT),
        name="mixer_prompt" if tail is not None else "mixer_meta",
    )(*args)


def _sample_in_kernel(layer, x_ref, sc_ref, nm_ref, win_ref, cw_ref, rlb_ref,
                      yc_ref, nc_ref, f_ref, k_ref, q_ref, v_ref, og_ref):
    d_conv = cw_ref.shape[-1]
    d_rec = rlb_ref.shape[-1]
    u = _project(x_ref[...], nm_ref, win_ref)
    bg, cg, hv, q, fx, iv, og = _split_u(u, d_conv, d_rec)
    bx = bg * hv
    sc = sc_ref[...]
    s0, s1 = sc[:, :d_conv], sc[:, d_conv:]
    cw = cw_ref[...]
    yc_ref[...] = cg * (cw[0:1] * s0 + cw[1:2] * s1 + cw[2:3] * bx)
    nc_ref[...] = jnp.concatenate([s1, bx], axis=-1)
    f, kk = _forget(fx, _lower_bound(rlb_ref, layer))
    f_ref[...] = f
    k_ref[...] = kk
    q_ref[...] = q
    v_ref[...] = iv
    og_ref[...] = og


def _sample_state_kernel(group, f_ref, k_ref, q_ref, v_ref, s_ref, sn_ref, o_ref):
    for j in range(group):
        for h in range(N_HEADS):
            rs = slice(h * HEAD, (h + 1) * HEAD)
            fcol = f_ref[rs, j:j + 1]
            kcol = k_ref[rs, j:j + 1]
            qcol = q_ref[rs, j:j + 1]
            vrow = v_ref[j:j + 1, rs]
            sn = fcol * s_ref[j, h] + kcol * vrow
            sn_ref[j, h] = sn
            o_ref[j:j + 1, rs] = jnp.sum(qcol * sn, axis=0, keepdims=True)


def _sample_tail_kernel(x_ref, yc_ref, o_ref, og_ref, rg_ref, wout_ref, nf_ref, rw_ref, rb_ref,
                        h_ref, xn_ref, sel_ref, gm_ref):
    y = jnp.concatenate([yc_ref[...], _rec_out(o_ref[...], og_ref[...], rg_ref[...])], axis=-1)
    h, xn, sel, gm = _tail(x_ref[...], y, wout_ref, nf_ref, rw_ref, rb_ref)
    h_ref[...] = h
    xn_ref[...] = xn
    sel_ref[...] = sel
    gm_ref[...] = gm


def _rank_kernel(sel_ref, gm_ref, eid_ref, rank_ref, gate_ref, cnt_ref, carry):
    i = pl.program_id(0)

    @pl.when(i == 0)
    def _():
        carry[...] = jnp.zeros_like(carry)

    sel = sel_ref[...]
    gm = gm_ref[...]
    tb, ne = sel.shape
    row = lax.broadcasted_iota(jnp.int32, (tb, tb), 0)
    col = lax.broadcasted_iota(jnp.int32, (tb, tb), 1)
    before = jnp.where(col < row, 1.0, 0.0).astype(BF16)
    selb = sel.astype(BF16)
    rank = _dot(before, selb) + carry[...]
    carry[...] = carry[...] + jnp.sum(sel, axis=0, keepdims=True)
    cnt_ref[...] = carry[...]
    er = lax.broadcasted_iota(jnp.int32, (ne, ne), 0)
    ec = lax.broadcasted_iota(jnp.int32, (ne, ne), 1)
    lower = jnp.where(er < ec, 1.0, 0.0).astype(BF16)
    order = _dot(selb, lower)
    lane_e = lax.broadcasted_iota(jnp.int32, (tb, ne), 1).astype(F32)
    lane = lax.broadcasted_iota(jnp.int32, (tb, LANES), 1)
    eid = jnp.zeros((tb, LANES), F32)
    rnk = jnp.zeros((tb, LANES), F32)
    gat = jnp.zeros((tb, LANES), F32)
    for k in range(TOP_K):
        pick = jnp.where(order == float(k), sel, 0.0)
        eid = jnp.where(lane == k, jnp.sum(pick * lane_e, axis=-1, keepdims=True), eid)
        rnk = jnp.where(lane == k, jnp.sum(pick * rank, axis=-1, keepdims=True), rnk)
        gat = jnp.where(lane == k, jnp.sum(pick * gm, axis=-1, keepdims=True), gat)
    eid_ref[...] = eid
    rank_ref[...] = rnk
    gate_ref[...] = gat


def _dispatch(xn, pos_w, n_slots):
    n_workers, n_chunks, top_k, ch = pos_w.shape
    assert n_workers == SC_CORES * SC_SUBCORES and ch % 8 == 0 and ch <= LANES
    d = xn.shape[1]
    mesh = plsc.VectorSubcoreMesh(core_axis_name="c", subcore_axis_name="s")

    @functools.partial(
        pl.kernel, mesh=mesh,
        out_type=jax.ShapeDtypeStruct((n_slots, d), xn.dtype),
        scratch_types=[pltpu.VMEM((n_chunks, top_k, ch), jnp.int32), pltpu.VMEM((2, ch, d), xn.dtype),
                       pltpu.SemaphoreType.DMA((2,)), pltpu.SemaphoreType.DMA((2,))],
        name="dispatch",
    )
    def run(xn_hbm, pos_hbm, xs_hbm, idx_v, rows_v, sem_r, sem_w):
        wid = lax.axis_index("s") * SC_CORES + lax.axis_index("c")

        def read(c, b):
            src = xn_hbm.at[pl.ds((wid * n_chunks + c) * ch, ch)]
            return pltpu.make_async_copy(src, rows_v.at[b], sem_r.at[b])

        def write(c, b, k):
            return pltpu.make_async_copy(rows_v.at[b], xs_hbm.at[idx_v.at[c, k]], sem_w.at[b])

        pltpu.sync_copy(pos_hbm.at[wid], idx_v)
        read(0, 0).start()

        @pl.loop(0, n_chunks)
        def _(c):
            b = lax.rem(c, 2)
            read(c, b).wait()

            @pl.when(c >= 1)
            def _():
                for k in range(top_k):
                    write(c - 1, 1 - b, k).wait()

            @pl.when(c + 1 < n_chunks)
            def _():
                read(c + 1, 1 - b).start()

            for k in range(top_k):
                write(c, b, k).start()

        for k in range(top_k):
            write(n_chunks - 1, (n_chunks - 1) % 2, k).wait()

    return run(xn, pos_w)


def _collect(ys, pos_w):
    n_workers, n_chunks, ch = pos_w.shape
    assert n_workers == SC_CORES * SC_SUBCORES and ch % 8 == 0 and ch <= LANES
    d = ys.shape[1]
    mesh = plsc.VectorSubcoreMesh(core_axis_name="c", subcore_axis_name="s")

    @functools.partial(
        pl.kernel, mesh=mesh,
        out_type=jax.ShapeDtypeStruct((n_workers * n_chunks * ch, d), ys.dtype),
        scratch_types=[pltpu.VMEM((n_chunks, ch), jnp.int32), pltpu.VMEM((2, ch, d), ys.dtype),
                       pltpu.SemaphoreType.DMA((2,)), pltpu.SemaphoreType.DMA((2,))],
        name="collect",
    )
    def run(ys_hbm, pos_hbm, out_hbm, idx_v, rows_v, sem_r, sem_w):
        wid = lax.axis_index("s") * SC_CORES + lax.axis_index("c")

        def read(c, b):
            return pltpu.make_async_copy(ys_hbm.at[idx_v.at[c]], rows_v.at[b], sem_r.at[b])

        def write(c, b):
            dst = out_hbm.at[pl.ds((wid * n_chunks + c) * ch, ch)]
            return pltpu.make_async_copy(rows_v.at[b], dst, sem_w.at[b])

        pltpu.sync_copy(pos_hbm.at[wid], idx_v)
        read(0, 0).start()

        @pl.loop(0, n_chunks)
        def _(c):
            b = lax.rem(c, 2)
            read(c, b).wait()

            @pl.when(c >= 1)
            def _():
                write(c - 1, 1 - b).wait()

            @pl.when(c + 1 < n_chunks)
            def _():
                read(c + 1, 1 - b).start()

            write(c, b).start()

        write(n_chunks - 1, (n_chunks - 1) % 2).wait()

    return run(ys, pos_w)


def _moe_kernel(tm, te_ref, nu_ref, x_ref, w1_ref, bg_ref, bl_ref, w2_ref, b2_ref,
                ys_ref, w1p, w2b, act):
    i = pl.program_id(0)
    n_used = nu_ref[0]
    d_ff2 = w1_ref.shape[-1]
    n_blk = d_ff2 // MXU_N

    @pl.when(i >= n_used)
    def _():
        ys_ref[...] = jnp.zeros_like(ys_ref)

    prev = te_ref[jnp.maximum(i - 1, 0)]
    changed = jnp.logical_or(i == 0, te_ref[i] != prev)

    @pl.when(jnp.logical_and(changed, i < n_used))
    def _():
        r = lax.broadcasted_iota(jnp.int32, (MXU_N, MXU_N), 0)
        c = lax.broadcasted_iota(jnp.int32, (MXU_N, MXU_N), 1)
        src = jnp.where(c < MXU_N // 2, 2 * c, 2 * (c - MXU_N // 2) + 1)
        perm = jnp.where(r == src, 1.0, 0.0).astype(BF16)
        for blk in range(n_blk):
            cs = slice(blk * MXU_N, (blk + 1) * MXU_N)
            w1p[:, cs] = _dot(w1_ref[:, cs].astype(BF16), perm).astype(BF16)
        w2b[...] = w2_ref[...].astype(BF16)

    @pl.when(i < n_used)
    def _():
        x = x_ref[...].astype(BF16)
        for blk in range(n_blk):
            a = _dot(x, w1p[:, blk * MXU_N:(blk + 1) * MXU_N])
            half = MXU_N // 2
            hs = slice(blk * half, (blk + 1) * half)
            glu = jnp.minimum(a[:, :half] + bg_ref[:, hs], SWIGLU_LIMIT)
            lin = jnp.clip(a[:, half:] + bl_ref[:, hs], -SWIGLU_LIMIT, SWIGLU_LIMIT)
            act[:, hs] = (glu * _sigmoid(SWIGLU_ALPHA * glu) * (lin + 1.0)).astype(BF16)
        ys_ref[...] = _dot(act[...], w2b[...]) + b2_ref[...]


def _final_kernel(h_ref, gate_ref, nfin_ref, z_ref, yp_ref, ysm_ref):
    i = pl.program_id(0)
    n = pl.num_programs(0)
    d = h_ref.shape[-1]
    g = gate_ref[...]
    out = h_ref[...]
    for k in range(TOP_K):
        out = out + g[:, k:k + 1] * z_ref[:, k * d:(k + 1) * d]
    y = _rms(out, nfin_ref[...])

    @pl.when(i < n - 1)
    def _():
        yp_ref[...] = y

    @pl.when(i == n - 1)
    def _():
        ysm_ref[...] = y


def kernel(x_prompt, x_sample, state_conv, state_rec, meta_tokens, norm_mix, w_in, conv_w,
           rec_lower_bound, rec_norm, w_out, norm_ffn, router_w, router_b, expert_w1, expert_b1,
           expert_w2, expert_b2, norm_final):
    depth = norm_mix.shape[0]
    assert depth == 1, "single-layer step"
    layer = 0
    bp, seq, d = x_prompt.shape
    ns = x_sample.shape[0]
    assert x_sample.shape[1] == 1 and ns == TOKEN_TILE
    d_conv = conv_w.shape[-1]
    d_rec = rec_lower_bound.shape[-1]
    assert state_conv.shape[2] == 2 and d_rec == N_HEADS * HEAD
    n_exp = router_w.shape[-1]
    d_ff = expert_w2.shape[2]
    n_prompt = bp * seq
    n_tok = n_prompt + ns
    assert seq % PROMPT_TILE == 0 and n_prompt % TOKEN_TILE == 0 and n_tok % RANK_TILE == 0

    nm = norm_mix[layer][None]
    win = w_in[layer].astype(BF16)
    cw = conv_w[layer]
    rlb = rec_lower_bound
    rg = rec_norm[layer][None]
    wout = w_out[layer].astype(BF16)
    nf = norm_ffn[layer][None]
    rw = router_w[layer]
    rb = router_b[layer][None]
    mix_w = (nm, win, cw, rlb)
    tail_w = (rg, wout, nf, rw, rb)

    st_meta, cv_meta = _mixer_call(
        meta_tokens[None], jnp.zeros((1, N_HEADS, HEAD, HEAD), F32), jnp.zeros((1, 2, d_conv), F32),
        mix_w, layer, N_META, N_META)

    xs = x_sample.reshape(ns, d)
    wide = jax.ShapeDtypeStruct((ns, d_rec), F32)
    y_conv_s, new_conv_s, f_s, k_s, q_s, v_s, og_s = pl.pallas_call(
        functools.partial(_sample_in_kernel, layer),
        out_shape=[jax.ShapeDtypeStruct((ns, d_conv), F32), jax.ShapeDtypeStruct((ns, 2 * d_conv), F32),
                   wide, wide, wide, wide, wide],
        compiler_params=pltpu.CompilerParams(vmem_limit_bytes=VMEM_LIMIT),
        name="sample_in",
    )(xs, state_conv[layer].reshape(ns, 2 * d_conv), nm, win, cw, rlb)

    group = 8
    n_grp = ns // group

    def cols(a):
        return a.T.reshape(d_rec, n_grp, group).transpose(1, 0, 2)

    col_spec = pl.BlockSpec((None, d_rec, group), lambda g: (g, 0, 0))
    st_spec = pl.BlockSpec((group, N_HEADS, HEAD, HEAD), lambda g: (g, 0, 0, 0))
    row_spec = pl.BlockSpec((group, d_rec), lambda g: (g, 0))
    new_rec_s, o_s = pl.pallas_call(
        functools.partial(_sample_state_kernel, group),
        grid=(n_grp,),
        in_specs=[col_spec, col_spec, col_spec, row_spec, st_spec],
        out_specs=[st_spec, row_spec],
        out_shape=[jax.ShapeDtypeStruct(state_rec.shape[1:], F32), wide],
        compiler_params=pltpu.CompilerParams(dimension_semantics=("arbitrary",),
                                             vmem_limit_bytes=VMEM_LIMIT),
        name="sample_state",
    )(cols(f_s), cols(k_s), cols(q_s), v_s, state_rec[layer])

    decode = pl.pallas_call(
        _sample_tail_kernel,
        out_shape=[jax.ShapeDtypeStruct((ns, d), F32), jax.ShapeDtypeStruct((ns, d), F32),
                   jax.ShapeDtypeStruct((ns, n_exp), F32), jax.ShapeDtypeStruct((ns, n_exp), F32)],
        compiler_params=pltpu.CompilerParams(vmem_limit_bytes=VMEM_LIMIT),
        name="sample_tail",
    )(xs, y_conv_s, o_s, og_s, rg, wout, nf, rw, rb)

    h_all, xn_all, sel_all, gm_all, new_rec_p, new_conv_p = _mixer_call(
        x_prompt, st_meta, cv_meta, mix_w, layer, PROMPT_TILE, CHUNK, tail=(tail_w, decode))

    canvas = jax.ShapeDtypeStruct((n_tok, LANES), F32)
    tok_spec = pl.BlockSpec((RANK_TILE, n_exp), lambda i: (i, 0))
    can_spec = pl.BlockSpec((RANK_TILE, LANES), lambda i: (i, 0))
    eid, rnk, gate, counts = pl.pallas_call(
        _rank_kernel,
        grid=(n_tok // RANK_TILE,),
        in_specs=[tok_spec, tok_spec],
        out_specs=[can_spec, can_spec, can_spec, pl.BlockSpec((1, n_exp), lambda i: (0, 0))],
        out_shape=[canvas, canvas, canvas, jax.ShapeDtypeStruct((1, n_exp), F32)],
        scratch_shapes=[pltpu.VMEM((1, n_exp), F32)],
        compiler_params=pltpu.CompilerParams(dimension_semantics=("arbitrary",),
                                             vmem_limit_bytes=VMEM_LIMIT),
        name="rank",
    )(sel_all, gm_all)

    tm = MOE_TILE
    n_tiles = (n_tok * TOP_K) // tm + n_exp
    counts = counts[0].astype(jnp.int32)
    tiles_e = (counts + tm - 1) // tm
    tile_end = jnp.cumsum(tiles_e)
    n_used = tile_end[-1]
    offs = (tile_end - tiles_e) * tm
    eid4 = eid[:, :TOP_K].astype(jnp.int32)
    experts = jnp.arange(n_exp, dtype=jnp.int32)
    off4 = jnp.sum(jnp.where(eid4[..., None] == experts, offs, 0), axis=-1)
    pos = off4 + rnk[:, :TOP_K].astype(jnp.int32)
    tile_ids = jnp.minimum(jnp.arange(n_tiles, dtype=jnp.int32), n_used - 1)
    tile_expert = jnp.minimum(jnp.sum((tile_end[None, :] <= tile_ids[:, None]).astype(jnp.int32), axis=1),
                              n_exp - 1)

    n_rows = xn_all.shape[0]
    n_workers = SC_CORES * SC_SUBCORES
    assert n_rows % (n_workers * DISPATCH_CHUNK) == 0
    spare = jnp.full((n_rows - n_tok, TOP_K), n_tiles * tm, jnp.int32)
    pos_w = jnp.concatenate([pos, spare], axis=0).reshape(n_workers, -1, DISPATCH_CHUNK, TOP_K)
    xs = _dispatch(xn_all, pos_w.transpose(0, 1, 3, 2), n_tiles * tm + 8)

    w1 = expert_w1[layer]
    w2 = expert_w2[layer]
    b1 = expert_b1[layer]
    b1g = b1[:, 0::2][:, None, :]
    b1l = b1[:, 1::2][:, None, :]
    b2 = expert_b2[layer][:, None, :]
    ys = pl.pallas_call(
        functools.partial(_moe_kernel, tm),
        grid_spec=pltpu.PrefetchScalarGridSpec(
            num_scalar_prefetch=2,
            grid=(n_tiles,),
            in_specs=[
                pl.BlockSpec((tm, d), lambda i, te, nu: (jnp.minimum(i, nu[0] - 1), 0)),
                pl.BlockSpec((None, d, 2 * d_ff), lambda i, te, nu: (te[i], 0, 0)),
                pl.BlockSpec((None, 1, d_ff), lambda i, te, nu: (te[i], 0, 0)),
                pl.BlockSpec((None, 1, d_ff), lambda i, te, nu: (te[i], 0, 0)),
                pl.BlockSpec((None, d_ff, d), lambda i, te, nu: (te[i], 0, 0)),
                pl.BlockSpec((None, 1, d), lambda i, te, nu: (te[i], 0, 0)),
            ],
            out_specs=pl.BlockSpec((tm, d), lambda i, te, nu: (i, 0)),
            scratch_shapes=[pltpu.VMEM((d, 2 * d_ff), BF16), pltpu.VMEM((d_ff, d), BF16),
                            pltpu.VMEM((tm, d_ff), BF16)],
        ),
        out_shape=jax.ShapeDtypeStruct((n_tiles * tm, d), F32),
        compiler_params=pltpu.CompilerParams(dimension_semantics=("arbitrary",),
                                             vmem_limit_bytes=VMEM_LIMIT),
        name="moe",
    )(tile_expert, n_used[None].astype(jnp.int32), xs, w1, b1g, b1l, w2, b2)

    pos_flat = jnp.concatenate([pos, jnp.zeros((n_rows - n_tok, TOP_K), jnp.int32)], axis=0)
    z = _collect(ys, pos_flat.reshape(n_workers, -1, DISPATCH_CHUNK)).reshape(n_rows, TOP_K * d)
    tf = TOKEN_TILE
    n_fin = n_tok // tf
    y_p, y_s = pl.pallas_call(
        _final_kernel,
        grid=(n_fin,),
        in_specs=[
            pl.BlockSpec((tf, d), lambda i: (i, 0)),
            pl.BlockSpec((tf, LANES), lambda i: (i, 0)),
            pl.BlockSpec((1, d), lambda i: (0, 0)),
            pl.BlockSpec((tf, TOP_K * d), lambda i: (i, 0)),
        ],
        out_specs=[pl.BlockSpec((tf, d), lambda i: (jnp.minimum(i, n_fin - 2), 0)),
                   pl.BlockSpec((tf, d), lambda i: (0, 0))],
        out_shape=[jax.ShapeDtypeStruct((n_prompt, d), F32), jax.ShapeDtypeStruct((ns, d), F32)],
        compiler_params=pltpu.CompilerParams(dimension_semantics=("arbitrary",),
                                             vmem_limit_bytes=VMEM_LIMIT),
        name="final",
    )(h_all, gate, norm_final[None], z)

    return (y_p.reshape(bp, seq, d), y_s.reshape(ns, 1, d),
            new_conv_p[None], new_rec_p[None],
            new_conv_s.reshape(1, ns, 2, d_conv), new_rec_s[None])
```

```python
import functools

import jax
import jax.numpy as jnp
from jax import lax
from jax.experimental import pallas as pl
from jax.experimental.pallas import tpu as pltpu
from jax.experimental.pallas import tpu_sc as plsc

F32 = jnp.float32
BF16 = jnp.bfloat16

N_HEADS = 4
HEAD = 128
N_META = 16
CHUNK = 64
TOP_K = 4
SWIGLU_LIMIT = 7.0
SWIGLU_ALPHA = 1.702
EPS = 1e-5

LANES = 128
MXU_N = 256
PROMPT_TILE = 256
TOKEN_TILE = 128
RANK_TILE = 384
MOE_TILE = 256
SC_CORES = 2
SC_SUBCORES = 16
DISPATCH_CHUNK = 40
VMEM_LIMIT = 56 * 1024 * 1024


def _dot(a, b):
    return jnp.dot(a, b, preferred_element_type=F32)


def _dot_nt(a, b):
    return lax.dot_general(a, b, (((1,), (1,)), ((), ())), preferred_element_type=F32)


def _dot_tn(a, b):
    return lax.dot_general(a, b, (((0,), (0,)), ((), ())), preferred_element_type=F32)


def _split3(x):
    hi = x.astype(BF16)
    r = x - hi.astype(F32)
    mid = r.astype(BF16)
    lo = (r - mid.astype(F32)).astype(BF16)
    return hi, mid, lo


def _pack_halves(x):
    n = x.shape[-1] // 2
    lo = pltpu.bitcast(x[:, :n].astype(BF16).astype(F32), jnp.uint32)
    hi = pltpu.bitcast(x[:, n:].astype(BF16).astype(F32), jnp.uint32)
    return (lo >> 16) | (hi & jnp.uint32(0xFFFF0000))


def _unpack_halves(u):
    lo = pltpu.bitcast(u << 16, F32)
    hi = pltpu.bitcast(u & jnp.uint32(0xFFFF0000), F32)
    return jnp.concatenate([lo, hi], axis=-1)


def _sigmoid(x):
    return 1.0 / (1.0 + jnp.exp(-x))


def _rms(x, g):
    ms = jnp.mean(x * x, axis=-1, keepdims=True)
    return x * lax.rsqrt(ms + EPS) * g


def _project(x, nm_ref, win_ref):
    return _dot(_rms(x, nm_ref[...]).astype(BF16), win_ref[...])


def _split_u(u, d_conv, d_rec):
    pts = [0, d_conv, 2 * d_conv, 3 * d_conv, 3 * d_conv + d_rec, 3 * d_conv + 2 * d_rec,
           3 * d_conv + 3 * d_rec, 3 * d_conv + 4 * d_rec]
    return [u[:, pts[i]:pts[i + 1]] for i in range(7)]


def _lower_bound(rlb_ref, layer):
    r = rlb_ref[...]
    e = jnp.exp(r - jnp.max(r, axis=0, keepdims=True))
    return jnp.sum(e[0:layer + 1], axis=0, keepdims=True) / jnp.sum(e, axis=0, keepdims=True)


def _forget(fx, lb):
    f = lb + (1.0 - lb) * _sigmoid(fx)
    return f, 1.0 - f


def _rec_out(o, og, rg):
    parts = []
    for h in range(N_HEADS):
        oh = o[:, h * HEAD:(h + 1) * HEAD]
        parts.append(oh * lax.rsqrt(jnp.mean(oh * oh, axis=-1, keepdims=True) + EPS))
    return jnp.concatenate(parts, axis=-1) * rg * (og * _sigmoid(og))


def _route(logits):
    n = logits.shape[-1]
    lane = lax.broadcasted_iota(jnp.int32, logits.shape, 1).astype(F32)
    work = logits
    tops, hots = [], []
    for _ in range(TOP_K):
        m = jnp.max(work, axis=-1, keepdims=True)
        first = jnp.min(jnp.where(work == m, lane, float(n)), axis=-1, keepdims=True)
        hot = lane == first
        tops.append(m)
        hots.append(hot)
        work = jnp.where(hot, -jnp.inf, work)
    es = [jnp.exp(t - tops[0]) for t in tops]
    den = es[0]
    for e in es[1:]:
        den = den + e
    sel = jnp.zeros_like(logits)
    gm = jnp.zeros_like(logits)
    for hot, e in zip(hots, es):
        sel = sel + jnp.where(hot, 1.0, 0.0)
        gm = gm + jnp.where(hot, e / den, 0.0)
    return sel, gm


def _tail(x, y, wout_ref, nf_ref, rw_ref, rb_ref):
    h = x + _dot(y.astype(BF16), wout_ref[...])
    xn = _rms(h, nf_ref[...])
    xh = xn.astype(BF16)
    xl = (xn - xh.astype(F32)).astype(BF16)
    rw = rw_ref[...]
    wh = rw.astype(BF16)
    wl = (rw - wh.astype(F32)).astype(BF16)
    logits = _dot(xh, wh) + _dot(xh, wl) + _dot(xl, wh) + rb_ref[...]
    sel, gm = _route(logits)
    return h, xn, sel, gm


def _rec_chunk(q, kk, v, lf, st_ref, tri, causal):
    c = q.shape[0]
    hi, mid, lo = _split3(lf)
    b = _dot(tri, hi) + _dot(tri, mid) + _dot(tri, lo)
    eb = jnp.exp(b)
    qe = (q * eb).astype(BF16)
    ke = (kk * jnp.exp(-b)).astype(BF16)
    vb = v.astype(BF16)
    eb_last = eb[c - 1:c]
    outs = []
    for h in range(N_HEADS):
        sl = slice(h * HEAD, (h + 1) * HEAD)
        st = st_ref[h]
        sc = jnp.where(causal, _dot_nt(qe[:, sl], ke[:, sl]), 0.0)
        outs.append(_dot(sc.astype(BF16), vb[:, sl]) + _dot_nt(qe[:, sl], st.astype(BF16)))
        st_ref[h] = (st + _dot_tn(vb[:, sl], ke[:, sl])) * eb_last[:, sl]
    return jnp.concatenate(outs, axis=-1)


def _mixer_kernel(layer, tm, chunk, nt, n_live, emit, *refs):
    if emit:
        (x_ref, s0_ref, c0_ref, nm_ref, win_ref, cw_ref, rlb_ref, rg_ref, wout_ref, nf_ref, rw_ref,
         rb_ref, hs_ref, xns_ref, sels_ref, gms_ref,
         h_ref, xn_ref, sel_ref, gm_ref, st_out, cv_out, convbuf, st) = refs
    else:
        (x_ref, s0_ref, c0_ref, nm_ref, win_ref, cw_ref, rlb_ref, st_out, cv_out, convbuf, st) = refs
    step = pl.program_id(0)
    t = lax.rem(step, nt)
    live = step < n_live
    d_conv = cw_ref.shape[-1]
    d_rec = rlb_ref.shape[-1]

    @pl.when(jnp.logical_and(t == 0, live))
    def _():
        for h in range(N_HEADS):
            st[h] = s0_ref[h].T
        convbuf[6:8, :] = c0_ref[...]

    @pl.when(live)
    def _():
        x = x_ref[...]
        u = _project(x, nm_ref, win_ref)
        bg, cg, hv, q, fx, iv, og = _split_u(u, d_conv, d_rec)

        bx = bg * hv
        convbuf[8:8 + tm, :] = bx
        cw = cw_ref[...]
        conv = cw[0:1] * convbuf[6:6 + tm, :] + cw[1:2] * convbuf[7:7 + tm, :] + cw[2:3] * bx
        y_conv = cg * conv
        convbuf[6:8, :] = bx[tm - 2:tm]
        cv_out[...] = bx[tm - 2:tm]

        f, kk = _forget(fx, _lower_bound(rlb_ref, layer))
        lf = jnp.log(f)
        row = lax.broadcasted_iota(jnp.int32, (chunk, chunk), 0)
        col = lax.broadcasted_iota(jnp.int32, (chunk, chunk), 1)
        causal = row >= col
        tri = jnp.where(causal, 1.0, 0.0).astype(BF16)
        outs = []
        for c in range(tm // chunk):
            rs = slice(c * chunk, (c + 1) * chunk)
            outs.append(_rec_chunk(q[rs], kk[rs], iv[rs], lf[rs], st, tri, causal))

        if emit:
            o = jnp.concatenate(outs, axis=0)
            y = jnp.concatenate([y_conv, _rec_out(o, og, rg_ref[...])], axis=-1)
            h, xn, sel, gm = _tail(x, y, wout_ref, nf_ref, rw_ref, rb_ref)
            h_ref[...] = h
            xn_ref[...] = _pack_halves(xn)
            sel_ref[...] = sel
            gm_ref[...] = gm

    @pl.when(jnp.logical_and(t == nt - 1, live))
    def _():
        for h in range(N_HEADS):
            st_out[h] = st[h].T

    if emit:
        @pl.when(step == n_live)
        def _():
            ns = hs_ref.shape[0]
            for dst, val in ((h_ref, hs_ref[...]), (xn_ref, _pack_halves(xns_ref[...])),
                             (sel_ref, sels_ref[...]), (gm_ref, gms_ref[...])):
                dst[0:ns, :] = val
                dst[ns:tm, :] = jnp.zeros((tm - ns, dst.shape[-1]), dst.dtype)


def _const_spec(shape):
    return pl.BlockSpec(shape, lambda *_: (0,) * len(shape))


def _mixer_call(x, s0, c0, weights, layer, tm, chunk, tail=None):
    nseq, length, d = x.shape
    nt = length // tm
    n_live = nseq * nt
    nm, win, cw, rlb = weights
    d_conv = cw.shape[-1]
    per_seq = s0.shape[0] == nseq

    def seq_of(s):
        return jnp.minimum(s // nt, nseq - 1)

    in_specs = [
        pl.BlockSpec((None, tm, d), lambda s: (seq_of(s), lax.rem(s, nt), 0)),
        pl.BlockSpec((None,) + s0.shape[1:], lambda s: (seq_of(s) if per_seq else 0, 0, 0, 0)),
        pl.BlockSpec((None,) + c0.shape[1:], lambda s: (seq_of(s) if per_seq else 0, 0, 0)),
        _const_spec(nm.shape), _const_spec(win.shape), _const_spec(cw.shape), _const_spec(rlb.shape),
    ]
    args = [x, s0, c0, nm, win, cw, rlb]
    out_shape = [jax.ShapeDtypeStruct((nseq,) + s0.shape[1:], F32),
                 jax.ShapeDtypeStruct((nseq,) + c0.shape[1:], F32)]
    out_specs = [pl.BlockSpec((None,) + s0.shape[1:], lambda s: (seq_of(s), 0, 0, 0)),
                 pl.BlockSpec((None,) + c0.shape[1:], lambda s: (seq_of(s), 0, 0))]
    n_steps = n_live
    if tail is not None:
        tail_w, decode = tail
        assert decode[0].shape[0] <= tm
        n_steps = n_live + 1
        extra = list(tail_w) + list(decode)
        args += extra
        in_specs += [_const_spec(a.shape) for a in extra]
        tok = [(a.shape[-1], F32) for a in decode]
        tok[1] = (tok[1][0] // 2, jnp.uint32)
        out_shape = [jax.ShapeDtypeStruct((n_steps * tm, w), t) for w, t in tok] + out_shape
        out_specs = [pl.BlockSpec((tm, w), lambda s: (s, 0)) for w, _ in tok] + out_specs
    return pl.pallas_call(
        functools.partial(_mixer_kernel, layer, tm, chunk, nt, n_live, tail is not None),
        grid=(n_steps,),
        in_specs=in_specs,
        out_specs=out_specs,
        out_shape=out_shape,
        scratch_shapes=[pltpu.VMEM((tm + 8, d_conv), F32), pltpu.VMEM(s0.shape[1:], F32)],
        compiler_params=pltpu.CompilerParams(dimension_semantics=("arbitrary",),
                                             vmem_limit_bytes=VMEM_LIMIT),
        name="mixer_prompt" if tail is not None else "mixer_meta",
    )(*args)


def _sample_in_kernel(layer, x_ref, sc_ref, nm_ref, win_ref, cw_ref, rlb_ref,
                      yc_ref, nc_ref, f_ref, k_ref, q_ref, v_ref, og_ref):
    d_conv = cw_ref.shape[-1]
    d_rec = rlb_ref.shape[-1]
    u = _project(x_ref[...], nm_ref, win_ref)
    bg, cg, hv, q, fx, iv, og = _split_u(u, d_conv, d_rec)
    bx = bg * hv
    sc = sc_ref[...]
    s0, s1 = sc[:, :d_conv], sc[:, d_conv:]
    cw = cw_ref[...]
    yc_ref[...] = cg * (cw[0:1] * s0 + cw[1:2] * s1 + cw[2:3] * bx)
    nc_ref[...] = jnp.concatenate([s1, bx], axis=-1)
    f, kk = _forget(fx, _lower_bound(rlb_ref, layer))
    f_ref[...] = f
    k_ref[...] = kk
    q_ref[...] = q
    v_ref[...] = iv
    og_ref[...] = og


def _sample_state_kernel(group, f_ref, k_ref, q_ref, v_ref, s_ref, sn_ref, o_ref):
    for j in range(group):
        for h in range(N_HEADS):
            rs = slice(h * HEAD, (h + 1) * HEAD)
            fcol = f_ref[rs, j:j + 1]
            kcol = k_ref[rs, j:j + 1]
            qcol = q_ref[rs, j:j + 1]
            vrow = v_ref[j:j + 1, rs]
            sn = fcol * s_ref[j, h] + kcol * vrow
            sn_ref[j, h] = sn
            o_ref[j:j + 1, rs] = jnp.sum(qcol * sn, axis=0, keepdims=True)


def _sample_tail_kernel(x_ref, yc_ref, o_ref, og_ref, rg_ref, wout_ref, nf_ref, rw_ref, rb_ref,
                        h_ref, xn_ref, sel_ref, gm_ref):
    y = jnp.concatenate([yc_ref[...], _rec_out(o_ref[...], og_ref[...], rg_ref[...])], axis=-1)
    h, xn, sel, gm = _tail(x_ref[...], y, wout_ref, nf_ref, rw_ref, rb_ref)
    h_ref[...] = h
    xn_ref[...] = xn
    sel_ref[...] = sel
    gm_ref[...] = gm


def _rank_kernel(sel_ref, gm_ref, eid_ref, rank_ref, gate_ref, cnt_ref, carry):
    i = pl.program_id(0)

    @pl.when(i == 0)
    def _():
        carry[...] = jnp.zeros_like(carry)

    sel = sel_ref[...]
    gm = gm_ref[...]
    tb, ne = sel.shape
    row = lax.broadcasted_iota(jnp.int32, (tb, tb), 0)
    col = lax.broadcasted_iota(jnp.int32, (tb, tb), 1)
    before = jnp.where(col < row, 1.0, 0.0).astype(BF16)
    selb = sel.astype(BF16)
    rank = _dot(before, selb) + carry[...]
    carry[...] = carry[...] + jnp.sum(sel, axis=0, keepdims=True)
    cnt_ref[...] = carry[...]
    er = lax.broadcasted_iota(jnp.int32, (ne, ne), 0)
    ec = lax.broadcasted_iota(jnp.int32, (ne, ne), 1)
    lower = jnp.where(er < ec, 1.0, 0.0).astype(BF16)
    order = _dot(selb, lower)
    lane_e = lax.broadcasted_iota(jnp.int32, (tb, ne), 1).astype(F32)
    lane = lax.broadcasted_iota(jnp.int32, (tb, LANES), 1)
    eid = jnp.zeros((tb, LANES), F32)
    rnk = jnp.zeros((tb, LANES), F32)
    gat = jnp.zeros((tb, LANES), F32)
    for k in range(TOP_K):
        pick = jnp.where(order == float(k), sel, 0.0)
        eid = jnp.where(lane == k, jnp.sum(pick * lane_e, axis=-1, keepdims=True), eid)
        rnk = jnp.where(lane == k, jnp.sum(pick * rank, axis=-1, keepdims=True), rnk)
        gat = jnp.where(lane == k, jnp.sum(pick * gm, axis=-1, keepdims=True), gat)
    eid_ref[...] = eid
    rank_ref[...] = rnk
    gate_ref[...] = gat


def _dispatch(xn, pos_w, n_slots):
    n_workers, n_chunks, top_k, ch = pos_w.shape
    assert n_workers == SC_CORES * SC_SUBCORES and ch % 8 == 0 and ch <= LANES
    d = xn.shape[1]
    mesh = plsc.VectorSubcoreMesh(core_axis_name="c", subcore_axis_name="s")

    @functools.partial(
        pl.kernel, mesh=mesh,
        out_type=jax.ShapeDtypeStruct((n_slots, d), xn.dtype),
        scratch_types=[pltpu.VMEM((n_chunks, top_k, ch), jnp.int32), pltpu.VMEM((2, ch, d), xn.dtype),
                       pltpu.SemaphoreType.DMA((2,)), pltpu.SemaphoreType.DMA((2,))],
        name="dispatch",
    )
    def run(xn_hbm, pos_hbm, xs_hbm, idx_v, rows_v, sem_r, sem_w):
        wid = lax.axis_index("s") * SC_CORES + lax.axis_index("c")

        def read(c, b):
            src = xn_hbm.at[pl.ds((wid * n_chunks + c) * ch, ch)]
            return pltpu.make_async_copy(src, rows_v.at[b], sem_r.at[b])

        def write(c, b, k):
            return pltpu.make_async_copy(rows_v.at[b], xs_hbm.at[idx_v.at[c, k]], sem_w.at[b])

        pltpu.sync_copy(pos_hbm.at[wid], idx_v)
        read(0, 0).start()

        @pl.loop(0, n_chunks)
        def _(c):
            b = lax.rem(c, 2)
            read(c, b).wait()

            @pl.when(c >= 1)
            def _():
                for k in range(top_k):
                    write(c - 1, 1 - b, k).wait()

            @pl.when(c + 1 < n_chunks)
            def _():
                read(c + 1, 1 - b).start()

            for k in range(top_k):
                write(c, b, k).start()

        for k in range(top_k):
            write(n_chunks - 1, (n_chunks - 1) % 2, k).wait()

    return run(xn, pos_w)


def _collect(ys, pos_w):
    n_workers, n_chunks, top_k, ch = pos_w.shape
    assert n_workers == SC_CORES * SC_SUBCORES and ch % 8 == 0 and ch <= LANES and top_k % 2 == 0
    d = ys.shape[1]
    mesh = plsc.VectorSubcoreMesh(core_axis_name="c", subcore_axis_name="s")

    @functools.partial(
        pl.kernel, mesh=mesh,
        out_type=jax.ShapeDtypeStruct((n_workers * n_chunks * ch, top_k * d), ys.dtype),
        scratch_types=[pltpu.VMEM((n_chunks, top_k, ch), jnp.int32), pltpu.VMEM((2, ch, d), ys.dtype),
                       pltpu.SemaphoreType.DMA((2,)), pltpu.SemaphoreType.DMA((2,))],
        name="collect",
    )
    def run(ys_hbm, pos_hbm, out_hbm, idx_v, rows_v, sem_r, sem_w):
        wid = lax.axis_index("s") * SC_CORES + lax.axis_index("c")

        def read(c, k):
            b = k % 2
            return pltpu.make_async_copy(ys_hbm.at[idx_v.at[c, k]], rows_v.at[b], sem_r.at[b])

        def write(c, k):
            b = k % 2
            dst = out_hbm.at[pl.ds((wid * n_chunks + c) * ch, ch), pl.ds(k * d, d)]
            return pltpu.make_async_copy(rows_v.at[b], dst, sem_w.at[b])

        pltpu.sync_copy(pos_hbm.at[wid], idx_v)
        read(0, 0).start()

        @pl.loop(0, n_chunks)
        def _(c):
            for k in range(top_k):
                read(c, k).wait()
                if k >= 1:
                    write(c, k - 1).wait()
                else:
                    @pl.when(c >= 1)
                    def _():
                        write(c - 1, top_k - 1).wait()
                if k + 1 < top_k:
                    read(c, k + 1).start()
                else:
                    @pl.when(c + 1 < n_chunks)
                    def _():
                        read(c + 1, 0).start()
                write(c, k).start()

        write(n_chunks - 1, top_k - 1).wait()

    return run(ys, pos_w)


def _moe_kernel(tm, te_ref, nu_ref, x_ref, w1_ref, bg_ref, bl_ref, w2_ref, b2_ref,
                ys_ref, w1p, w2b, act):
    i = pl.program_id(0)
    n_used = nu_ref[0]
    d_ff2 = w1_ref.shape[-1]
    n_blk = d_ff2 // MXU_N

    @pl.when(i >= n_used)
    def _():
        ys_ref[...] = jnp.zeros_like(ys_ref)

    prev = te_ref[jnp.maximum(i - 1, 0)]
    changed = jnp.logical_or(i == 0, te_ref[i] != prev)

    @pl.when(jnp.logical_and(changed, i < n_used))
    def _():
        r = lax.broadcasted_iota(jnp.int32, (MXU_N, MXU_N), 0)
        c = lax.broadcasted_iota(jnp.int32, (MXU_N, MXU_N), 1)
        src = jnp.where(c < MXU_N // 2, 2 * c, 2 * (c - MXU_N // 2) + 1)
        perm = jnp.where(r == src, 1.0, 0.0).astype(BF16)
        for blk in range(n_blk):
            cs = slice(blk * MXU_N, (blk + 1) * MXU_N)
            w1p[:, cs] = _dot(w1_ref[:, cs].astype(BF16), perm).astype(BF16)
        w2b[...] = w2_ref[...].astype(BF16)

    @pl.when(i < n_used)
    def _():
        x = _unpack_halves(x_ref[...]).astype(BF16)
        for blk in range(n_blk):
            a = _dot(x, w1p[:, blk * MXU_N:(blk + 1) * MXU_N])
            half = MXU_N // 2
            hs = slice(blk * half, (blk + 1) * half)
            glu = jnp.minimum(a[:, :half] + bg_ref[:, hs], SWIGLU_LIMIT)
            lin = jnp.clip(a[:, half:] + bl_ref[:, hs], -SWIGLU_LIMIT, SWIGLU_LIMIT)
            act[:, hs] = (glu * _sigmoid(SWIGLU_ALPHA * glu) * (lin + 1.0)).astype(BF16)
        ys_ref[...] = _pack_halves(_dot(act[...], w2b[...]) + b2_ref[...])


def _final_kernel(h_ref, gate_ref, nfin_ref, z_ref, yp_ref, ysm_ref):
    i = pl.program_id(0)
    n = pl.num_programs(0)
    w = z_ref.shape[-1] // TOP_K
    g = gate_ref[...]
    out = h_ref[...]
    for k in range(TOP_K):
        out = out + g[:, k:k + 1] * _unpack_halves(z_ref[:, k * w:(k + 1) * w])
    y = _rms(out, nfin_ref[...])

    @pl.when(i < n - 1)
    def _():
        yp_ref[...] = y

    @pl.when(i == n - 1)
    def _():
        ysm_ref[...] = y


def kernel(x_prompt, x_sample, state_conv, state_rec, meta_tokens, norm_mix, w_in, conv_w,
           rec_lower_bound, rec_norm, w_out, norm_ffn, router_w, router_b, expert_w1, expert_b1,
           expert_w2, expert_b2, norm_final):
    depth = norm_mix.shape[0]
    assert depth == 1, "single-layer step"
    layer = 0
    bp, seq, d = x_prompt.shape
    ns = x_sample.shape[0]
    assert x_sample.shape[1] == 1 and ns == TOKEN_TILE
    d_conv = conv_w.shape[-1]
    d_rec = rec_lower_bound.shape[-1]
    assert state_conv.shape[2] == 2 and d_rec == N_HEADS * HEAD
    n_exp = router_w.shape[-1]
    d_ff = expert_w2.shape[2]
    n_prompt = bp * seq
    n_tok = n_prompt + ns
    assert seq % PROMPT_TILE == 0 and n_prompt % TOKEN_TILE == 0 and n_tok % RANK_TILE == 0

    nm = norm_mix[layer][None]
    win = w_in[layer].astype(BF16)
    cw = conv_w[layer]
    rlb = rec_lower_bound
    rg = rec_norm[layer][None]
    wout = w_out[layer].astype(BF16)
    nf = norm_ffn[layer][None]
    rw = router_w[layer]
    rb = router_b[layer][None]
    mix_w = (nm, win, cw, rlb)
    tail_w = (rg, wout, nf, rw, rb)

    st_meta, cv_meta = _mixer_call(
        meta_tokens[None], jnp.zeros((1, N_HEADS, HEAD, HEAD), F32), jnp.zeros((1, 2, d_conv), F32),
        mix_w, layer, N_META, N_META)

    xs = x_sample.reshape(ns, d)
    wide = jax.ShapeDtypeStruct((ns, d_rec), F32)
    y_conv_s, new_conv_s, f_s, k_s, q_s, v_s, og_s = pl.pallas_call(
        functools.partial(_sample_in_kernel, layer),
        out_shape=[jax.ShapeDtypeStruct((ns, d_conv), F32), jax.ShapeDtypeStruct((ns, 2 * d_conv), F32),
                   wide, wide, wide, wide, wide],
        compiler_params=pltpu.CompilerParams(vmem_limit_bytes=VMEM_LIMIT),
        name="sample_in",
    )(xs, state_conv[layer].reshape(ns, 2 * d_conv), nm, win, cw, rlb)

    group = 8
    n_grp = ns // group

    def cols(a):
        return a.T.reshape(d_rec, n_grp, group).transpose(1, 0, 2)

    col_spec = pl.BlockSpec((None, d_rec, group), lambda g: (g, 0, 0))
    st_spec = pl.BlockSpec((group, N_HEADS, HEAD, HEAD), lambda g: (g, 0, 0, 0))
    row_spec = pl.BlockSpec((group, d_rec), lambda g: (g, 0))
    new_rec_s, o_s = pl.pallas_call(
        functools.partial(_sample_state_kernel, group),
        grid=(n_grp,),
        in_specs=[col_spec, col_spec, col_spec, row_spec, st_spec],
        out_specs=[st_spec, row_spec],
        out_shape=[jax.ShapeDtypeStruct(state_rec.shape[1:], F32), wide],
        compiler_params=pltpu.CompilerParams(dimension_semantics=("arbitrary",),
                                             vmem_limit_bytes=VMEM_LIMIT),
        name="sample_state",
    )(cols(f_s), cols(k_s), cols(q_s), v_s, state_rec[layer])

    decode = pl.pallas_call(
        _sample_tail_kernel,
        out_shape=[jax.ShapeDtypeStruct((ns, d), F32), jax.ShapeDtypeStruct((ns, d), F32),
                   jax.ShapeDtypeStruct((ns, n_exp), F32), jax.ShapeDtypeStruct((ns, n_exp), F32)],
        compiler_params=pltpu.CompilerParams(vmem_limit_bytes=VMEM_LIMIT),
        name="sample_tail",
    )(xs, y_conv_s, o_s, og_s, rg, wout, nf, rw, rb)

    h_all, xn_all, sel_all, gm_all, new_rec_p, new_conv_p = _mixer_call(
        x_prompt, st_meta, cv_meta, mix_w, layer, PROMPT_TILE, CHUNK, tail=(tail_w, decode))

    canvas = jax.ShapeDtypeStruct((n_tok, LANES), F32)
    tok_spec = pl.BlockSpec((RANK_TILE, n_exp), lambda i: (i, 0))
    can_spec = pl.BlockSpec((RANK_TILE, LANES), lambda i: (i, 0))
    eid, rnk, gate, counts = pl.pallas_call(
        _rank_kernel,
        grid=(n_tok // RANK_TILE,),
        in_specs=[tok_spec, tok_spec],
        out_specs=[can_spec, can_spec, can_spec, pl.BlockSpec((1, n_exp), lambda i: (0, 0))],
        out_shape=[canvas, canvas, canvas, jax.ShapeDtypeStruct((1, n_exp), F32)],
        scratch_shapes=[pltpu.VMEM((1, n_exp), F32)],
        compiler_params=pltpu.CompilerParams(dimension_semantics=("arbitrary",),
                                             vmem_limit_bytes=VMEM_LIMIT),
        name="rank",
    )(sel_all, gm_all)

    tm = MOE_TILE
    n_tiles = (n_tok * TOP_K) // tm + n_exp
    counts = counts[0].astype(jnp.int32)
    tiles_e = (counts + tm - 1) // tm
    tile_end = jnp.cumsum(tiles_e)
    n_used = tile_end[-1]
    offs = (tile_end - tiles_e) * tm
    eid4 = eid[:, :TOP_K].astype(jnp.int32)
    experts = jnp.arange(n_exp, dtype=jnp.int32)
    off4 = jnp.sum(jnp.where(eid4[..., None] == experts, offs, 0), axis=-1)
    pos = off4 + rnk[:, :TOP_K].astype(jnp.int32)
    tile_ids = jnp.minimum(jnp.arange(n_tiles, dtype=jnp.int32), n_used - 1)
    tile_expert = jnp.minimum(jnp.sum((tile_end[None, :] <= tile_ids[:, None]).astype(jnp.int32), axis=1),
                              n_exp - 1)

    n_rows = xn_all.shape[0]
    n_workers = SC_CORES * SC_SUBCORES
    assert n_rows % (n_workers * DISPATCH_CHUNK) == 0
    spare = jnp.full((n_rows - n_tok, TOP_K), n_tiles * tm, jnp.int32)
    pos_w = jnp.concatenate([pos, spare], axis=0).reshape(n_workers, -1, DISPATCH_CHUNK, TOP_K)
    pos_w = pos_w.transpose(0, 1, 3, 2)
    xs = _dispatch(xn_all, pos_w, n_tiles * tm + 8)

    w1 = expert_w1[layer]
    w2 = expert_w2[layer]
    b1 = expert_b1[layer]
    b1g = b1[:, 0::2][:, None, :]
    b1l = b1[:, 1::2][:, None, :]
    b2 = expert_b2[layer][:, None, :]
    ys = pl.pallas_call(
        functools.partial(_moe_kernel, tm),
        grid_spec=pltpu.PrefetchScalarGridSpec(
            num_scalar_prefetch=2,
            grid=(n_tiles,),
            in_specs=[
                pl.BlockSpec((tm, d // 2), lambda i, te, nu: (jnp.minimum(i, nu[0] - 1), 0)),
                pl.BlockSpec((None, d, 2 * d_ff), lambda i, te, nu: (te[i], 0, 0)),
                pl.BlockSpec((None, 1, d_ff), lambda i, te, nu: (te[i], 0, 0)),
                pl.BlockSpec((None, 1, d_ff), lambda i, te, nu: (te[i], 0, 0)),
                pl.BlockSpec((None, d_ff, d), lambda i, te, nu: (te[i], 0, 0)),
                pl.BlockSpec((None, 1, d), lambda i, te, nu: (te[i], 0, 0)),
            ],
            out_specs=pl.BlockSpec((tm, d // 2), lambda i, te, nu: (i, 0)),
            scratch_shapes=[pltpu.VMEM((d, 2 * d_ff), BF16), pltpu.VMEM((d_ff, d), BF16),
                            pltpu.VMEM((tm, d_ff), BF16)],
        ),
        out_shape=jax.ShapeDtypeStruct((n_tiles * tm + 8, d // 2), jnp.uint32),
        compiler_params=pltpu.CompilerParams(dimension_semantics=("arbitrary",),
                                             vmem_limit_bytes=VMEM_LIMIT),
        name="moe",
    )(tile_expert, n_used[None].astype(jnp.int32), xs, w1, b1g, b1l, w2, b2)

    z = _collect(ys, pos_w)
    tf = TOKEN_TILE
    n_fin = n_tok // tf
    y_p, y_s = pl.pallas_call(
        _final_kernel,
        grid=(n_fin,),
        in_specs=[
            pl.BlockSpec((tf, d), lambda i: (i, 0)),
            pl.BlockSpec((tf, LANES), lambda i: (i, 0)),
            pl.BlockSpec((1, d), lambda i: (0, 0)),
            pl.BlockSpec((tf, TOP_K * d // 2), lambda i: (i, 0)),
        ],
        out_specs=[pl.BlockSpec((tf, d), lambda i: (jnp.minimum(i, n_fin - 2), 0)),
                   pl.BlockSpec((tf, d), lambda i: (0, 0))],
        out_shape=[jax.ShapeDtypeStruct((n_prompt, d), F32), jax.ShapeDtypeStruct((ns, d), F32)],
        compiler_params=pltpu.CompilerParams(dimension_semantics=("arbitrary",),
                                             vmem_limit_bytes=VMEM_LIMIT),
        name="final",
    )(h_all, gate, norm_final[None], z)

    return (y_p.reshape(bp, seq, d), y_s.reshape(ns, 1, d),
            new_conv_p[None], new_rec_p[None],
            new_conv_s.reshape(1, ns, 2, d_conv), new_rec_s[None])
```

```python
import functools

import jax
import jax.numpy as jnp
from jax import lax
from jax.experimental import pallas as pl
from jax.experimental.pallas import tpu as pltpu
from jax.experimental.pallas import tpu_sc as plsc

F32 = jnp.float32
BF16 = jnp.bfloat16

N_HEADS = 4
HEAD = 128
N_META = 16
CHUNK = 64
TOP_K = 4
SWIGLU_LIMIT = 7.0
SWIGLU_ALPHA = 1.702
EPS = 1e-5

LANES = 128
MXU_N = 256
PROMPT_TILE = 256
TOKEN_TILE = 128
FINAL_TILE = 512
RANK_TILE = 384
MOE_TILE = 256
SC_CORES = 2
SC_SUBCORES = 16
DISPATCH_CHUNK = 88
VMEM_LIMIT = 56 * 1024 * 1024


def _dot(a, b):
    return jnp.dot(a, b, preferred_element_type=F32)


def _dot_nt(a, b):
    return lax.dot_general(a, b, (((1,), (1,)), ((), ())), preferred_element_type=F32)


def _dot_tn(a, b):
    return lax.dot_general(a, b, (((0,), (0,)), ((), ())), preferred_element_type=F32)


def _split3(x):
    hi = x.astype(BF16)
    r = x - hi.astype(F32)
    mid = r.astype(BF16)
    lo = (r - mid.astype(F32)).astype(BF16)
    return hi, mid, lo


def _pack_halves(x):
    n = x.shape[-1] // 2
    lo = pltpu.bitcast(x[:, :n].astype(BF16).astype(F32), jnp.uint32)
    hi = pltpu.bitcast(x[:, n:].astype(BF16).astype(F32), jnp.uint32)
    return (lo >> 16) | (hi & jnp.uint32(0xFFFF0000))


def _unpack_halves(u):
    lo = pltpu.bitcast(u << 16, F32)
    hi = pltpu.bitcast(u & jnp.uint32(0xFFFF0000), F32)
    return jnp.concatenate([lo, hi], axis=-1)


def _sigmoid(x):
    return 1.0 / (1.0 + jnp.exp(-x))


def _rms(x, g):
    ms = jnp.mean(x * x, axis=-1, keepdims=True)
    return x * lax.rsqrt(ms + EPS) * g


def _project(x, nm_ref, win_ref):
    return _dot(_rms(x, nm_ref[...]).astype(BF16), win_ref[...])


def _split_u(u, d_conv, d_rec):
    pts = [0, d_conv, 2 * d_conv, 3 * d_conv, 3 * d_conv + d_rec, 3 * d_conv + 2 * d_rec,
           3 * d_conv + 3 * d_rec, 3 * d_conv + 4 * d_rec]
    return [u[:, pts[i]:pts[i + 1]] for i in range(7)]


def _lower_bound(rlb_ref, layer):
    r = rlb_ref[...]
    e = jnp.exp(r - jnp.max(r, axis=0, keepdims=True))
    return jnp.sum(e[0:layer + 1], axis=0, keepdims=True) / jnp.sum(e, axis=0, keepdims=True)


def _forget(fx, lb):
    f = lb + (1.0 - lb) * _sigmoid(fx)
    return f, 1.0 - f


def _rec_out(o, og, rg):
    parts = []
    for h in range(N_HEADS):
        oh = o[:, h * HEAD:(h + 1) * HEAD]
        parts.append(oh * lax.rsqrt(jnp.mean(oh * oh, axis=-1, keepdims=True) + EPS))
    return jnp.concatenate(parts, axis=-1) * rg * (og * _sigmoid(og))


def _route(logits):
    n = logits.shape[-1]
    lane = lax.broadcasted_iota(jnp.int32, logits.shape, 1).astype(F32)
    work = logits
    tops, hots = [], []
    for _ in range(TOP_K):
        m = jnp.max(work, axis=-1, keepdims=True)
        first = jnp.min(jnp.where(work == m, lane, float(n)), axis=-1, keepdims=True)
        hot = lane == first
        tops.append(m)
        hots.append(hot)
        work = jnp.where(hot, -jnp.inf, work)
    es = [jnp.exp(t - tops[0]) for t in tops]
    den = es[0]
    for e in es[1:]:
        den = den + e
    sel = jnp.zeros_like(logits)
    gm = jnp.zeros_like(logits)
    for hot, e in zip(hots, es):
        sel = sel + jnp.where(hot, 1.0, 0.0)
        gm = gm + jnp.where(hot, e / den, 0.0)
    return sel, gm


def _tail(x, y, wout_ref, nf_ref, rw_ref, rb_ref):
    h = x + _dot(y.astype(BF16), wout_ref[...])
    xn = _rms(h, nf_ref[...])
    xh = xn.astype(BF16)
    xl = (xn - xh.astype(F32)).astype(BF16)
    rw = rw_ref[...]
    wh = rw.astype(BF16)
    wl = (rw - wh.astype(F32)).astype(BF16)
    logits = _dot(xh, wh) + _dot(xh, wl) + _dot(xl, wh) + rb_ref[...]
    sel, gm = _route(logits)
    return h, xn, sel, gm


def _rec_chunk(q, kk, v, lf, st_ref, tri, causal):
    c = q.shape[0]
    hi, mid, lo = _split3(lf)
    b = _dot(tri, hi) + _dot(tri, mid) + _dot(tri, lo)
    eb = jnp.exp(b)
    qe = (q * eb).astype(BF16)
    ke = (kk * jnp.exp(-b)).astype(BF16)
    vb = v.astype(BF16)
    eb_last = eb[c - 1:c]
    outs = []
    for h in range(N_HEADS):
        sl = slice(h * HEAD, (h + 1) * HEAD)
        st = st_ref[h]
        sc = jnp.where(causal, _dot_nt(qe[:, sl], ke[:, sl]), 0.0)
        outs.append(_dot(sc.astype(BF16), vb[:, sl]) + _dot_nt(qe[:, sl], st.astype(BF16)))
        st_ref[h] = (st + _dot_tn(vb[:, sl], ke[:, sl])) * eb_last[:, sl]
    return jnp.concatenate(outs, axis=-1)


def _mix_tile(u, layer, chunk, cw_ref, rlb_ref, convbuf, st, cv_out):
    tm = u.shape[0]
    d_conv = cw_ref.shape[-1]
    d_rec = rlb_ref.shape[-1]
    bg, cg, hv, q, fx, iv, og = _split_u(u, d_conv, d_rec)

    bx = bg * hv
    convbuf[8:8 + tm, :] = bx
    cw = cw_ref[...]
    conv = cw[0:1] * convbuf[6:6 + tm, :] + cw[1:2] * convbuf[7:7 + tm, :] + cw[2:3] * bx
    convbuf[6:8, :] = bx[tm - 2:tm]
    cv_out[...] = bx[tm - 2:tm]

    f, kk = _forget(fx, _lower_bound(rlb_ref, layer))
    lf = jnp.log(f)
    row = lax.broadcasted_iota(jnp.int32, (chunk, chunk), 0)
    col = lax.broadcasted_iota(jnp.int32, (chunk, chunk), 1)
    causal = row >= col
    tri = jnp.where(causal, 1.0, 0.0).astype(BF16)
    outs = []
    for c in range(tm // chunk):
        rs = slice(c * chunk, (c + 1) * chunk)
        outs.append(_rec_chunk(q[rs], kk[rs], iv[rs], lf[rs], st, tri, causal))
    return cg * conv, jnp.concatenate(outs, axis=0), og


def _load_state(st, convbuf, s0_ref, c0_ref):
    for h in range(N_HEADS):
        st[h] = s0_ref[h].T
    convbuf[6:8, :] = c0_ref[...]


def _meta_kernel(layer, x_ref, s0_ref, c0_ref, nm_ref, win_ref, cw_ref, rlb_ref, st_out, cv_out,
                 convbuf, st):
    _load_state(st, convbuf, s0_ref, c0_ref)
    _mix_tile(_project(x_ref[...], nm_ref, win_ref), layer, x_ref.shape[0], cw_ref, rlb_ref,
              convbuf, st, cv_out)
    for h in range(N_HEADS):
        st_out[h] = st[h].T


def _prompt_kernel(layer, tm, chunk, n_pairs, pairs_per_seq,
                   xr0_ref, x1_ref, xp2_ref, s0_ref, c0_ref, nm_ref, win_ref, cw_ref, rlb_ref,
                   rg_ref, wout_ref, nf_ref, rw_ref, rb_ref, hs_ref, xns_ref, sels_ref, gms_ref,
                   h_ref, xn_ref, sel_ref, gm_ref, st_out, cv_out, convbuf, st, ua, ub):
    step = pl.program_id(0)
    live = step < n_pairs

    @pl.when(step == 0)
    def _():
        ua[...] = _project(xr0_ref[...], nm_ref, win_ref)

    @pl.when(jnp.logical_and(lax.rem(step, pairs_per_seq) == 0, live))
    def _():
        _load_state(st, convbuf, s0_ref, c0_ref)

    @pl.when(live)
    def _():
        def finish(x, u, rows):
            y_conv, o, og = _mix_tile(u, layer, chunk, cw_ref, rlb_ref, convbuf, st, cv_out)
            y = jnp.concatenate([y_conv, _rec_out(o, og, rg_ref[...])], axis=-1)
            h, xn, sel, gm = _tail(x, y, wout_ref, nf_ref, rw_ref, rb_ref)
            h_ref[rows, :] = h
            xn_ref[rows, :] = _pack_halves(xn)
            sel_ref[rows, :] = sel
            gm_ref[rows, :] = gm

        ub[...] = _project(x1_ref[...], nm_ref, win_ref)
        finish(xr0_ref[...], ua[...], slice(0, tm))
        ua[...] = _project(xp2_ref[...], nm_ref, win_ref)
        finish(x1_ref[...], ub[...], slice(tm, 2 * tm))

    @pl.when(jnp.logical_and(lax.rem(step, pairs_per_seq) == pairs_per_seq - 1, live))
    def _():
        for h in range(N_HEADS):
            st_out[h] = st[h].T

    @pl.when(step == n_pairs)
    def _():
        ns = hs_ref.shape[0]
        for dst, val in ((h_ref, hs_ref[...]), (xn_ref, _pack_halves(xns_ref[...])),
                         (sel_ref, sels_ref[...]), (gm_ref, gms_ref[...])):
            dst[0:ns, :] = val
            dst[ns:2 * tm, :] = jnp.zeros((2 * tm - ns, dst.shape[-1]), dst.dtype)


def _const_spec(shape):
    return pl.BlockSpec(shape, lambda *_: (0,) * len(shape))


def _meta_call(x, weights, layer):
    nm, win, cw, rlb = weights
    d_conv = cw.shape[-1]
    s0 = jnp.zeros((N_HEADS, HEAD, HEAD), F32)
    c0 = jnp.zeros((2, d_conv), F32)
    return pl.pallas_call(
        functools.partial(_meta_kernel, layer),
        out_shape=[jax.ShapeDtypeStruct(s0.shape, F32), jax.ShapeDtypeStruct(c0.shape, F32)],
        scratch_shapes=[pltpu.VMEM((x.shape[0] + 8, d_conv), F32), pltpu.VMEM(s0.shape, F32)],
        compiler_params=pltpu.CompilerParams(vmem_limit_bytes=VMEM_LIMIT),
        name="mixer_meta",
    )(x, s0, c0, nm, win, cw, rlb)


def _prompt_call(x, s0, c0, weights, tail_w, decode, layer, tm, chunk):
    nseq, length, d = x.shape
    nt = length // tm
    assert nt % 2 == 0 and decode[0].shape[0] <= 2 * tm
    pairs_per_seq = nt // 2
    n_pairs = nseq * pairs_per_seq
    nm, win, cw, rlb = weights
    d_conv = cw.shape[-1]
    d_in = win.shape[-1]

    def tile_spec(offset):
        def index(s):
            tile = jnp.minimum(2 * s + offset, 2 * n_pairs - 1)
            return (tile // nt, lax.rem(tile, nt), 0)
        return pl.BlockSpec((None, tm, d), index)

    def seq_of(s):
        return jnp.minimum(s // pairs_per_seq, nseq - 1)

    consts = [s0, c0, nm, win, cw, rlb] + list(tail_w) + list(decode)
    tok = [(a.shape[-1], F32) for a in decode]
    tok[1] = (tok[1][0] // 2, jnp.uint32)
    n_steps = n_pairs + 1
    return pl.pallas_call(
        functools.partial(_prompt_kernel, layer, tm, chunk, n_pairs, pairs_per_seq),
        grid=(n_steps,),
        in_specs=[tile_spec(0), tile_spec(1), tile_spec(2)] + [_const_spec(a.shape) for a in consts],
        out_specs=[pl.BlockSpec((2 * tm, w), lambda s: (s, 0)) for w, _ in tok] + [
            pl.BlockSpec((None,) + s0.shape, lambda s: (seq_of(s), 0, 0, 0)),
            pl.BlockSpec((None,) + c0.shape, lambda s: (seq_of(s), 0, 0))],
        out_shape=[jax.ShapeDtypeStruct((n_steps * 2 * tm, w), t) for w, t in tok] + [
            jax.ShapeDtypeStruct((nseq,) + s0.shape, F32), jax.ShapeDtypeStruct((nseq,) + c0.shape, F32)],
        scratch_shapes=[pltpu.VMEM((tm + 8, d_conv), F32), pltpu.VMEM(s0.shape, F32),
                        pltpu.VMEM((tm, d_in), F32), pltpu.VMEM((tm, d_in), F32)],
        compiler_params=pltpu.CompilerParams(dimension_semantics=("arbitrary",),
                                             vmem_limit_bytes=VMEM_LIMIT),
        name="mixer_prompt",
    )(x, x, x, *consts)


def _sample_in_kernel(layer, x_ref, sc_ref, nm_ref, win_ref, cw_ref, rlb_ref,
                      yc_ref, nc_ref, f_ref, k_ref, q_ref, v_ref, og_ref):
    d_conv = cw_ref.shape[-1]
    d_rec = rlb_ref.shape[-1]
    u = _project(x_ref[...], nm_ref, win_ref)
    bg, cg, hv, q, fx, iv, og = _split_u(u, d_conv, d_rec)
    bx = bg * hv
    sc = sc_ref[...]
    s0, s1 = sc[:, :d_conv], sc[:, d_conv:]
    cw = cw_ref[...]
    yc_ref[...] = cg * (cw[0:1] * s0 + cw[1:2] * s1 + cw[2:3] * bx)
    nc_ref[...] = jnp.concatenate([s1, bx], axis=-1)
    f, kk = _forget(fx, _lower_bound(rlb_ref, layer))
    f_ref[...] = f
    k_ref[...] = kk
    q_ref[...] = q
    v_ref[...] = iv
    og_ref[...] = og


def _sample_state_kernel(group, f_ref, k_ref, q_ref, v_ref, s_ref, sn_ref, o_ref):
    for j in range(group):
        for h in range(N_HEADS):
            rs = slice(h * HEAD, (h + 1) * HEAD)
            fcol = f_ref[rs, j:j + 1]
            kcol = k_ref[rs, j:j + 1]
            qcol = q_ref[rs, j:j + 1]
            vrow = v_ref[j:j + 1, rs]
            sn = fcol * s_ref[j, h] + kcol * vrow
            sn_ref[j, h] = sn
            o_ref[j:j + 1, rs] = jnp.sum(qcol * sn, axis=0, keepdims=True)


def _sample_tail_kernel(x_ref, yc_ref, o_ref, og_ref, rg_ref, wout_ref, nf_ref, rw_ref, rb_ref,
                        h_ref, xn_ref, sel_ref, gm_ref):
    y = jnp.concatenate([yc_ref[...], _rec_out(o_ref[...], og_ref[...], rg_ref[...])], axis=-1)
    h, xn, sel, gm = _tail(x_ref[...], y, wout_ref, nf_ref, rw_ref, rb_ref)
    h_ref[...] = h
    xn_ref[...] = xn
    sel_ref[...] = sel
    gm_ref[...] = gm


def _rank_kernel(sel_ref, gm_ref, eid_ref, rank_ref, gate_ref, cnt_ref, carry):
    i = pl.program_id(0)

    @pl.when(i == 0)
    def _():
        carry[...] = jnp.zeros_like(carry)

    sel = sel_ref[...]
    gm = gm_ref[...]
    tb, ne = sel.shape
    row = lax.broadcasted_iota(jnp.int32, (tb, tb), 0)
    col = lax.broadcasted_iota(jnp.int32, (tb, tb), 1)
    before = jnp.where(col < row, 1.0, 0.0).astype(BF16)
    selb = sel.astype(BF16)
    rank = _dot(before, selb) + carry[...]
    carry[...] = carry[...] + jnp.sum(sel, axis=0, keepdims=True)
    cnt_ref[...] = carry[...]
    er = lax.broadcasted_iota(jnp.int32, (ne, ne), 0)
    ec = lax.broadcasted_iota(jnp.int32, (ne, ne), 1)
    lower = jnp.where(er < ec, 1.0, 0.0).astype(BF16)
    order = _dot(selb, lower)
    lane_e = lax.broadcasted_iota(jnp.int32, (tb, ne), 1).astype(F32)
    lane = lax.broadcasted_iota(jnp.int32, (tb, LANES), 1)
    eid = jnp.zeros((tb, LANES), F32)
    rnk = jnp.zeros((tb, LANES), F32)
    gat = jnp.zeros((tb, LANES), F32)
    for k in range(TOP_K):
        pick = jnp.where(order == float(k), sel, 0.0)
        eid = jnp.where(lane == k, jnp.sum(pick * lane_e, axis=-1, keepdims=True), eid)
        rnk = jnp.where(lane == k, jnp.sum(pick * rank, axis=-1, keepdims=True), rnk)
        gat = jnp.where(lane == k, jnp.sum(pick * gm, axis=-1, keepdims=True), gat)
    eid_ref[...] = eid
    rank_ref[...] = rnk
    gate_ref[...] = gat


def _dispatch(xn, pos_w, n_slots):
    n_workers, n_chunks, top_k, ch = pos_w.shape
    assert n_workers == SC_CORES * SC_SUBCORES and ch % 8 == 0 and ch <= LANES
    d = xn.shape[1]
    mesh = plsc.VectorSubcoreMesh(core_axis_name="c", subcore_axis_name="s")

    @functools.partial(
        pl.kernel, mesh=mesh,
        out_type=jax.ShapeDtypeStruct((n_slots, d), xn.dtype),
        scratch_types=[pltpu.VMEM((n_chunks, top_k, ch), jnp.int32), pltpu.VMEM((2, ch, d), xn.dtype),
                       pltpu.SemaphoreType.DMA((2,)), pltpu.SemaphoreType.DMA((2,))],
        name="dispatch",
    )
    def run(xn_hbm, pos_hbm, xs_hbm, idx_v, rows_v, sem_r, sem_w):
        wid = lax.axis_index("s") * SC_CORES + lax.axis_index("c")

        def read(c, b):
            src = xn_hbm.at[pl.ds((wid * n_chunks + c) * ch, ch)]
            return pltpu.make_async_copy(src, rows_v.at[b], sem_r.at[b])

        def write(c, b, k):
            return pltpu.make_async_copy(rows_v.at[b], xs_hbm.at[idx_v.at[c, k]], sem_w.at[b])

        pltpu.sync_copy(pos_hbm.at[wid], idx_v)
        read(0, 0).start()

        @pl.loop(0, n_chunks)
        def _(c):
            b = lax.rem(c, 2)
            read(c, b).wait()

            @pl.when(c >= 1)
            def _():
                for k in range(top_k):
                    write(c - 1, 1 - b, k).wait()

            @pl.when(c + 1 < n_chunks)
            def _():
                read(c + 1, 1 - b).start()

            for k in range(top_k):
                write(c, b, k).start()

        for k in range(top_k):
            write(n_chunks - 1, (n_chunks - 1) % 2, k).wait()

    return run(xn, pos_w)


def _collect(ys, pos_w):
    n_workers, n_chunks, top_k, ch = pos_w.shape
    assert n_workers == SC_CORES * SC_SUBCORES and ch % 8 == 0 and ch <= LANES and top_k % 2 == 0
    d = ys.shape[1]
    mesh = plsc.VectorSubcoreMesh(core_axis_name="c", subcore_axis_name="s")

    @functools.partial(
        pl.kernel, mesh=mesh,
        out_type=jax.ShapeDtypeStruct((n_workers * n_chunks * ch, top_k * d), ys.dtype),
        scratch_types=[pltpu.VMEM((n_chunks, top_k, ch), jnp.int32), pltpu.VMEM((2, ch, d), ys.dtype),
                       pltpu.SemaphoreType.DMA((2,)), pltpu.SemaphoreType.DMA((2,))],
        name="collect",
    )
    def run(ys_hbm, pos_hbm, out_hbm, idx_v, rows_v, sem_r, sem_w):
        wid = lax.axis_index("s") * SC_CORES + lax.axis_index("c")

        def read(c, k):
            b = k % 2
            return pltpu.make_async_copy(ys_hbm.at[idx_v.at[c, k]], rows_v.at[b], sem_r.at[b])

        def write(c, k):
            b = k % 2
            dst = out_hbm.at[pl.ds((wid * n_chunks + c) * ch, ch), pl.ds(k * d, d)]
            return pltpu.make_async_copy(rows_v.at[b], dst, sem_w.at[b])

        pltpu.sync_copy(pos_hbm.at[wid], idx_v)
        read(0, 0).start()

        @pl.loop(0, n_chunks)
        def _(c):
            for k in range(top_k):
                read(c, k).wait()
                if k >= 1:
                    write(c, k - 1).wait()
                else:
                    @pl.when(c >= 1)
                    def _():
                        write(c - 1, top_k - 1).wait()
                if k + 1 < top_k:
                    read(c, k + 1).start()
                else:
                    @pl.when(c + 1 < n_chunks)
                    def _():
                        read(c + 1, 0).start()
                write(c, k).start()

        write(n_chunks - 1, top_k - 1).wait()

    return run(ys, pos_w)


def _moe_kernel(tm, te_ref, nu_ref, x_ref, w1_ref, bg_ref, bl_ref, w2_ref, b2_ref,
                ys_ref, w1p, w2b, act):
    i = pl.program_id(0)
    n_used = nu_ref[0]
    d_ff2 = w1_ref.shape[-1]
    n_blk = d_ff2 // MXU_N

    @pl.when(i >= n_used)
    def _():
        ys_ref[...] = jnp.zeros_like(ys_ref)

    prev = te_ref[jnp.maximum(i - 1, 0)]
    changed = jnp.logical_or(i == 0, te_ref[i] != prev)

    @pl.when(jnp.logical_and(changed, i < n_used))
    def _():
        r = lax.broadcasted_iota(jnp.int32, (MXU_N, MXU_N), 0)
        c = lax.broadcasted_iota(jnp.int32, (MXU_N, MXU_N), 1)
        src = jnp.where(c < MXU_N // 2, 2 * c, 2 * (c - MXU_N // 2) + 1)
        perm = jnp.where(r == src, 1.0, 0.0).astype(BF16)
        for blk in range(n_blk):
            cs = slice(blk * MXU_N, (blk + 1) * MXU_N)
            w1p[:, cs] = _dot(w1_ref[:, cs].astype(BF16), perm).astype(BF16)
        w2b[...] = w2_ref[...].astype(BF16)

    @pl.when(i < n_used)
    def _():
        x = _unpack_halves(x_ref[...]).astype(BF16)
        for blk in range(n_blk):
            a = _dot(x, w1p[:, blk * MXU_N:(blk + 1) * MXU_N])
            half = MXU_N // 2
            hs = slice(blk * half, (blk + 1) * half)
            glu = jnp.minimum(a[:, :half] + bg_ref[:, hs], SWIGLU_LIMIT)
            lin = jnp.clip(a[:, half:] + bl_ref[:, hs], -SWIGLU_LIMIT, SWIGLU_LIMIT)
            act[:, hs] = (glu * _sigmoid(SWIGLU_ALPHA * glu) * (lin + 1.0)).astype(BF16)
        ys_ref[...] = _pack_halves(_dot(act[...], w2b[...]) + b2_ref[...])


def _final_kernel(h_ref, gate_ref, nfin_ref, z_ref, y_ref):
    w = z_ref.shape[-1] // TOP_K
    g = gate_ref[...]
    out = h_ref[...]
    for k in range(TOP_K):
        out = out + g[:, k:k + 1] * _unpack_halves(z_ref[:, k * w:(k + 1) * w])
    y_ref[...] = _rms(out, nfin_ref[...])


def _final_call(h_all, gate, nfin, z, first_row, n_rows, tile):
    d = h_all.shape[-1]
    off = first_row // tile
    assert first_row % tile == 0 and n_rows % tile == 0
    return pl.pallas_call(
        _final_kernel,
        grid=(n_rows // tile,),
        in_specs=[
            pl.BlockSpec((tile, d), lambda i: (i + off, 0)),
            pl.BlockSpec((tile, gate.shape[-1]), lambda i: (i + off, 0)),
            pl.BlockSpec((1, d), lambda i: (0, 0)),
            pl.BlockSpec((tile, z.shape[-1]), lambda i: (i + off, 0)),
        ],
        out_specs=pl.BlockSpec((tile, d), lambda i: (i, 0)),
        out_shape=jax.ShapeDtypeStruct((n_rows, d), F32),
        compiler_params=pltpu.CompilerParams(dimension_semantics=("arbitrary",),
                                             vmem_limit_bytes=VMEM_LIMIT),
        name="final",
    )(h_all, gate, nfin, z)


def kernel(x_prompt, x_sample, state_conv, state_rec, meta_tokens, norm_mix, w_in, conv_w,
           rec_lower_bound, rec_norm, w_out, norm_ffn, router_w, router_b, expert_w1, expert_b1,
           expert_w2, expert_b2, norm_final):
    depth = norm_mix.shape[0]
    assert depth == 1, "single-layer step"
    layer = 0
    bp, seq, d = x_prompt.shape
    ns = x_sample.shape[0]
    assert x_sample.shape[1] == 1 and ns == TOKEN_TILE
    d_conv = conv_w.shape[-1]
    d_rec = rec_lower_bound.shape[-1]
    assert state_conv.shape[2] == 2 and d_rec == N_HEADS * HEAD
    n_exp = router_w.shape[-1]
    d_ff = expert_w2.shape[2]
    n_prompt = bp * seq
    n_tok = n_prompt + ns
    assert seq % PROMPT_TILE == 0 and n_prompt % TOKEN_TILE == 0 and n_tok % RANK_TILE == 0

    nm = norm_mix[layer][None]
    win = w_in[layer].astype(BF16)
    cw = conv_w[layer]
    rlb = rec_lower_bound
    rg = rec_norm[layer][None]
    wout = w_out[layer].astype(BF16)
    nf = norm_ffn[layer][None]
    rw = router_w[layer]
    rb = router_b[layer][None]
    mix_w = (nm, win, cw, rlb)
    tail_w = (rg, wout, nf, rw, rb)

    st_meta, cv_meta = _meta_call(meta_tokens, mix_w, layer)

    xs = x_sample.reshape(ns, d)
    wide = jax.ShapeDtypeStruct((ns, d_rec), F32)
    y_conv_s, new_conv_s, f_s, k_s, q_s, v_s, og_s = pl.pallas_call(
        functools.partial(_sample_in_kernel, layer),
        out_shape=[jax.ShapeDtypeStruct((ns, d_conv), F32), jax.ShapeDtypeStruct((ns, 2 * d_conv), F32),
                   wide, wide, wide, wide, wide],
        compiler_params=pltpu.CompilerParams(vmem_limit_bytes=VMEM_LIMIT),
        name="sample_in",
    )(xs, state_conv[layer].reshape(ns, 2 * d_conv), nm, win, cw, rlb)

    group = 8
    n_grp = ns // group

    def cols(a):
        return a.T.reshape(d_rec, n_grp, group).transpose(1, 0, 2)

    col_spec = pl.BlockSpec((None, d_rec, group), lambda g: (g, 0, 0))
    st_spec = pl.BlockSpec((group, N_HEADS, HEAD, HEAD), lambda g: (g, 0, 0, 0))
    row_spec = pl.BlockSpec((group, d_rec), lambda g: (g, 0))
    new_rec_s, o_s = pl.pallas_call(
        functools.partial(_sample_state_kernel, group),
        grid=(n_grp,),
        in_specs=[col_spec, col_spec, col_spec, row_spec, st_spec],
        out_specs=[st_spec, row_spec],
        out_shape=[jax.ShapeDtypeStruct(state_rec.shape[1:], F32), wide],
        compiler_params=pltpu.CompilerParams(dimension_semantics=("arbitrary",),
                                             vmem_limit_bytes=VMEM_LIMIT),
        name="sample_state",
    )(cols(f_s), cols(k_s), cols(q_s), v_s, state_rec[layer])

    decode = pl.pallas_call(
        _sample_tail_kernel,
        out_shape=[jax.ShapeDtypeStruct((ns, d), F32), jax.ShapeDtypeStruct((ns, d), F32),
                   jax.ShapeDtypeStruct((ns, n_exp), F32), jax.ShapeDtypeStruct((ns, n_exp), F32)],
        compiler_params=pltpu.CompilerParams(vmem_limit_bytes=VMEM_LIMIT),
        name="sample_tail",
    )(xs, y_conv_s, o_s, og_s, rg, wout, nf, rw, rb)

    h_all, xn_all, sel_all, gm_all, new_rec_p, new_conv_p = _prompt_call(
        x_prompt, st_meta, cv_meta, mix_w, tail_w, decode, layer, PROMPT_TILE, CHUNK)

    canvas = jax.ShapeDtypeStruct((n_tok, LANES), F32)
    tok_spec = pl.BlockSpec((RANK_TILE, n_exp), lambda i: (i, 0))
    can_spec = pl.BlockSpec((RANK_TILE, LANES), lambda i: (i, 0))
    eid, rnk, gate, counts = pl.pallas_call(
        _rank_kernel,
        grid=(n_tok // RANK_TILE,),
        in_specs=[tok_spec, tok_spec],
        out_specs=[can_spec, can_spec, can_spec, pl.BlockSpec((1, n_exp), lambda i: (0, 0))],
        out_shape=[canvas, canvas, canvas, jax.ShapeDtypeStruct((1, n_exp), F32)],
        scratch_shapes=[pltpu.VMEM((1, n_exp), F32)],
        compiler_params=pltpu.CompilerParams(dimension_semantics=("arbitrary",),
                                             vmem_limit_bytes=VMEM_LIMIT),
        name="rank",
    )(sel_all, gm_all)

    tm = MOE_TILE
    n_tiles = (n_tok * TOP_K) // tm + n_exp
    counts = counts[0].astype(jnp.int32)
    tiles_e = (counts + tm - 1) // tm
    tile_end = jnp.cumsum(tiles_e)
    n_used = tile_end[-1]
    offs = (tile_end - tiles_e) * tm
    eid4 = eid[:, :TOP_K].astype(jnp.int32)
    experts = jnp.arange(n_exp, dtype=jnp.int32)
    off4 = jnp.sum(jnp.where(eid4[..., None] == experts, offs, 0), axis=-1)
    pos = off4 + rnk[:, :TOP_K].astype(jnp.int32)
    tile_ids = jnp.minimum(jnp.arange(n_tiles, dtype=jnp.int32), n_used - 1)
    tile_expert = jnp.minimum(jnp.sum((tile_end[None, :] <= tile_ids[:, None]).astype(jnp.int32), axis=1),
                              n_exp - 1)

    n_rows = xn_all.shape[0]
    n_workers = SC_CORES * SC_SUBCORES
    assert n_rows % (n_workers * DISPATCH_CHUNK) == 0
    spare = jnp.full((n_rows - n_tok, TOP_K), n_tiles * tm, jnp.int32)
    pos_w = jnp.concatenate([pos, spare], axis=0).reshape(n_workers, -1, DISPATCH_CHUNK, TOP_K)
    pos_w = pos_w.transpose(0, 1, 3, 2)
    xs = _dispatch(xn_all, pos_w, n_tiles * tm + 8)

    w1 = expert_w1[layer]
    w2 = expert_w2[layer]
    b1 = expert_b1[layer]
    b1g = b1[:, 0::2][:, None, :]
    b1l = b1[:, 1::2][:, None, :]
    b2 = expert_b2[layer][:, None, :]
    ys = pl.pallas_call(
        functools.partial(_moe_kernel, tm),
        grid_spec=pltpu.PrefetchScalarGridSpec(
            num_scalar_prefetch=2,
            grid=(n_tiles,),
            in_specs=[
                pl.BlockSpec((tm, d // 2), lambda i, te, nu: (jnp.minimum(i, nu[0] - 1), 0)),
                pl.BlockSpec((None, d, 2 * d_ff), lambda i, te, nu: (te[i], 0, 0)),
                pl.BlockSpec((None, 1, d_ff), lambda i, te, nu: (te[i], 0, 0)),
                pl.BlockSpec((None, 1, d_ff), lambda i, te, nu: (te[i], 0, 0)),
                pl.BlockSpec((None, d_ff, d), lambda i, te, nu: (te[i], 0, 0)),
                pl.BlockSpec((None, 1, d), lambda i, te, nu: (te[i], 0, 0)),
            ],
            out_specs=pl.BlockSpec((tm, d // 2), lambda i, te, nu: (i, 0)),
            scratch_shapes=[pltpu.VMEM((d, 2 * d_ff), BF16), pltpu.VMEM((d_ff, d), BF16),
                            pltpu.VMEM((tm, d_ff), BF16)],
        ),
        out_shape=jax.ShapeDtypeStruct((n_tiles * tm + 8, d // 2), jnp.uint32),
        compiler_params=pltpu.CompilerParams(dimension_semantics=("arbitrary",),
                                             vmem_limit_bytes=VMEM_LIMIT),
        name="moe",
    )(tile_expert, n_used[None].astype(jnp.int32), xs, w1, b1g, b1l, w2, b2)

    z = _collect(ys, pos_w)
    y_p = _final_call(h_all, gate, norm_final[None], z, 0, n_prompt, FINAL_TILE)
    y_s = _final_call(h_all, gate, norm_final[None], z, n_prompt, ns, TOKEN_TILE)

    return (y_p.reshape(bp, seq, d), y_s.reshape(ns, 1, d),
            new_conv_p[None], new_rec_p[None],
            new_conv_s.reshape(1, ns, 2, d_conv), new_rec_s[None])
```

```python
import functools

import jax
import jax.numpy as jnp
from jax import lax
from jax.experimental import pallas as pl
from jax.experimental.pallas import tpu as pltpu
from jax.experimental.pallas import tpu_sc as plsc

F32 = jnp.float32
BF16 = jnp.bfloat16

N_HEADS = 4
HEAD = 128
N_META = 16
CHUNK = 64
TOP_K = 4
SWIGLU_LIMIT = 7.0
SWIGLU_ALPHA = 1.702
EPS = 1e-5

LANES = 128
MXU_N = 256
PROMPT_TILE = 256
TOKEN_TILE = 128
FINAL_TILE = 512
RANK_TILE = 384
MOE_TILE = 256
SC_CORES = 2
SC_SUBCORES = 16
DISPATCH_CHUNK = 24
SC_RING = 4
VMEM_LIMIT = 56 * 1024 * 1024


def _dot(a, b):
    return jnp.dot(a, b, preferred_element_type=F32)


def _dot_nt(a, b):
    return lax.dot_general(a, b, (((1,), (1,)), ((), ())), preferred_element_type=F32)


def _dot_tn(a, b):
    return lax.dot_general(a, b, (((0,), (0,)), ((), ())), preferred_element_type=F32)


def _split3(x):
    hi = x.astype(BF16)
    r = x - hi.astype(F32)
    mid = r.astype(BF16)
    lo = (r - mid.astype(F32)).astype(BF16)
    return hi, mid, lo


def _pack_halves(x):
    n = x.shape[-1] // 2
    lo = pltpu.bitcast(x[:, :n].astype(BF16).astype(F32), jnp.uint32)
    hi = pltpu.bitcast(x[:, n:].astype(BF16).astype(F32), jnp.uint32)
    return (lo >> 16) | (hi & jnp.uint32(0xFFFF0000))


def _unpack_halves(u):
    lo = pltpu.bitcast(u << 16, F32)
    hi = pltpu.bitcast(u & jnp.uint32(0xFFFF0000), F32)
    return jnp.concatenate([lo, hi], axis=-1)


def _sigmoid(x):
    return 1.0 / (1.0 + jnp.exp(-x))


def _rms(x, g):
    ms = jnp.mean(x * x, axis=-1, keepdims=True)
    return x * lax.rsqrt(ms + EPS) * g


def _project(x, nm_ref, win_ref):
    return _dot(_rms(x, nm_ref[...]).astype(BF16), win_ref[...])


def _split_u(u, d_conv, d_rec):
    pts = [0, d_conv, 2 * d_conv, 3 * d_conv, 3 * d_conv + d_rec, 3 * d_conv + 2 * d_rec,
           3 * d_conv + 3 * d_rec, 3 * d_conv + 4 * d_rec]
    return [u[:, pts[i]:pts[i + 1]] for i in range(7)]


def _lower_bound(rlb_ref, layer):
    r = rlb_ref[...]
    e = jnp.exp(r - jnp.max(r, axis=0, keepdims=True))
    return jnp.sum(e[0:layer + 1], axis=0, keepdims=True) / jnp.sum(e, axis=0, keepdims=True)


def _forget(fx, lb):
    f = lb + (1.0 - lb) * _sigmoid(fx)
    return f, 1.0 - f


def _rec_out(o, og, rg):
    parts = []
    for h in range(N_HEADS):
        oh = o[:, h * HEAD:(h + 1) * HEAD]
        parts.append(oh * lax.rsqrt(jnp.mean(oh * oh, axis=-1, keepdims=True) + EPS))
    return jnp.concatenate(parts, axis=-1) * rg * (og * _sigmoid(og))


def _route(logits):
    n = logits.shape[-1]
    lane = lax.broadcasted_iota(jnp.int32, logits.shape, 1).astype(F32)
    work = logits
    tops, hots = [], []
    for _ in range(TOP_K):
        m = jnp.max(work, axis=-1, keepdims=True)
        first = jnp.min(jnp.where(work == m, lane, float(n)), axis=-1, keepdims=True)
        hot = lane == first
        tops.append(m)
        hots.append(hot)
        work = jnp.where(hot, -jnp.inf, work)
    es = [jnp.exp(t - tops[0]) for t in tops]
    den = es[0]
    for e in es[1:]:
        den = den + e
    sel = jnp.zeros_like(logits)
    gm = jnp.zeros_like(logits)
    for hot, e in zip(hots, es):
        sel = sel + jnp.where(hot, 1.0, 0.0)
        gm = gm + jnp.where(hot, e / den, 0.0)
    return sel, gm


def _tail(x, y, wout_ref, nf_ref, rw_ref, rb_ref):
    h = x + _dot(y.astype(BF16), wout_ref[...])
    xn = _rms(h, nf_ref[...])
    xh = xn.astype(BF16)
    xl = (xn - xh.astype(F32)).astype(BF16)
    rw = rw_ref[...]
    wh = rw.astype(BF16)
    wl = (rw - wh.astype(F32)).astype(BF16)
    logits = _dot(xh, wh) + _dot(xh, wl) + _dot(xl, wh) + rb_ref[...]
    sel, gm = _route(logits)
    return h, xn, sel, gm


def _rec_chunk(q, kk, v, lf, st_ref, tri, causal):
    c = q.shape[0]
    hi, mid, lo = _split3(lf)
    b = _dot(tri, hi) + _dot(tri, mid) + _dot(tri, lo)
    eb = jnp.exp(b)
    qe = (q * eb).astype(BF16)
    ke = (kk * jnp.exp(-b)).astype(BF16)
    vb = v.astype(BF16)
    eb_last = eb[c - 1:c]
    outs = []
    for h in range(N_HEADS):
        sl = slice(h * HEAD, (h + 1) * HEAD)
        st = st_ref[h]
        sc = jnp.where(causal, _dot_nt(qe[:, sl], ke[:, sl]), 0.0)
        outs.append(_dot(sc.astype(BF16), vb[:, sl]) + _dot_nt(qe[:, sl], st.astype(BF16)))
        st_ref[h] = (st + _dot_tn(vb[:, sl], ke[:, sl])) * eb_last[:, sl]
    return jnp.concatenate(outs, axis=-1)


def _mix_tile(u, layer, chunk, cw_ref, rlb_ref, convbuf, st, cv_out):
    tm = u.shape[0]
    d_conv = cw_ref.shape[-1]
    d_rec = rlb_ref.shape[-1]
    bg, cg, hv, q, fx, iv, og = _split_u(u, d_conv, d_rec)

    bx = bg * hv
    convbuf[8:8 + tm, :] = bx
    cw = cw_ref[...]
    conv = cw[0:1] * convbuf[6:6 + tm, :] + cw[1:2] * convbuf[7:7 + tm, :] + cw[2:3] * bx
    convbuf[6:8, :] = bx[tm - 2:tm]
    cv_out[...] = bx[tm - 2:tm]

    f, kk = _forget(fx, _lower_bound(rlb_ref, layer))
    lf = jnp.log(f)
    row = lax.broadcasted_iota(jnp.int32, (chunk, chunk), 0)
    col = lax.broadcasted_iota(jnp.int32, (chunk, chunk), 1)
    causal = row >= col
    tri = jnp.where(causal, 1.0, 0.0).astype(BF16)
    outs = []
    for c in range(tm // chunk):
        rs = slice(c * chunk, (c + 1) * chunk)
        outs.append(_rec_chunk(q[rs], kk[rs], iv[rs], lf[rs], st, tri, causal))
    return cg * conv, jnp.concatenate(outs, axis=0), og


def _load_state(st, convbuf, s0_ref, c0_ref):
    for h in range(N_HEADS):
        st[h] = s0_ref[h].T
    convbuf[6:8, :] = c0_ref[...]


def _meta_kernel(layer, x_ref, s0_ref, c0_ref, nm_ref, win_ref, cw_ref, rlb_ref, st_out, cv_out,
                 convbuf, st):
    _load_state(st, convbuf, s0_ref, c0_ref)
    _mix_tile(_project(x_ref[...], nm_ref, win_ref), layer, x_ref.shape[0], cw_ref, rlb_ref,
              convbuf, st, cv_out)
    for h in range(N_HEADS):
        st_out[h] = st[h].T


def _prompt_kernel(layer, tm, chunk, n_pairs, pairs_per_seq,
                   xr0_ref, x1_ref, xp2_ref, s0_ref, c0_ref, nm_ref, win_ref, cw_ref, rlb_ref,
                   rg_ref, wout_ref, nf_ref, rw_ref, rb_ref, hs_ref, xns_ref, sels_ref, gms_ref,
                   h_ref, xn_ref, sel_ref, gm_ref, st_out, cv_out, convbuf, st, ua, ub):
    step = pl.program_id(0)
    live = step < n_pairs

    @pl.when(step == 0)
    def _():
        ua[...] = _project(xr0_ref[...], nm_ref, win_ref)

    @pl.when(jnp.logical_and(lax.rem(step, pairs_per_seq) == 0, live))
    def _():
        _load_state(st, convbuf, s0_ref, c0_ref)

    @pl.when(live)
    def _():
        def finish(x, u, rows):
            y_conv, o, og = _mix_tile(u, layer, chunk, cw_ref, rlb_ref, convbuf, st, cv_out)
            y = jnp.concatenate([y_conv, _rec_out(o, og, rg_ref[...])], axis=-1)
            h, xn, sel, gm = _tail(x, y, wout_ref, nf_ref, rw_ref, rb_ref)
            h_ref[rows, :] = h
            xn_ref[rows, :] = _pack_halves(xn)
            sel_ref[rows, :] = sel
            gm_ref[rows, :] = gm

        ub[...] = _project(x1_ref[...], nm_ref, win_ref)
        finish(xr0_ref[...], ua[...], slice(0, tm))
        ua[...] = _project(xp2_ref[...], nm_ref, win_ref)
        finish(x1_ref[...], ub[...], slice(tm, 2 * tm))

    @pl.when(jnp.logical_and(lax.rem(step, pairs_per_seq) == pairs_per_seq - 1, live))
    def _():
        for h in range(N_HEADS):
            st_out[h] = st[h].T

    @pl.when(step == n_pairs)
    def _():
        ns = hs_ref.shape[0]
        for dst, val in ((h_ref, hs_ref[...]), (xn_ref, _pack_halves(xns_ref[...])),
                         (sel_ref, sels_ref[...]), (gm_ref, gms_ref[...])):
            dst[0:ns, :] = val
            dst[ns:2 * tm, :] = jnp.zeros((2 * tm - ns, dst.shape[-1]), dst.dtype)


def _const_spec(shape):
    return pl.BlockSpec(shape, lambda *_: (0,) * len(shape))


def _meta_call(x, weights, layer):
    nm, win, cw, rlb = weights
    d_conv = cw.shape[-1]
    s0 = jnp.zeros((N_HEADS, HEAD, HEAD), F32)
    c0 = jnp.zeros((2, d_conv), F32)
    return pl.pallas_call(
        functools.partial(_meta_kernel, layer),
        out_shape=[jax.ShapeDtypeStruct(s0.shape, F32), jax.ShapeDtypeStruct(c0.shape, F32)],
        scratch_shapes=[pltpu.VMEM((x.shape[0] + 8, d_conv), F32), pltpu.VMEM(s0.shape, F32)],
        compiler_params=pltpu.CompilerParams(vmem_limit_bytes=VMEM_LIMIT),
        name="mixer_meta",
    )(x, s0, c0, nm, win, cw, rlb)


def _prompt_call(x, s0, c0, weights, tail_w, decode, layer, tm, chunk):
    nseq, length, d = x.shape
    nt = length // tm
    assert nt % 2 == 0 and decode[0].shape[0] <= 2 * tm
    pairs_per_seq = nt // 2
    n_pairs = nseq * pairs_per_seq
    nm, win, cw, rlb = weights
    d_conv = cw.shape[-1]
    d_in = win.shape[-1]

    def tile_spec(offset):
        def index(s):
            tile = jnp.minimum(2 * s + offset, 2 * n_pairs - 1)
            return (tile // nt, lax.rem(tile, nt), 0)
        return pl.BlockSpec((None, tm, d), index)

    def seq_of(s):
        return jnp.minimum(s // pairs_per_seq, nseq - 1)

    consts = [s0, c0, nm, win, cw, rlb] + list(tail_w) + list(decode)
    tok = [(a.shape[-1], F32) for a in decode]
    tok[1] = (tok[1][0] // 2, jnp.uint32)
    n_steps = n_pairs + 1
    return pl.pallas_call(
        functools.partial(_prompt_kernel, layer, tm, chunk, n_pairs, pairs_per_seq),
        grid=(n_steps,),
        in_specs=[tile_spec(0), tile_spec(1), tile_spec(2)] + [_const_spec(a.shape) for a in consts],
        out_specs=[pl.BlockSpec((2 * tm, w), lambda s: (s, 0)) for w, _ in tok] + [
            pl.BlockSpec((None,) + s0.shape, lambda s: (seq_of(s), 0, 0, 0)),
            pl.BlockSpec((None,) + c0.shape, lambda s: (seq_of(s), 0, 0))],
        out_shape=[jax.ShapeDtypeStruct((n_steps * 2 * tm, w), t) for w, t in tok] + [
            jax.ShapeDtypeStruct((nseq,) + s0.shape, F32), jax.ShapeDtypeStruct((nseq,) + c0.shape, F32)],
        scratch_shapes=[pltpu.VMEM((tm + 8, d_conv), F32), pltpu.VMEM(s0.shape, F32),
                        pltpu.VMEM((tm, d_in), F32), pltpu.VMEM((tm, d_in), F32)],
        compiler_params=pltpu.CompilerParams(dimension_semantics=("arbitrary",),
                                             vmem_limit_bytes=VMEM_LIMIT),
        name="mixer_prompt",
    )(x, x, x, *consts)


def _sample_in_kernel(layer, x_ref, sc_ref, nm_ref, win_ref, cw_ref, rlb_ref,
                      yc_ref, nc_ref, f_ref, k_ref, q_ref, v_ref, og_ref):
    d_conv = cw_ref.shape[-1]
    d_rec = rlb_ref.shape[-1]
    u = _project(x_ref[...], nm_ref, win_ref)
    bg, cg, hv, q, fx, iv, og = _split_u(u, d_conv, d_rec)
    bx = bg * hv
    sc = sc_ref[...]
    s0, s1 = sc[:, :d_conv], sc[:, d_conv:]
    cw = cw_ref[...]
    yc_ref[...] = cg * (cw[0:1] * s0 + cw[1:2] * s1 + cw[2:3] * bx)
    nc_ref[...] = jnp.concatenate([s1, bx], axis=-1)
    f, kk = _forget(fx, _lower_bound(rlb_ref, layer))
    f_ref[...] = f
    k_ref[...] = kk
    q_ref[...] = q
    v_ref[...] = iv
    og_ref[...] = og


def _sample_state_kernel(group, f_ref, k_ref, q_ref, v_ref, s_ref, sn_ref, o_ref):
    for j in range(group):
        for h in range(N_HEADS):
            rs = slice(h * HEAD, (h + 1) * HEAD)
            fcol = f_ref[rs, j:j + 1]
            kcol = k_ref[rs, j:j + 1]
            qcol = q_ref[rs, j:j + 1]
            vrow = v_ref[j:j + 1, rs]
            sn = fcol * s_ref[j, h] + kcol * vrow
            sn_ref[j, h] = sn
            o_ref[j:j + 1, rs] = jnp.sum(qcol * sn, axis=0, keepdims=True)


def _sample_tail_kernel(x_ref, yc_ref, o_ref, og_ref, rg_ref, wout_ref, nf_ref, rw_ref, rb_ref,
                        h_ref, xn_ref, sel_ref, gm_ref):
    y = jnp.concatenate([yc_ref[...], _rec_out(o_ref[...], og_ref[...], rg_ref[...])], axis=-1)
    h, xn, sel, gm = _tail(x_ref[...], y, wout_ref, nf_ref, rw_ref, rb_ref)
    h_ref[...] = h
    xn_ref[...] = xn
    sel_ref[...] = sel
    gm_ref[...] = gm


def _rank_kernel(sel_ref, gm_ref, eid_ref, rank_ref, gate_ref, cnt_ref, carry):
    i = pl.program_id(0)

    @pl.when(i == 0)
    def _():
        carry[...] = jnp.zeros_like(carry)

    sel = sel_ref[...]
    gm = gm_ref[...]
    tb, ne = sel.shape
    row = lax.broadcasted_iota(jnp.int32, (tb, tb), 0)
    col = lax.broadcasted_iota(jnp.int32, (tb, tb), 1)
    before = jnp.where(col < row, 1.0, 0.0).astype(BF16)
    selb = sel.astype(BF16)
    rank = _dot(before, selb) + carry[...]
    carry[...] = carry[...] + jnp.sum(sel, axis=0, keepdims=True)
    cnt_ref[...] = carry[...]
    er = lax.broadcasted_iota(jnp.int32, (ne, ne), 0)
    ec = lax.broadcasted_iota(jnp.int32, (ne, ne), 1)
    lower = jnp.where(er < ec, 1.0, 0.0).astype(BF16)
    order = _dot(selb, lower)
    lane_e = lax.broadcasted_iota(jnp.int32, (tb, ne), 1).astype(F32)
    lane = lax.broadcasted_iota(jnp.int32, (tb, LANES), 1)
    eid = jnp.zeros((tb, LANES), F32)
    rnk = jnp.zeros((tb, LANES), F32)
    gat = jnp.zeros((tb, LANES), F32)
    for k in range(TOP_K):
        pick = jnp.where(order == float(k), sel, 0.0)
        eid = jnp.where(lane == k, jnp.sum(pick * lane_e, axis=-1, keepdims=True), eid)
        rnk = jnp.where(lane == k, jnp.sum(pick * rank, axis=-1, keepdims=True), rnk)
        gat = jnp.where(lane == k, jnp.sum(pick * gm, axis=-1, keepdims=True), gat)
    eid_ref[...] = eid
    rank_ref[...] = rnk
    gate_ref[...] = gat


def _dispatch(xn, pos_w, n_slots):
    n_workers, n_chunks, top_k, ch = pos_w.shape
    assert n_workers == SC_CORES * SC_SUBCORES and ch % 8 == 0 and ch <= LANES
    d = xn.shape[1]
    mesh = plsc.VectorSubcoreMesh(core_axis_name="c", subcore_axis_name="s")

    @functools.partial(
        pl.kernel, mesh=mesh,
        out_type=jax.ShapeDtypeStruct((n_slots, d), xn.dtype),
        scratch_types=[pltpu.VMEM((n_chunks, top_k, ch), jnp.int32), pltpu.VMEM((SC_RING, ch, d), xn.dtype),
                       pltpu.SemaphoreType.DMA((SC_RING,)), pltpu.SemaphoreType.DMA((SC_RING,))],
        name="dispatch",
    )
    def run(xn_hbm, pos_hbm, xs_hbm, idx_v, rows_v, sem_r, sem_w):
        wid = lax.axis_index("s") * SC_CORES + lax.axis_index("c")

        def read(c, b):
            src = xn_hbm.at[pl.ds((wid * n_chunks + c) * ch, ch)]
            return pltpu.make_async_copy(src, rows_v.at[b], sem_r.at[b])

        def write(c, b, k):
            return pltpu.make_async_copy(rows_v.at[b], xs_hbm.at[idx_v.at[c, k]], sem_w.at[b])

        pltpu.sync_copy(pos_hbm.at[wid], idx_v)
        ahead = SC_RING - 1
        for c0 in range(min(ahead, n_chunks)):
            read(c0, c0).start()

        @pl.loop(0, n_chunks)
        def _(c):
            b = lax.rem(c, SC_RING)
            read(c, b).wait()
            for k in range(top_k):
                write(c, b, k).start()

            @pl.when(c >= 1)
            def _():
                for k in range(top_k):
                    write(c - 1, lax.rem(c - 1, SC_RING), k).wait()

            @pl.when(c + ahead < n_chunks)
            def _():
                read(c + ahead, lax.rem(c + ahead, SC_RING)).start()

        for k in range(top_k):
            write(n_chunks - 1, (n_chunks - 1) % SC_RING, k).wait()

    return run(xn, pos_w)


def _collect(ys, pos_w):
    n_workers, n_chunks, top_k, ch = pos_w.shape
    assert n_workers == SC_CORES * SC_SUBCORES and ch % 8 == 0 and ch <= LANES
    d = ys.shape[1]
    mesh = plsc.VectorSubcoreMesh(core_axis_name="c", subcore_axis_name="s")

    @functools.partial(
        pl.kernel, mesh=mesh,
        out_type=jax.ShapeDtypeStruct((n_workers * n_chunks * ch, top_k * d), ys.dtype),
        scratch_types=[pltpu.VMEM((n_chunks, top_k, ch), jnp.int32), pltpu.VMEM((top_k, ch, d), ys.dtype),
                       pltpu.SemaphoreType.DMA((top_k,)), pltpu.SemaphoreType.DMA((top_k,))],
        name="collect",
    )
    def run(ys_hbm, pos_hbm, out_hbm, idx_v, rows_v, sem_r, sem_w):
        wid = lax.axis_index("s") * SC_CORES + lax.axis_index("c")

        def read(c, k):
            return pltpu.make_async_copy(ys_hbm.at[idx_v.at[c, k]], rows_v.at[k], sem_r.at[k])

        def write(c, k):
            dst = out_hbm.at[pl.ds((wid * n_chunks + c) * ch, ch), pl.ds(k * d, d)]
            return pltpu.make_async_copy(rows_v.at[k], dst, sem_w.at[k])

        pltpu.sync_copy(pos_hbm.at[wid], idx_v)
        for k in range(top_k - 1):
            read(0, k).start()

        @pl.loop(0, n_chunks)
        def _(c):
            for k in range(top_k):
                read(c, k).wait()
                write(c, k).start()
                if k >= 1:
                    write(c, k - 1).wait()

                    @pl.when(c + 1 < n_chunks)
                    def _():
                        read(c + 1, k - 1).start()
                else:
                    @pl.when(c >= 1)
                    def _():
                        write(c - 1, top_k - 1).wait()
                    read(c, top_k - 1).start()

        write(n_chunks - 1, top_k - 1).wait()

    return run(ys, pos_w)


def _moe_kernel(tm, te_ref, nu_ref, x_ref, w1_ref, bg_ref, bl_ref, w2_ref, b2_ref,
                ys_ref, w1p, w2b, act):
    i = pl.program_id(0)
    n_used = nu_ref[0]
    d_ff2 = w1_ref.shape[-1]
    n_blk = d_ff2 // MXU_N

    @pl.when(i >= n_used)
    def _():
        ys_ref[...] = jnp.zeros_like(ys_ref)

    prev = te_ref[jnp.maximum(i - 1, 0)]
    changed = jnp.logical_or(i == 0, te_ref[i] != prev)

    @pl.when(jnp.logical_and(changed, i < n_used))
    def _():
        r = lax.broadcasted_iota(jnp.int32, (MXU_N, MXU_N), 0)
        c = lax.broadcasted_iota(jnp.int32, (MXU_N, MXU_N), 1)
        src = jnp.where(c < MXU_N // 2, 2 * c, 2 * (c - MXU_N // 2) + 1)
        perm = jnp.where(r == src, 1.0, 0.0).astype(BF16)
        for blk in range(n_blk):
            cs = slice(blk * MXU_N, (blk + 1) * MXU_N)
            w1p[:, cs] = _dot(w1_ref[:, cs].astype(BF16), perm).astype(BF16)
        w2b[...] = w2_ref[...].astype(BF16)

    @pl.when(i < n_used)
    def _():
        x = _unpack_halves(x_ref[...]).astype(BF16)
        for blk in range(n_blk):
            a = _dot(x, w1p[:, blk * MXU_N:(blk + 1) * MXU_N])
            half = MXU_N // 2
            hs = slice(blk * half, (blk + 1) * half)
            glu = jnp.minimum(a[:, :half] + bg_ref[:, hs], SWIGLU_LIMIT)
            lin = jnp.clip(a[:, half:] + bl_ref[:, hs], -SWIGLU_LIMIT, SWIGLU_LIMIT)
            act[:, hs] = (glu * _sigmoid(SWIGLU_ALPHA * glu) * (lin + 1.0)).astype(BF16)
        ys_ref[...] = _pack_halves(_dot(act[...], w2b[...]) + b2_ref[...])


def _final_kernel(h_ref, gate_ref, nfin_ref, z_ref, y_ref):
    w = z_ref.shape[-1] // TOP_K
    g = gate_ref[...]
    out = h_ref[...]
    for k in range(TOP_K):
        out = out + g[:, k:k + 1] * _unpack_halves(z_ref[:, k * w:(k + 1) * w])
    y_ref[...] = _rms(out, nfin_ref[...])


def _final_call(h_all, gate, nfin, z, first_row, n_rows, tile):
    d = h_all.shape[-1]
    off = first_row // tile
    assert first_row % tile == 0 and n_rows % tile == 0
    return pl.pallas_call(
        _final_kernel,
        grid=(n_rows // tile,),
        in_specs=[
            pl.BlockSpec((tile, d), lambda i: (i + off, 0)),
            pl.BlockSpec((tile, gate.shape[-1]), lambda i: (i + off, 0)),
            pl.BlockSpec((1, d), lambda i: (0, 0)),
            pl.BlockSpec((tile, z.shape[-1]), lambda i: (i + off, 0)),
        ],
        out_specs=pl.BlockSpec((tile, d), lambda i: (i, 0)),
        out_shape=jax.ShapeDtypeStruct((n_rows, d), F32),
        compiler_params=pltpu.CompilerParams(dimension_semantics=("arbitrary",),
                                             vmem_limit_bytes=VMEM_LIMIT),
        name="final",
    )(h_all, gate, nfin, z)


def kernel(x_prompt, x_sample, state_conv, state_rec, meta_tokens, norm_mix, w_in, conv_w,
           rec_lower_bound, rec_norm, w_out, norm_ffn, router_w, router_b, expert_w1, expert_b1,
           expert_w2, expert_b2, norm_final):
    depth = norm_mix.shape[0]
    assert depth == 1, "single-layer step"
    layer = 0
    bp, seq, d = x_prompt.shape
    ns = x_sample.shape[0]
    assert x_sample.shape[1] == 1 and ns == TOKEN_TILE
    d_conv = conv_w.shape[-1]
    d_rec = rec_lower_bound.shape[-1]
    assert state_conv.shape[2] == 2 and d_rec == N_HEADS * HEAD
    n_exp = router_w.shape[-1]
    d_ff = expert_w2.shape[2]
    n_prompt = bp * seq
    n_tok = n_prompt + ns
    assert seq % PROMPT_TILE == 0 and n_prompt % TOKEN_TILE == 0 and n_tok % RANK_TILE == 0

    nm = norm_mix[layer][None]
    win = w_in[layer].astype(BF16)
    cw = conv_w[layer]
    rlb = rec_lower_bound
    rg = rec_norm[layer][None]
    wout = w_out[layer].astype(BF16)
    nf = norm_ffn[layer][None]
    rw = router_w[layer]
    rb = router_b[layer][None]
    mix_w = (nm, win, cw, rlb)
    tail_w = (rg, wout, nf, rw, rb)

    st_meta, cv_meta = _meta_call(meta_tokens, mix_w, layer)

    xs = x_sample.reshape(ns, d)
    wide = jax.ShapeDtypeStruct((ns, d_rec), F32)
    y_conv_s, new_conv_s, f_s, k_s, q_s, v_s, og_s = pl.pallas_call(
        functools.partial(_sample_in_kernel, layer),
        out_shape=[jax.ShapeDtypeStruct((ns, d_conv), F32), jax.ShapeDtypeStruct((ns, 2 * d_conv), F32),
                   wide, wide, wide, wide, wide],
        compiler_params=pltpu.CompilerParams(vmem_limit_bytes=VMEM_LIMIT),
        name="sample_in",
    )(xs, state_conv[layer].reshape(ns, 2 * d_conv), nm, win, cw, rlb)

    group = 8
    n_grp = ns // group

    def cols(a):
        return a.T.reshape(d_rec, n_grp, group).transpose(1, 0, 2)

    col_spec = pl.BlockSpec((None, d_rec, group), lambda g: (g, 0, 0))
    st_spec = pl.BlockSpec((group, N_HEADS, HEAD, HEAD), lambda g: (g, 0, 0, 0))
    row_spec = pl.BlockSpec((group, d_rec), lambda g: (g, 0))
    new_rec_s, o_s = pl.pallas_call(
        functools.partial(_sample_state_kernel, group),
        grid=(n_grp,),
        in_specs=[col_spec, col_spec, col_spec, row_spec, st_spec],
        out_specs=[st_spec, row_spec],
        out_shape=[jax.ShapeDtypeStruct(state_rec.shape[1:], F32), wide],
        compiler_params=pltpu.CompilerParams(dimension_semantics=("arbitrary",),
                                             vmem_limit_bytes=VMEM_LIMIT),
        name="sample_state",
    )(cols(f_s), cols(k_s), cols(q_s), v_s, state_rec[layer])

    decode = pl.pallas_call(
        _sample_tail_kernel,
        out_shape=[jax.ShapeDtypeStruct((ns, d), F32), jax.ShapeDtypeStruct((ns, d), F32),
                   jax.ShapeDtypeStruct((ns, n_exp), F32), jax.ShapeDtypeStruct((ns, n_exp), F32)],
        compiler_params=pltpu.CompilerParams(vmem_limit_bytes=VMEM_LIMIT),
        name="sample_tail",
    )(xs, y_conv_s, o_s, og_s, rg, wout, nf, rw, rb)

    h_all, xn_all, sel_all, gm_all, new_rec_p, new_conv_p = _prompt_call(
        x_prompt, st_meta, cv_meta, mix_w, tail_w, decode, layer, PROMPT_TILE, CHUNK)

    canvas = jax.ShapeDtypeStruct((n_tok, LANES), F32)
    tok_spec = pl.BlockSpec((RANK_TILE, n_exp), lambda i: (i, 0))
    can_spec = pl.BlockSpec((RANK_TILE, LANES), lambda i: (i, 0))
    eid, rnk, gate, counts = pl.pallas_call(
        _rank_kernel,
        grid=(n_tok // RANK_TILE,),
        in_specs=[tok_spec, tok_spec],
        out_specs=[can_spec, can_spec, can_spec, pl.BlockSpec((1, n_exp), lambda i: (0, 0))],
        out_shape=[canvas, canvas, canvas, jax.ShapeDtypeStruct((1, n_exp), F32)],
        scratch_shapes=[pltpu.VMEM((1, n_exp), F32)],
        compiler_params=pltpu.CompilerParams(dimension_semantics=("arbitrary",),
                                             vmem_limit_bytes=VMEM_LIMIT),
        name="rank",
    )(sel_all, gm_all)

    tm = MOE_TILE
    n_tiles = (n_tok * TOP_K) // tm + n_exp
    counts = counts[0].astype(jnp.int32)
    tiles_e = (counts + tm - 1) // tm
    tile_end = jnp.cumsum(tiles_e)
    n_used = tile_end[-1]
    offs = (tile_end - tiles_e) * tm
    eid4 = eid[:, :TOP_K].astype(jnp.int32)
    experts = jnp.arange(n_exp, dtype=jnp.int32)
    off4 = jnp.sum(jnp.where(eid4[..., None] == experts, offs, 0), axis=-1)
    pos = off4 + rnk[:, :TOP_K].astype(jnp.int32)
    tile_ids = jnp.minimum(jnp.arange(n_tiles, dtype=jnp.int32), n_used - 1)
    tile_expert = jnp.minimum(jnp.sum((tile_end[None, :] <= tile_ids[:, None]).astype(jnp.int32), axis=1),
                              n_exp - 1)

    n_rows = xn_all.shape[0]
    n_workers = SC_CORES * SC_SUBCORES
    assert n_rows % (n_workers * DISPATCH_CHUNK) == 0
    spare = jnp.full((n_rows - n_tok, TOP_K), n_tiles * tm, jnp.int32)
    pos_w = jnp.concatenate([pos, spare], axis=0).reshape(n_workers, -1, DISPATCH_CHUNK, TOP_K)
    pos_w = pos_w.transpose(0, 1, 3, 2)
    xs = _dispatch(xn_all, pos_w, n_tiles * tm + 8)

    w1 = expert_w1[layer]
    w2 = expert_w2[layer]
    b1 = expert_b1[layer]
    b1g = b1[:, 0::2][:, None, :]
    b1l = b1[:, 1::2][:, None, :]
    b2 = expert_b2[layer][:, None, :]
    ys = pl.pallas_call(
        functools.partial(_moe_kernel, tm),
        grid_spec=pltpu.PrefetchScalarGridSpec(
            num_scalar_prefetch=2,
            grid=(n_tiles,),
            in_specs=[
                pl.BlockSpec((tm, d // 2), lambda i, te, nu: (jnp.minimum(i, nu[0] - 1), 0)),
                pl.BlockSpec((None, d, 2 * d_ff), lambda i, te, nu: (te[i], 0, 0)),
                pl.BlockSpec((None, 1, d_ff), lambda i, te, nu: (te[i], 0, 0)),
                pl.BlockSpec((None, 1, d_ff), lambda i, te, nu: (te[i], 0, 0)),
                pl.BlockSpec((None, d_ff, d), lambda i, te, nu: (te[i], 0, 0)),
                pl.BlockSpec((None, 1, d), lambda i, te, nu: (te[i], 0, 0)),
            ],
            out_specs=pl.BlockSpec((tm, d // 2), lambda i, te, nu: (i, 0)),
            scratch_shapes=[pltpu.VMEM((d, 2 * d_ff), BF16), pltpu.VMEM((d_ff, d), BF16),
                            pltpu.VMEM((tm, d_ff), BF16)],
        ),
        out_shape=jax.ShapeDtypeStruct((n_tiles * tm + 8, d // 2), jnp.uint32),
        compiler_params=pltpu.CompilerParams(dimension_semantics=("arbitrary",),
                                             vmem_limit_bytes=VMEM_LIMIT),
        name="moe",
    )(tile_expert, n_used[None].astype(jnp.int32), xs, w1, b1g, b1l, w2, b2)

    z = _collect(ys, pos_w)
    y_p = _final_call(h_all, gate, norm_final[None], z, 0, n_prompt, FINAL_TILE)
    y_s = _final_call(h_all, gate, norm_final[None], z, n_prompt, ns, TOKEN_TILE)

    return (y_p.reshape(bp, seq, d), y_s.reshape(ns, 1, d),
            new_conv_p[None], new_rec_p[None],
            new_conv_s.reshape(1, ns, 2, d_conv), new_rec_s[None])
```

```python
import functools

import jax
import jax.numpy as jnp
from jax import lax
from jax.experimental import pallas as pl
from jax.experimental.pallas import tpu as pltpu
from jax.experimental.pallas import tpu_sc as plsc

F32 = jnp.float32
BF16 = jnp.bfloat16

N_HEADS = 4
HEAD = 128
N_META = 16
CHUNK = 64
TOP_K = 4
SWIGLU_LIMIT = 7.0
SWIGLU_ALPHA = 1.702
EPS = 1e-5

LANES = 128
MXU_N = 256
PROMPT_TILE = 256
TOKEN_TILE = 128
FINAL_TILE = 512
RANK_TILE = 384
MOE_TILE = 256
SC_CORES = 2
SC_SUBCORES = 16
DISPATCH_CHUNK = 24
SC_RING = 4
VMEM_LIMIT = 56 * 1024 * 1024


def _dot(a, b):
    return jnp.dot(a, b, preferred_element_type=F32)


def _dot_nt(a, b):
    return lax.dot_general(a, b, (((1,), (1,)), ((), ())), preferred_element_type=F32)


def _dot_tn(a, b):
    return lax.dot_general(a, b, (((0,), (0,)), ((), ())), preferred_element_type=F32)


def _split3(x):
    hi = x.astype(BF16)
    r = x - hi.astype(F32)
    mid = r.astype(BF16)
    lo = (r - mid.astype(F32)).astype(BF16)
    return hi, mid, lo


def _pack_halves(x):
    n = x.shape[-1] // 2
    lo = pltpu.bitcast(x[:, :n].astype(BF16).astype(F32), jnp.uint32)
    hi = pltpu.bitcast(x[:, n:].astype(BF16).astype(F32), jnp.uint32)
    return (lo >> 16) | (hi & jnp.uint32(0xFFFF0000))


def _unpack_halves(u):
    lo = pltpu.bitcast(u << 16, F32)
    hi = pltpu.bitcast(u & jnp.uint32(0xFFFF0000), F32)
    return jnp.concatenate([lo, hi], axis=-1)


def _sigmoid(x):
    return 1.0 / (1.0 + jnp.exp(-x))


def _rms(x, g):
    ms = jnp.mean(x * x, axis=-1, keepdims=True)
    return x * lax.rsqrt(ms + EPS) * g


def _project(x, nm_ref, win_ref):
    return _dot(_rms(x, nm_ref[...]).astype(BF16), win_ref[...])


def _split_u(u, d_conv, d_rec):
    pts = [0, d_conv, 2 * d_conv, 3 * d_conv, 3 * d_conv + d_rec, 3 * d_conv + 2 * d_rec,
           3 * d_conv + 3 * d_rec, 3 * d_conv + 4 * d_rec]
    return [u[:, pts[i]:pts[i + 1]] for i in range(7)]


def _lower_bound(rlb_ref, layer):
    r = rlb_ref[...]
    e = jnp.exp(r - jnp.max(r, axis=0, keepdims=True))
    return jnp.sum(e[0:layer + 1], axis=0, keepdims=True) / jnp.sum(e, axis=0, keepdims=True)


def _forget(fx, lb):
    f = lb + (1.0 - lb) * _sigmoid(fx)
    return f, 1.0 - f


def _rec_out(o, og, rg):
    parts = []
    for h in range(N_HEADS):
        oh = o[:, h * HEAD:(h + 1) * HEAD]
        parts.append(oh * lax.rsqrt(jnp.mean(oh * oh, axis=-1, keepdims=True) + EPS))
    return jnp.concatenate(parts, axis=-1) * rg * (og * _sigmoid(og))


def _route(logits):
    n = logits.shape[-1]
    lane = lax.broadcasted_iota(jnp.int32, logits.shape, 1).astype(F32)
    work = logits
    tops, hots = [], []
    for _ in range(TOP_K):
        m = jnp.max(work, axis=-1, keepdims=True)
        first = jnp.min(jnp.where(work == m, lane, float(n)), axis=-1, keepdims=True)
        hot = lane == first
        tops.append(m)
        hots.append(hot)
        work = jnp.where(hot, -jnp.inf, work)
    es = [jnp.exp(t - tops[0]) for t in tops]
    den = es[0]
    for e in es[1:]:
        den = den + e
    sel = jnp.zeros_like(logits)
    gm = jnp.zeros_like(logits)
    for hot, e in zip(hots, es):
        sel = sel + jnp.where(hot, 1.0, 0.0)
        gm = gm + jnp.where(hot, e / den, 0.0)
    return sel, gm


def _tail(x, y, wout_ref, nf_ref, rw_ref, rb_ref):
    h = x + _dot(y.astype(BF16), wout_ref[...])
    xn = _rms(h, nf_ref[...])
    xh = xn.astype(BF16)
    xl = (xn - xh.astype(F32)).astype(BF16)
    rw = rw_ref[...]
    wh = rw.astype(BF16)
    wl = (rw - wh.astype(F32)).astype(BF16)
    logits = _dot(xh, wh) + _dot(xh, wl) + _dot(xl, wh) + rb_ref[...]
    sel, gm = _route(logits)
    return h, xn, sel, gm


def _rec_chunk(q, kk, v, lf, st_ref, tri, causal):
    c = q.shape[0]
    hi, mid, lo = _split3(lf)
    b = _dot(tri, hi) + _dot(tri, mid) + _dot(tri, lo)
    eb = jnp.exp(b)
    qe = (q * eb).astype(BF16)
    ke = (kk * jnp.exp(-b)).astype(BF16)
    vb = v.astype(BF16)
    eb_last = eb[c - 1:c]
    outs = []
    for h in range(N_HEADS):
        sl = slice(h * HEAD, (h + 1) * HEAD)
        st = st_ref[h]
        sc = jnp.where(causal, _dot_nt(qe[:, sl], ke[:, sl]), 0.0)
        outs.append(_dot(sc.astype(BF16), vb[:, sl]) + _dot_nt(qe[:, sl], st.astype(BF16)))
        st_ref[h] = (st + _dot_tn(vb[:, sl], ke[:, sl])) * eb_last[:, sl]
    return jnp.concatenate(outs, axis=-1)


def _mix_tile(u, layer, chunk, cw_ref, rlb_ref, convbuf, st, cv_out):
    tm = u.shape[0]
    d_conv = cw_ref.shape[-1]
    d_rec = rlb_ref.shape[-1]
    bg, cg, hv, q, fx, iv, og = _split_u(u, d_conv, d_rec)

    bx = bg * hv
    convbuf[8:8 + tm, :] = bx
    cw = cw_ref[...]
    conv = cw[0:1] * convbuf[6:6 + tm, :] + cw[1:2] * convbuf[7:7 + tm, :] + cw[2:3] * bx
    convbuf[6:8, :] = bx[tm - 2:tm]
    cv_out[...] = bx[tm - 2:tm]

    f, kk = _forget(fx, _lower_bound(rlb_ref, layer))
    lf = jnp.log(f)
    row = lax.broadcasted_iota(jnp.int32, (chunk, chunk), 0)
    col = lax.broadcasted_iota(jnp.int32, (chunk, chunk), 1)
    causal = row >= col
    tri = jnp.where(causal, 1.0, 0.0).astype(BF16)
    outs = []
    for c in range(tm // chunk):
        rs = slice(c * chunk, (c + 1) * chunk)
        outs.append(_rec_chunk(q[rs], kk[rs], iv[rs], lf[rs], st, tri, causal))
    return cg * conv, jnp.concatenate(outs, axis=0), og


def _load_state(st, convbuf, s0_ref, c0_ref):
    for h in range(N_HEADS):
        st[h] = s0_ref[h].T
    convbuf[6:8, :] = c0_ref[...]


def _meta_kernel(layer, x_ref, s0_ref, c0_ref, nm_ref, win_ref, cw_ref, rlb_ref, st_out, cv_out,
                 convbuf, st):
    _load_state(st, convbuf, s0_ref, c0_ref)
    _mix_tile(_project(x_ref[...], nm_ref, win_ref), layer, x_ref.shape[0], cw_ref, rlb_ref,
              convbuf, st, cv_out)
    for h in range(N_HEADS):
        st_out[h] = st[h].T


def _prompt_kernel(layer, tm, chunk, n_pairs, pairs_per_seq,
                   xr0_ref, x1_ref, xp2_ref, s0_ref, c0_ref, nm_ref, win_ref, cw_ref, rlb_ref,
                   rg_ref, wout_ref, nf_ref, rw_ref, rb_ref, hs_ref, xns_ref, sels_ref, gms_ref,
                   h_ref, xn_ref, sel_ref, gm_ref, st_out, cv_out, convbuf, st, ua, ub):
    step = pl.program_id(0)
    live = step < n_pairs

    @pl.when(step == 0)
    def _():
        ua[...] = _project(xr0_ref[...], nm_ref, win_ref)

    @pl.when(jnp.logical_and(lax.rem(step, pairs_per_seq) == 0, live))
    def _():
        _load_state(st, convbuf, s0_ref, c0_ref)

    @pl.when(live)
    def _():
        def finish(x, u, rows):
            y_conv, o, og = _mix_tile(u, layer, chunk, cw_ref, rlb_ref, convbuf, st, cv_out)
            y = jnp.concatenate([y_conv, _rec_out(o, og, rg_ref[...])], axis=-1)
            h, xn, sel, gm = _tail(x, y, wout_ref, nf_ref, rw_ref, rb_ref)
            h_ref[rows, :] = h
            xn_ref[rows, :] = _pack_halves(xn)
            sel_ref[rows, :] = sel
            gm_ref[rows, :] = gm

        ub[...] = _project(x1_ref[...], nm_ref, win_ref)
        finish(xr0_ref[...], ua[...], slice(0, tm))
        ua[...] = _project(xp2_ref[...], nm_ref, win_ref)
        finish(x1_ref[...], ub[...], slice(tm, 2 * tm))

    @pl.when(jnp.logical_and(lax.rem(step, pairs_per_seq) == pairs_per_seq - 1, live))
    def _():
        for h in range(N_HEADS):
            st_out[h] = st[h].T

    @pl.when(step == n_pairs)
    def _():
        ns = hs_ref.shape[0]
        for dst, val in ((h_ref, hs_ref[...]), (xn_ref, _pack_halves(xns_ref[...])),
                         (sel_ref, sels_ref[...]), (gm_ref, gms_ref[...])):
            dst[0:ns, :] = val
            dst[ns:2 * tm, :] = jnp.zeros((2 * tm - ns, dst.shape[-1]), dst.dtype)


def _const_spec(shape):
    return pl.BlockSpec(shape, lambda *_: (0,) * len(shape))


def _meta_call(x, weights, layer):
    nm, win, cw, rlb = weights
    d_conv = cw.shape[-1]
    s0 = jnp.zeros((N_HEADS, HEAD, HEAD), F32)
    c0 = jnp.zeros((2, d_conv), F32)
    return pl.pallas_call(
        functools.partial(_meta_kernel, layer),
        out_shape=[jax.ShapeDtypeStruct(s0.shape, F32), jax.ShapeDtypeStruct(c0.shape, F32)],
        scratch_shapes=[pltpu.VMEM((x.shape[0] + 8, d_conv), F32), pltpu.VMEM(s0.shape, F32)],
        compiler_params=pltpu.CompilerParams(vmem_limit_bytes=VMEM_LIMIT),
        name="mixer_meta",
    )(x, s0, c0, nm, win, cw, rlb)


def _prompt_call(x, s0, c0, weights, tail_w, decode, layer, tm, chunk):
    nseq, length, d = x.shape
    nt = length // tm
    assert nt % 2 == 0 and decode[0].shape[0] <= 2 * tm
    pairs_per_seq = nt // 2
    n_pairs = nseq * pairs_per_seq
    nm, win, cw, rlb = weights
    d_conv = cw.shape[-1]
    d_in = win.shape[-1]

    def tile_spec(offset):
        def index(s):
            tile = jnp.minimum(2 * s + offset, 2 * n_pairs - 1)
            return (tile // nt, lax.rem(tile, nt), 0)
        return pl.BlockSpec((None, tm, d), index)

    def seq_of(s):
        return jnp.minimum(s // pairs_per_seq, nseq - 1)

    consts = [s0, c0, nm, win, cw, rlb] + list(tail_w) + list(decode)
    tok = [(a.shape[-1], F32) for a in decode]
    tok[1] = (tok[1][0] // 2, jnp.uint32)
    n_steps = n_pairs + 1
    return pl.pallas_call(
        functools.partial(_prompt_kernel, layer, tm, chunk, n_pairs, pairs_per_seq),
        grid=(n_steps,),
        in_specs=[tile_spec(0), tile_spec(1), tile_spec(2)] + [_const_spec(a.shape) for a in consts],
        out_specs=[pl.BlockSpec((2 * tm, w), lambda s: (s, 0)) for w, _ in tok] + [
            pl.BlockSpec((None,) + s0.shape, lambda s: (seq_of(s), 0, 0, 0)),
            pl.BlockSpec((None,) + c0.shape, lambda s: (seq_of(s), 0, 0))],
        out_shape=[jax.ShapeDtypeStruct((n_steps * 2 * tm, w), t) for w, t in tok] + [
            jax.ShapeDtypeStruct((nseq,) + s0.shape, F32), jax.ShapeDtypeStruct((nseq,) + c0.shape, F32)],
        scratch_shapes=[pltpu.VMEM((tm + 8, d_conv), F32), pltpu.VMEM(s0.shape, F32),
                        pltpu.VMEM((tm, d_in), F32), pltpu.VMEM((tm, d_in), F32)],
        compiler_params=pltpu.CompilerParams(dimension_semantics=("arbitrary",),
                                             vmem_limit_bytes=VMEM_LIMIT),
        name="mixer_prompt",
    )(x, x, x, *consts)


def _sample_in_kernel(layer, x_ref, sc_ref, nm_ref, win_ref, cw_ref, rlb_ref,
                      yc_ref, nc_ref, f_ref, k_ref, q_ref, v_ref, og_ref):
    d_conv = cw_ref.shape[-1]
    d_rec = rlb_ref.shape[-1]
    u = _project(x_ref[...], nm_ref, win_ref)
    bg, cg, hv, q, fx, iv, og = _split_u(u, d_conv, d_rec)
    bx = bg * hv
    sc = sc_ref[...]
    s0, s1 = sc[:, :d_conv], sc[:, d_conv:]
    cw = cw_ref[...]
    yc_ref[...] = cg * (cw[0:1] * s0 + cw[1:2] * s1 + cw[2:3] * bx)
    nc_ref[...] = jnp.concatenate([s1, bx], axis=-1)
    f, kk = _forget(fx, _lower_bound(rlb_ref, layer))
    f_ref[...] = f
    k_ref[...] = kk
    q_ref[...] = q
    v_ref[...] = iv
    og_ref[...] = og


def _sample_state_kernel(group, f_ref, k_ref, q_ref, v_ref, s_ref, sn_ref, o_ref):
    for j in range(group):
        for h in range(N_HEADS):
            rs = slice(h * HEAD, (h + 1) * HEAD)
            fcol = f_ref[rs, j:j + 1]
            kcol = k_ref[rs, j:j + 1]
            qcol = q_ref[rs, j:j + 1]
            vrow = v_ref[j:j + 1, rs]
            sn = fcol * s_ref[j, h] + kcol * vrow
            sn_ref[j, h] = sn
            o_ref[j:j + 1, rs] = jnp.sum(qcol * sn, axis=0, keepdims=True)


def _sample_tail_kernel(x_ref, yc_ref, o_ref, og_ref, rg_ref, wout_ref, nf_ref, rw_ref, rb_ref,
                        h_ref, xn_ref, sel_ref, gm_ref):
    y = jnp.concatenate([yc_ref[...], _rec_out(o_ref[...], og_ref[...], rg_ref[...])], axis=-1)
    h, xn, sel, gm = _tail(x_ref[...], y, wout_ref, nf_ref, rw_ref, rb_ref)
    h_ref[...] = h
    xn_ref[...] = xn
    sel_ref[...] = sel
    gm_ref[...] = gm


def _rank_kernel(sel_ref, gm_ref, eid_ref, rank_ref, gate_ref, cnt_ref, carry):
    i = pl.program_id(0)

    @pl.when(i == 0)
    def _():
        carry[...] = jnp.zeros_like(carry)

    sel = sel_ref[...]
    gm = gm_ref[...]
    tb, ne = sel.shape
    row = lax.broadcasted_iota(jnp.int32, (tb, tb), 0)
    col = lax.broadcasted_iota(jnp.int32, (tb, tb), 1)
    before = jnp.where(col < row, 1.0, 0.0).astype(BF16)
    selb = sel.astype(BF16)
    rank = _dot(before, selb) + carry[...]
    carry[...] = carry[...] + jnp.sum(sel, axis=0, keepdims=True)
    cnt_ref[...] = carry[...]
    er = lax.broadcasted_iota(jnp.int32, (ne, ne), 0)
    ec = lax.broadcasted_iota(jnp.int32, (ne, ne), 1)
    lower = jnp.where(er < ec, 1.0, 0.0).astype(BF16)
    order = _dot(selb, lower)
    lane_e = lax.broadcasted_iota(jnp.int32, (tb, ne), 1).astype(F32)
    lane = lax.broadcasted_iota(jnp.int32, (tb, LANES), 1)
    eid = jnp.zeros((tb, LANES), F32)
    rnk = jnp.zeros((tb, LANES), F32)
    gat = jnp.zeros((tb, LANES), F32)
    for k in range(TOP_K):
        pick = jnp.where(order == float(k), sel, 0.0)
        eid = jnp.where(lane == k, jnp.sum(pick * lane_e, axis=-1, keepdims=True), eid)
        rnk = jnp.where(lane == k, jnp.sum(pick * rank, axis=-1, keepdims=True), rnk)
        gat = jnp.where(lane == k, jnp.sum(pick * gm, axis=-1, keepdims=True), gat)
    eid_ref[...] = eid
    rank_ref[...] = rnk
    gate_ref[...] = gat


def _dispatch(xn, pos_w, n_slots):
    n_workers, n_chunks, top_k, ch = pos_w.shape
    assert n_workers == SC_CORES * SC_SUBCORES and ch % 8 == 0 and ch <= LANES
    d = xn.shape[1]
    mesh = plsc.VectorSubcoreMesh(core_axis_name="c", subcore_axis_name="s")

    @functools.partial(
        pl.kernel, mesh=mesh,
        out_type=jax.ShapeDtypeStruct((n_slots, d), xn.dtype),
        scratch_types=[pltpu.VMEM((n_chunks, top_k, ch), jnp.int32), pltpu.VMEM((SC_RING, ch, d), xn.dtype),
                       pltpu.SemaphoreType.DMA((SC_RING,)), pltpu.SemaphoreType.DMA((SC_RING,))],
        name="dispatch",
    )
    def run(xn_hbm, pos_hbm, xs_hbm, idx_v, rows_v, sem_r, sem_w):
        wid = lax.axis_index("s") * SC_CORES + lax.axis_index("c")

        def read(c, b):
            src = xn_hbm.at[pl.ds((wid * n_chunks + c) * ch, ch)]
            return pltpu.make_async_copy(src, rows_v.at[b], sem_r.at[b])

        def write(c, b, k):
            return pltpu.make_async_copy(rows_v.at[b], xs_hbm.at[idx_v.at[c, k]], sem_w.at[b])

        pltpu.sync_copy(pos_hbm.at[wid], idx_v)
        ahead = SC_RING - 1
        for c0 in range(min(ahead, n_chunks)):
            read(c0, c0).start()

        @pl.loop(0, n_chunks)
        def _(c):
            b = lax.rem(c, SC_RING)
            read(c, b).wait()
            for k in range(top_k):
                write(c, b, k).start()

            @pl.when(c >= 1)
            def _():
                for k in range(top_k):
                    write(c - 1, lax.rem(c - 1, SC_RING), k).wait()

            @pl.when(c + ahead < n_chunks)
            def _():
                read(c + ahead, lax.rem(c + ahead, SC_RING)).start()

        for k in range(top_k):
            write(n_chunks - 1, (n_chunks - 1) % SC_RING, k).wait()

    return run(xn, pos_w)


def _collect(ys, pos_w):
    n_workers, n_chunks, top_k, ch = pos_w.shape
    assert n_workers == SC_CORES * SC_SUBCORES and ch % 8 == 0 and ch <= LANES
    d = ys.shape[1]
    mesh = plsc.VectorSubcoreMesh(core_axis_name="c", subcore_axis_name="s")

    @functools.partial(
        pl.kernel, mesh=mesh,
        out_type=jax.ShapeDtypeStruct((n_workers * n_chunks * ch, top_k * d), ys.dtype),
        scratch_types=[pltpu.VMEM((n_chunks, top_k, ch), jnp.int32), pltpu.VMEM((top_k, ch, d), ys.dtype),
                       pltpu.SemaphoreType.DMA((top_k,)), pltpu.SemaphoreType.DMA((top_k,))],
        name="collect",
    )
    def run(ys_hbm, pos_hbm, out_hbm, idx_v, rows_v, sem_r, sem_w):
        wid = lax.axis_index("s") * SC_CORES + lax.axis_index("c")

        def read(c, k):
            return pltpu.make_async_copy(ys_hbm.at[idx_v.at[c, k]], rows_v.at[k], sem_r.at[k])

        def write(c, k):
            dst = out_hbm.at[pl.ds((wid * n_chunks + c) * ch, ch), pl.ds(k * d, d)]
            return pltpu.make_async_copy(rows_v.at[k], dst, sem_w.at[k])

        pltpu.sync_copy(pos_hbm.at[wid], idx_v)
        for k in range(top_k - 1):
            read(0, k).start()

        @pl.loop(0, n_chunks)
        def _(c):
            for k in range(top_k):
                read(c, k).wait()
                write(c, k).start()
                if k >= 1:
                    write(c, k - 1).wait()

                    @pl.when(c + 1 < n_chunks)
                    def _():
                        read(c + 1, k - 1).start()
                else:
                    @pl.when(c >= 1)
                    def _():
                        write(c - 1, top_k - 1).wait()
                    read(c, top_k - 1).start()

        write(n_chunks - 1, top_k - 1).wait()

    return run(ys, pos_w)


def _moe_kernel(tm, te_ref, nu_ref, x_ref, w1_ref, bg_ref, bl_ref, w2_ref, b2_ref,
                ys_ref, w1p, w2b, act):
    i = pl.program_id(0)
    n_used = nu_ref[0]
    d_ff2 = w1_ref.shape[-1]
    n_blk = d_ff2 // MXU_N

    @pl.when(i >= n_used)
    def _():
        ys_ref[...] = jnp.zeros_like(ys_ref)

    prev = te_ref[jnp.maximum(i - 1, 0)]
    changed = jnp.logical_or(i == 0, te_ref[i] != prev)

    @pl.when(jnp.logical_and(changed, i < n_used))
    def _():
        r = lax.broadcasted_iota(jnp.int32, (MXU_N, MXU_N), 0)
        c = lax.broadcasted_iota(jnp.int32, (MXU_N, MXU_N), 1)
        src = jnp.where(c < MXU_N // 2, 2 * c, 2 * (c - MXU_N // 2) + 1)
        perm = jnp.where(r == src, 1.0, 0.0).astype(BF16)
        for blk in range(n_blk):
            cs = slice(blk * MXU_N, (blk + 1) * MXU_N)
            w1p[:, cs] = _dot(w1_ref[:, cs].astype(BF16), perm).astype(BF16)
        w2b[...] = w2_ref[...].astype(BF16)

    @pl.when(i < n_used)
    def _():
        x = _unpack_halves(x_ref[...]).astype(BF16)
        for blk in range(n_blk):
            a = _dot(x, w1p[:, blk * MXU_N:(blk + 1) * MXU_N])
            half = MXU_N // 2
            hs = slice(blk * half, (blk + 1) * half)
            glu = jnp.minimum(a[:, :half] + bg_ref[:, hs], SWIGLU_LIMIT)
            lin = jnp.clip(a[:, half:] + bl_ref[:, hs], -SWIGLU_LIMIT, SWIGLU_LIMIT)
            act[:, hs] = (glu * _sigmoid(SWIGLU_ALPHA * glu) * (lin + 1.0)).astype(BF16)
        ys_ref[...] = _pack_halves(_dot(act[...], w2b[...]) + b2_ref[...])


def _final_kernel(h_ref, gate_ref, nfin_ref, z_ref, y_ref):
    w = z_ref.shape[-1] // TOP_K
    g = gate_ref[...]
    out = h_ref[...]
    for k in range(TOP_K):
        out = out + g[:, k:k + 1] * _unpack_halves(z_ref[:, k * w:(k + 1) * w])
    y_ref[...] = _rms(out, nfin_ref[...])


def _final_call(h_all, gate, nfin, z, first_row, n_rows, tile):
    d = h_all.shape[-1]
    off = first_row // tile
    assert first_row % tile == 0 and n_rows % tile == 0
    return pl.pallas_call(
        _final_kernel,
        grid=(n_rows // tile,),
        in_specs=[
            pl.BlockSpec((tile, d), lambda i: (i + off, 0)),
            pl.BlockSpec((tile, gate.shape[-1]), lambda i: (i + off, 0)),
            pl.BlockSpec((1, d), lambda i: (0, 0)),
            pl.BlockSpec((tile, z.shape[-1]), lambda i: (i + off, 0)),
        ],
        out_specs=pl.BlockSpec((tile, d), lambda i: (i, 0)),
        out_shape=jax.ShapeDtypeStruct((n_rows, d), F32),
        compiler_params=pltpu.CompilerParams(dimension_semantics=("arbitrary",),
                                             vmem_limit_bytes=VMEM_LIMIT),
        name="final",
    )(h_all, gate, nfin, z)


def kernel(x_prompt, x_sample, state_conv, state_rec, meta_tokens, norm_mix, w_in, conv_w,
           rec_lower_bound, rec_norm, w_out, norm_ffn, router_w, router_b, expert_w1, expert_b1,
           expert_w2, expert_b2, norm_final):
    depth = norm_mix.shape[0]
    assert depth == 1, "single-layer step"
    layer = 0
    bp, seq, d = x_prompt.shape
    ns = x_sample.shape[0]
    assert x_sample.shape[1] == 1 and ns == TOKEN_TILE
    d_conv = conv_w.shape[-1]
    d_rec = rec_lower_bound.shape[-1]
    assert state_conv.shape[2] == 2 and d_rec == N_HEADS * HEAD
    n_exp = router_w.shape[-1]
    d_ff = expert_w2.shape[2]
    n_prompt = bp * seq
    n_tok = n_prompt + ns
    assert seq % PROMPT_TILE == 0 and n_prompt % TOKEN_TILE == 0 and n_tok % RANK_TILE == 0

    nm = norm_mix[layer][None]
    win = w_in[layer].astype(BF16)
    cw = conv_w[layer]
    rlb = rec_lower_bound
    rg = rec_norm[layer][None]
    wout = w_out[layer].astype(BF16)
    nf = norm_ffn[layer][None]
    rw = router_w[layer]
    rb = router_b[layer][None]
    mix_w = (nm, win, cw, rlb)
    tail_w = (rg, wout, nf, rw, rb)

    st_meta, cv_meta = _meta_call(meta_tokens, mix_w, layer)

    xs = x_sample.reshape(ns, d)
    wide = jax.ShapeDtypeStruct((ns, d_rec), F32)
    y_conv_s, new_conv_s, f_s, k_s, q_s, v_s, og_s = pl.pallas_call(
        functools.partial(_sample_in_kernel, layer),
        out_shape=[jax.ShapeDtypeStruct((ns, d_conv), F32), jax.ShapeDtypeStruct((ns, 2 * d_conv), F32),
                   wide, wide, wide, wide, wide],
        compiler_params=pltpu.CompilerParams(vmem_limit_bytes=VMEM_LIMIT),
        name="sample_in",
    )(xs, state_conv[layer].reshape(ns, 2 * d_conv), nm, win, cw, rlb)

    group = 8
    n_grp = ns // group

    def cols(a):
        return a.T.reshape(d_rec, n_grp, group).transpose(1, 0, 2)

    col_spec = pl.BlockSpec((None, d_rec, group), lambda g: (g, 0, 0))
    st_spec = pl.BlockSpec((group, N_HEADS, HEAD, HEAD), lambda g: (g, 0, 0, 0))
    row_spec = pl.BlockSpec((group, d_rec), lambda g: (g, 0))
    new_rec_s, o_s = pl.pallas_call(
        functools.partial(_sample_state_kernel, group),
        grid=(n_grp,),
        in_specs=[col_spec, col_spec, col_spec, row_spec, st_spec],
        out_specs=[st_spec, row_spec],
        out_shape=[jax.ShapeDtypeStruct(state_rec.shape[1:], F32), wide],
        compiler_params=pltpu.CompilerParams(dimension_semantics=("arbitrary",),
                                             vmem_limit_bytes=VMEM_LIMIT),
        name="sample_state",
    )(cols(f_s), cols(k_s), cols(q_s), v_s, state_rec[layer])

    decode = pl.pallas_call(
        _sample_tail_kernel,
        out_shape=[jax.ShapeDtypeStruct((ns, d), F32), jax.ShapeDtypeStruct((ns, d), F32),
                   jax.ShapeDtypeStruct((ns, n_exp), F32), jax.ShapeDtypeStruct((ns, n_exp), F32)],
        compiler_params=pltpu.CompilerParams(vmem_limit_bytes=VMEM_LIMIT),
        name="sample_tail",
    )(xs, y_conv_s, o_s, og_s, rg, wout, nf, rw, rb)

    h_all, xn_all, sel_all, gm_all, new_rec_p, new_conv_p = _prompt_call(
        x_prompt, st_meta, cv_meta, mix_w, tail_w, decode, layer, PROMPT_TILE, CHUNK)

    canvas = jax.ShapeDtypeStruct((n_tok, LANES), F32)
    tok_spec = pl.BlockSpec((RANK_TILE, n_exp), lambda i: (i, 0))
    can_spec = pl.BlockSpec((RANK_TILE, LANES), lambda i: (i, 0))
    eid, rnk, gate, counts = pl.pallas_call(
        _rank_kernel,
        grid=(n_tok // RANK_TILE,),
        in_specs=[tok_spec, tok_spec],
        out_specs=[can_spec, can_spec, can_spec, pl.BlockSpec((1, n_exp), lambda i: (0, 0))],
        out_shape=[canvas, canvas, canvas, jax.ShapeDtypeStruct((1, n_exp), F32)],
        scratch_shapes=[pltpu.VMEM((1, n_exp), F32)],
        compiler_params=pltpu.CompilerParams(dimension_semantics=("arbitrary",),
                                             vmem_limit_bytes=VMEM_LIMIT),
        name="rank",
    )(sel_all, gm_all)

    tm = MOE_TILE
    n_tiles = (n_tok * TOP_K) // tm + n_exp
    counts = counts[0].astype(jnp.int32)
    tiles_e = (counts + tm - 1) // tm
    tile_end = jnp.cumsum(tiles_e)
    n_used = tile_end[-1]
    offs = (tile_end - tiles_e) * tm
    eid4 = eid[:, :TOP_K].astype(jnp.int32)
    experts = jnp.arange(n_exp, dtype=jnp.int32)
    off4 = jnp.sum(jnp.where(eid4[..., None] == experts, offs, 0), axis=-1)
    pos = off4 + rnk[:, :TOP_K].astype(jnp.int32)
    tile_ids = jnp.minimum(jnp.arange(n_tiles, dtype=jnp.int32), n_used - 1)
    tile_expert = jnp.minimum(jnp.sum((tile_end[None, :] <= tile_ids[:, None]).astype(jnp.int32), axis=1),
                              n_exp - 1)

    n_rows = xn_all.shape[0]
    n_workers = SC_CORES * SC_SUBCORES
    assert n_rows % (n_workers * DISPATCH_CHUNK) == 0
    n_spare = (n_rows - n_tok) * TOP_K
    n_slots = n_tiles * tm + n_spare
    spare = n_tiles * tm + jnp.arange(n_spare, dtype=jnp.int32).reshape(-1, TOP_K)
    pos_w = jnp.concatenate([pos, spare], axis=0).reshape(n_workers, -1, DISPATCH_CHUNK, TOP_K)
    pos_w = pos_w.transpose(0, 1, 3, 2)
    xs = _dispatch(xn_all, pos_w, n_slots)

    w1 = expert_w1[layer]
    w2 = expert_w2[layer]
    b1 = expert_b1[layer]
    b1g = b1[:, 0::2][:, None, :]
    b1l = b1[:, 1::2][:, None, :]
    b2 = expert_b2[layer][:, None, :]
    ys = pl.pallas_call(
        functools.partial(_moe_kernel, tm),
        grid_spec=pltpu.PrefetchScalarGridSpec(
            num_scalar_prefetch=2,
            grid=(n_tiles,),
            in_specs=[
                pl.BlockSpec((tm, d // 2), lambda i, te, nu: (jnp.minimum(i, nu[0] - 1), 0)),
                pl.BlockSpec((None, d, 2 * d_ff), lambda i, te, nu: (te[i], 0, 0)),
                pl.BlockSpec((None, 1, d_ff), lambda i, te, nu: (te[i], 0, 0)),
                pl.BlockSpec((None, 1, d_ff), lambda i, te, nu: (te[i], 0, 0)),
                pl.BlockSpec((None, d_ff, d), lambda i, te, nu: (te[i], 0, 0)),
                pl.BlockSpec((None, 1, d), lambda i, te, nu: (te[i], 0, 0)),
            ],
            out_specs=pl.BlockSpec((tm, d // 2), lambda i, te, nu: (i, 0)),
            scratch_shapes=[pltpu.VMEM((d, 2 * d_ff), BF16), pltpu.VMEM((d_ff, d), BF16),
                            pltpu.VMEM((tm, d_ff), BF16)],
        ),
        out_shape=jax.ShapeDtypeStruct((n_slots, d // 2), jnp.uint32),
        compiler_params=pltpu.CompilerParams(dimension_semantics=("arbitrary",),
                                             vmem_limit_bytes=VMEM_LIMIT),
        name="moe",
    )(tile_expert, n_used[None].astype(jnp.int32), xs, w1, b1g, b1l, w2, b2)

    z = _collect(ys, pos_w)
    y_p = _final_call(h_all, gate, norm_final[None], z, 0, n_prompt, FINAL_TILE)
    y_s = _final_call(h_all, gate, norm_final[None], z, n_prompt, ns, TOKEN_TILE)

    return (y_p.reshape(bp, seq, d), y_s.reshape(ns, 1, d),
            new_conv_p[None], new_rec_p[None],
            new_conv_s.reshape(1, ns, 2, d_conv), new_rec_s[None])
```

```python
import functools

import jax
import jax.numpy as jnp
from jax import lax
from jax.experimental import pallas as pl
from jax.experimental.pallas import tpu as pltpu
from jax.experimental.pallas import tpu_sc as plsc

F32 = jnp.float32
BF16 = jnp.bfloat16

N_HEADS = 4
HEAD = 128
N_META = 16
CHUNK = 64
TOP_K = 4
SWIGLU_LIMIT = 7.0
SWIGLU_ALPHA = 1.702
EPS = 1e-5

LANES = 128
MXU_N = 256
PROMPT_TILE = 256
TOKEN_TILE = 128
FINAL_TILE = 512
RANK_TILE = 384
MOE_TILE = 256
SC_CORES = 2
SC_SUBCORES = 16
DISPATCH_CHUNK = 24
SC_RING = 4
VMEM_LIMIT = 56 * 1024 * 1024


def _dot(a, b):
    return jnp.dot(a, b, preferred_element_type=F32)


def _dot_nt(a, b):
    return lax.dot_general(a, b, (((1,), (1,)), ((), ())), preferred_element_type=F32)


def _dot_tn(a, b):
    return lax.dot_general(a, b, (((0,), (0,)), ((), ())), preferred_element_type=F32)


def _split3(x):
    hi = x.astype(BF16)
    r = x - hi.astype(F32)
    mid = r.astype(BF16)
    lo = (r - mid.astype(F32)).astype(BF16)
    return hi, mid, lo


def _pack_halves(x):
    n = x.shape[-1] // 2
    lo = pltpu.bitcast(x[:, :n].astype(BF16).astype(F32), jnp.uint32)
    hi = pltpu.bitcast(x[:, n:].astype(BF16).astype(F32), jnp.uint32)
    return (lo >> 16) | (hi & jnp.uint32(0xFFFF0000))


def _unpack_halves(u):
    lo = pltpu.bitcast(u << 16, F32)
    hi = pltpu.bitcast(u & jnp.uint32(0xFFFF0000), F32)
    return jnp.concatenate([lo, hi], axis=-1)


def _sigmoid(x):
    return 1.0 / (1.0 + jnp.exp(-x))


def _rms(x, g):
    ms = jnp.mean(x * x, axis=-1, keepdims=True)
    return x * lax.rsqrt(ms + EPS) * g


def _project(x, nm_ref, win_ref):
    return _dot(_rms(x, nm_ref[...]).astype(BF16), win_ref[...])


def _split_u(u, d_conv, d_rec):
    pts = [0, d_conv, 2 * d_conv, 3 * d_conv, 3 * d_conv + d_rec, 3 * d_conv + 2 * d_rec,
           3 * d_conv + 3 * d_rec, 3 * d_conv + 4 * d_rec]
    return [u[:, pts[i]:pts[i + 1]] for i in range(7)]


def _lower_bound(rlb_ref, layer):
    r = rlb_ref[...]
    e = jnp.exp(r - jnp.max(r, axis=0, keepdims=True))
    return jnp.sum(e[0:layer + 1], axis=0, keepdims=True) / jnp.sum(e, axis=0, keepdims=True)


def _forget(fx, lb):
    f = lb + (1.0 - lb) * _sigmoid(fx)
    return f, 1.0 - f


def _rec_out(o, og, rg):
    parts = []
    for h in range(N_HEADS):
        oh = o[:, h * HEAD:(h + 1) * HEAD]
        parts.append(oh * lax.rsqrt(jnp.mean(oh * oh, axis=-1, keepdims=True) + EPS))
    return jnp.concatenate(parts, axis=-1) * rg * (og * _sigmoid(og))


def _route(logits):
    n = logits.shape[-1]
    lane = lax.broadcasted_iota(jnp.int32, logits.shape, 1).astype(F32)
    work = logits
    tops, hots = [], []
    for _ in range(TOP_K):
        m = jnp.max(work, axis=-1, keepdims=True)
        first = jnp.min(jnp.where(work == m, lane, float(n)), axis=-1, keepdims=True)
        hot = lane == first
        tops.append(m)
        hots.append(hot)
        work = jnp.where(hot, -jnp.inf, work)
    es = [jnp.exp(t - tops[0]) for t in tops]
    den = es[0]
    for e in es[1:]:
        den = den + e
    sel = jnp.zeros_like(logits)
    gm = jnp.zeros_like(logits)
    for hot, e in zip(hots, es):
        sel = sel + jnp.where(hot, 1.0, 0.0)
        gm = gm + jnp.where(hot, e / den, 0.0)
    return sel, gm


def _tail(x, y, wout_ref, nf_ref, rw_ref, rb_ref):
    h = x + _dot(y.astype(BF16), wout_ref[...])
    xn = _rms(h, nf_ref[...])
    xh = xn.astype(BF16)
    xl = (xn - xh.astype(F32)).astype(BF16)
    rw = rw_ref[...]
    wh = rw.astype(BF16)
    wl = (rw - wh.astype(F32)).astype(BF16)
    logits = _dot(xh, wh) + _dot(xh, wl) + _dot(xl, wh) + rb_ref[...]
    sel, gm = _route(logits)
    return h, xn, sel, gm


def _rec_chunk(q, kk, v, lf, st_ref, tri, causal):
    c = q.shape[0]
    hi, mid, lo = _split3(lf)
    b = _dot(tri, hi) + _dot(tri, mid) + _dot(tri, lo)
    eb = jnp.exp(b)
    qe = (q * eb).astype(BF16)
    ke = (kk * jnp.exp(-b)).astype(BF16)
    vb = v.astype(BF16)
    eb_last = eb[c - 1:c]
    outs = []
    for h in range(N_HEADS):
        sl = slice(h * HEAD, (h + 1) * HEAD)
        st = st_ref[h]
        sc = jnp.where(causal, _dot_nt(qe[:, sl], ke[:, sl]), 0.0)
        outs.append(_dot(sc.astype(BF16), vb[:, sl]) + _dot_nt(qe[:, sl], st.astype(BF16)))
        st_ref[h] = (st + _dot_tn(vb[:, sl], ke[:, sl])) * eb_last[:, sl]
    return jnp.concatenate(outs, axis=-1)


def _mix_tile(u, layer, chunk, cw_ref, rlb_ref, convbuf, st, cv_out):
    tm = u.shape[0]
    d_conv = cw_ref.shape[-1]
    d_rec = rlb_ref.shape[-1]
    bg, cg, hv, q, fx, iv, og = _split_u(u, d_conv, d_rec)

    bx = bg * hv
    convbuf[8:8 + tm, :] = bx
    cw = cw_ref[...]
    conv = cw[0:1] * convbuf[6:6 + tm, :] + cw[1:2] * convbuf[7:7 + tm, :] + cw[2:3] * bx
    convbuf[6:8, :] = bx[tm - 2:tm]
    cv_out[...] = bx[tm - 2:tm]

    f, kk = _forget(fx, _lower_bound(rlb_ref, layer))
    lf = jnp.log(f)
    row = lax.broadcasted_iota(jnp.int32, (chunk, chunk), 0)
    col = lax.broadcasted_iota(jnp.int32, (chunk, chunk), 1)
    causal = row >= col
    tri = jnp.where(causal, 1.0, 0.0).astype(BF16)
    outs = []
    for c in range(tm // chunk):
        rs = slice(c * chunk, (c + 1) * chunk)
        outs.append(_rec_chunk(q[rs], kk[rs], iv[rs], lf[rs], st, tri, causal))
    return cg * conv, jnp.concatenate(outs, axis=0), og


def _load_state(st, convbuf, s0_ref, c0_ref):
    for h in range(N_HEADS):
        st[h] = s0_ref[h].T
    convbuf[6:8, :] = c0_ref[...]


def _meta_kernel(layer, x_ref, s0_ref, c0_ref, nm_ref, win_ref, cw_ref, rlb_ref, st_out, cv_out,
                 convbuf, st):
    _load_state(st, convbuf, s0_ref, c0_ref)
    _mix_tile(_project(x_ref[...], nm_ref, win_ref), layer, x_ref.shape[0], cw_ref, rlb_ref,
              convbuf, st, cv_out)
    for h in range(N_HEADS):
        st_out[h] = st[h].T


def _prompt_kernel(layer, tm, chunk, n_pairs, pairs_per_seq,
                   xr0_ref, x1_ref, xp2_ref, s0_ref, c0_ref, nm_ref, win_ref, cw_ref, rlb_ref,
                   rg_ref, wout_ref, nf_ref, rw_ref, rb_ref, hs_ref, xns_ref, sels_ref, gms_ref,
                   h_ref, xn_ref, sel_ref, gm_ref, st_out, cv_out, convbuf, st, ua, ub):
    step = pl.program_id(0)
    live = step < n_pairs

    @pl.when(step == 0)
    def _():
        ua[...] = _project(xr0_ref[...], nm_ref, win_ref)

    @pl.when(jnp.logical_and(lax.rem(step, pairs_per_seq) == 0, live))
    def _():
        _load_state(st, convbuf, s0_ref, c0_ref)

    @pl.when(live)
    def _():
        def finish(x, u, rows):
            y_conv, o, og = _mix_tile(u, layer, chunk, cw_ref, rlb_ref, convbuf, st, cv_out)
            y = jnp.concatenate([y_conv, _rec_out(o, og, rg_ref[...])], axis=-1)
            h, xn, sel, gm = _tail(x, y, wout_ref, nf_ref, rw_ref, rb_ref)
            h_ref[rows, :] = h
            xn_ref[rows, :] = _pack_halves(xn)
            sel_ref[rows, :] = sel
            gm_ref[rows, :] = gm

        ub[...] = _project(x1_ref[...], nm_ref, win_ref)
        finish(xr0_ref[...], ua[...], slice(0, tm))
        ua[...] = _project(xp2_ref[...], nm_ref, win_ref)
        finish(x1_ref[...], ub[...], slice(tm, 2 * tm))

    @pl.when(jnp.logical_and(lax.rem(step, pairs_per_seq) == pairs_per_seq - 1, live))
    def _():
        for h in range(N_HEADS):
            st_out[h] = st[h].T

    @pl.when(step == n_pairs)
    def _():
        ns = hs_ref.shape[0]
        for dst, val in ((h_ref, hs_ref[...]), (xn_ref, _pack_halves(xns_ref[...])),
                         (sel_ref, sels_ref[...]), (gm_ref, gms_ref[...])):
            dst[0:ns, :] = val
            dst[ns:2 * tm, :] = jnp.zeros((2 * tm - ns, dst.shape[-1]), dst.dtype)


def _const_spec(shape):
    return pl.BlockSpec(shape, lambda *_: (0,) * len(shape))


def _meta_call(x, weights, layer):
    nm, win, cw, rlb = weights
    d_conv = cw.shape[-1]
    s0 = jnp.zeros((N_HEADS, HEAD, HEAD), F32)
    c0 = jnp.zeros((2, d_conv), F32)
    return pl.pallas_call(
        functools.partial(_meta_kernel, layer),
        out_shape=[jax.ShapeDtypeStruct(s0.shape, F32), jax.ShapeDtypeStruct(c0.shape, F32)],
        scratch_shapes=[pltpu.VMEM((x.shape[0] + 8, d_conv), F32), pltpu.VMEM(s0.shape, F32)],
        compiler_params=pltpu.CompilerParams(vmem_limit_bytes=VMEM_LIMIT),
        name="mixer_meta",
    )(x, s0, c0, nm, win, cw, rlb)


def _prompt_call(x, s0, c0, weights, tail_w, decode, layer, tm, chunk):
    nseq, length, d = x.shape
    nt = length // tm
    assert nt % 2 == 0 and decode[0].shape[0] <= 2 * tm
    pairs_per_seq = nt // 2
    n_pairs = nseq * pairs_per_seq
    nm, win, cw, rlb = weights
    d_conv = cw.shape[-1]
    d_in = win.shape[-1]

    def tile_spec(offset):
        def index(s):
            tile = jnp.minimum(2 * s + offset, 2 * n_pairs - 1)
            return (tile // nt, lax.rem(tile, nt), 0)
        return pl.BlockSpec((None, tm, d), index)

    def seq_of(s):
        return jnp.minimum(s // pairs_per_seq, nseq - 1)

    consts = [s0, c0, nm, win, cw, rlb] + list(tail_w) + list(decode)
    tok = [(a.shape[-1], F32) for a in decode]
    tok[1] = (tok[1][0] // 2, jnp.uint32)
    n_steps = n_pairs + 1
    return pl.pallas_call(
        functools.partial(_prompt_kernel, layer, tm, chunk, n_pairs, pairs_per_seq),
        grid=(n_steps,),
        in_specs=[tile_spec(0), tile_spec(1), tile_spec(2)] + [_const_spec(a.shape) for a in consts],
        out_specs=[pl.BlockSpec((2 * tm, w), lambda s: (s, 0)) for w, _ in tok] + [
            pl.BlockSpec((None,) + s0.shape, lambda s: (seq_of(s), 0, 0, 0)),
            pl.BlockSpec((None,) + c0.shape, lambda s: (seq_of(s), 0, 0))],
        out_shape=[jax.ShapeDtypeStruct((n_steps * 2 * tm, w), t) for w, t in tok] + [
            jax.ShapeDtypeStruct((nseq,) + s0.shape, F32), jax.ShapeDtypeStruct((nseq,) + c0.shape, F32)],
        scratch_shapes=[pltpu.VMEM((tm + 8, d_conv), F32), pltpu.VMEM(s0.shape, F32),
                        pltpu.VMEM((tm, d_in), F32), pltpu.VMEM((tm, d_in), F32)],
        compiler_params=pltpu.CompilerParams(dimension_semantics=("arbitrary",),
                                             vmem_limit_bytes=VMEM_LIMIT),
        name="mixer_prompt",
    )(x, x, x, *consts)


def _sample_in_kernel(layer, x_ref, sc_ref, nm_ref, win_ref, cw_ref, rlb_ref,
                      yc_ref, nc_ref, f_ref, k_ref, q_ref, v_ref, og_ref):
    d_conv = cw_ref.shape[-1]
    d_rec = rlb_ref.shape[-1]
    u = _project(x_ref[...], nm_ref, win_ref)
    bg, cg, hv, q, fx, iv, og = _split_u(u, d_conv, d_rec)
    bx = bg * hv
    sc = sc_ref[...]
    s0, s1 = sc[:, :d_conv], sc[:, d_conv:]
    cw = cw_ref[...]
    yc_ref[...] = cg * (cw[0:1] * s0 + cw[1:2] * s1 + cw[2:3] * bx)
    nc_ref[...] = jnp.concatenate([s1, bx], axis=-1)
    f, kk = _forget(fx, _lower_bound(rlb_ref, layer))
    f_ref[...] = f
    k_ref[...] = kk
    q_ref[...] = q
    v_ref[...] = iv
    og_ref[...] = og


def _sample_state_kernel(group, f_ref, k_ref, q_ref, v_ref, s_ref, sn_ref, o_ref):
    for j in range(group):
        for h in range(N_HEADS):
            rs = slice(h * HEAD, (h + 1) * HEAD)
            fcol = f_ref[rs, j:j + 1]
            kcol = k_ref[rs, j:j + 1]
            qcol = q_ref[rs, j:j + 1]
            vrow = v_ref[j:j + 1, rs]
            sn = fcol * s_ref[j, h] + kcol * vrow
            sn_ref[j, h] = sn
            o_ref[j:j + 1, rs] = jnp.sum(qcol * sn, axis=0, keepdims=True)


def _sample_tail_kernel(x_ref, yc_ref, o_ref, og_ref, rg_ref, wout_ref, nf_ref, rw_ref, rb_ref,
                        h_ref, xn_ref, sel_ref, gm_ref):
    y = jnp.concatenate([yc_ref[...], _rec_out(o_ref[...], og_ref[...], rg_ref[...])], axis=-1)
    h, xn, sel, gm = _tail(x_ref[...], y, wout_ref, nf_ref, rw_ref, rb_ref)
    h_ref[...] = h
    xn_ref[...] = xn
    sel_ref[...] = sel
    gm_ref[...] = gm


def _rank_kernel(sel_ref, gm_ref, eid_ref, rank_ref, gate_ref, cnt_ref, carry):
    i = pl.program_id(0)

    @pl.when(i == 0)
    def _():
        carry[...] = jnp.zeros_like(carry)

    sel = sel_ref[...]
    gm = gm_ref[...]
    tb, ne = sel.shape
    row = lax.broadcasted_iota(jnp.int32, (tb, tb), 0)
    col = lax.broadcasted_iota(jnp.int32, (tb, tb), 1)
    before = jnp.where(col < row, 1.0, 0.0).astype(BF16)
    selb = sel.astype(BF16)
    rank = _dot(before, selb) + carry[...]
    carry[...] = carry[...] + jnp.sum(sel, axis=0, keepdims=True)
    cnt_ref[...] = carry[...]
    er = lax.broadcasted_iota(jnp.int32, (ne, ne), 0)
    ec = lax.broadcasted_iota(jnp.int32, (ne, ne), 1)
    lower = jnp.where(er < ec, 1.0, 0.0).astype(BF16)
    order = _dot(selb, lower)
    lane_e = lax.broadcasted_iota(jnp.int32, (tb, ne), 1).astype(F32)
    lane = lax.broadcasted_iota(jnp.int32, (tb, LANES), 1)
    eid = jnp.zeros((tb, LANES), F32)
    rnk = jnp.zeros((tb, LANES), F32)
    gat = jnp.zeros((tb, LANES), F32)
    for k in range(TOP_K):
        pick = jnp.where(order == float(k), sel, 0.0)
        eid = jnp.where(lane == k, jnp.sum(pick * lane_e, axis=-1, keepdims=True), eid)
        rnk = jnp.where(lane == k, jnp.sum(pick * rank, axis=-1, keepdims=True), rnk)
        gat = jnp.where(lane == k, jnp.sum(pick * gm, axis=-1, keepdims=True), gat)
    eid_ref[...] = eid
    rank_ref[...] = rnk
    gate_ref[...] = gat


def _dispatch(xn, pos_w, n_slots):
    n_workers, n_chunks, top_k, ch = pos_w.shape
    assert n_workers == SC_CORES * SC_SUBCORES and ch % 8 == 0 and ch <= LANES
    d = xn.shape[1]
    mesh = plsc.VectorSubcoreMesh(core_axis_name="c", subcore_axis_name="s")

    @functools.partial(
        pl.kernel, mesh=mesh,
        out_type=jax.ShapeDtypeStruct((n_slots, d), xn.dtype),
        scratch_types=[pltpu.VMEM((n_chunks, top_k, ch), jnp.int32), pltpu.VMEM((SC_RING, ch, d), xn.dtype),
                       pltpu.SemaphoreType.DMA((SC_RING,)), pltpu.SemaphoreType.DMA((SC_RING,))],
        name="dispatch",
    )
    def run(xn_hbm, pos_hbm, xs_hbm, idx_v, rows_v, sem_r, sem_w):
        wid = lax.axis_index("s") * SC_CORES + lax.axis_index("c")

        def read(c, b):
            src = xn_hbm.at[pl.ds((wid * n_chunks + c) * ch, ch)]
            return pltpu.make_async_copy(src, rows_v.at[b], sem_r.at[b])

        def write(c, b, k):
            return pltpu.make_async_copy(rows_v.at[b], xs_hbm.at[idx_v.at[c, k]], sem_w.at[b])

        pltpu.sync_copy(pos_hbm.at[wid], idx_v)
        ahead = SC_RING - 1
        for c0 in range(min(ahead, n_chunks)):
            read(c0, c0).start()

        @pl.loop(0, n_chunks)
        def _(c):
            b = lax.rem(c, SC_RING)
            read(c, b).wait()
            for k in range(top_k):
                write(c, b, k).start()

            @pl.when(c >= 1)
            def _():
                for k in range(top_k):
                    write(c - 1, lax.rem(c - 1, SC_RING), k).wait()

            @pl.when(c + ahead < n_chunks)
            def _():
                read(c + ahead, lax.rem(c + ahead, SC_RING)).start()

        for k in range(top_k):
            write(n_chunks - 1, (n_chunks - 1) % SC_RING, k).wait()

    return run(xn, pos_w)


def _collect(ys, pos_w):
    n_workers, n_chunks, top_k, ch = pos_w.shape
    assert n_workers == SC_CORES * SC_SUBCORES and ch % 8 == 0 and ch <= LANES
    d = ys.shape[1]
    mesh = plsc.VectorSubcoreMesh(core_axis_name="c", subcore_axis_name="s")

    @functools.partial(
        pl.kernel, mesh=mesh,
        out_type=jax.ShapeDtypeStruct((n_workers * n_chunks * ch, top_k * d), ys.dtype),
        scratch_types=[pltpu.VMEM((n_chunks, top_k, ch), jnp.int32), pltpu.VMEM((top_k, ch, d), ys.dtype),
                       pltpu.SemaphoreType.DMA((top_k,)), pltpu.SemaphoreType.DMA((top_k,))],
        name="collect",
    )
    def run(ys_hbm, pos_hbm, out_hbm, idx_v, rows_v, sem_r, sem_w):
        wid = lax.axis_index("s") * SC_CORES + lax.axis_index("c")

        def read(c, k):
            return pltpu.make_async_copy(ys_hbm.at[idx_v.at[c, k]], rows_v.at[k], sem_r.at[k])

        def write(c, k):
            dst = out_hbm.at[pl.ds((wid * n_chunks + c) * ch, ch), pl.ds(k * d, d)]
            return pltpu.make_async_copy(rows_v.at[k], dst, sem_w.at[k])

        pltpu.sync_copy(pos_hbm.at[wid], idx_v)
        for k in range(top_k - 1):
            read(0, k).start()

        @pl.loop(0, n_chunks)
        def _(c):
            for k in range(top_k):
                read(c, k).wait()
                write(c, k).start()
                if k >= 1:
                    write(c, k - 1).wait()

                    @pl.when(c + 1 < n_chunks)
                    def _():
                        read(c + 1, k - 1).start()
                else:
                    @pl.when(c >= 1)
                    def _():
                        write(c - 1, top_k - 1).wait()
                    read(c, top_k - 1).start()

        write(n_chunks - 1, top_k - 1).wait()

    return run(ys, pos_w)


def _moe_kernel(tm, te_ref, nu_ref, tn_ref, x_ref, w1_hbm, bg_ref, bl_ref, w2_hbm, b2_ref,
                ys_ref, w1f, w2f, sem, w1p, w2b, act):
    i = pl.program_id(0)
    n_used = nu_ref[0]
    d_ff2 = w1f.shape[-1]
    n_blk = d_ff2 // MXU_N
    expert = te_ref[i]

    def weight_copies(e):
        return (pltpu.make_async_copy(w1_hbm.at[e], w1f, sem.at[0]),
                pltpu.make_async_copy(w2_hbm.at[e], w2f, sem.at[1]))

    @pl.when(i == 0)
    def _():
        for cp in weight_copies(expert):
            cp.start()

    @pl.when(i >= n_used)
    def _():
        ys_ref[...] = jnp.zeros_like(ys_ref)

    prev = te_ref[jnp.maximum(i - 1, 0)]
    changed = jnp.logical_or(i == 0, expert != prev)

    @pl.when(jnp.logical_and(changed, i < n_used))
    def _():
        for cp in weight_copies(expert):
            cp.wait()
        r = lax.broadcasted_iota(jnp.int32, (MXU_N, MXU_N), 0)
        c = lax.broadcasted_iota(jnp.int32, (MXU_N, MXU_N), 1)
        src = jnp.where(c < MXU_N // 2, 2 * c, 2 * (c - MXU_N // 2) + 1)
        perm = jnp.where(r == src, 1.0, 0.0).astype(BF16)
        for blk in range(n_blk):
            cs = slice(blk * MXU_N, (blk + 1) * MXU_N)
            w1p[:, cs] = _dot(w1f[:, cs].astype(BF16), perm).astype(BF16)
        w2b[...] = w2f[...].astype(BF16)

        @pl.when(tn_ref[i] != expert)
        def _():
            for cp in weight_copies(tn_ref[i]):
                cp.start()

    @pl.when(i < n_used)
    def _():
        x = _unpack_halves(x_ref[...]).astype(BF16)
        for blk in range(n_blk):
            a = _dot(x, w1p[:, blk * MXU_N:(blk + 1) * MXU_N])
            half = MXU_N // 2
            hs = slice(blk * half, (blk + 1) * half)
            glu = jnp.minimum(a[:, :half] + bg_ref[:, hs], SWIGLU_LIMIT)
            lin = jnp.clip(a[:, half:] + bl_ref[:, hs], -SWIGLU_LIMIT, SWIGLU_LIMIT)
            act[:, hs] = (glu * _sigmoid(SWIGLU_ALPHA * glu) * (lin + 1.0)).astype(BF16)
        ys_ref[...] = _pack_halves(_dot(act[...], w2b[...]) + b2_ref[...])


def _final_kernel(h_ref, gate_ref, nfin_ref, z_ref, y_ref):
    w = z_ref.shape[-1] // TOP_K
    g = gate_ref[...]
    out = h_ref[...]
    for k in range(TOP_K):
        out = out + g[:, k:k + 1] * _unpack_halves(z_ref[:, k * w:(k + 1) * w])
    y_ref[...] = _rms(out, nfin_ref[...])


def _final_call(h_all, gate, nfin, z, first_row, n_rows, tile):
    d = h_all.shape[-1]
    off = first_row // tile
    assert first_row % tile == 0 and n_rows % tile == 0
    return pl.pallas_call(
        _final_kernel,
        grid=(n_rows // tile,),
        in_specs=[
            pl.BlockSpec((tile, d), lambda i: (i + off, 0)),
            pl.BlockSpec((tile, gate.shape[-1]), lambda i: (i + off, 0)),
            pl.BlockSpec((1, d), lambda i: (0, 0)),
            pl.BlockSpec((tile, z.shape[-1]), lambda i: (i + off, 0)),
        ],
        out_specs=pl.BlockSpec((tile, d), lambda i: (i, 0)),
        out_shape=jax.ShapeDtypeStruct((n_rows, d), F32),
        compiler_params=pltpu.CompilerParams(dimension_semantics=("arbitrary",),
                                             vmem_limit_bytes=VMEM_LIMIT),
        name="final",
    )(h_all, gate, nfin, z)


def kernel(x_prompt, x_sample, state_conv, state_rec, meta_tokens, norm_mix, w_in, conv_w,
           rec_lower_bound, rec_norm, w_out, norm_ffn, router_w, router_b, expert_w1, expert_b1,
           expert_w2, expert_b2, norm_final):
    depth = norm_mix.shape[0]
    assert depth == 1, "single-layer step"
    layer = 0
    bp, seq, d = x_prompt.shape
    ns = x_sample.shape[0]
    assert x_sample.shape[1] == 1 and ns == TOKEN_TILE
    d_conv = conv_w.shape[-1]
    d_rec = rec_lower_bound.shape[-1]
    assert state_conv.shape[2] == 2 and d_rec == N_HEADS * HEAD
    n_exp = router_w.shape[-1]
    d_ff = expert_w2.shape[2]
    n_prompt = bp * seq
    n_tok = n_prompt + ns
    assert seq % PROMPT_TILE == 0 and n_prompt % TOKEN_TILE == 0 and n_tok % RANK_TILE == 0

    nm = norm_mix[layer][None]
    win = w_in[layer].astype(BF16)
    cw = conv_w[layer]
    rlb = rec_lower_bound
    rg = rec_norm[layer][None]
    wout = w_out[layer].astype(BF16)
    nf = norm_ffn[layer][None]
    rw = router_w[layer]
    rb = router_b[layer][None]
    mix_w = (nm, win, cw, rlb)
    tail_w = (rg, wout, nf, rw, rb)

    st_meta, cv_meta = _meta_call(meta_tokens, mix_w, layer)

    xs = x_sample.reshape(ns, d)
    wide = jax.ShapeDtypeStruct((ns, d_rec), F32)
    y_conv_s, new_conv_s, f_s, k_s, q_s, v_s, og_s = pl.pallas_call(
        functools.partial(_sample_in_kernel, layer),
        out_shape=[jax.ShapeDtypeStruct((ns, d_conv), F32), jax.ShapeDtypeStruct((ns, 2 * d_conv), F32),
                   wide, wide, wide, wide, wide],
        compiler_params=pltpu.CompilerParams(vmem_limit_bytes=VMEM_LIMIT),
        name="sample_in",
    )(xs, state_conv[layer].reshape(ns, 2 * d_conv), nm, win, cw, rlb)

    group = 8
    n_grp = ns // group

    def cols(a):
        return a.T.reshape(d_rec, n_grp, group).transpose(1, 0, 2)

    col_spec = pl.BlockSpec((None, d_rec, group), lambda g: (g, 0, 0))
    st_spec = pl.BlockSpec((group, N_HEADS, HEAD, HEAD), lambda g: (g, 0, 0, 0))
    row_spec = pl.BlockSpec((group, d_rec), lambda g: (g, 0))
    new_rec_s, o_s = pl.pallas_call(
        functools.partial(_sample_state_kernel, group),
        grid=(n_grp,),
        in_specs=[col_spec, col_spec, col_spec, row_spec, st_spec],
        out_specs=[st_spec, row_spec],
        out_shape=[jax.ShapeDtypeStruct(state_rec.shape[1:], F32), wide],
        compiler_params=pltpu.CompilerParams(dimension_semantics=("arbitrary",),
                                             vmem_limit_bytes=VMEM_LIMIT),
        name="sample_state",
    )(cols(f_s), cols(k_s), cols(q_s), v_s, state_rec[layer])

    decode = pl.pallas_call(
        _sample_tail_kernel,
        out_shape=[jax.ShapeDtypeStruct((ns, d), F32), jax.ShapeDtypeStruct((ns, d), F32),
                   jax.ShapeDtypeStruct((ns, n_exp), F32), jax.ShapeDtypeStruct((ns, n_exp), F32)],
        compiler_params=pltpu.CompilerParams(vmem_limit_bytes=VMEM_LIMIT),
        name="sample_tail",
    )(xs, y_conv_s, o_s, og_s, rg, wout, nf, rw, rb)

    h_all, xn_all, sel_all, gm_all, new_rec_p, new_conv_p = _prompt_call(
        x_prompt, st_meta, cv_meta, mix_w, tail_w, decode, layer, PROMPT_TILE, CHUNK)

    canvas = jax.ShapeDtypeStruct((n_tok, LANES), F32)
    tok_spec = pl.BlockSpec((RANK_TILE, n_exp), lambda i: (i, 0))
    can_spec = pl.BlockSpec((RANK_TILE, LANES), lambda i: (i, 0))
    eid, rnk, gate, counts = pl.pallas_call(
        _rank_kernel,
        grid=(n_tok // RANK_TILE,),
        in_specs=[tok_spec, tok_spec],
        out_specs=[can_spec, can_spec, can_spec, pl.BlockSpec((1, n_exp), lambda i: (0, 0))],
        out_shape=[canvas, canvas, canvas, jax.ShapeDtypeStruct((1, n_exp), F32)],
        scratch_shapes=[pltpu.VMEM((1, n_exp), F32)],
        compiler_params=pltpu.CompilerParams(dimension_semantics=("arbitrary",),
                                             vmem_limit_bytes=VMEM_LIMIT),
        name="rank",
    )(sel_all, gm_all)

    tm = MOE_TILE
    n_tiles = (n_tok * TOP_K) // tm + n_exp
    counts = counts[0].astype(jnp.int32)
    tiles_e = (counts + tm - 1) // tm
    tile_end = jnp.cumsum(tiles_e)
    n_used = tile_end[-1]
    offs = (tile_end - tiles_e) * tm
    eid4 = eid[:, :TOP_K].astype(jnp.int32)
    experts = jnp.arange(n_exp, dtype=jnp.int32)
    off4 = jnp.sum(jnp.where(eid4[..., None] == experts, offs, 0), axis=-1)
    pos = off4 + rnk[:, :TOP_K].astype(jnp.int32)
    tile_ids = jnp.minimum(jnp.arange(n_tiles, dtype=jnp.int32), n_used - 1)
    tile_expert = jnp.minimum(jnp.sum((tile_end[None, :] <= tile_ids[:, None]).astype(jnp.int32), axis=1),
                              n_exp - 1)
    after = tile_end[tile_expert]
    next_expert = jnp.where(after < n_used, tile_expert[jnp.minimum(after, n_tiles - 1)], tile_expert)

    n_rows = xn_all.shape[0]
    n_workers = SC_CORES * SC_SUBCORES
    assert n_rows % (n_workers * DISPATCH_CHUNK) == 0
    n_spare = (n_rows - n_tok) * TOP_K
    n_slots = n_tiles * tm + n_spare
    spare = n_tiles * tm + jnp.arange(n_spare, dtype=jnp.int32).reshape(-1, TOP_K)
    pos_w = jnp.concatenate([pos, spare], axis=0).reshape(n_workers, -1, DISPATCH_CHUNK, TOP_K)
    pos_w = pos_w.transpose(0, 1, 3, 2)
    xs = _dispatch(xn_all, pos_w, n_slots)

    w1 = expert_w1[layer]
    w2 = expert_w2[layer]
    b1 = expert_b1[layer]
    b1g = b1[:, 0::2][:, None, :]
    b1l = b1[:, 1::2][:, None, :]
    b2 = expert_b2[layer][:, None, :]
    ys = pl.pallas_call(
        functools.partial(_moe_kernel, tm),
        grid_spec=pltpu.PrefetchScalarGridSpec(
            num_scalar_prefetch=3,
            grid=(n_tiles,),
            in_specs=[
                pl.BlockSpec((tm, d // 2), lambda i, te, nu, tn: (jnp.minimum(i, nu[0] - 1), 0)),
                pl.BlockSpec(memory_space=pl.ANY),
                pl.BlockSpec((None, 1, d_ff), lambda i, te, nu, tn: (te[i], 0, 0)),
                pl.BlockSpec((None, 1, d_ff), lambda i, te, nu, tn: (te[i], 0, 0)),
                pl.BlockSpec(memory_space=pl.ANY),
                pl.BlockSpec((None, 1, d), lambda i, te, nu, tn: (te[i], 0, 0)),
            ],
            out_specs=pl.BlockSpec((tm, d // 2), lambda i, te, nu, tn: (i, 0)),
            scratch_shapes=[pltpu.VMEM((d, 2 * d_ff), F32), pltpu.VMEM((d_ff, d), F32),
                            pltpu.SemaphoreType.DMA((2,)),
                            pltpu.VMEM((d, 2 * d_ff), BF16), pltpu.VMEM((d_ff, d), BF16),
                            pltpu.VMEM((tm, d_ff), BF16)],
        ),
        out_shape=jax.ShapeDtypeStruct((n_slots, d // 2), jnp.uint32),
        compiler_params=pltpu.CompilerParams(dimension_semantics=("arbitrary",),
                                             vmem_limit_bytes=VMEM_LIMIT),
        name="moe",
    )(tile_expert, n_used[None].astype(jnp.int32), next_expert, xs, w1, b1g, b1l, w2, b2)

    z = _collect(ys, pos_w)
    y_p = _final_call(h_all, gate, norm_final[None], z, 0, n_prompt, FINAL_TILE)
    y_s = _final_call(h_all, gate, norm_final[None], z, n_prompt, ns, TOKEN_TILE)

    return (y_p.reshape(bp, seq, d), y_s.reshape(ns, 1, d),
            new_conv_p[None], new_rec_p[None],
            new_conv_s.reshape(1, ns, 2, d_conv), new_rec_s[None])
```

```python
import functools

import jax
import jax.numpy as jnp
from jax import lax
from jax.experimental import pallas as pl
from jax.experimental.pallas import tpu as pltpu
from jax.experimental.pallas import tpu_sc as plsc

F32 = jnp.float32
BF16 = jnp.bfloat16

N_HEADS = 4
HEAD = 128
N_META = 16
CHUNK = 64
TOP_K = 4
SWIGLU_LIMIT = 7.0
SWIGLU_ALPHA = 1.702
EPS = 1e-5

LANES = 128
MXU_N = 256
PROMPT_TILE = 256
TOKEN_TILE = 128
FINAL_TILE = 512
RANK_TILE = 384
MOE_TILE = 512
SC_CORES = 2
SC_SUBCORES = 16
DISPATCH_CHUNK = 24
SC_RING = 4
VMEM_LIMIT = 56 * 1024 * 1024


def _dot(a, b):
    return jnp.dot(a, b, preferred_element_type=F32)


def _dot_nt(a, b):
    return lax.dot_general(a, b, (((1,), (1,)), ((), ())), preferred_element_type=F32)


def _dot_tn(a, b):
    return lax.dot_general(a, b, (((0,), (0,)), ((), ())), preferred_element_type=F32)


def _split3(x):
    hi = x.astype(BF16)
    r = x - hi.astype(F32)
    mid = r.astype(BF16)
    lo = (r - mid.astype(F32)).astype(BF16)
    return hi, mid, lo


def _pack_halves(x):
    n = x.shape[-1] // 2
    lo = pltpu.bitcast(x[:, :n].astype(BF16).astype(F32), jnp.uint32)
    hi = pltpu.bitcast(x[:, n:].astype(BF16).astype(F32), jnp.uint32)
    return (lo >> 16) | (hi & jnp.uint32(0xFFFF0000))


def _unpack_halves(u):
    lo = pltpu.bitcast(u << 16, F32)
    hi = pltpu.bitcast(u & jnp.uint32(0xFFFF0000), F32)
    return jnp.concatenate([lo, hi], axis=-1)


def _sigmoid(x):
    return 1.0 / (1.0 + jnp.exp(-x))


def _rms(x, g):
    ms = jnp.mean(x * x, axis=-1, keepdims=True)
    return x * lax.rsqrt(ms + EPS) * g


def _project(x, nm_ref, win_ref):
    return _dot(_rms(x, nm_ref[...]).astype(BF16), win_ref[...])


def _split_u(u, d_conv, d_rec):
    pts = [0, d_conv, 2 * d_conv, 3 * d_conv, 3 * d_conv + d_rec, 3 * d_conv + 2 * d_rec,
           3 * d_conv + 3 * d_rec, 3 * d_conv + 4 * d_rec]
    return [u[:, pts[i]:pts[i + 1]] for i in range(7)]


def _lower_bound(rlb_ref, layer):
    r = rlb_ref[...]
    e = jnp.exp(r - jnp.max(r, axis=0, keepdims=True))
    return jnp.sum(e[0:layer + 1], axis=0, keepdims=True) / jnp.sum(e, axis=0, keepdims=True)


def _forget(fx, lb):
    f = lb + (1.0 - lb) * _sigmoid(fx)
    return f, 1.0 - f


def _rec_out(o, og, rg):
    parts = []
    for h in range(N_HEADS):
        oh = o[:, h * HEAD:(h + 1) * HEAD]
        parts.append(oh * lax.rsqrt(jnp.mean(oh * oh, axis=-1, keepdims=True) + EPS))
    return jnp.concatenate(parts, axis=-1) * rg * (og * _sigmoid(og))


def _route(logits):
    n = logits.shape[-1]
    lane = lax.broadcasted_iota(jnp.int32, logits.shape, 1).astype(F32)
    work = logits
    tops, hots = [], []
    for _ in range(TOP_K):
        m = jnp.max(work, axis=-1, keepdims=True)
        first = jnp.min(jnp.where(work == m, lane, float(n)), axis=-1, keepdims=True)
        hot = lane == first
        tops.append(m)
        hots.append(hot)
        work = jnp.where(hot, -jnp.inf, work)
    es = [jnp.exp(t - tops[0]) for t in tops]
    den = es[0]
    for e in es[1:]:
        den = den + e
    sel = jnp.zeros_like(logits)
    gm = jnp.zeros_like(logits)
    for hot, e in zip(hots, es):
        sel = sel + jnp.where(hot, 1.0, 0.0)
        gm = gm + jnp.where(hot, e / den, 0.0)
    return sel, gm


def _tail(x, y, wout_ref, nf_ref, rw_ref, rb_ref):
    h = x + _dot(y.astype(BF16), wout_ref[...])
    xn = _rms(h, nf_ref[...])
    xh = xn.astype(BF16)
    xl = (xn - xh.astype(F32)).astype(BF16)
    rw = rw_ref[...]
    wh = rw.astype(BF16)
    wl = (rw - wh.astype(F32)).astype(BF16)
    logits = _dot(xh, wh) + _dot(xh, wl) + _dot(xl, wh) + rb_ref[...]
    sel, gm = _route(logits)
    return h, xn, sel, gm


def _rec_chunk(q, kk, v, lf, st_ref, tri, causal):
    c = q.shape[0]
    hi, mid, lo = _split3(lf)
    b = _dot(tri, hi) + _dot(tri, mid) + _dot(tri, lo)
    eb = jnp.exp(b)
    qe = (q * eb).astype(BF16)
    ke = (kk * jnp.exp(-b)).astype(BF16)
    vb = v.astype(BF16)
    eb_last = eb[c - 1:c]
    outs = []
    for h in range(N_HEADS):
        sl = slice(h * HEAD, (h + 1) * HEAD)
        st = st_ref[h]
        sc = jnp.where(causal, _dot_nt(qe[:, sl], ke[:, sl]), 0.0)
        outs.append(_dot(sc.astype(BF16), vb[:, sl]) + _dot_nt(qe[:, sl], st.astype(BF16)))
        st_ref[h] = (st + _dot_tn(vb[:, sl], ke[:, sl])) * eb_last[:, sl]
    return jnp.concatenate(outs, axis=-1)


def _mix_tile(u, layer, chunk, cw_ref, rlb_ref, convbuf, st, cv_out):
    tm = u.shape[0]
    d_conv = cw_ref.shape[-1]
    d_rec = rlb_ref.shape[-1]
    bg, cg, hv, q, fx, iv, og = _split_u(u, d_conv, d_rec)

    bx = bg * hv
    convbuf[8:8 + tm, :] = bx
    cw = cw_ref[...]
    conv = cw[0:1] * convbuf[6:6 + tm, :] + cw[1:2] * convbuf[7:7 + tm, :] + cw[2:3] * bx
    convbuf[6:8, :] = bx[tm - 2:tm]
    cv_out[...] = bx[tm - 2:tm]

    f, kk = _forget(fx, _lower_bound(rlb_ref, layer))
    lf = jnp.log(f)
    row = lax.broadcasted_iota(jnp.int32, (chunk, chunk), 0)
    col = lax.broadcasted_iota(jnp.int32, (chunk, chunk), 1)
    causal = row >= col
    tri = jnp.where(causal, 1.0, 0.0).astype(BF16)
    outs = []
    for c in range(tm // chunk):
        rs = slice(c * chunk, (c + 1) * chunk)
        outs.append(_rec_chunk(q[rs], kk[rs], iv[rs], lf[rs], st, tri, causal))
    return cg * conv, jnp.concatenate(outs, axis=0), og


def _load_state(st, convbuf, s0_ref, c0_ref):
    for h in range(N_HEADS):
        st[h] = s0_ref[h].T
    convbuf[6:8, :] = c0_ref[...]


def _meta_kernel(layer, x_ref, s0_ref, c0_ref, nm_ref, win_ref, cw_ref, rlb_ref, st_out, cv_out,
                 convbuf, st):
    _load_state(st, convbuf, s0_ref, c0_ref)
    _mix_tile(_project(x_ref[...], nm_ref, win_ref), layer, x_ref.shape[0], cw_ref, rlb_ref,
              convbuf, st, cv_out)
    for h in range(N_HEADS):
        st_out[h] = st[h].T


def _prompt_kernel(layer, tm, chunk, n_pairs, pairs_per_seq,
                   xr0_ref, x1_ref, xp2_ref, s0_ref, c0_ref, nm_ref, win_ref, cw_ref, rlb_ref,
                   rg_ref, wout_ref, nf_ref, rw_ref, rb_ref, hs_ref, xns_ref, sels_ref, gms_ref,
                   h_ref, xn_ref, sel_ref, gm_ref, st_out, cv_out, convbuf, st, ua, ub):
    step = pl.program_id(0)
    live = step < n_pairs

    @pl.when(step == 0)
    def _():
        ua[...] = _project(xr0_ref[...], nm_ref, win_ref)

    @pl.when(jnp.logical_and(lax.rem(step, pairs_per_seq) == 0, live))
    def _():
        _load_state(st, convbuf, s0_ref, c0_ref)

    @pl.when(live)
    def _():
        def finish(x, u, rows):
            y_conv, o, og = _mix_tile(u, layer, chunk, cw_ref, rlb_ref, convbuf, st, cv_out)
            y = jnp.concatenate([y_conv, _rec_out(o, og, rg_ref[...])], axis=-1)
            h, xn, sel, gm = _tail(x, y, wout_ref, nf_ref, rw_ref, rb_ref)
            h_ref[rows, :] = h
            xn_ref[rows, :] = _pack_halves(xn)
            sel_ref[rows, :] = sel
            gm_ref[rows, :] = gm

        ub[...] = _project(x1_ref[...], nm_ref, win_ref)
        finish(xr0_ref[...], ua[...], slice(0, tm))
        ua[...] = _project(xp2_ref[...], nm_ref, win_ref)
        finish(x1_ref[...], ub[...], slice(tm, 2 * tm))

    @pl.when(jnp.logical_and(lax.rem(step, pairs_per_seq) == pairs_per_seq - 1, live))
    def _():
        for h in range(N_HEADS):
            st_out[h] = st[h].T

    @pl.when(step == n_pairs)
    def _():
        ns = hs_ref.shape[0]
        for dst, val in ((h_ref, hs_ref[...]), (xn_ref, _pack_halves(xns_ref[...])),
                         (sel_ref, sels_ref[...]), (gm_ref, gms_ref[...])):
            dst[0:ns, :] = val
            dst[ns:2 * tm, :] = jnp.zeros((2 * tm - ns, dst.shape[-1]), dst.dtype)


def _const_spec(shape):
    return pl.BlockSpec(shape, lambda *_: (0,) * len(shape))


def _meta_call(x, weights, layer):
    nm, win, cw, rlb = weights
    d_conv = cw.shape[-1]
    s0 = jnp.zeros((N_HEADS, HEAD, HEAD), F32)
    c0 = jnp.zeros((2, d_conv), F32)
    return pl.pallas_call(
        functools.partial(_meta_kernel, layer),
        out_shape=[jax.ShapeDtypeStruct(s0.shape, F32), jax.ShapeDtypeStruct(c0.shape, F32)],
        scratch_shapes=[pltpu.VMEM((x.shape[0] + 8, d_conv), F32), pltpu.VMEM(s0.shape, F32)],
        compiler_params=pltpu.CompilerParams(vmem_limit_bytes=VMEM_LIMIT),
        name="mixer_meta",
    )(x, s0, c0, nm, win, cw, rlb)


def _prompt_call(x, s0, c0, weights, tail_w, decode, layer, tm, chunk):
    nseq, length, d = x.shape
    nt = length // tm
    assert nt % 2 == 0 and decode[0].shape[0] <= 2 * tm
    pairs_per_seq = nt // 2
    n_pairs = nseq * pairs_per_seq
    nm, win, cw, rlb = weights
    d_conv = cw.shape[-1]
    d_in = win.shape[-1]

    def tile_spec(offset):
        def index(s):
            tile = jnp.minimum(2 * s + offset, 2 * n_pairs - 1)
            return (tile // nt, lax.rem(tile, nt), 0)
        return pl.BlockSpec((None, tm, d), index)

    def seq_of(s):
        return jnp.minimum(s // pairs_per_seq, nseq - 1)

    consts = [s0, c0, nm, win, cw, rlb] + list(tail_w) + list(decode)
    tok = [(a.shape[-1], F32) for a in decode]
    tok[1] = (tok[1][0] // 2, jnp.uint32)
    n_steps = n_pairs + 1
    return pl.pallas_call(
        functools.partial(_prompt_kernel, layer, tm, chunk, n_pairs, pairs_per_seq),
        grid=(n_steps,),
        in_specs=[tile_spec(0), tile_spec(1), tile_spec(2)] + [_const_spec(a.shape) for a in consts],
        out_specs=[pl.BlockSpec((2 * tm, w), lambda s: (s, 0)) for w, _ in tok] + [
            pl.BlockSpec((None,) + s0.shape, lambda s: (seq_of(s), 0, 0, 0)),
            pl.BlockSpec((None,) + c0.shape, lambda s: (seq_of(s), 0, 0))],
        out_shape=[jax.ShapeDtypeStruct((n_steps * 2 * tm, w), t) for w, t in tok] + [
            jax.ShapeDtypeStruct((nseq,) + s0.shape, F32), jax.ShapeDtypeStruct((nseq,) + c0.shape, F32)],
        scratch_shapes=[pltpu.VMEM((tm + 8, d_conv), F32), pltpu.VMEM(s0.shape, F32),
                        pltpu.VMEM((tm, d_in), F32), pltpu.VMEM((tm, d_in), F32)],
        compiler_params=pltpu.CompilerParams(dimension_semantics=("arbitrary",),
                                             vmem_limit_bytes=VMEM_LIMIT),
        name="mixer_prompt",
    )(x, x, x, *consts)


def _sample_in_kernel(layer, x_ref, sc_ref, nm_ref, win_ref, cw_ref, rlb_ref,
                      yc_ref, nc_ref, f_ref, k_ref, q_ref, v_ref, og_ref):
    d_conv = cw_ref.shape[-1]
    d_rec = rlb_ref.shape[-1]
    u = _project(x_ref[...], nm_ref, win_ref)
    bg, cg, hv, q, fx, iv, og = _split_u(u, d_conv, d_rec)
    bx = bg * hv
    sc = sc_ref[...]
    s0, s1 = sc[:, :d_conv], sc[:, d_conv:]
    cw = cw_ref[...]
    yc_ref[...] = cg * (cw[0:1] * s0 + cw[1:2] * s1 + cw[2:3] * bx)
    nc_ref[...] = jnp.concatenate([s1, bx], axis=-1)
    f, kk = _forget(fx, _lower_bound(rlb_ref, layer))
    f_ref[...] = f
    k_ref[...] = kk
    q_ref[...] = q
    v_ref[...] = iv
    og_ref[...] = og


def _sample_state_kernel(group, f_ref, k_ref, q_ref, v_ref, s_ref, sn_ref, o_ref):
    for j in range(group):
        for h in range(N_HEADS):
            rs = slice(h * HEAD, (h + 1) * HEAD)
            fcol = f_ref[rs, j:j + 1]
            kcol = k_ref[rs, j:j + 1]
            qcol = q_ref[rs, j:j + 1]
            vrow = v_ref[j:j + 1, rs]
            sn = fcol * s_ref[j, h] + kcol * vrow
            sn_ref[j, h] = sn
            o_ref[j:j + 1, rs] = jnp.sum(qcol * sn, axis=0, keepdims=True)


def _sample_tail_kernel(x_ref, yc_ref, o_ref, og_ref, rg_ref, wout_ref, nf_ref, rw_ref, rb_ref,
                        h_ref, xn_ref, sel_ref, gm_ref):
    y = jnp.concatenate([yc_ref[...], _rec_out(o_ref[...], og_ref[...], rg_ref[...])], axis=-1)
    h, xn, sel, gm = _tail(x_ref[...], y, wout_ref, nf_ref, rw_ref, rb_ref)
    h_ref[...] = h
    xn_ref[...] = xn
    sel_ref[...] = sel
    gm_ref[...] = gm


def _rank_kernel(sel_ref, gm_ref, eid_ref, rank_ref, gate_ref, cnt_ref, carry):
    i = pl.program_id(0)

    @pl.when(i == 0)
    def _():
        carry[...] = jnp.zeros_like(carry)

    sel = sel_ref[...]
    gm = gm_ref[...]
    tb, ne = sel.shape
    row = lax.broadcasted_iota(jnp.int32, (tb, tb), 0)
    col = lax.broadcasted_iota(jnp.int32, (tb, tb), 1)
    before = jnp.where(col < row, 1.0, 0.0).astype(BF16)
    selb = sel.astype(BF16)
    rank = _dot(before, selb) + carry[...]
    carry[...] = carry[...] + jnp.sum(sel, axis=0, keepdims=True)
    cnt_ref[...] = carry[...]
    er = lax.broadcasted_iota(jnp.int32, (ne, ne), 0)
    ec = lax.broadcasted_iota(jnp.int32, (ne, ne), 1)
    lower = jnp.where(er < ec, 1.0, 0.0).astype(BF16)
    order = _dot(selb, lower)
    lane_e = lax.broadcasted_iota(jnp.int32, (tb, ne), 1).astype(F32)
    lane = lax.broadcasted_iota(jnp.int32, (tb, LANES), 1)
    eid = jnp.zeros((tb, LANES), F32)
    rnk = jnp.zeros((tb, LANES), F32)
    gat = jnp.zeros((tb, LANES), F32)
    for k in range(TOP_K):
        pick = jnp.where(order == float(k), sel, 0.0)
        eid = jnp.where(lane == k, jnp.sum(pick * lane_e, axis=-1, keepdims=True), eid)
        rnk = jnp.where(lane == k, jnp.sum(pick * rank, axis=-1, keepdims=True), rnk)
        gat = jnp.where(lane == k, jnp.sum(pick * gm, axis=-1, keepdims=True), gat)
    eid_ref[...] = eid
    rank_ref[...] = rnk
    gate_ref[...] = gat


def _dispatch(xn, pos_w, n_slots):
    n_workers, n_chunks, top_k, ch = pos_w.shape
    assert n_workers == SC_CORES * SC_SUBCORES and ch % 8 == 0 and ch <= LANES
    d = xn.shape[1]
    mesh = plsc.VectorSubcoreMesh(core_axis_name="c", subcore_axis_name="s")

    @functools.partial(
        pl.kernel, mesh=mesh,
        out_type=jax.ShapeDtypeStruct((n_slots, d), xn.dtype),
        scratch_types=[pltpu.VMEM((n_chunks, top_k, ch), jnp.int32), pltpu.VMEM((SC_RING, ch, d), xn.dtype),
                       pltpu.SemaphoreType.DMA((SC_RING,)), pltpu.SemaphoreType.DMA((SC_RING,))],
        name="dispatch",
    )
    def run(xn_hbm, pos_hbm, xs_hbm, idx_v, rows_v, sem_r, sem_w):
        wid = lax.axis_index("s") * SC_CORES + lax.axis_index("c")

        def read(c, b):
            src = xn_hbm.at[pl.ds((wid * n_chunks + c) * ch, ch)]
            return pltpu.make_async_copy(src, rows_v.at[b], sem_r.at[b])

        def write(c, b, k):
            return pltpu.make_async_copy(rows_v.at[b], xs_hbm.at[idx_v.at[c, k]], sem_w.at[b])

        pltpu.sync_copy(pos_hbm.at[wid], idx_v)
        ahead = SC_RING - 1
        for c0 in range(min(ahead, n_chunks)):
            read(c0, c0).start()

        @pl.loop(0, n_chunks)
        def _(c):
            b = lax.rem(c, SC_RING)
            read(c, b).wait()
            for k in range(top_k):
                write(c, b, k).start()

            @pl.when(c >= 1)
            def _():
                for k in range(top_k):
                    write(c - 1, lax.rem(c - 1, SC_RING), k).wait()

            @pl.when(c + ahead < n_chunks)
            def _():
                read(c + ahead, lax.rem(c + ahead, SC_RING)).start()

        for k in range(top_k):
            write(n_chunks - 1, (n_chunks - 1) % SC_RING, k).wait()

    return run(xn, pos_w)


def _collect(ys, pos_w):
    n_workers, n_chunks, top_k, ch = pos_w.shape
    assert n_workers == SC_CORES * SC_SUBCORES and ch % 8 == 0 and ch <= LANES
    d = ys.shape[1]
    mesh = plsc.VectorSubcoreMesh(core_axis_name="c", subcore_axis_name="s")

    @functools.partial(
        pl.kernel, mesh=mesh,
        out_type=jax.ShapeDtypeStruct((n_workers * n_chunks * ch, top_k * d), ys.dtype),
        scratch_types=[pltpu.VMEM((n_chunks, top_k, ch), jnp.int32), pltpu.VMEM((top_k, ch, d), ys.dtype),
                       pltpu.SemaphoreType.DMA((top_k,)), pltpu.SemaphoreType.DMA((top_k,))],
        name="collect",
    )
    def run(ys_hbm, pos_hbm, out_hbm, idx_v, rows_v, sem_r, sem_w):
        wid = lax.axis_index("s") * SC_CORES + lax.axis_index("c")

        def read(c, k):
            return pltpu.make_async_copy(ys_hbm.at[idx_v.at[c, k]], rows_v.at[k], sem_r.at[k])

        def write(c, k):
            dst = out_hbm.at[pl.ds((wid * n_chunks + c) * ch, ch), pl.ds(k * d, d)]
            return pltpu.make_async_copy(rows_v.at[k], dst, sem_w.at[k])

        pltpu.sync_copy(pos_hbm.at[wid], idx_v)
        for k in range(top_k - 1):
            read(0, k).start()

        @pl.loop(0, n_chunks)
        def _(c):
            for k in range(top_k):
                read(c, k).wait()
                write(c, k).start()
                if k >= 1:
                    write(c, k - 1).wait()

                    @pl.when(c + 1 < n_chunks)
                    def _():
                        read(c + 1, k - 1).start()
                else:
                    @pl.when(c >= 1)
                    def _():
                        write(c - 1, top_k - 1).wait()
                    read(c, top_k - 1).start()

        write(n_chunks - 1, top_k - 1).wait()

    return run(ys, pos_w)


def _moe_kernel(tm, te_ref, nu_ref, tn_ref, x_ref, w1_hbm, bg_ref, bl_ref, w2_hbm, b2_ref,
                ys_ref, w1f, w2f, sem, w1p, w2b, act):
    i = pl.program_id(0)
    n_used = nu_ref[0]
    d_ff2 = w1f.shape[-1]
    n_blk = d_ff2 // MXU_N
    expert = te_ref[i]

    def weight_copies(e):
        return (pltpu.make_async_copy(w1_hbm.at[e], w1f, sem.at[0]),
                pltpu.make_async_copy(w2_hbm.at[e], w2f, sem.at[1]))

    @pl.when(i == 0)
    def _():
        for cp in weight_copies(expert):
            cp.start()

    @pl.when(i >= n_used)
    def _():
        ys_ref[...] = jnp.zeros_like(ys_ref)

    prev = te_ref[jnp.maximum(i - 1, 0)]
    changed = jnp.logical_or(i == 0, expert != prev)

    @pl.when(jnp.logical_and(changed, i < n_used))
    def _():
        for cp in weight_copies(expert):
            cp.wait()
        r = lax.broadcasted_iota(jnp.int32, (MXU_N, MXU_N), 0)
        c = lax.broadcasted_iota(jnp.int32, (MXU_N, MXU_N), 1)
        src = jnp.where(c < MXU_N // 2, 2 * c, 2 * (c - MXU_N // 2) + 1)
        perm = jnp.where(r == src, 1.0, 0.0).astype(BF16)
        for blk in range(n_blk):
            cs = slice(blk * MXU_N, (blk + 1) * MXU_N)
            w1p[:, cs] = _dot(w1f[:, cs].astype(BF16), perm).astype(BF16)
        w2b[...] = w2f[...].astype(BF16)

        @pl.when(tn_ref[i] != expert)
        def _():
            for cp in weight_copies(tn_ref[i]):
                cp.start()

    @pl.when(i < n_used)
    def _():
        x = _unpack_halves(x_ref[...]).astype(BF16)
        for blk in range(n_blk):
            a = _dot(x, w1p[:, blk * MXU_N:(blk + 1) * MXU_N])
            half = MXU_N // 2
            hs = slice(blk * half, (blk + 1) * half)
            glu = jnp.minimum(a[:, :half] + bg_ref[:, hs], SWIGLU_LIMIT)
            lin = jnp.clip(a[:, half:] + bl_ref[:, hs], -SWIGLU_LIMIT, SWIGLU_LIMIT)
            act[:, hs] = (glu * _sigmoid(SWIGLU_ALPHA * glu) * (lin + 1.0)).astype(BF16)
        ys_ref[...] = _pack_halves(_dot(act[...], w2b[...]) + b2_ref[...])


def _final_kernel(h_ref, gate_ref, nfin_ref, z_ref, y_ref):
    w = z_ref.shape[-1] // TOP_K
    g = gate_ref[...]
    out = h_ref[...]
    for k in range(TOP_K):
        out = out + g[:, k:k + 1] * _unpack_halves(z_ref[:, k * w:(k + 1) * w])
    y_ref[...] = _rms(out, nfin_ref[...])


def _final_call(h_all, gate, nfin, z, first_row, n_rows, tile):
    d = h_all.shape[-1]
    off = first_row // tile
    assert first_row % tile == 0 and n_rows % tile == 0
    return pl.pallas_call(
        _final_kernel,
        grid=(n_rows // tile,),
        in_specs=[
            pl.BlockSpec((tile, d), lambda i: (i + off, 0)),
            pl.BlockSpec((tile, gate.shape[-1]), lambda i: (i + off, 0)),
            pl.BlockSpec((1, d), lambda i: (0, 0)),
            pl.BlockSpec((tile, z.shape[-1]), lambda i: (i + off, 0)),
        ],
        out_specs=pl.BlockSpec((tile, d), lambda i: (i, 0)),
        out_shape=jax.ShapeDtypeStruct((n_rows, d), F32),
        compiler_params=pltpu.CompilerParams(dimension_semantics=("arbitrary",),
                                             vmem_limit_bytes=VMEM_LIMIT),
        name="final",
    )(h_all, gate, nfin, z)


def kernel(x_prompt, x_sample, state_conv, state_rec, meta_tokens, norm_mix, w_in, conv_w,
           rec_lower_bound, rec_norm, w_out, norm_ffn, router_w, router_b, expert_w1, expert_b1,
           expert_w2, expert_b2, norm_final):
    depth = norm_mix.shape[0]
    assert depth == 1, "single-layer step"
    layer = 0
    bp, seq, d = x_prompt.shape
    ns = x_sample.shape[0]
    assert x_sample.shape[1] == 1 and ns == TOKEN_TILE
    d_conv = conv_w.shape[-1]
    d_rec = rec_lower_bound.shape[-1]
    assert state_conv.shape[2] == 2 and d_rec == N_HEADS * HEAD
    n_exp = router_w.shape[-1]
    d_ff = expert_w2.shape[2]
    n_prompt = bp * seq
    n_tok = n_prompt + ns
    assert seq % PROMPT_TILE == 0 and n_prompt % TOKEN_TILE == 0 and n_tok % RANK_TILE == 0

    nm = norm_mix[layer][None]
    win = w_in[layer].astype(BF16)
    cw = conv_w[layer]
    rlb = rec_lower_bound
    rg = rec_norm[layer][None]
    wout = w_out[layer].astype(BF16)
    nf = norm_ffn[layer][None]
    rw = router_w[layer]
    rb = router_b[layer][None]
    mix_w = (nm, win, cw, rlb)
    tail_w = (rg, wout, nf, rw, rb)

    st_meta, cv_meta = _meta_call(meta_tokens, mix_w, layer)

    xs = x_sample.reshape(ns, d)
    wide = jax.ShapeDtypeStruct((ns, d_rec), F32)
    y_conv_s, new_conv_s, f_s, k_s, q_s, v_s, og_s = pl.pallas_call(
        functools.partial(_sample_in_kernel, layer),
        out_shape=[jax.ShapeDtypeStruct((ns, d_conv), F32), jax.ShapeDtypeStruct((ns, 2 * d_conv), F32),
                   wide, wide, wide, wide, wide],
        compiler_params=pltpu.CompilerParams(vmem_limit_bytes=VMEM_LIMIT),
        name="sample_in",
    )(xs, state_conv[layer].reshape(ns, 2 * d_conv), nm, win, cw, rlb)

    group = 8
    n_grp = ns // group

    def cols(a):
        return a.T.reshape(d_rec, n_grp, group).transpose(1, 0, 2)

    col_spec = pl.BlockSpec((None, d_rec, group), lambda g: (g, 0, 0))
    st_spec = pl.BlockSpec((group, N_HEADS, HEAD, HEAD), lambda g: (g, 0, 0, 0))
    row_spec = pl.BlockSpec((group, d_rec), lambda g: (g, 0))
    new_rec_s, o_s = pl.pallas_call(
        functools.partial(_sample_state_kernel, group),
        grid=(n_grp,),
        in_specs=[col_spec, col_spec, col_spec, row_spec, st_spec],
        out_specs=[st_spec, row_spec],
        out_shape=[jax.ShapeDtypeStruct(state_rec.shape[1:], F32), wide],
        compiler_params=pltpu.CompilerParams(dimension_semantics=("arbitrary",),
                                             vmem_limit_bytes=VMEM_LIMIT),
        name="sample_state",
    )(cols(f_s), cols(k_s), cols(q_s), v_s, state_rec[layer])

    decode = pl.pallas_call(
        _sample_tail_kernel,
        out_shape=[jax.ShapeDtypeStruct((ns, d), F32), jax.ShapeDtypeStruct((ns, d), F32),
                   jax.ShapeDtypeStruct((ns, n_exp), F32), jax.ShapeDtypeStruct((ns, n_exp), F32)],
        compiler_params=pltpu.CompilerParams(vmem_limit_bytes=VMEM_LIMIT),
        name="sample_tail",
    )(xs, y_conv_s, o_s, og_s, rg, wout, nf, rw, rb)

    h_all, xn_all, sel_all, gm_all, new_rec_p, new_conv_p = _prompt_call(
        x_prompt, st_meta, cv_meta, mix_w, tail_w, decode, layer, PROMPT_TILE, CHUNK)

    canvas = jax.ShapeDtypeStruct((n_tok, LANES), F32)
    tok_spec = pl.BlockSpec((RANK_TILE, n_exp), lambda i: (i, 0))
    can_spec = pl.BlockSpec((RANK_TILE, LANES), lambda i: (i, 0))
    eid, rnk, gate, counts = pl.pallas_call(
        _rank_kernel,
        grid=(n_tok // RANK_TILE,),
        in_specs=[tok_spec, tok_spec],
        out_specs=[can_spec, can_spec, can_spec, pl.BlockSpec((1, n_exp), lambda i: (0, 0))],
        out_shape=[canvas, canvas, canvas, jax.ShapeDtypeStruct((1, n_exp), F32)],
        scratch_shapes=[pltpu.VMEM((1, n_exp), F32)],
        compiler_params=pltpu.CompilerParams(dimension_semantics=("arbitrary",),
                                             vmem_limit_bytes=VMEM_LIMIT),
        name="rank",
    )(sel_all, gm_all)

    tm = MOE_TILE
    n_tiles = (n_tok * TOP_K) // tm + n_exp
    counts = counts[0].astype(jnp.int32)
    tiles_e = (counts + tm - 1) // tm
    tile_end = jnp.cumsum(tiles_e)
    n_used = tile_end[-1]
    offs = (tile_end - tiles_e) * tm
    eid4 = eid[:, :TOP_K].astype(jnp.int32)
    experts = jnp.arange(n_exp, dtype=jnp.int32)
    off4 = jnp.sum(jnp.where(eid4[..., None] == experts, offs, 0), axis=-1)
    pos = off4 + rnk[:, :TOP_K].astype(jnp.int32)
    tile_ids = jnp.minimum(jnp.arange(n_tiles, dtype=jnp.int32), n_used - 1)
    tile_expert = jnp.minimum(jnp.sum((tile_end[None, :] <= tile_ids[:, None]).astype(jnp.int32), axis=1),
                              n_exp - 1)
    after = tile_end[tile_expert]
    next_expert = jnp.where(after < n_used, tile_expert[jnp.minimum(after, n_tiles - 1)], tile_expert)

    n_rows = xn_all.shape[0]
    n_workers = SC_CORES * SC_SUBCORES
    assert n_rows % (n_workers * DISPATCH_CHUNK) == 0
    n_spare = (n_rows - n_tok) * TOP_K
    n_slots = n_tiles * tm + n_spare
    spare = n_tiles * tm + jnp.arange(n_spare, dtype=jnp.int32).reshape(-1, TOP_K)
    pos_w = jnp.concatenate([pos, spare], axis=0).reshape(n_workers, -1, DISPATCH_CHUNK, TOP_K)
    pos_w = pos_w.transpose(0, 1, 3, 2)
    xs = _dispatch(xn_all, pos_w, n_slots)

    w1 = expert_w1[layer]
    w2 = expert_w2[layer]
    b1 = expert_b1[layer]
    b1g = b1[:, 0::2][:, None, :]
    b1l = b1[:, 1::2][:, None, :]
    b2 = expert_b2[layer][:, None, :]
    ys = pl.pallas_call(
        functools.partial(_moe_kernel, tm),
        grid_spec=pltpu.PrefetchScalarGridSpec(
            num_scalar_prefetch=3,
            grid=(n_tiles,),
            in_specs=[
                pl.BlockSpec((tm, d // 2), lambda i, te, nu, tn: (jnp.minimum(i, nu[0] - 1), 0)),
                pl.BlockSpec(memory_space=pl.ANY),
                pl.BlockSpec((None, 1, d_ff), lambda i, te, nu, tn: (te[i], 0, 0)),
                pl.BlockSpec((None, 1, d_ff), lambda i, te, nu, tn: (te[i], 0, 0)),
                pl.BlockSpec(memory_space=pl.ANY),
                pl.BlockSpec((None, 1, d), lambda i, te, nu, tn: (te[i], 0, 0)),
            ],
            out_specs=pl.BlockSpec((tm, d // 2), lambda i, te, nu, tn: (i, 0)),
            scratch_shapes=[pltpu.VMEM((d, 2 * d_ff), F32), pltpu.VMEM((d_ff, d), F32),
                            pltpu.SemaphoreType.DMA((2,)),
                            pltpu.VMEM((d, 2 * d_ff), BF16), pltpu.VMEM((d_ff, d), BF16),
                            pltpu.VMEM((tm, d_ff), BF16)],
        ),
        out_shape=jax.ShapeDtypeStruct((n_slots, d // 2), jnp.uint32),
        compiler_params=pltpu.CompilerParams(dimension_semantics=("arbitrary",),
                                             vmem_limit_bytes=VMEM_LIMIT),
        name="moe",
    )(tile_expert, n_used[None].astype(jnp.int32), next_expert, xs, w1, b1g, b1l, w2, b2)

    z = _collect(ys, pos_w)
    y_p = _final_call(h_all, gate, norm_final[None], z, 0, n_prompt, FINAL_TILE)
    y_s = _final_call(h_all, gate, norm_final[None], z, n_prompt, ns, TOKEN_TILE)

    return (y_p.reshape(bp, seq, d), y_s.reshape(ns, 1, d),
            new_conv_p[None], new_rec_p[None],
            new_conv_s.reshape(1, ns, 2, d_conv), new_rec_s[None])
```

```python
import functools

import jax
import jax.numpy as jnp
from jax import lax
from jax.experimental import pallas as pl
from jax.experimental.pallas import tpu as pltpu
from jax.experimental.pallas import tpu_sc as plsc

F32 = jnp.float32
BF16 = jnp.bfloat16

N_HEADS = 4
HEAD = 128
N_META = 16
CHUNK = 64
TOP_K = 4
SWIGLU_LIMIT = 7.0
SWIGLU_ALPHA = 1.702
EPS = 1e-5

LANES = 128
MXU_N = 256
PROMPT_TILE = 256
TOKEN_TILE = 128
FINAL_TILE = 512
MOE_TILE = 512
SC_CORES = 2
SC_SUBCORES = 16
DISPATCH_CHUNK = 24
SC_RING = 4
COLLECT_CHUNKS = (32, 16)
VMEM_LIMIT = 56 * 1024 * 1024


def _dot(a, b):
    return jnp.dot(a, b, preferred_element_type=F32)


def _dot_nt(a, b):
    return lax.dot_general(a, b, (((1,), (1,)), ((), ())), preferred_element_type=F32)


def _dot_tn(a, b):
    return lax.dot_general(a, b, (((0,), (0,)), ((), ())), preferred_element_type=F32)


def _split3(x):
    hi = x.astype(BF16)
    r = x - hi.astype(F32)
    mid = r.astype(BF16)
    lo = (r - mid.astype(F32)).astype(BF16)
    return hi, mid, lo


def _pack_halves(x):
    n = x.shape[-1] // 2
    lo = pltpu.bitcast(x[:, :n].astype(BF16).astype(F32), jnp.uint32)
    hi = pltpu.bitcast(x[:, n:].astype(BF16).astype(F32), jnp.uint32)
    return (lo >> 16) | (hi & jnp.uint32(0xFFFF0000))


def _unpack_halves(u):
    lo = pltpu.bitcast(u << 16, F32)
    hi = pltpu.bitcast(u & jnp.uint32(0xFFFF0000), F32)
    return jnp.concatenate([lo, hi], axis=-1)


def _sigmoid(x):
    return 1.0 / (1.0 + jnp.exp(-x))


def _rms(x, g):
    ms = jnp.mean(x * x, axis=-1, keepdims=True)
    return x * lax.rsqrt(ms + EPS) * g


def _project(x, nm_ref, win_ref):
    return _dot(_rms(x, nm_ref[...]).astype(BF16), win_ref[...])


def _split_u(u, d_conv, d_rec):
    pts = [0, d_conv, 2 * d_conv, 3 * d_conv, 3 * d_conv + d_rec, 3 * d_conv + 2 * d_rec,
           3 * d_conv + 3 * d_rec, 3 * d_conv + 4 * d_rec]
    return [u[:, pts[i]:pts[i + 1]] for i in range(7)]


def _lower_bound(rlb_ref, layer):
    r = rlb_ref[...]
    e = jnp.exp(r - jnp.max(r, axis=0, keepdims=True))
    return jnp.sum(e[0:layer + 1], axis=0, keepdims=True) / jnp.sum(e, axis=0, keepdims=True)


def _forget(fx, lb):
    f = lb + (1.0 - lb) * _sigmoid(fx)
    return f, 1.0 - f


def _rec_out(o, og, rg):
    parts = []
    for h in range(N_HEADS):
        oh = o[:, h * HEAD:(h + 1) * HEAD]
        parts.append(oh * lax.rsqrt(jnp.mean(oh * oh, axis=-1, keepdims=True) + EPS))
    return jnp.concatenate(parts, axis=-1) * rg * (og * _sigmoid(og))


def _route(logits):
    n = logits.shape[-1]
    lane = lax.broadcasted_iota(jnp.int32, logits.shape, 1).astype(F32)
    work = logits
    tops, hots = [], []
    for _ in range(TOP_K):
        m = jnp.max(work, axis=-1, keepdims=True)
        first = jnp.min(jnp.where(work == m, lane, float(n)), axis=-1, keepdims=True)
        hot = lane == first
        tops.append(m)
        hots.append(hot)
        work = jnp.where(hot, -jnp.inf, work)
    es = [jnp.exp(t - tops[0]) for t in tops]
    den = es[0]
    for e in es[1:]:
        den = den + e
    sel = jnp.zeros_like(logits)
    gm = jnp.zeros_like(logits)
    for hot, e in zip(hots, es):
        sel = sel + jnp.where(hot, 1.0, 0.0)
        gm = gm + jnp.where(hot, e / den, 0.0)
    return sel, gm


def _tail(x, y, wout_ref, nf_ref, rw_ref, rb_ref):
    h = x + _dot(y.astype(BF16), wout_ref[...])
    xn = _rms(h, nf_ref[...])
    xh = xn.astype(BF16)
    xl = (xn - xh.astype(F32)).astype(BF16)
    rw = rw_ref[...]
    wh = rw.astype(BF16)
    wl = (rw - wh.astype(F32)).astype(BF16)
    logits = _dot(xh, wh) + _dot(xh, wl) + _dot(xl, wh) + rb_ref[...]
    sel, gm = _route(logits)
    return h, xn, sel, gm


def _rec_chunk(q, kk, v, lf, st_ref, tri, causal):
    c = q.shape[0]
    hi, mid, lo = _split3(lf)
    b = _dot(tri, hi) + _dot(tri, mid) + _dot(tri, lo)
    eb = jnp.exp(b)
    qe = (q * eb).astype(BF16)
    ke = (kk * jnp.exp(-b)).astype(BF16)
    vb = v.astype(BF16)
    eb_last = eb[c - 1:c]
    outs = []
    for h in range(N_HEADS):
        sl = slice(h * HEAD, (h + 1) * HEAD)
        st = st_ref[h]
        sc = jnp.where(causal, _dot_nt(qe[:, sl], ke[:, sl]), 0.0)
        outs.append(_dot(sc.astype(BF16), vb[:, sl]) + _dot_nt(qe[:, sl], st.astype(BF16)))
        st_ref[h] = (st + _dot_tn(vb[:, sl], ke[:, sl])) * eb_last[:, sl]
    return jnp.concatenate(outs, axis=-1)


def _mix_tile(u, layer, chunk, cw_ref, rlb_ref, convbuf, st, cv_out):
    tm = u.shape[0]
    d_conv = cw_ref.shape[-1]
    d_rec = rlb_ref.shape[-1]
    bg, cg, hv, q, fx, iv, og = _split_u(u, d_conv, d_rec)

    bx = bg * hv
    convbuf[8:8 + tm, :] = bx
    cw = cw_ref[...]
    conv = cw[0:1] * convbuf[6:6 + tm, :] + cw[1:2] * convbuf[7:7 + tm, :] + cw[2:3] * bx
    convbuf[6:8, :] = bx[tm - 2:tm]
    cv_out[...] = bx[tm - 2:tm]

    f, kk = _forget(fx, _lower_bound(rlb_ref, layer))
    lf = jnp.log(f)
    row = lax.broadcasted_iota(jnp.int32, (chunk, chunk), 0)
    col = lax.broadcasted_iota(jnp.int32, (chunk, chunk), 1)
    causal = row >= col
    tri = jnp.where(causal, 1.0, 0.0).astype(BF16)
    outs = []
    for c in range(tm // chunk):
        rs = slice(c * chunk, (c + 1) * chunk)
        outs.append(_rec_chunk(q[rs], kk[rs], iv[rs], lf[rs], st, tri, causal))
    return cg * conv, jnp.concatenate(outs, axis=0), og


def _load_state(st, convbuf, s0_ref, c0_ref):
    for h in range(N_HEADS):
        st[h] = s0_ref[h].T
    convbuf[6:8, :] = c0_ref[...]


def _meta_kernel(layer, x_ref, s0_ref, c0_ref, nm_ref, win_ref, cw_ref, rlb_ref, st_out, cv_out,
                 convbuf, st):
    _load_state(st, convbuf, s0_ref, c0_ref)
    _mix_tile(_project(x_ref[...], nm_ref, win_ref), layer, x_ref.shape[0], cw_ref, rlb_ref,
              convbuf, st, cv_out)
    for h in range(N_HEADS):
        st_out[h] = st[h].T


def _prompt_kernel(layer, tm, chunk, n_pairs, pairs_per_seq,
                   xr0_ref, x1_ref, xp2_ref, s0_ref, c0_ref, nm_ref, win_ref, cw_ref, rlb_ref,
                   rg_ref, wout_ref, nf_ref, rw_ref, rb_ref, hs_ref, xns_ref, sels_ref, gms_ref,
                   h_ref, xn_ref, eid_ref, rnk_ref, gat_ref, cnt_ref, st_out, cv_out,
                   convbuf, st, ua, ub, carry):
    step = pl.program_id(0)
    live = step < n_pairs

    @pl.when(step == 0)
    def _():
        ua[...] = _project(xr0_ref[...], nm_ref, win_ref)
        carry[...] = jnp.zeros_like(carry)

    @pl.when(jnp.logical_and(lax.rem(step, pairs_per_seq) == 0, live))
    def _():
        _load_state(st, convbuf, s0_ref, c0_ref)

    @pl.when(live)
    def _():
        def finish(x, u, rows):
            y_conv, o, og = _mix_tile(u, layer, chunk, cw_ref, rlb_ref, convbuf, st, cv_out)
            y = jnp.concatenate([y_conv, _rec_out(o, og, rg_ref[...])], axis=-1)
            h, xn, sel, gm = _tail(x, y, wout_ref, nf_ref, rw_ref, rb_ref)
            h_ref[rows, :] = h
            xn_ref[rows, :] = _pack_halves(xn)
            eid_ref[rows, :], rnk_ref[rows, :], gat_ref[rows, :] = _rank_block(sel, gm, carry)

        ub[...] = _project(x1_ref[...], nm_ref, win_ref)
        finish(xr0_ref[...], ua[...], slice(0, tm))
        ua[...] = _project(xp2_ref[...], nm_ref, win_ref)
        finish(x1_ref[...], ub[...], slice(tm, 2 * tm))

    @pl.when(jnp.logical_and(lax.rem(step, pairs_per_seq) == pairs_per_seq - 1, live))
    def _():
        for h in range(N_HEADS):
            st_out[h] = st[h].T

    @pl.when(step == n_pairs)
    def _():
        ns = hs_ref.shape[0]
        eid, rnk, gat = _rank_block(sels_ref[...], gms_ref[...], carry)
        for dst, val in ((h_ref, hs_ref[...]), (xn_ref, _pack_halves(xns_ref[...])),
                         (eid_ref, eid), (rnk_ref, rnk), (gat_ref, gat)):
            dst[0:ns, :] = val
            dst[ns:2 * tm, :] = jnp.zeros((2 * tm - ns, dst.shape[-1]), dst.dtype)
        cnt_ref[...] = carry[...]


def _const_spec(shape):
    return pl.BlockSpec(shape, lambda *_: (0,) * len(shape))


def _meta_call(x, weights, layer):
    nm, win, cw, rlb = weights
    d_conv = cw.shape[-1]
    s0 = jnp.zeros((N_HEADS, HEAD, HEAD), F32)
    c0 = jnp.zeros((2, d_conv), F32)
    return pl.pallas_call(
        functools.partial(_meta_kernel, layer),
        out_shape=[jax.ShapeDtypeStruct(s0.shape, F32), jax.ShapeDtypeStruct(c0.shape, F32)],
        scratch_shapes=[pltpu.VMEM((x.shape[0] + 8, d_conv), F32), pltpu.VMEM(s0.shape, F32)],
        compiler_params=pltpu.CompilerParams(vmem_limit_bytes=VMEM_LIMIT),
        name="mixer_meta",
    )(x, s0, c0, nm, win, cw, rlb)


def _prompt_call(x, s0, c0, weights, tail_w, decode, layer, tm, chunk):
    nseq, length, d = x.shape
    nt = length // tm
    assert nt % 2 == 0 and decode[0].shape[0] <= 2 * tm
    pairs_per_seq = nt // 2
    n_pairs = nseq * pairs_per_seq
    nm, win, cw, rlb = weights
    d_conv = cw.shape[-1]
    d_in = win.shape[-1]

    def tile_spec(offset):
        def index(s):
            tile = jnp.minimum(2 * s + offset, 2 * n_pairs - 1)
            return (tile // nt, lax.rem(tile, nt), 0)
        return pl.BlockSpec((None, tm, d), index)

    def seq_of(s):
        return jnp.minimum(s // pairs_per_seq, nseq - 1)

    consts = [s0, c0, nm, win, cw, rlb] + list(tail_w) + list(decode)
    n_exp = decode[2].shape[-1]
    tok = [(d, F32), (d // 2, jnp.uint32), (LANES, F32), (LANES, F32), (LANES, F32)]
    n_steps = n_pairs + 1
    return pl.pallas_call(
        functools.partial(_prompt_kernel, layer, tm, chunk, n_pairs, pairs_per_seq),
        grid=(n_steps,),
        in_specs=[tile_spec(0), tile_spec(1), tile_spec(2)] + [_const_spec(a.shape) for a in consts],
        out_specs=[pl.BlockSpec((2 * tm, w), lambda s: (s, 0)) for w, _ in tok] + [
            pl.BlockSpec((1, n_exp), lambda s: (0, 0)),
            pl.BlockSpec((None,) + s0.shape, lambda s: (seq_of(s), 0, 0, 0)),
            pl.BlockSpec((None,) + c0.shape, lambda s: (seq_of(s), 0, 0))],
        out_shape=[jax.ShapeDtypeStruct((n_steps * 2 * tm, w), t) for w, t in tok] + [
            jax.ShapeDtypeStruct((1, n_exp), F32),
            jax.ShapeDtypeStruct((nseq,) + s0.shape, F32), jax.ShapeDtypeStruct((nseq,) + c0.shape, F32)],
        scratch_shapes=[pltpu.VMEM((tm + 8, d_conv), F32), pltpu.VMEM(s0.shape, F32),
                        pltpu.VMEM((tm, d_in), F32), pltpu.VMEM((tm, d_in), F32),
                        pltpu.VMEM((1, n_exp), F32)],
        compiler_params=pltpu.CompilerParams(dimension_semantics=("arbitrary",),
                                             vmem_limit_bytes=VMEM_LIMIT),
        name="mixer_prompt",
    )(x, x, x, *consts)


def _sample_in_kernel(layer, x_ref, sc_ref, nm_ref, win_ref, cw_ref, rlb_ref,
                      yc_ref, nc_ref, f_ref, k_ref, q_ref, v_ref, og_ref):
    d_conv = cw_ref.shape[-1]
    d_rec = rlb_ref.shape[-1]
    u = _project(x_ref[...], nm_ref, win_ref)
    bg, cg, hv, q, fx, iv, og = _split_u(u, d_conv, d_rec)
    bx = bg * hv
    sc = sc_ref[...]
    s0, s1 = sc[:, :d_conv], sc[:, d_conv:]
    cw = cw_ref[...]
    yc_ref[...] = cg * (cw[0:1] * s0 + cw[1:2] * s1 + cw[2:3] * bx)
    nc_ref[...] = jnp.concatenate([s1, bx], axis=-1)
    f, kk = _forget(fx, _lower_bound(rlb_ref, layer))
    f_ref[...] = f
    k_ref[...] = kk
    q_ref[...] = q
    v_ref[...] = iv
    og_ref[...] = og


def _sample_state_kernel(group, f_ref, k_ref, q_ref, v_ref, s_ref, sn_ref, o_ref):
    for j in range(group):
        for h in range(N_HEADS):
            rs = slice(h * HEAD, (h + 1) * HEAD)
            fcol = f_ref[rs, j:j + 1]
            kcol = k_ref[rs, j:j + 1]
            qcol = q_ref[rs, j:j + 1]
            vrow = v_ref[j:j + 1, rs]
            sn = fcol * s_ref[j, h] + kcol * vrow
            sn_ref[j, h] = sn
            o_ref[j:j + 1, rs] = jnp.sum(qcol * sn, axis=0, keepdims=True)


def _sample_tail_kernel(x_ref, yc_ref, o_ref, og_ref, rg_ref, wout_ref, nf_ref, rw_ref, rb_ref,
                        h_ref, xn_ref, sel_ref, gm_ref):
    y = jnp.concatenate([yc_ref[...], _rec_out(o_ref[...], og_ref[...], rg_ref[...])], axis=-1)
    h, xn, sel, gm = _tail(x_ref[...], y, wout_ref, nf_ref, rw_ref, rb_ref)
    h_ref[...] = h
    xn_ref[...] = xn
    sel_ref[...] = sel
    gm_ref[...] = gm


def _rank_block(sel, gm, carry):
    tb, ne = sel.shape
    row = lax.broadcasted_iota(jnp.int32, (tb, tb), 0)
    col = lax.broadcasted_iota(jnp.int32, (tb, tb), 1)
    before = jnp.where(col < row, 1.0, 0.0).astype(BF16)
    selb = sel.astype(BF16)
    rank = _dot(before, selb) + carry[...]
    carry[...] = carry[...] + jnp.sum(sel, axis=0, keepdims=True)
    er = lax.broadcasted_iota(jnp.int32, (ne, ne), 0)
    ec = lax.broadcasted_iota(jnp.int32, (ne, ne), 1)
    lower = jnp.where(er < ec, 1.0, 0.0).astype(BF16)
    order = _dot(selb, lower)
    lane_e = lax.broadcasted_iota(jnp.int32, (tb, ne), 1).astype(F32)
    lane = lax.broadcasted_iota(jnp.int32, (tb, LANES), 1)
    eid = jnp.zeros((tb, LANES), F32)
    rnk = jnp.zeros((tb, LANES), F32)
    gat = jnp.zeros((tb, LANES), F32)
    for k in range(TOP_K):
        pick = jnp.where(order == float(k), sel, 0.0)
        eid = jnp.where(lane == k, jnp.sum(pick * lane_e, axis=-1, keepdims=True), eid)
        rnk = jnp.where(lane == k, jnp.sum(pick * rank, axis=-1, keepdims=True), rnk)
        gat = jnp.where(lane == k, jnp.sum(pick * gm, axis=-1, keepdims=True), gat)
    return eid, rnk, gat


def _dispatch(xn, pos_w, n_slots):
    n_workers, n_chunks, top_k, ch = pos_w.shape
    assert n_workers == SC_CORES * SC_SUBCORES and ch % 8 == 0 and ch <= LANES
    d = xn.shape[1]
    mesh = plsc.VectorSubcoreMesh(core_axis_name="c", subcore_axis_name="s")

    @functools.partial(
        pl.kernel, mesh=mesh,
        out_type=jax.ShapeDtypeStruct((n_slots, d), xn.dtype),
        scratch_types=[pltpu.VMEM((n_chunks, top_k, ch), jnp.int32), pltpu.VMEM((SC_RING, ch, d), xn.dtype),
                       pltpu.SemaphoreType.DMA((SC_RING,)), pltpu.SemaphoreType.DMA((SC_RING,))],
        name="dispatch",
    )
    def run(xn_hbm, pos_hbm, xs_hbm, idx_v, rows_v, sem_r, sem_w):
        wid = lax.axis_index("s") * SC_CORES + lax.axis_index("c")

        def read(c, b):
            src = xn_hbm.at[pl.ds((wid * n_chunks + c) * ch, ch)]
            return pltpu.make_async_copy(src, rows_v.at[b], sem_r.at[b])

        def write(c, b, k):
            return pltpu.make_async_copy(rows_v.at[b], xs_hbm.at[idx_v.at[c, k]], sem_w.at[b])

        pltpu.sync_copy(pos_hbm.at[wid], idx_v)
        ahead = SC_RING - 1
        for c0 in range(min(ahead, n_chunks)):
            read(c0, c0).start()

        @pl.loop(0, n_chunks)
        def _(c):
            b = lax.rem(c, SC_RING)
            read(c, b).wait()
            for k in range(top_k):
                write(c, b, k).start()

            @pl.when(c >= 1)
            def _():
                for k in range(top_k):
                    write(c - 1, lax.rem(c - 1, SC_RING), k).wait()

            @pl.when(c + ahead < n_chunks)
            def _():
                read(c + ahead, lax.rem(c + ahead, SC_RING)).start()

        for k in range(top_k):
            write(n_chunks - 1, (n_chunks - 1) % SC_RING, k).wait()

    return run(xn, pos_w)


def _collect(ys, pos_w):
    n_workers, n_chunks, top_k, ch = pos_w.shape
    assert n_workers == SC_CORES * SC_SUBCORES and ch % 8 == 0 and ch <= LANES
    d = ys.shape[1]
    mesh = plsc.VectorSubcoreMesh(core_axis_name="c", subcore_axis_name="s")

    @functools.partial(
        pl.kernel, mesh=mesh,
        out_type=jax.ShapeDtypeStruct((n_workers * n_chunks * ch, top_k * d), ys.dtype),
        scratch_types=[pltpu.VMEM((n_chunks, top_k, ch), jnp.int32), pltpu.VMEM((top_k, ch, d), ys.dtype),
                       pltpu.SemaphoreType.DMA((top_k,)), pltpu.SemaphoreType.DMA((top_k,))],
        name="collect",
    )
    def run(ys_hbm, pos_hbm, out_hbm, idx_v, rows_v, sem_r, sem_w):
        wid = lax.axis_index("s") * SC_CORES + lax.axis_index("c")

        def read(c, k):
            return pltpu.make_async_copy(ys_hbm.at[idx_v.at[c, k]], rows_v.at[k], sem_r.at[k])

        def write(c, k):
            dst = out_hbm.at[pl.ds((wid * n_chunks + c) * ch, ch), pl.ds(k * d, d)]
            return pltpu.make_async_copy(rows_v.at[k], dst, sem_w.at[k])

        pltpu.sync_copy(pos_hbm.at[wid], idx_v)
        for k in range(top_k - 1):
            read(0, k).start()

        @pl.loop(0, n_chunks)
        def _(c):
            for k in range(top_k):
                read(c, k).wait()
                write(c, k).start()
                if k >= 1:
                    write(c, k - 1).wait()

                    @pl.when(c + 1 < n_chunks)
                    def _():
                        read(c + 1, k - 1).start()
                else:
                    @pl.when(c >= 1)
                    def _():
                        write(c - 1, top_k - 1).wait()
                    read(c, top_k - 1).start()

        write(n_chunks - 1, top_k - 1).wait()

    return run(ys, pos_w)


def _moe_kernel(tm, te_ref, nu_ref, tn_ref, x_ref, w1_hbm, bg_ref, bl_ref, w2_hbm, b2_ref,
                ys_ref, w1f, w2f, sem, w1p, w2b, act):
    i = pl.program_id(0)
    n_used = nu_ref[0]
    d_ff2 = w1f.shape[-1]
    n_blk = d_ff2 // MXU_N
    expert = te_ref[i]

    def weight_copies(e):
        return (pltpu.make_async_copy(w1_hbm.at[e], w1f, sem.at[0]),
                pltpu.make_async_copy(w2_hbm.at[e], w2f, sem.at[1]))

    @pl.when(i == 0)
    def _():
        for cp in weight_copies(expert):
            cp.start()

    @pl.when(i >= n_used)
    def _():
        ys_ref[...] = jnp.zeros_like(ys_ref)

    prev = te_ref[jnp.maximum(i - 1, 0)]
    changed = jnp.logical_or(i == 0, expert != prev)

    @pl.when(jnp.logical_and(changed, i < n_used))
    def _():
        for cp in weight_copies(expert):
            cp.wait()
        r = lax.broadcasted_iota(jnp.int32, (MXU_N, MXU_N), 0)
        c = lax.broadcasted_iota(jnp.int32, (MXU_N, MXU_N), 1)
        src = jnp.where(c < MXU_N // 2, 2 * c, 2 * (c - MXU_N // 2) + 1)
        perm = jnp.where(r == src, 1.0, 0.0).astype(BF16)
        for blk in range(n_blk):
            cs = slice(blk * MXU_N, (blk + 1) * MXU_N)
            w1p[:, cs] = _dot(w1f[:, cs].astype(BF16), perm).astype(BF16)
        w2b[...] = w2f[...].astype(BF16)

        @pl.when(tn_ref[i] != expert)
        def _():
            for cp in weight_copies(tn_ref[i]):
                cp.start()

    @pl.when(i < n_used)
    def _():
        x = _unpack_halves(x_ref[...]).astype(BF16)
        for blk in range(n_blk):
            a = _dot(x, w1p[:, blk * MXU_N:(blk + 1) * MXU_N])
            half = MXU_N // 2
            hs = slice(blk * half, (blk + 1) * half)
            glu = jnp.minimum(a[:, :half] + bg_ref[:, hs], SWIGLU_LIMIT)
            lin = jnp.clip(a[:, half:] + bl_ref[:, hs], -SWIGLU_LIMIT, SWIGLU_LIMIT)
            act[:, hs] = (glu * _sigmoid(SWIGLU_ALPHA * glu) * (lin + 1.0)).astype(BF16)
        ys_ref[...] = _pack_halves(_dot(act[...], w2b[...]) + b2_ref[...])


def _final_kernel(h_ref, gate_ref, nfin_ref, z_ref, y_ref):
    w = z_ref.shape[-1] // TOP_K
    g = gate_ref[...]
    out = h_ref[...]
    for k in range(TOP_K):
        out = out + g[:, k:k + 1] * _unpack_halves(z_ref[:, k * w:(k + 1) * w])
    y_ref[...] = _rms(out, nfin_ref[...])


def _final_call(h_all, gate, nfin, z, rows, z_row, tile, out_rows, out_row, prev=None):
    d = h_all.shape[-1]
    first, n_rows = rows[0], rows[1] - rows[0]
    assert first % tile == 0 and n_rows % tile == 0 and z_row % tile == 0 and out_row % tile == 0
    off, z_off, o_off = first // tile, z_row // tile, out_row // tile
    in_specs = [
        pl.BlockSpec((tile, d), lambda i: (i + off, 0)),
        pl.BlockSpec((tile, gate.shape[-1]), lambda i: (i + off, 0)),
        pl.BlockSpec((1, d), lambda i: (0, 0)),
        pl.BlockSpec((tile, z.shape[-1]), lambda i: (i + z_off, 0)),
    ]
    args = [h_all, gate, nfin, z]
    body, aliases = _final_kernel, {}
    if prev is not None:
        in_specs.append(pl.BlockSpec(memory_space=pl.ANY))
        args.append(prev)
        aliases = {len(args) - 1: 0}

        def body(h_ref, gate_ref, nfin_ref, z_ref, _, y_ref):
            _final_kernel(h_ref, gate_ref, nfin_ref, z_ref, y_ref)
    return pl.pallas_call(
        body,
        grid=(n_rows // tile,),
        in_specs=in_specs,
        out_specs=pl.BlockSpec((tile, d), lambda i: (i + o_off, 0)),
        out_shape=jax.ShapeDtypeStruct((out_rows, d), F32),
        input_output_aliases=aliases,
        compiler_params=pltpu.CompilerParams(dimension_semantics=("arbitrary",),
                                             vmem_limit_bytes=VMEM_LIMIT),
        name="final",
    )(*args)


def kernel(x_prompt, x_sample, state_conv, state_rec, meta_tokens, norm_mix, w_in, conv_w,
           rec_lower_bound, rec_norm, w_out, norm_ffn, router_w, router_b, expert_w1, expert_b1,
           expert_w2, expert_b2, norm_final):
    depth = norm_mix.shape[0]
    assert depth == 1, "single-layer step"
    layer = 0
    bp, seq, d = x_prompt.shape
    ns = x_sample.shape[0]
    assert x_sample.shape[1] == 1 and ns == TOKEN_TILE
    d_conv = conv_w.shape[-1]
    d_rec = rec_lower_bound.shape[-1]
    assert state_conv.shape[2] == 2 and d_rec == N_HEADS * HEAD
    n_exp = router_w.shape[-1]
    d_ff = expert_w2.shape[2]
    n_prompt = bp * seq
    n_tok = n_prompt + ns
    assert seq % PROMPT_TILE == 0 and n_prompt % TOKEN_TILE == 0

    nm = norm_mix[layer][None]
    win = w_in[layer].astype(BF16)
    cw = conv_w[layer]
    rlb = rec_lower_bound
    rg = rec_norm[layer][None]
    wout = w_out[layer].astype(BF16)
    nf = norm_ffn[layer][None]
    rw = router_w[layer]
    rb = router_b[layer][None]
    mix_w = (nm, win, cw, rlb)
    tail_w = (rg, wout, nf, rw, rb)

    st_meta, cv_meta = _meta_call(meta_tokens, mix_w, layer)

    xs = x_sample.reshape(ns, d)
    wide = jax.ShapeDtypeStruct((ns, d_rec), F32)
    y_conv_s, new_conv_s, f_s, k_s, q_s, v_s, og_s = pl.pallas_call(
        functools.partial(_sample_in_kernel, layer),
        out_shape=[jax.ShapeDtypeStruct((ns, d_conv), F32), jax.ShapeDtypeStruct((ns, 2 * d_conv), F32),
                   wide, wide, wide, wide, wide],
        compiler_params=pltpu.CompilerParams(vmem_limit_bytes=VMEM_LIMIT),
        name="sample_in",
    )(xs, state_conv[layer].reshape(ns, 2 * d_conv), nm, win, cw, rlb)

    group = 8
    n_grp = ns // group

    def cols(a):
        return a.T.reshape(d_rec, n_grp, group).transpose(1, 0, 2)

    col_spec = pl.BlockSpec((None, d_rec, group), lambda g: (g, 0, 0))
    st_spec = pl.BlockSpec((group, N_HEADS, HEAD, HEAD), lambda g: (g, 0, 0, 0))
    row_spec = pl.BlockSpec((group, d_rec), lambda g: (g, 0))
    new_rec_s, o_s = pl.pallas_call(
        functools.partial(_sample_state_kernel, group),
        grid=(n_grp,),
        in_specs=[col_spec, col_spec, col_spec, row_spec, st_spec],
        out_specs=[st_spec, row_spec],
        out_shape=[jax.ShapeDtypeStruct(state_rec.shape[1:], F32), wide],
        compiler_params=pltpu.CompilerParams(dimension_semantics=("arbitrary",),
                                             vmem_limit_bytes=VMEM_LIMIT),
        name="sample_state",
    )(cols(f_s), cols(k_s), cols(q_s), v_s, state_rec[layer])

    decode = pl.pallas_call(
        _sample_tail_kernel,
        out_shape=[jax.ShapeDtypeStruct((ns, d), F32), jax.ShapeDtypeStruct((ns, d), F32),
                   jax.ShapeDtypeStruct((ns, n_exp), F32), jax.ShapeDtypeStruct((ns, n_exp), F32)],
        compiler_params=pltpu.CompilerParams(vmem_limit_bytes=VMEM_LIMIT),
        name="sample_tail",
    )(xs, y_conv_s, o_s, og_s, rg, wout, nf, rw, rb)

    h_all, xn_all, eid, rnk, gate, counts, new_rec_p, new_conv_p = _prompt_call(
        x_prompt, st_meta, cv_meta, mix_w, tail_w, decode, layer, PROMPT_TILE, CHUNK)

    tm = MOE_TILE
    n_tiles = (n_tok * TOP_K) // tm + n_exp
    counts = counts[0].astype(jnp.int32)
    tiles_e = (counts + tm - 1) // tm
    tile_end = jnp.cumsum(tiles_e)
    n_used = tile_end[-1]
    offs = (tile_end - tiles_e) * tm
    eid4 = eid[:n_tok, :TOP_K].astype(jnp.int32)
    experts = jnp.arange(n_exp, dtype=jnp.int32)
    off4 = jnp.sum(jnp.where(eid4[..., None] == experts, offs, 0), axis=-1)
    pos = off4 + rnk[:n_tok, :TOP_K].astype(jnp.int32)
    tile_ids = jnp.minimum(jnp.arange(n_tiles, dtype=jnp.int32), n_used - 1)
    tile_expert = jnp.minimum(jnp.sum((tile_end[None, :] <= tile_ids[:, None]).astype(jnp.int32), axis=1),
                              n_exp - 1)
    after = tile_end[tile_expert]
    next_expert = jnp.where(after < n_used, tile_expert[jnp.minimum(after, n_tiles - 1)], tile_expert)

    n_rows = xn_all.shape[0]
    n_workers = SC_CORES * SC_SUBCORES
    assert n_rows % (n_workers * DISPATCH_CHUNK) == 0
    n_spare = (n_rows - n_tok) * TOP_K
    n_slots = n_tiles * tm + n_spare
    spare = n_tiles * tm + jnp.arange(n_spare, dtype=jnp.int32).reshape(-1, TOP_K)
    pos_rows = jnp.concatenate([pos, spare], axis=0)

    def to_workers(p, ch):
        assert p.shape[0] % (n_workers * ch) == 0
        return p.reshape(n_workers, -1, ch, TOP_K).transpose(0, 1, 3, 2)

    xs = _dispatch(xn_all, to_workers(pos_rows, DISPATCH_CHUNK), n_slots)

    w1 = expert_w1[layer]
    w2 = expert_w2[layer]
    b1 = expert_b1[layer]
    b1g = b1[:, 0::2][:, None, :]
    b1l = b1[:, 1::2][:, None, :]
    b2 = expert_b2[layer][:, None, :]
    ys = pl.pallas_call(
        functools.partial(_moe_kernel, tm),
        grid_spec=pltpu.PrefetchScalarGridSpec(
            num_scalar_prefetch=3,
            grid=(n_tiles,),
            in_specs=[
                pl.BlockSpec((tm, d // 2), lambda i, te, nu, tn: (jnp.minimum(i, nu[0] - 1), 0)),
                pl.BlockSpec(memory_space=pl.ANY),
                pl.BlockSpec((None, 1, d_ff), lambda i, te, nu, tn: (te[i], 0, 0)),
                pl.BlockSpec((None, 1, d_ff), lambda i, te, nu, tn: (te[i], 0, 0)),
                pl.BlockSpec(memory_space=pl.ANY),
                pl.BlockSpec((None, 1, d), lambda i, te, nu, tn: (te[i], 0, 0)),
            ],
            out_specs=pl.BlockSpec((tm, d // 2), lambda i, te, nu, tn: (i, 0)),
            scratch_shapes=[pltpu.VMEM((d, 2 * d_ff), F32), pltpu.VMEM((d_ff, d), F32),
                            pltpu.SemaphoreType.DMA((2,)),
                            pltpu.VMEM((d, 2 * d_ff), BF16), pltpu.VMEM((d_ff, d), BF16),
                            pltpu.VMEM((tm, d_ff), BF16)],
        ),
        out_shape=jax.ShapeDtypeStruct((n_slots, d // 2), jnp.uint32),
        compiler_params=pltpu.CompilerParams(dimension_semantics=("arbitrary",),
                                             vmem_limit_bytes=VMEM_LIMIT),
        name="moe",
    )(tile_expert, n_used[None].astype(jnp.int32), next_expert, xs, w1, b1g, b1l, w2, b2)

    cut = n_prompt // 2
    z_a = _collect(ys, to_workers(pos_rows[:cut], COLLECT_CHUNKS[0]))
    z_b = _collect(ys, to_workers(pos_rows[cut:], COLLECT_CHUNKS[1]))
    nfin = norm_final[None]
    y_p = _final_call(h_all, gate, nfin, z_a, (0, cut), 0, FINAL_TILE, n_prompt, 0)
    y_p = _final_call(h_all, gate, nfin, z_b, (cut, n_prompt), 0, FINAL_TILE, n_prompt, cut, prev=y_p)
    y_s = _final_call(h_all, gate, nfin, z_b, (n_prompt, n_tok), n_prompt - cut, TOKEN_TILE, ns, 0)

    return (y_p.reshape(bp, seq, d), y_s.reshape(ns, 1, d),
            new_conv_p[None], new_rec_p[None],
            new_conv_s.reshape(1, ns, 2, d_conv), new_rec_s[None])
```

```python
import functools

import jax
import jax.numpy as jnp
from jax import lax
from jax.experimental import pallas as pl
from jax.experimental.pallas import tpu as pltpu
from jax.experimental.pallas import tpu_sc as plsc

F32 = jnp.float32
BF16 = jnp.bfloat16

N_HEADS = 4
HEAD = 128
N_META = 16
CHUNK = 64
SAFE_CHUNK = 16
DECAY_LIMIT = 0.0
TOP_K = 4
SWIGLU_LIMIT = 7.0
SWIGLU_ALPHA = 1.702
EPS = 1e-5

LANES = 128
MXU_N = 256
PROMPT_TILE = 256
TOKEN_TILE = 128
FINAL_TILE = 512
MOE_TILE = 512
SC_CORES = 2
SC_SUBCORES = 16
DISPATCH_CHUNK = 24
SC_RING = 4
VMEM_LIMIT = 56 * 1024 * 1024


def _dot(a, b):
    return jnp.dot(a, b, preferred_element_type=F32)


def _dot_nt(a, b):
    return lax.dot_general(a, b, (((1,), (1,)), ((), ())), preferred_element_type=F32)


def _dot_tn(a, b):
    return lax.dot_general(a, b, (((0,), (0,)), ((), ())), preferred_element_type=F32)


def _split3(x):
    hi = x.astype(BF16)
    r = x - hi.astype(F32)
    mid = r.astype(BF16)
    lo = (r - mid.astype(F32)).astype(BF16)
    return hi, mid, lo


def _pack_halves(x):
    n = x.shape[-1] // 2
    lo = pltpu.bitcast(x[:, :n].astype(BF16).astype(F32), jnp.uint32)
    hi = pltpu.bitcast(x[:, n:].astype(BF16).astype(F32), jnp.uint32)
    return (lo >> 16) | (hi & jnp.uint32(0xFFFF0000))


def _unpack_halves(u):
    lo = pltpu.bitcast(u << 16, F32)
    hi = pltpu.bitcast(u & jnp.uint32(0xFFFF0000), F32)
    return jnp.concatenate([lo, hi], axis=-1)


def _sigmoid(x):
    return 1.0 / (1.0 + jnp.exp(-x))


def _rms(x, g):
    ms = jnp.mean(x * x, axis=-1, keepdims=True)
    return x * lax.rsqrt(ms + EPS) * g


def _project(x, nm_ref, win_ref):
    return _dot(_rms(x, nm_ref[...]).astype(BF16), win_ref[...])


def _split_u(u, d_conv, d_rec):
    pts = [0, d_conv, 2 * d_conv, 3 * d_conv, 3 * d_conv + d_rec, 3 * d_conv + 2 * d_rec,
           3 * d_conv + 3 * d_rec, 3 * d_conv + 4 * d_rec]
    return [u[:, pts[i]:pts[i + 1]] for i in range(7)]


def _lower_bound(rlb_ref, layer):
    r = rlb_ref[...]
    e = jnp.exp(r - jnp.max(r, axis=0, keepdims=True))
    return jnp.sum(e[0:layer + 1], axis=0, keepdims=True) / jnp.sum(e, axis=0, keepdims=True)


def _forget(fx, lb):
    f = lb + (1.0 - lb) * _sigmoid(fx)
    return f, 1.0 - f


def _rec_out(o, og, rg):
    parts = []
    for h in range(N_HEADS):
        oh = o[:, h * HEAD:(h + 1) * HEAD]
        parts.append(oh * lax.rsqrt(jnp.mean(oh * oh, axis=-1, keepdims=True) + EPS))
    return jnp.concatenate(parts, axis=-1) * rg * (og * _sigmoid(og))


def _route(logits):
    n = logits.shape[-1]
    lane = lax.broadcasted_iota(jnp.int32, logits.shape, 1).astype(F32)
    work = logits
    tops, hots = [], []
    for _ in range(TOP_K):
        m = jnp.max(work, axis=-1, keepdims=True)
        first = jnp.min(jnp.where(work == m, lane, float(n)), axis=-1, keepdims=True)
        hot = lane == first
        tops.append(m)
        hots.append(hot)
        work = jnp.where(hot, -jnp.inf, work)
    es = [jnp.exp(t - tops[0]) for t in tops]
    den = es[0]
    for e in es[1:]:
        den = den + e
    sel = jnp.zeros_like(logits)
    gm = jnp.zeros_like(logits)
    for hot, e in zip(hots, es):
        sel = sel + jnp.where(hot, 1.0, 0.0)
        gm = gm + jnp.where(hot, e / den, 0.0)
    return sel, gm


def _tail(x, y, wout_ref, nf_ref, rw_ref, rb_ref):
    h = x + _dot(y.astype(BF16), wout_ref[...])
    xn = _rms(h, nf_ref[...])
    xh = xn.astype(BF16)
    xl = (xn - xh.astype(F32)).astype(BF16)
    rw = rw_ref[...]
    wh = rw.astype(BF16)
    wl = (rw - wh.astype(F32)).astype(BF16)
    logits = _dot(xh, wh) + _dot(xh, wl) + _dot(xl, wh) + rb_ref[...]
    sel, gm = _route(logits)
    return h, xn, sel, gm


def _rec_chunk(q, kk, v, lf, st_ref, tri, causal):
    c = q.shape[0]
    hi, mid, lo = _split3(lf)
    b = _dot(tri, hi) + _dot(tri, mid) + _dot(tri, lo)
    eb = jnp.exp(b)
    qe = (q * eb).astype(BF16)
    ke = (kk * jnp.exp(-b)).astype(BF16)
    vb = v.astype(BF16)
    eb_last = eb[c - 1:c]
    outs = []
    for h in range(N_HEADS):
        sl = slice(h * HEAD, (h + 1) * HEAD)
        st = st_ref[h]
        sc = jnp.where(causal, _dot_nt(qe[:, sl], ke[:, sl]), 0.0)
        outs.append(_dot(sc.astype(BF16), vb[:, sl]) + _dot_nt(qe[:, sl], st.astype(BF16)))
        st_ref[h] = (st + _dot_tn(vb[:, sl], ke[:, sl])) * eb_last[:, sl]
    return jnp.concatenate(outs, axis=-1)


def _rec_chunk_safe(q, kk, v, lf, st_ref):
    c = q.shape[0]
    row = lax.broadcasted_iota(jnp.int32, (c, 1), 0)
    b = lf
    shift = 1
    while shift < c:
        b = b + jnp.where(row >= shift, pltpu.roll(b, shift, axis=0), 0.0)
        shift *= 2
    eb = jnp.exp(b)
    qe = (q * eb).astype(BF16)
    kl = (kk * jnp.exp(b[c - 1:c] - b)).astype(BF16)
    vb = v.astype(BF16)
    outs = []
    for h in range(N_HEADS):
        sl = slice(h * HEAD, (h + 1) * HEAD)
        st = st_ref[h]
        bh, kh, qh, vh = b[:, sl], kk[:, sl], q[:, sl], v[:, sl]
        intra = jnp.zeros((c, HEAD), F32)
        for t in range(c):
            w = kh * jnp.exp(jnp.where(row <= t, bh[t:t + 1] - bh, -jnp.inf))
            score = jnp.sum(w * qh[t:t + 1], axis=-1, keepdims=True)
            intra = jnp.where(row == t, jnp.sum(score * vh, axis=0, keepdims=True), intra)
        outs.append(intra + _dot_nt(qe[:, sl], st.astype(BF16)))
        st_ref[h] = st * eb[c - 1:c, sl] + _dot_tn(vb[:, sl], kl[:, sl])
    return jnp.concatenate(outs, axis=-1)


def _mix_tile(u, layer, chunk, cw_ref, rlb_ref, convbuf, st, cv_out):
    tm = u.shape[0]
    d_conv = cw_ref.shape[-1]
    d_rec = rlb_ref.shape[-1]
    bg, cg, hv, q, fx, iv, og = _split_u(u, d_conv, d_rec)

    bx = bg * hv
    convbuf[8:8 + tm, :] = bx
    cw = cw_ref[...]
    conv = cw[0:1] * convbuf[6:6 + tm, :] + cw[1:2] * convbuf[7:7 + tm, :] + cw[2:3] * bx
    convbuf[6:8, :] = bx[tm - 2:tm]
    cv_out[...] = bx[tm - 2:tm]

    f, kk = _forget(fx, _lower_bound(rlb_ref, layer))
    lf = jnp.log(f)
    outs = []
    if chunk is None:
        for c in range(tm // SAFE_CHUNK):
            rs = slice(c * SAFE_CHUNK, (c + 1) * SAFE_CHUNK)
            outs.append(_rec_chunk_safe(q[rs], kk[rs], iv[rs], lf[rs], st))
    else:
        row = lax.broadcasted_iota(jnp.int32, (chunk, chunk), 0)
        col = lax.broadcasted_iota(jnp.int32, (chunk, chunk), 1)
        causal = row >= col
        tri = jnp.where(causal, 1.0, 0.0).astype(BF16)
        for c in range(tm // chunk):
            rs = slice(c * chunk, (c + 1) * chunk)
            outs.append(_rec_chunk(q[rs], kk[rs], iv[rs], lf[rs], st, tri, causal))
    return cg * conv, jnp.concatenate(outs, axis=0), og


def _load_state(st, convbuf, s0_ref, c0_ref):
    for h in range(N_HEADS):
        st[h] = s0_ref[h].T
    convbuf[6:8, :] = c0_ref[...]


def _meta_kernel(layer, chunk, x_ref, s0_ref, c0_ref, nm_ref, win_ref, cw_ref, rlb_ref, st_out, cv_out,
                 convbuf, st):
    _load_state(st, convbuf, s0_ref, c0_ref)
    _mix_tile(_project(x_ref[...], nm_ref, win_ref), layer, chunk, cw_ref, rlb_ref, convbuf, st, cv_out)
    for h in range(N_HEADS):
        st_out[h] = st[h].T


def _prompt_kernel(layer, tm, chunk, n_pairs, pairs_per_seq,
                   xr0_ref, x1_ref, xp2_ref, s0_ref, c0_ref, nm_ref, win_ref, cw_ref, rlb_ref,
                   rg_ref, wout_ref, nf_ref, rw_ref, rb_ref, hs_ref, xns_ref, sels_ref, gms_ref,
                   h_ref, xn_ref, eid_ref, rnk_ref, gat_ref, cnt_ref, st_out, cv_out,
                   convbuf, st, ua, ub, carry):
    step = pl.program_id(0)
    live = step < n_pairs

    @pl.when(step == 0)
    def _():
        ua[...] = _project(xr0_ref[...], nm_ref, win_ref)
        carry[...] = jnp.zeros_like(carry)

    @pl.when(jnp.logical_and(lax.rem(step, pairs_per_seq) == 0, live))
    def _():
        _load_state(st, convbuf, s0_ref, c0_ref)

    @pl.when(live)
    def _():
        def finish(x, u, rows):
            y_conv, o, og = _mix_tile(u, layer, chunk, cw_ref, rlb_ref, convbuf, st, cv_out)
            y = jnp.concatenate([y_conv, _rec_out(o, og, rg_ref[...])], axis=-1)
            h, xn, sel, gm = _tail(x, y, wout_ref, nf_ref, rw_ref, rb_ref)
            h_ref[rows, :] = h
            xn_ref[rows, :] = _pack_halves(xn)
            eid_ref[rows, :], rnk_ref[rows, :], gat_ref[rows, :] = _rank_block(sel, gm, carry)

        ub[...] = _project(x1_ref[...], nm_ref, win_ref)
        finish(xr0_ref[...], ua[...], slice(0, tm))
        ua[...] = _project(xp2_ref[...], nm_ref, win_ref)
        finish(x1_ref[...], ub[...], slice(tm, 2 * tm))

    @pl.when(jnp.logical_and(lax.rem(step, pairs_per_seq) == pairs_per_seq - 1, live))
    def _():
        for h in range(N_HEADS):
            st_out[h] = st[h].T

    @pl.when(step == n_pairs)
    def _():
        ns = hs_ref.shape[0]
        eid, rnk, gat = _rank_block(sels_ref[...], gms_ref[...], carry)
        for dst, val in ((h_ref, hs_ref[...]), (xn_ref, _pack_halves(xns_ref[...])),
                         (eid_ref, eid), (rnk_ref, rnk), (gat_ref, gat)):
            dst[0:ns, :] = val
            dst[ns:2 * tm, :] = jnp.zeros((2 * tm - ns, dst.shape[-1]), dst.dtype)
        cnt_ref[...] = carry[...]


def _const_spec(shape):
    return pl.BlockSpec(shape, lambda *_: (0,) * len(shape))


def _meta_call(x, weights, layer, chunk):
    assert x.shape[0] % (chunk or SAFE_CHUNK) == 0
    nm, win, cw, rlb = weights
    d_conv = cw.shape[-1]
    s0 = jnp.zeros((N_HEADS, HEAD, HEAD), F32)
    c0 = jnp.zeros((2, d_conv), F32)
    return pl.pallas_call(
        functools.partial(_meta_kernel, layer, chunk),
        out_shape=[jax.ShapeDtypeStruct(s0.shape, F32), jax.ShapeDtypeStruct(c0.shape, F32)],
        scratch_shapes=[pltpu.VMEM((x.shape[0] + 8, d_conv), F32), pltpu.VMEM(s0.shape, F32)],
        compiler_params=pltpu.CompilerParams(vmem_limit_bytes=VMEM_LIMIT),
        name="mixer_meta",
    )(x, s0, c0, nm, win, cw, rlb)


def _prompt_call(x, s0, c0, weights, tail_w, decode, layer, tm, chunk):
    nseq, length, d = x.shape
    nt = length // tm
    assert nt % 2 == 0 and decode[0].shape[0] <= 2 * tm and tm % (chunk or SAFE_CHUNK) == 0
    pairs_per_seq = nt // 2
    n_pairs = nseq * pairs_per_seq
    nm, win, cw, rlb = weights
    d_conv = cw.shape[-1]
    d_in = win.shape[-1]

    def tile_spec(offset):
        def index(s):
            tile = jnp.minimum(2 * s + offset, 2 * n_pairs - 1)
            return (tile // nt, lax.rem(tile, nt), 0)
        return pl.BlockSpec((None, tm, d), index)

    def seq_of(s):
        return jnp.minimum(s // pairs_per_seq, nseq - 1)

    consts = [s0, c0, nm, win, cw, rlb] + list(tail_w) + list(decode)
    n_exp = decode[2].shape[-1]
    tok = [(d, F32), (d // 2, jnp.uint32), (LANES, F32), (LANES, F32), (LANES, F32)]
    n_steps = n_pairs + 1
    return pl.pallas_call(
        functools.partial(_prompt_kernel, layer, tm, chunk, n_pairs, pairs_per_seq),
        grid=(n_steps,),
        in_specs=[tile_spec(0), tile_spec(1), tile_spec(2)] + [_const_spec(a.shape) for a in consts],
        out_specs=[pl.BlockSpec((2 * tm, w), lambda s: (s, 0)) for w, _ in tok] + [
            pl.BlockSpec((1, n_exp), lambda s: (0, 0)),
            pl.BlockSpec((None,) + s0.shape, lambda s: (seq_of(s), 0, 0, 0)),
            pl.BlockSpec((None,) + c0.shape, lambda s: (seq_of(s), 0, 0))],
        out_shape=[jax.ShapeDtypeStruct((n_steps * 2 * tm, w), t) for w, t in tok] + [
            jax.ShapeDtypeStruct((1, n_exp), F32),
            jax.ShapeDtypeStruct((nseq,) + s0.shape, F32), jax.ShapeDtypeStruct((nseq,) + c0.shape, F32)],
        scratch_shapes=[pltpu.VMEM((tm + 8, d_conv), F32), pltpu.VMEM(s0.shape, F32),
                        pltpu.VMEM((tm, d_in), F32), pltpu.VMEM((tm, d_in), F32),
                        pltpu.VMEM((1, n_exp), F32)],
        compiler_params=pltpu.CompilerParams(dimension_semantics=("arbitrary",),
                                             vmem_limit_bytes=VMEM_LIMIT),
        name="mixer_prompt",
    )(x, x, x, *consts)


def _sample_in_kernel(layer, x_ref, sc_ref, nm_ref, win_ref, cw_ref, rlb_ref,
                      yc_ref, nc_ref, f_ref, k_ref, q_ref, v_ref, og_ref):
    d_conv = cw_ref.shape[-1]
    d_rec = rlb_ref.shape[-1]
    u = _project(x_ref[...], nm_ref, win_ref)
    bg, cg, hv, q, fx, iv, og = _split_u(u, d_conv, d_rec)
    bx = bg * hv
    sc = sc_ref[...]
    s0, s1 = sc[:, :d_conv], sc[:, d_conv:]
    cw = cw_ref[...]
    yc_ref[...] = cg * (cw[0:1] * s0 + cw[1:2] * s1 + cw[2:3] * bx)
    nc_ref[...] = jnp.concatenate([s1, bx], axis=-1)
    f, kk = _forget(fx, _lower_bound(rlb_ref, layer))
    f_ref[...] = f
    k_ref[...] = kk
    q_ref[...] = q
    v_ref[...] = iv
    og_ref[...] = og


def _sample_state_kernel(group, f_ref, k_ref, q_ref, v_ref, s_ref, sn_ref, o_ref):
    for j in range(group):
        for h in range(N_HEADS):
            rs = slice(h * HEAD, (h + 1) * HEAD)
            fcol = f_ref[rs, j:j + 1]
            kcol = k_ref[rs, j:j + 1]
            qcol = q_ref[rs, j:j + 1]
            vrow = v_ref[j:j + 1, rs]
            sn = fcol * s_ref[j, h] + kcol * vrow
            sn_ref[j, h] = sn
            o_ref[j:j + 1, rs] = jnp.sum(qcol * sn, axis=0, keepdims=True)


def _sample_tail_kernel(x_ref, yc_ref, o_ref, og_ref, rg_ref, wout_ref, nf_ref, rw_ref, rb_ref,
                        h_ref, xn_ref, sel_ref, gm_ref):
    y = jnp.concatenate([yc_ref[...], _rec_out(o_ref[...], og_ref[...], rg_ref[...])], axis=-1)
    h, xn, sel, gm = _tail(x_ref[...], y, wout_ref, nf_ref, rw_ref, rb_ref)
    h_ref[...] = h
    xn_ref[...] = xn
    sel_ref[...] = sel
    gm_ref[...] = gm


def _rank_block(sel, gm, carry):
    tb, ne = sel.shape
    row = lax.broadcasted_iota(jnp.int32, (tb, tb), 0)
    col = lax.broadcasted_iota(jnp.int32, (tb, tb), 1)
    before = jnp.where(col < row, 1.0, 0.0).astype(BF16)
    selb = sel.astype(BF16)
    rank = _dot(before, selb) + carry[...]
    carry[...] = carry[...] + jnp.sum(sel, axis=0, keepdims=True)
    er = lax.broadcasted_iota(jnp.int32, (ne, ne), 0)
    ec = lax.broadcasted_iota(jnp.int32, (ne, ne), 1)
    lower = jnp.where(er < ec, 1.0, 0.0).astype(BF16)
    order = _dot(selb, lower)
    lane_e = lax.broadcasted_iota(jnp.int32, (tb, ne), 1).astype(F32)
    lane = lax.broadcasted_iota(jnp.int32, (tb, LANES), 1)
    eid = jnp.zeros((tb, LANES), F32)
    rnk = jnp.zeros((tb, LANES), F32)
    gat = jnp.zeros((tb, LANES), F32)
    for k in range(TOP_K):
        pick = jnp.where(order == float(k), sel, 0.0)
        eid = jnp.where(lane == k, jnp.sum(pick * lane_e, axis=-1, keepdims=True), eid)
        rnk = jnp.where(lane == k, jnp.sum(pick * rank, axis=-1, keepdims=True), rnk)
        gat = jnp.where(lane == k, jnp.sum(pick * gm, axis=-1, keepdims=True), gat)
    return eid, rnk, gat


def _dispatch(xn, pos_w, n_slots):
    n_workers, n_chunks, top_k, ch = pos_w.shape
    assert n_workers == SC_CORES * SC_SUBCORES and ch % 8 == 0 and ch <= LANES
    d = xn.shape[1]
    mesh = plsc.VectorSubcoreMesh(core_axis_name="c", subcore_axis_name="s")

    @functools.partial(
        pl.kernel, mesh=mesh,
        out_type=jax.ShapeDtypeStruct((n_slots, d), xn.dtype),
        scratch_types=[pltpu.VMEM((n_chunks, top_k, ch), jnp.int32), pltpu.VMEM((SC_RING, ch, d), xn.dtype),
                       pltpu.SemaphoreType.DMA((SC_RING,)), pltpu.SemaphoreType.DMA((SC_RING,))],
        name="dispatch",
    )
    def run(xn_hbm, pos_hbm, xs_hbm, idx_v, rows_v, sem_r, sem_w):
        wid = lax.axis_index("s") * SC_CORES + lax.axis_index("c")

        def read(c, b):
            src = xn_hbm.at[pl.ds((wid * n_chunks + c) * ch, ch)]
            return pltpu.make_async_copy(src, rows_v.at[b], sem_r.at[b])

        def write(c, b, k):
            return pltpu.make_async_copy(rows_v.at[b], xs_hbm.at[idx_v.at[c, k]], sem_w.at[b])

        pltpu.sync_copy(pos_hbm.at[wid], idx_v)
        ahead = SC_RING - 1
        for c0 in range(min(ahead, n_chunks)):
            read(c0, c0).start()

        @pl.loop(0, n_chunks)
        def _(c):
            b = lax.rem(c, SC_RING)
            read(c, b).wait()
            for k in range(top_k):
                write(c, b, k).start()

            @pl.when(c >= 1)
            def _():
                for k in range(top_k):
                    write(c - 1, lax.rem(c - 1, SC_RING), k).wait()

            @pl.when(c + ahead < n_chunks)
            def _():
                read(c + ahead, lax.rem(c + ahead, SC_RING)).start()

        for k in range(top_k):
            write(n_chunks - 1, (n_chunks - 1) % SC_RING, k).wait()

    return run(xn, pos_w)


def _collect(ys, pos_w):
    n_workers, n_chunks, top_k, ch = pos_w.shape
    assert n_workers == SC_CORES * SC_SUBCORES and ch % 8 == 0 and ch <= LANES
    d = ys.shape[1]
    mesh = plsc.VectorSubcoreMesh(core_axis_name="c", subcore_axis_name="s")

    @functools.partial(
        pl.kernel, mesh=mesh,
        out_type=jax.ShapeDtypeStruct((n_workers * n_chunks * ch, top_k * d), ys.dtype),
        scratch_types=[pltpu.VMEM((n_chunks, top_k, ch), jnp.int32), pltpu.VMEM((top_k, ch, d), ys.dtype),
                       pltpu.SemaphoreType.DMA((top_k,)), pltpu.SemaphoreType.DMA((top_k,))],
        name="collect",
    )
    def run(ys_hbm, pos_hbm, out_hbm, idx_v, rows_v, sem_r, sem_w):
        wid = lax.axis_index("s") * SC_CORES + lax.axis_index("c")

        def read(c, k):
            return pltpu.make_async_copy(ys_hbm.at[idx_v.at[c, k]], rows_v.at[k], sem_r.at[k])

        def write(c, k):
            dst = out_hbm.at[pl.ds((wid * n_chunks + c) * ch, ch), pl.ds(k * d, d)]
            return pltpu.make_async_copy(rows_v.at[k], dst, sem_w.at[k])

        pltpu.sync_copy(pos_hbm.at[wid], idx_v)
        for k in range(top_k - 1):
            read(0, k).start()

        @pl.loop(0, n_chunks)
        def _(c):
            for k in range(top_k):
                read(c, k).wait()
                write(c, k).start()
                if k >= 1:
                    write(c, k - 1).wait()

                    @pl.when(c + 1 < n_chunks)
                    def _():
                        read(c + 1, k - 1).start()
                else:
                    @pl.when(c >= 1)
                    def _():
                        write(c - 1, top_k - 1).wait()
                    read(c, top_k - 1).start()

        write(n_chunks - 1, top_k - 1).wait()

    return run(ys, pos_w)


def _moe_kernel(tm, te_ref, nu_ref, tn_ref, x_ref, w1_hbm, bg_ref, bl_ref, w2_hbm, b2_ref,
                ys_ref, w1f, w2f, sem, w1p, w2b, act):
    i = pl.program_id(0)
    n_used = nu_ref[0]
    d_ff2 = w1f.shape[-1]
    n_blk = d_ff2 // MXU_N
    expert = te_ref[i]

    def weight_copies(e):
        return (pltpu.make_async_copy(w1_hbm.at[e], w1f, sem.at[0]),
                pltpu.make_async_copy(w2_hbm.at[e], w2f, sem.at[1]))

    @pl.when(i == 0)
    def _():
        for cp in weight_copies(expert):
            cp.start()

    @pl.when(i >= n_used)
    def _():
        ys_ref[...] = jnp.zeros_like(ys_ref)

    prev = te_ref[jnp.maximum(i - 1, 0)]
    changed = jnp.logical_or(i == 0, expert != prev)

    @pl.when(jnp.logical_and(changed, i < n_used))
    def _():
        for cp in weight_copies(expert):
            cp.wait()
        r = lax.broadcasted_iota(jnp.int32, (MXU_N, MXU_N), 0)
        c = lax.broadcasted_iota(jnp.int32, (MXU_N, MXU_N), 1)
        src = jnp.where(c < MXU_N // 2, 2 * c, 2 * (c - MXU_N // 2) + 1)
        perm = jnp.where(r == src, 1.0, 0.0).astype(BF16)
        for blk in range(n_blk):
            cs = slice(blk * MXU_N, (blk + 1) * MXU_N)
            w1p[:, cs] = _dot(w1f[:, cs].astype(BF16), perm).astype(BF16)
        w2b[...] = w2f[...].astype(BF16)

        @pl.when(tn_ref[i] != expert)
        def _():
            for cp in weight_copies(tn_ref[i]):
                cp.start()

    @pl.when(i < n_used)
    def _():
        x = _unpack_halves(x_ref[...]).astype(BF16)
        for blk in range(n_blk):
            a = _dot(x, w1p[:, blk * MXU_N:(blk + 1) * MXU_N])
            half = MXU_N // 2
            hs = slice(blk * half, (blk + 1) * half)
            glu = jnp.minimum(a[:, :half] + bg_ref[:, hs], SWIGLU_LIMIT)
            lin = jnp.clip(a[:, half:] + bl_ref[:, hs], -SWIGLU_LIMIT, SWIGLU_LIMIT)
            act[:, hs] = (glu * _sigmoid(SWIGLU_ALPHA * glu) * (lin + 1.0)).astype(BF16)
        ys_ref[...] = _pack_halves(_dot(act[...], w2b[...]) + b2_ref[...])


def _final_kernel(h_ref, gate_ref, nfin_ref, z_ref, y_ref):
    w = z_ref.shape[-1] // TOP_K
    g = gate_ref[...]
    out = h_ref[...]
    for k in range(TOP_K):
        out = out + g[:, k:k + 1] * _unpack_halves(z_ref[:, k * w:(k + 1) * w])
    y_ref[...] = _rms(out, nfin_ref[...])


def _final_call(h_all, gate, nfin, z, first_row, n_rows, tile):
    d = h_all.shape[-1]
    off = first_row // tile
    assert first_row % tile == 0 and n_rows % tile == 0
    return pl.pallas_call(
        _final_kernel,
        grid=(n_rows // tile,),
        in_specs=[
            pl.BlockSpec((tile, d), lambda i: (i + off, 0)),
            pl.BlockSpec((tile, gate.shape[-1]), lambda i: (i + off, 0)),
            pl.BlockSpec((1, d), lambda i: (0, 0)),
            pl.BlockSpec((tile, z.shape[-1]), lambda i: (i + off, 0)),
        ],
        out_specs=pl.BlockSpec((tile, d), lambda i: (i, 0)),
        out_shape=jax.ShapeDtypeStruct((n_rows, d), F32),
        compiler_params=pltpu.CompilerParams(dimension_semantics=("arbitrary",),
                                             vmem_limit_bytes=VMEM_LIMIT),
        name="final",
    )(h_all, gate, nfin, z)


def kernel(x_prompt, x_sample, state_conv, state_rec, meta_tokens, norm_mix, w_in, conv_w,
           rec_lower_bound, rec_norm, w_out, norm_ffn, router_w, router_b, expert_w1, expert_b1,
           expert_w2, expert_b2, norm_final):
    depth = norm_mix.shape[0]
    assert depth == 1, "single-layer step"
    layer = 0
    bp, seq, d = x_prompt.shape
    ns = x_sample.shape[0]
    assert x_sample.shape[1] == 1 and ns == TOKEN_TILE
    d_conv = conv_w.shape[-1]
    d_rec = rec_lower_bound.shape[-1]
    assert state_conv.shape[2] == 2 and d_rec == N_HEADS * HEAD
    n_exp = router_w.shape[-1]
    d_ff = expert_w2.shape[2]
    n_prompt = bp * seq
    n_tok = n_prompt + ns
    assert seq % PROMPT_TILE == 0 and n_prompt % TOKEN_TILE == 0

    nm = norm_mix[layer][None]
    win = w_in[layer].astype(BF16)
    cw = conv_w[layer]
    rlb = rec_lower_bound
    rg = rec_norm[layer][None]
    wout = w_out[layer].astype(BF16)
    nf = norm_ffn[layer][None]
    rw = router_w[layer]
    rb = router_b[layer][None]
    mix_w = (nm, win, cw, rlb)
    tail_w = (rg, wout, nf, rw, rb)

    lb = jnp.sum(jax.nn.softmax(rlb, axis=0)[:layer + 1], axis=0)
    fast = CHUNK * jnp.max(-jnp.log(lb)) < DECAY_LIMIT

    st_meta, cv_meta = lax.cond(
        fast, lambda: _meta_call(meta_tokens, mix_w, layer, N_META),
        lambda: _meta_call(meta_tokens, mix_w, layer, None))

    xs = x_sample.reshape(ns, d)
    wide = jax.ShapeDtypeStruct((ns, d_rec), F32)
    y_conv_s, new_conv_s, f_s, k_s, q_s, v_s, og_s = pl.pallas_call(
        functools.partial(_sample_in_kernel, layer),
        out_shape=[jax.ShapeDtypeStruct((ns, d_conv), F32), jax.ShapeDtypeStruct((ns, 2 * d_conv), F32),
                   wide, wide, wide, wide, wide],
        compiler_params=pltpu.CompilerParams(vmem_limit_bytes=VMEM_LIMIT),
        name="sample_in",
    )(xs, state_conv[layer].reshape(ns, 2 * d_conv), nm, win, cw, rlb)

    group = 8
    n_grp = ns // group

    def cols(a):
        return a.T.reshape(d_rec, n_grp, group).transpose(1, 0, 2)

    col_spec = pl.BlockSpec((None, d_rec, group), lambda g: (g, 0, 0))
    st_spec = pl.BlockSpec((group, N_HEADS, HEAD, HEAD), lambda g: (g, 0, 0, 0))
    row_spec = pl.BlockSpec((group, d_rec), lambda g: (g, 0))
    new_rec_s, o_s = pl.pallas_call(
        functools.partial(_sample_state_kernel, group),
        grid=(n_grp,),
        in_specs=[col_spec, col_spec, col_spec, row_spec, st_spec],
        out_specs=[st_spec, row_spec],
        out_shape=[jax.ShapeDtypeStruct(state_rec.shape[1:], F32), wide],
        compiler_params=pltpu.CompilerParams(dimension_semantics=("arbitrary",),
                                             vmem_limit_bytes=VMEM_LIMIT),
        name="sample_state",
    )(cols(f_s), cols(k_s), cols(q_s), v_s, state_rec[layer])

    decode = pl.pallas_call(
        _sample_tail_kernel,
        out_shape=[jax.ShapeDtypeStruct((ns, d), F32), jax.ShapeDtypeStruct((ns, d), F32),
                   jax.ShapeDtypeStruct((ns, n_exp), F32), jax.ShapeDtypeStruct((ns, n_exp), F32)],
        compiler_params=pltpu.CompilerParams(vmem_limit_bytes=VMEM_LIMIT),
        name="sample_tail",
    )(xs, y_conv_s, o_s, og_s, rg, wout, nf, rw, rb)

    h_all, xn_all, eid, rnk, gate, counts, new_rec_p, new_conv_p = lax.cond(
        fast,
        lambda: _prompt_call(x_prompt, st_meta, cv_meta, mix_w, tail_w, decode, layer, PROMPT_TILE, CHUNK),
        lambda: _prompt_call(x_prompt, st_meta, cv_meta, mix_w, tail_w, decode, layer, PROMPT_TILE, None))

    tm = MOE_TILE
    n_tiles = (n_tok * TOP_K) // tm + n_exp
    counts = counts[0].astype(jnp.int32)
    tiles_e = (counts + tm - 1) // tm
    tile_end = jnp.cumsum(tiles_e)
    n_used = tile_end[-1]
    offs = (tile_end - tiles_e) * tm
    eid4 = eid[:n_tok, :TOP_K].astype(jnp.int32)
    experts = jnp.arange(n_exp, dtype=jnp.int32)
    off4 = jnp.sum(jnp.where(eid4[..., None] == experts, offs, 0), axis=-1)
    pos = off4 + rnk[:n_tok, :TOP_K].astype(jnp.int32)
    tile_ids = jnp.minimum(jnp.arange(n_tiles, dtype=jnp.int32), n_used - 1)
    tile_expert = jnp.minimum(jnp.sum((tile_end[None, :] <= tile_ids[:, None]).astype(jnp.int32), axis=1),
                              n_exp - 1)
    after = tile_end[tile_expert]
    next_expert = jnp.where(after < n_used, tile_expert[jnp.minimum(after, n_tiles - 1)], tile_expert)

    n_rows = xn_all.shape[0]
    n_workers = SC_CORES * SC_SUBCORES
    assert n_rows % (n_workers * DISPATCH_CHUNK) == 0
    n_spare = (n_rows - n_tok) * TOP_K
    n_slots = n_tiles * tm + n_spare
    spare = n_tiles * tm + jnp.arange(n_spare, dtype=jnp.int32).reshape(-1, TOP_K)
    pos_rows = jnp.concatenate([pos, spare], axis=0)

    def to_workers(p, ch):
        assert p.shape[0] % (n_workers * ch) == 0
        return p.reshape(n_workers, -1, ch, TOP_K).transpose(0, 1, 3, 2)

    xs = _dispatch(xn_all, to_workers(pos_rows, DISPATCH_CHUNK), n_slots)

    w1 = expert_w1[layer]
    w2 = expert_w2[layer]
    b1 = expert_b1[layer]
    b1g = b1[:, 0::2][:, None, :]
    b1l = b1[:, 1::2][:, None, :]
    b2 = expert_b2[layer][:, None, :]
    ys = pl.pallas_call(
        functools.partial(_moe_kernel, tm),
        grid_spec=pltpu.PrefetchScalarGridSpec(
            num_scalar_prefetch=3,
            grid=(n_tiles,),
            in_specs=[
                pl.BlockSpec((tm, d // 2), lambda i, te, nu, tn: (jnp.minimum(i, nu[0] - 1), 0)),
                pl.BlockSpec(memory_space=pl.ANY),
                pl.BlockSpec((None, 1, d_ff), lambda i, te, nu, tn: (te[i], 0, 0)),
                pl.BlockSpec((None, 1, d_ff), lambda i, te, nu, tn: (te[i], 0, 0)),
                pl.BlockSpec(memory_space=pl.ANY),
                pl.BlockSpec((None, 1, d), lambda i, te, nu, tn: (te[i], 0, 0)),
            ],
            out_specs=pl.BlockSpec((tm, d // 2), lambda i, te, nu, tn: (i, 0)),
            scratch_shapes=[pltpu.VMEM((d, 2 * d_ff), F32), pltpu.VMEM((d_ff, d), F32),
                            pltpu.SemaphoreType.DMA((2,)),
                            pltpu.VMEM((d, 2 * d_ff), BF16), pltpu.VMEM((d_ff, d), BF16),
                            pltpu.VMEM((tm, d_ff), BF16)],
        ),
        out_shape=jax.ShapeDtypeStruct((n_slots, d // 2), jnp.uint32),
        compiler_params=pltpu.CompilerParams(dimension_semantics=("arbitrary",),
                                             vmem_limit_bytes=VMEM_LIMIT),
        name="moe",
    )(tile_expert, n_used[None].astype(jnp.int32), next_expert, xs, w1, b1g, b1l, w2, b2)

    z = _collect(ys, to_workers(pos_rows, DISPATCH_CHUNK))
    y_p = _final_call(h_all, gate, norm_final[None], z, 0, n_prompt, FINAL_TILE)
    y_s = _final_call(h_all, gate, norm_final[None], z, n_prompt, ns, TOKEN_TILE)

    return (y_p.reshape(bp, seq, d), y_s.reshape(ns, 1, d),
            new_conv_p[None], new_rec_p[None],
            new_conv_s.reshape(1, ns, 2, d_conv), new_rec_s[None])
```

```python
import functools

import jax
import jax.numpy as jnp
from jax import lax
from jax.experimental import pallas as pl
from jax.experimental.pallas import tpu as pltpu
from jax.experimental.pallas import tpu_sc as plsc

F32 = jnp.float32
BF16 = jnp.bfloat16

N_HEADS = 4
HEAD = 128
N_META = 16
CHUNK = 64
SAFE_CHUNK = 16
DECAY_LIMIT = 80.0
TOP_K = 4
SWIGLU_LIMIT = 7.0
SWIGLU_ALPHA = 1.702
EPS = 1e-5

LANES = 128
MXU_N = 256
PROMPT_TILE = 256
TOKEN_TILE = 128
FINAL_TILE = 512
MOE_TILE = 512
SC_CORES = 2
SC_SUBCORES = 16
DISPATCH_CHUNK = 24
SC_RING = 4
VMEM_LIMIT = 56 * 1024 * 1024


def _dot(a, b):
    return jnp.dot(a, b, preferred_element_type=F32)


def _dot_nt(a, b):
    return lax.dot_general(a, b, (((1,), (1,)), ((), ())), preferred_element_type=F32)


def _dot_tn(a, b):
    return lax.dot_general(a, b, (((0,), (0,)), ((), ())), preferred_element_type=F32)


def _split3(x):
    hi = x.astype(BF16)
    r = x - hi.astype(F32)
    mid = r.astype(BF16)
    lo = (r - mid.astype(F32)).astype(BF16)
    return hi, mid, lo


def _pack_halves(x):
    n = x.shape[-1] // 2
    lo = pltpu.bitcast(x[:, :n].astype(BF16).astype(F32), jnp.uint32)
    hi = pltpu.bitcast(x[:, n:].astype(BF16).astype(F32), jnp.uint32)
    return (lo >> 16) | (hi & jnp.uint32(0xFFFF0000))


def _unpack_halves(u):
    lo = pltpu.bitcast(u << 16, F32)
    hi = pltpu.bitcast(u & jnp.uint32(0xFFFF0000), F32)
    return jnp.concatenate([lo, hi], axis=-1)


def _sigmoid(x):
    return 1.0 / (1.0 + jnp.exp(-x))


def _rms(x, g):
    ms = jnp.mean(x * x, axis=-1, keepdims=True)
    return x * lax.rsqrt(ms + EPS) * g


def _project(x, nm_ref, win_ref):
    return _dot(_rms(x, nm_ref[...]).astype(BF16), win_ref[...])


def _split_u(u, d_conv, d_rec):
    pts = [0, d_conv, 2 * d_conv, 3 * d_conv, 3 * d_conv + d_rec, 3 * d_conv + 2 * d_rec,
           3 * d_conv + 3 * d_rec, 3 * d_conv + 4 * d_rec]
    return [u[:, pts[i]:pts[i + 1]] for i in range(7)]


def _lower_bound(rlb_ref, layer):
    r = rlb_ref[...]
    e = jnp.exp(r - jnp.max(r, axis=0, keepdims=True))
    return jnp.sum(e[0:layer + 1], axis=0, keepdims=True) / jnp.sum(e, axis=0, keepdims=True)


def _forget(fx, lb):
    f = lb + (1.0 - lb) * _sigmoid(fx)
    return f, 1.0 - f


def _rec_out(o, og, rg):
    parts = []
    for h in range(N_HEADS):
        oh = o[:, h * HEAD:(h + 1) * HEAD]
        parts.append(oh * lax.rsqrt(jnp.mean(oh * oh, axis=-1, keepdims=True) + EPS))
    return jnp.concatenate(parts, axis=-1) * rg * (og * _sigmoid(og))


def _route(logits):
    n = logits.shape[-1]
    lane = lax.broadcasted_iota(jnp.int32, logits.shape, 1).astype(F32)
    work = logits
    tops, hots = [], []
    for _ in range(TOP_K):
        m = jnp.max(work, axis=-1, keepdims=True)
        first = jnp.min(jnp.where(work == m, lane, float(n)), axis=-1, keepdims=True)
        hot = lane == first
        tops.append(m)
        hots.append(hot)
        work = jnp.where(hot, -jnp.inf, work)
    es = [jnp.exp(t - tops[0]) for t in tops]
    den = es[0]
    for e in es[1:]:
        den = den + e
    sel = jnp.zeros_like(logits)
    gm = jnp.zeros_like(logits)
    for hot, e in zip(hots, es):
        sel = sel + jnp.where(hot, 1.0, 0.0)
        gm = gm + jnp.where(hot, e / den, 0.0)
    return sel, gm


def _tail(x, y, wout_ref, nf_ref, rw_ref, rb_ref):
    h = x + _dot(y.astype(BF16), wout_ref[...])
    xn = _rms(h, nf_ref[...])
    xh = xn.astype(BF16)
    xl = (xn - xh.astype(F32)).astype(BF16)
    rw = rw_ref[...]
    wh = rw.astype(BF16)
    wl = (rw - wh.astype(F32)).astype(BF16)
    logits = _dot(xh, wh) + _dot(xh, wl) + _dot(xl, wh) + rb_ref[...]
    sel, gm = _route(logits)
    return h, xn, sel, gm


def _rec_chunk(q, kk, v, lf, st_ref, tri, causal):
    c = q.shape[0]
    hi, mid, lo = _split3(lf)
    b = _dot(tri, hi) + _dot(tri, mid) + _dot(tri, lo)
    eb = jnp.exp(b)
    qe = (q * eb).astype(BF16)
    ke = (kk * jnp.exp(-b)).astype(BF16)
    vb = v.astype(BF16)
    eb_last = eb[c - 1:c]
    outs = []
    for h in range(N_HEADS):
        sl = slice(h * HEAD, (h + 1) * HEAD)
        st = st_ref[h]
        sc = jnp.where(causal, _dot_nt(qe[:, sl], ke[:, sl]), 0.0)
        outs.append(_dot(sc.astype(BF16), vb[:, sl]) + _dot_nt(qe[:, sl], st.astype(BF16)))
        st_ref[h] = (st + _dot_tn(vb[:, sl], ke[:, sl])) * eb_last[:, sl]
    return jnp.concatenate(outs, axis=-1)


def _rec_chunk_safe(q, kk, v, lf, st_ref):
    c = q.shape[0]
    row = lax.broadcasted_iota(jnp.int32, (c, 1), 0)
    b = lf
    shift = 1
    while shift < c:
        b = b + jnp.where(row >= shift, pltpu.roll(b, shift, axis=0), 0.0)
        shift *= 2
    eb = jnp.exp(b)
    qe = (q * eb).astype(BF16)
    kl = (kk * jnp.exp(b[c - 1:c] - b)).astype(BF16)
    vb = v.astype(BF16)
    outs = []
    for h in range(N_HEADS):
        sl = slice(h * HEAD, (h + 1) * HEAD)
        st = st_ref[h]
        bh, kh, qh, vh = b[:, sl], kk[:, sl], q[:, sl], v[:, sl]
        intra = jnp.zeros((c, HEAD), F32)
        for t in range(c):
            w = kh * jnp.exp(jnp.where(row <= t, bh[t:t + 1] - bh, -jnp.inf))
            score = jnp.sum(w * qh[t:t + 1], axis=-1, keepdims=True)
            intra = jnp.where(row == t, jnp.sum(score * vh, axis=0, keepdims=True), intra)
        outs.append(intra + _dot_nt(qe[:, sl], st.astype(BF16)))
        st_ref[h] = st * eb[c - 1:c, sl] + _dot_tn(vb[:, sl], kl[:, sl])
    return jnp.concatenate(outs, axis=-1)


def _mix_tile(u, layer, chunk, cw_ref, rlb_ref, convbuf, st, cv_out):
    tm = u.shape[0]
    d_conv = cw_ref.shape[-1]
    d_rec = rlb_ref.shape[-1]
    bg, cg, hv, q, fx, iv, og = _split_u(u, d_conv, d_rec)

    bx = bg * hv
    convbuf[8:8 + tm, :] = bx
    cw = cw_ref[...]
    conv = cw[0:1] * convbuf[6:6 + tm, :] + cw[1:2] * convbuf[7:7 + tm, :] + cw[2:3] * bx
    convbuf[6:8, :] = bx[tm - 2:tm]
    cv_out[...] = bx[tm - 2:tm]

    f, kk = _forget(fx, _lower_bound(rlb_ref, layer))
    lf = jnp.log(f)
    outs = []
    if chunk is None:
        for c in range(tm // SAFE_CHUNK):
            rs = slice(c * SAFE_CHUNK, (c + 1) * SAFE_CHUNK)
            outs.append(_rec_chunk_safe(q[rs], kk[rs], iv[rs], lf[rs], st))
    else:
        row = lax.broadcasted_iota(jnp.int32, (chunk, chunk), 0)
        col = lax.broadcasted_iota(jnp.int32, (chunk, chunk), 1)
        causal = row >= col
        tri = jnp.where(causal, 1.0, 0.0).astype(BF16)
        for c in range(tm // chunk):
            rs = slice(c * chunk, (c + 1) * chunk)
            outs.append(_rec_chunk(q[rs], kk[rs], iv[rs], lf[rs], st, tri, causal))
    return cg * conv, jnp.concatenate(outs, axis=0), og


def _load_state(st, convbuf, s0_ref, c0_ref):
    for h in range(N_HEADS):
        st[h] = s0_ref[h].T
    convbuf[6:8, :] = c0_ref[...]


def _meta_kernel(layer, chunk, x_ref, s0_ref, c0_ref, nm_ref, win_ref, cw_ref, rlb_ref, st_out, cv_out,
                 convbuf, st):
    _load_state(st, convbuf, s0_ref, c0_ref)
    _mix_tile(_project(x_ref[...], nm_ref, win_ref), layer, chunk, cw_ref, rlb_ref, convbuf, st, cv_out)
    for h in range(N_HEADS):
        st_out[h] = st[h].T


def _prompt_kernel(layer, tm, chunk, n_pairs, pairs_per_seq,
                   xr0_ref, x1_ref, xp2_ref, s0_ref, c0_ref, nm_ref, win_ref, cw_ref, rlb_ref,
                   rg_ref, wout_ref, nf_ref, rw_ref, rb_ref, hs_ref, xns_ref, sels_ref, gms_ref,
                   h_ref, xn_ref, eid_ref, rnk_ref, gat_ref, cnt_ref, st_out, cv_out,
                   convbuf, st, ua, ub, carry):
    step = pl.program_id(0)
    live = step < n_pairs

    @pl.when(step == 0)
    def _():
        ua[...] = _project(xr0_ref[...], nm_ref, win_ref)
        carry[...] = jnp.zeros_like(carry)

    @pl.when(jnp.logical_and(lax.rem(step, pairs_per_seq) == 0, live))
    def _():
        _load_state(st, convbuf, s0_ref, c0_ref)

    @pl.when(live)
    def _():
        def finish(x, u, rows):
            y_conv, o, og = _mix_tile(u, layer, chunk, cw_ref, rlb_ref, convbuf, st, cv_out)
            y = jnp.concatenate([y_conv, _rec_out(o, og, rg_ref[...])], axis=-1)
            h, xn, sel, gm = _tail(x, y, wout_ref, nf_ref, rw_ref, rb_ref)
            h_ref[rows, :] = h
            xn_ref[rows, :] = _pack_halves(xn)
            eid_ref[rows, :], rnk_ref[rows, :], gat_ref[rows, :] = _rank_block(sel, gm, carry)

        ub[...] = _project(x1_ref[...], nm_ref, win_ref)
        finish(xr0_ref[...], ua[...], slice(0, tm))
        ua[...] = _project(xp2_ref[...], nm_ref, win_ref)
        finish(x1_ref[...], ub[...], slice(tm, 2 * tm))

    @pl.when(jnp.logical_and(lax.rem(step, pairs_per_seq) == pairs_per_seq - 1, live))
    def _():
        for h in range(N_HEADS):
            st_out[h] = st[h].T

    @pl.when(step == n_pairs)
    def _():
        ns = hs_ref.shape[0]
        eid, rnk, gat = _rank_block(sels_ref[...], gms_ref[...], carry)
        for dst, val in ((h_ref, hs_ref[...]), (xn_ref, _pack_halves(xns_ref[...])),
                         (eid_ref, eid), (rnk_ref, rnk), (gat_ref, gat)):
            dst[0:ns, :] = val
            dst[ns:2 * tm, :] = jnp.zeros((2 * tm - ns, dst.shape[-1]), dst.dtype)
        cnt_ref[...] = carry[...]


def _const_spec(shape):
    return pl.BlockSpec(shape, lambda *_: (0,) * len(shape))


def _meta_call(x, weights, layer, chunk):
    assert x.shape[0] % (chunk or SAFE_CHUNK) == 0
    nm, win, cw, rlb = weights
    d_conv = cw.shape[-1]
    s0 = jnp.zeros((N_HEADS, HEAD, HEAD), F32)
    c0 = jnp.zeros((2, d_conv), F32)
    return pl.pallas_call(
        functools.partial(_meta_kernel, layer, chunk),
        out_shape=[jax.ShapeDtypeStruct(s0.shape, F32), jax.ShapeDtypeStruct(c0.shape, F32)],
        scratch_shapes=[pltpu.VMEM((x.shape[0] + 8, d_conv), F32), pltpu.VMEM(s0.shape, F32)],
        compiler_params=pltpu.CompilerParams(vmem_limit_bytes=VMEM_LIMIT),
        name="mixer_meta",
    )(x, s0, c0, nm, win, cw, rlb)


def _prompt_call(x, s0, c0, weights, tail_w, decode, layer, tm, chunk):
    nseq, length, d = x.shape
    nt = length // tm
    assert nt % 2 == 0 and decode[0].shape[0] <= 2 * tm and tm % (chunk or SAFE_CHUNK) == 0
    pairs_per_seq = nt // 2
    n_pairs = nseq * pairs_per_seq
    nm, win, cw, rlb = weights
    d_conv = cw.shape[-1]
    d_in = win.shape[-1]

    def tile_spec(offset):
        def index(s):
            tile = jnp.minimum(2 * s + offset, 2 * n_pairs - 1)
            return (tile // nt, lax.rem(tile, nt), 0)
        return pl.BlockSpec((None, tm, d), index)

    def seq_of(s):
        return jnp.minimum(s // pairs_per_seq, nseq - 1)

    consts = [s0, c0, nm, win, cw, rlb] + list(tail_w) + list(decode)
    n_exp = decode[2].shape[-1]
    tok = [(d, F32), (d // 2, jnp.uint32), (LANES, F32), (LANES, F32), (LANES, F32)]
    n_steps = n_pairs + 1
    return pl.pallas_call(
        functools.partial(_prompt_kernel, layer, tm, chunk, n_pairs, pairs_per_seq),
        grid=(n_steps,),
        in_specs=[tile_spec(0), tile_spec(1), tile_spec(2)] + [_const_spec(a.shape) for a in consts],
        out_specs=[pl.BlockSpec((2 * tm, w), lambda s: (s, 0)) for w, _ in tok] + [
            pl.BlockSpec((1, n_exp), lambda s: (0, 0)),
            pl.BlockSpec((None,) + s0.shape, lambda s: (seq_of(s), 0, 0, 0)),
            pl.BlockSpec((None,) + c0.shape, lambda s: (seq_of(s), 0, 0))],
        out_shape=[jax.ShapeDtypeStruct((n_steps * 2 * tm, w), t) for w, t in tok] + [
            jax.ShapeDtypeStruct((1, n_exp), F32),
            jax.ShapeDtypeStruct((nseq,) + s0.shape, F32), jax.ShapeDtypeStruct((nseq,) + c0.shape, F32)],
        scratch_shapes=[pltpu.VMEM((tm + 8, d_conv), F32), pltpu.VMEM(s0.shape, F32),
                        pltpu.VMEM((tm, d_in), F32), pltpu.VMEM((tm, d_in), F32),
                        pltpu.VMEM((1, n_exp), F32)],
        compiler_params=pltpu.CompilerParams(dimension_semantics=("arbitrary",),
                                             vmem_limit_bytes=VMEM_LIMIT),
        name="mixer_prompt",
    )(x, x, x, *consts)


def _sample_in_kernel(layer, x_ref, sc_ref, nm_ref, win_ref, cw_ref, rlb_ref,
                      yc_ref, nc_ref, f_ref, k_ref, q_ref, v_ref, og_ref):
    d_conv = cw_ref.shape[-1]
    d_rec = rlb_ref.shape[-1]
    u = _project(x_ref[...], nm_ref, win_ref)
    bg, cg, hv, q, fx, iv, og = _split_u(u, d_conv, d_rec)
    bx = bg * hv
    sc = sc_ref[...]
    s0, s1 = sc[:, :d_conv], sc[:, d_conv:]
    cw = cw_ref[...]
    yc_ref[...] = cg * (cw[0:1] * s0 + cw[1:2] * s1 + cw[2:3] * bx)
    nc_ref[...] = jnp.concatenate([s1, bx], axis=-1)
    f, kk = _forget(fx, _lower_bound(rlb_ref, layer))
    f_ref[...] = f
    k_ref[...] = kk
    q_ref[...] = q
    v_ref[...] = iv
    og_ref[...] = og


def _sample_state_kernel(group, f_ref, k_ref, q_ref, v_ref, s_ref, sn_ref, o_ref):
    for j in range(group):
        for h in range(N_HEADS):
            rs = slice(h * HEAD, (h + 1) * HEAD)
            fcol = f_ref[rs, j:j + 1]
            kcol = k_ref[rs, j:j + 1]
            qcol = q_ref[rs, j:j + 1]
            vrow = v_ref[j:j + 1, rs]
            sn = fcol * s_ref[j, h] + kcol * vrow
            sn_ref[j, h] = sn
            o_ref[j:j + 1, rs] = jnp.sum(qcol * sn, axis=0, keepdims=True)


def _sample_tail_kernel(x_ref, yc_ref, o_ref, og_ref, rg_ref, wout_ref, nf_ref, rw_ref, rb_ref,
                        h_ref, xn_ref, sel_ref, gm_ref):
    y = jnp.concatenate([yc_ref[...], _rec_out(o_ref[...], og_ref[...], rg_ref[...])], axis=-1)
    h, xn, sel, gm = _tail(x_ref[...], y, wout_ref, nf_ref, rw_ref, rb_ref)
    h_ref[...] = h
    xn_ref[...] = xn
    sel_ref[...] = sel
    gm_ref[...] = gm


def _rank_block(sel, gm, carry):
    tb, ne = sel.shape
    row = lax.broadcasted_iota(jnp.int32, (tb, tb), 0)
    col = lax.broadcasted_iota(jnp.int32, (tb, tb), 1)
    before = jnp.where(col < row, 1.0, 0.0).astype(BF16)
    selb = sel.astype(BF16)
    rank = _dot(before, selb) + carry[...]
    carry[...] = carry[...] + jnp.sum(sel, axis=0, keepdims=True)
    er = lax.broadcasted_iota(jnp.int32, (ne, ne), 0)
    ec = lax.broadcasted_iota(jnp.int32, (ne, ne), 1)
    lower = jnp.where(er < ec, 1.0, 0.0).astype(BF16)
    order = _dot(selb, lower)
    lane_e = lax.broadcasted_iota(jnp.int32, (tb, ne), 1).astype(F32)
    lane = lax.broadcasted_iota(jnp.int32, (tb, LANES), 1)
    eid = jnp.zeros((tb, LANES), F32)
    rnk = jnp.zeros((tb, LANES), F32)
    gat = jnp.zeros((tb, LANES), F32)
    for k in range(TOP_K):
        pick = jnp.where(order == float(k), sel, 0.0)
        eid = jnp.where(lane == k, jnp.sum(pick * lane_e, axis=-1, keepdims=True), eid)
        rnk = jnp.where(lane == k, jnp.sum(pick * rank, axis=-1, keepdims=True), rnk)
        gat = jnp.where(lane == k, jnp.sum(pick * gm, axis=-1, keepdims=True), gat)
    return eid, rnk, gat


def _dispatch(xn, pos_w, n_slots):
    n_workers, n_chunks, top_k, ch = pos_w.shape
    assert n_workers == SC_CORES * SC_SUBCORES and ch % 8 == 0 and ch <= LANES
    d = xn.shape[1]
    mesh = plsc.VectorSubcoreMesh(core_axis_name="c", subcore_axis_name="s")

    @functools.partial(
        pl.kernel, mesh=mesh,
        out_type=jax.ShapeDtypeStruct((n_slots, d), xn.dtype),
        scratch_types=[pltpu.VMEM((n_chunks, top_k, ch), jnp.int32), pltpu.VMEM((SC_RING, ch, d), xn.dtype),
                       pltpu.SemaphoreType.DMA((SC_RING,)), pltpu.SemaphoreType.DMA((SC_RING,))],
        name="dispatch",
    )
    def run(xn_hbm, pos_hbm, xs_hbm, idx_v, rows_v, sem_r, sem_w):
        wid = lax.axis_index("s") * SC_CORES + lax.axis_index("c")

        def read(c, b):
            src = xn_hbm.at[pl.ds((wid * n_chunks + c) * ch, ch)]
            return pltpu.make_async_copy(src, rows_v.at[b], sem_r.at[b])

        def write(c, b, k):
            return pltpu.make_async_copy(rows_v.at[b], xs_hbm.at[idx_v.at[c, k]], sem_w.at[b])

        pltpu.sync_copy(pos_hbm.at[wid], idx_v)
        ahead = SC_RING - 1
        for c0 in range(min(ahead, n_chunks)):
            read(c0, c0).start()

        @pl.loop(0, n_chunks)
        def _(c):
            b = lax.rem(c, SC_RING)
            read(c, b).wait()
            for k in range(top_k):
                write(c, b, k).start()

            @pl.when(c >= 1)
            def _():
                for k in range(top_k):
                    write(c - 1, lax.rem(c - 1, SC_RING), k).wait()

            @pl.when(c + ahead < n_chunks)
            def _():
                read(c + ahead, lax.rem(c + ahead, SC_RING)).start()

        for k in range(top_k):
            write(n_chunks - 1, (n_chunks - 1) % SC_RING, k).wait()

    return run(xn, pos_w)


def _collect(ys, pos_w):
    n_workers, n_chunks, top_k, ch = pos_w.shape
    assert n_workers == SC_CORES * SC_SUBCORES and ch % 8 == 0 and ch <= LANES
    d = ys.shape[1]
    mesh = plsc.VectorSubcoreMesh(core_axis_name="c", subcore_axis_name="s")

    @functools.partial(
        pl.kernel, mesh=mesh,
        out_type=jax.ShapeDtypeStruct((n_workers * n_chunks * ch, top_k * d), ys.dtype),
        scratch_types=[pltpu.VMEM((n_chunks, top_k, ch), jnp.int32), pltpu.VMEM((top_k, ch, d), ys.dtype),
                       pltpu.SemaphoreType.DMA((top_k,)), pltpu.SemaphoreType.DMA((top_k,))],
        name="collect",
    )
    def run(ys_hbm, pos_hbm, out_hbm, idx_v, rows_v, sem_r, sem_w):
        wid = lax.axis_index("s") * SC_CORES + lax.axis_index("c")

        def read(c, k):
            return pltpu.make_async_copy(ys_hbm.at[idx_v.at[c, k]], rows_v.at[k], sem_r.at[k])

        def write(c, k):
            dst = out_hbm.at[pl.ds((wid * n_chunks + c) * ch, ch), pl.ds(k * d, d)]
            return pltpu.make_async_copy(rows_v.at[k], dst, sem_w.at[k])

        pltpu.sync_copy(pos_hbm.at[wid], idx_v)
        for k in range(top_k - 1):
            read(0, k).start()

        @pl.loop(0, n_chunks)
        def _(c):
            for k in range(top_k):
                read(c, k).wait()
                write(c, k).start()
                if k >= 1:
                    write(c, k - 1).wait()

                    @pl.when(c + 1 < n_chunks)
                    def _():
                        read(c + 1, k - 1).start()
                else:
                    @pl.when(c >= 1)
                    def _():
                        write(c - 1, top_k - 1).wait()
                    read(c, top_k - 1).start()

        write(n_chunks - 1, top_k - 1).wait()

    return run(ys, pos_w)


def _moe_kernel(tm, te_ref, nu_ref, tn_ref, x_ref, w1_hbm, bg_ref, bl_ref, w2_hbm, b2_ref,
                ys_ref, w1f, w2f, sem, w1p, w2b, act):
    i = pl.program_id(0)
    n_used = nu_ref[0]
    d_ff2 = w1f.shape[-1]
    n_blk = d_ff2 // MXU_N
    expert = te_ref[i]

    def weight_copies(e):
        return (pltpu.make_async_copy(w1_hbm.at[e], w1f, sem.at[0]),
                pltpu.make_async_copy(w2_hbm.at[e], w2f, sem.at[1]))

    @pl.when(i == 0)
    def _():
        for cp in weight_copies(expert):
            cp.start()

    @pl.when(i >= n_used)
    def _():
        ys_ref[...] = jnp.zeros_like(ys_ref)

    prev = te_ref[jnp.maximum(i - 1, 0)]
    changed = jnp.logical_or(i == 0, expert != prev)

    @pl.when(jnp.logical_and(changed, i < n_used))
    def _():
        for cp in weight_copies(expert):
            cp.wait()
        r = lax.broadcasted_iota(jnp.int32, (MXU_N, MXU_N), 0)
        c = lax.broadcasted_iota(jnp.int32, (MXU_N, MXU_N), 1)
        src = jnp.where(c < MXU_N // 2, 2 * c, 2 * (c - MXU_N // 2) + 1)
        perm = jnp.where(r == src, 1.0, 0.0).astype(BF16)
        for blk in range(n_blk):
            cs = slice(blk * MXU_N, (blk + 1) * MXU_N)
            w1p[:, cs] = _dot(w1f[:, cs].astype(BF16), perm).astype(BF16)
        w2b[...] = w2f[...].astype(BF16)

        @pl.when(tn_ref[i] != expert)
        def _():
            for cp in weight_copies(tn_ref[i]):
                cp.start()

    @pl.when(i < n_used)
    def _():
        x = _unpack_halves(x_ref[...]).astype(BF16)
        for blk in range(n_blk):
            a = _dot(x, w1p[:, blk * MXU_N:(blk + 1) * MXU_N])
            half = MXU_N // 2
            hs = slice(blk * half, (blk + 1) * half)
            glu = jnp.minimum(a[:, :half] + bg_ref[:, hs], SWIGLU_LIMIT)
            lin = jnp.clip(a[:, half:] + bl_ref[:, hs], -SWIGLU_LIMIT, SWIGLU_LIMIT)
            act[:, hs] = (glu * _sigmoid(SWIGLU_ALPHA * glu) * (lin + 1.0)).astype(BF16)
        ys_ref[...] = _pack_halves(_dot(act[...], w2b[...]) + b2_ref[...])


def _final_kernel(h_ref, gate_ref, nfin_ref, z_ref, y_ref):
    w = z_ref.shape[-1] // TOP_K
    g = gate_ref[...]
    out = h_ref[...]
    for k in range(TOP_K):
        out = out + g[:, k:k + 1] * _unpack_halves(z_ref[:, k * w:(k + 1) * w])
    y_ref[...] = _rms(out, nfin_ref[...])


def _final_call(h_all, gate, nfin, z, first_row, n_rows, tile):
    d = h_all.shape[-1]
    off = first_row // tile
    assert first_row % tile == 0 and n_rows % tile == 0
    return pl.pallas_call(
        _final_kernel,
        grid=(n_rows // tile,),
        in_specs=[
            pl.BlockSpec((tile, d), lambda i: (i + off, 0)),
            pl.BlockSpec((tile, gate.shape[-1]), lambda i: (i + off, 0)),
            pl.BlockSpec((1, d), lambda i: (0, 0)),
            pl.BlockSpec((tile, z.shape[-1]), lambda i: (i + off, 0)),
        ],
        out_specs=pl.BlockSpec((tile, d), lambda i: (i, 0)),
        out_shape=jax.ShapeDtypeStruct((n_rows, d), F32),
        compiler_params=pltpu.CompilerParams(dimension_semantics=("arbitrary",),
                                             vmem_limit_bytes=VMEM_LIMIT),
        name="final",
    )(h_all, gate, nfin, z)


def kernel(x_prompt, x_sample, state_conv, state_rec, meta_tokens, norm_mix, w_in, conv_w,
           rec_lower_bound, rec_norm, w_out, norm_ffn, router_w, router_b, expert_w1, expert_b1,
           expert_w2, expert_b2, norm_final):
    depth = norm_mix.shape[0]
    assert depth == 1, "single-layer step"
    layer = 0
    bp, seq, d = x_prompt.shape
    ns = x_sample.shape[0]
    assert x_sample.shape[1] == 1 and ns == TOKEN_TILE
    d_conv = conv_w.shape[-1]
    d_rec = rec_lower_bound.shape[-1]
    assert state_conv.shape[2] == 2 and d_rec == N_HEADS * HEAD
    n_exp = router_w.shape[-1]
    d_ff = expert_w2.shape[2]
    n_prompt = bp * seq
    n_tok = n_prompt + ns
    assert seq % PROMPT_TILE == 0 and n_prompt % TOKEN_TILE == 0

    nm = norm_mix[layer][None]
    win = w_in[layer].astype(BF16)
    cw = conv_w[layer]
    rlb = rec_lower_bound
    rg = rec_norm[layer][None]
    wout = w_out[layer].astype(BF16)
    nf = norm_ffn[layer][None]
    rw = router_w[layer]
    rb = router_b[layer][None]
    mix_w = (nm, win, cw, rlb)
    tail_w = (rg, wout, nf, rw, rb)

    lb = jnp.sum(jax.nn.softmax(rlb, axis=0)[:layer + 1], axis=0)
    fast = CHUNK * jnp.max(-jnp.log(lb)) < DECAY_LIMIT

    st_meta, cv_meta = lax.cond(
        fast, lambda: _meta_call(meta_tokens, mix_w, layer, N_META),
        lambda: _meta_call(meta_tokens, mix_w, layer, None))

    xs = x_sample.reshape(ns, d)
    wide = jax.ShapeDtypeStruct((ns, d_rec), F32)
    y_conv_s, new_conv_s, f_s, k_s, q_s, v_s, og_s = pl.pallas_call(
        functools.partial(_sample_in_kernel, layer),
        out_shape=[jax.ShapeDtypeStruct((ns, d_conv), F32), jax.ShapeDtypeStruct((ns, 2 * d_conv), F32),
                   wide, wide, wide, wide, wide],
        compiler_params=pltpu.CompilerParams(vmem_limit_bytes=VMEM_LIMIT),
        name="sample_in",
    )(xs, state_conv[layer].reshape(ns, 2 * d_conv), nm, win, cw, rlb)

    group = 8
    n_grp = ns // group

    def cols(a):
        return a.T.reshape(d_rec, n_grp, group).transpose(1, 0, 2)

    col_spec = pl.BlockSpec((None, d_rec, group), lambda g: (g, 0, 0))
    st_spec = pl.BlockSpec((group, N_HEADS, HEAD, HEAD), lambda g: (g, 0, 0, 0))
    row_spec = pl.BlockSpec((group, d_rec), lambda g: (g, 0))
    new_rec_s, o_s = pl.pallas_call(
        functools.partial(_sample_state_kernel, group),
        grid=(n_grp,),
        in_specs=[col_spec, col_spec, col_spec, row_spec, st_spec],
        out_specs=[st_spec, row_spec],
        out_shape=[jax.ShapeDtypeStruct(state_rec.shape[1:], F32), wide],
        compiler_params=pltpu.CompilerParams(dimension_semantics=("arbitrary",),
                                             vmem_limit_bytes=VMEM_LIMIT),
        name="sample_state",
    )(cols(f_s), cols(k_s), cols(q_s), v_s, state_rec[layer])

    decode = pl.pallas_call(
        _sample_tail_kernel,
        out_shape=[jax.ShapeDtypeStruct((ns, d), F32), jax.ShapeDtypeStruct((ns, d), F32),
                   jax.ShapeDtypeStruct((ns, n_exp), F32), jax.ShapeDtypeStruct((ns, n_exp), F32)],
        compiler_params=pltpu.CompilerParams(vmem_limit_bytes=VMEM_LIMIT),
        name="sample_tail",
    )(xs, y_conv_s, o_s, og_s, rg, wout, nf, rw, rb)

    h_all, xn_all, eid, rnk, gate, counts, new_rec_p, new_conv_p = lax.cond(
        fast,
        lambda: _prompt_call(x_prompt, st_meta, cv_meta, mix_w, tail_w, decode, layer, PROMPT_TILE, CHUNK),
        lambda: _prompt_call(x_prompt, st_meta, cv_meta, mix_w, tail_w, decode, layer, PROMPT_TILE, None))

    tm = MOE_TILE
    n_tiles = (n_tok * TOP_K) // tm + n_exp
    counts = counts[0].astype(jnp.int32)
    tiles_e = (counts + tm - 1) // tm
    tile_end = jnp.cumsum(tiles_e)
    n_used = tile_end[-1]
    offs = (tile_end - tiles_e) * tm
    eid4 = eid[:n_tok, :TOP_K].astype(jnp.int32)
    experts = jnp.arange(n_exp, dtype=jnp.int32)
    off4 = jnp.sum(jnp.where(eid4[..., None] == experts, offs, 0), axis=-1)
    pos = off4 + rnk[:n_tok, :TOP_K].astype(jnp.int32)
    tile_ids = jnp.minimum(jnp.arange(n_tiles, dtype=jnp.int32), n_used - 1)
    tile_expert = jnp.minimum(jnp.sum((tile_end[None, :] <= tile_ids[:, None]).astype(jnp.int32), axis=1),
                              n_exp - 1)
    after = tile_end[tile_expert]
    next_expert = jnp.where(after < n_used, tile_expert[jnp.minimum(after, n_tiles - 1)], tile_expert)

    n_rows = xn_all.shape[0]
    n_workers = SC_CORES * SC_SUBCORES
    assert n_rows % (n_workers * DISPATCH_CHUNK) == 0
    n_spare = (n_rows - n_tok) * TOP_K
    n_slots = n_tiles * tm + n_spare
    spare = n_tiles * tm + jnp.arange(n_spare, dtype=jnp.int32).reshape(-1, TOP_K)
    pos_rows = jnp.concatenate([pos, spare], axis=0)

    def to_workers(p, ch):
        assert p.shape[0] % (n_workers * ch) == 0
        return p.reshape(n_workers, -1, ch, TOP_K).transpose(0, 1, 3, 2)

    xs = _dispatch(xn_all, to_workers(pos_rows, DISPATCH_CHUNK), n_slots)

    w1 = expert_w1[layer]
    w2 = expert_w2[layer]
    b1 = expert_b1[layer]
    b1g = b1[:, 0::2][:, None, :]
    b1l = b1[:, 1::2][:, None, :]
    b2 = expert_b2[layer][:, None, :]
    ys = pl.pallas_call(
        functools.partial(_moe_kernel, tm),
        grid_spec=pltpu.PrefetchScalarGridSpec(
            num_scalar_prefetch=3,
            grid=(n_tiles,),
            in_specs=[
                pl.BlockSpec((tm, d // 2), lambda i, te, nu, tn: (jnp.minimum(i, nu[0] - 1), 0)),
                pl.BlockSpec(memory_space=pl.ANY),
                pl.BlockSpec((None, 1, d_ff), lambda i, te, nu, tn: (te[i], 0, 0)),
                pl.BlockSpec((None, 1, d_ff), lambda i, te, nu, tn: (te[i], 0, 0)),
                pl.BlockSpec(memory_space=pl.ANY),
                pl.BlockSpec((None, 1, d), lambda i, te, nu, tn: (te[i], 0, 0)),
            ],
            out_specs=pl.BlockSpec((tm, d // 2), lambda i, te, nu, tn: (i, 0)),
            scratch_shapes=[pltpu.VMEM((d, 2 * d_ff), F32), pltpu.VMEM((d_ff, d), F32),
                            pltpu.SemaphoreType.DMA((2,)),
                            pltpu.VMEM((d, 2 * d_ff), BF16), pltpu.VMEM((d_ff, d), BF16),
                            pltpu.VMEM((tm, d_ff), BF16)],
        ),
        out_shape=jax.ShapeDtypeStruct((n_slots, d // 2), jnp.uint32),
        compiler_params=pltpu.CompilerParams(dimension_semantics=("arbitrary",),
                                             vmem_limit_bytes=VMEM_LIMIT),
        name="moe",
    )(tile_expert, n_used[None].astype(jnp.int32), next_expert, xs, w1, b1g, b1l, w2, b2)

    z = _collect(ys, to_workers(pos_rows, DISPATCH_CHUNK))
    y_p = _final_call(h_all, gate, norm_final[None], z, 0, n_prompt, FINAL_TILE)
    y_s = _final_call(h_all, gate, norm_final[None], z, n_prompt, ns, TOKEN_TILE)

    return (y_p.reshape(bp, seq, d), y_s.reshape(ns, 1, d),
            new_conv_p[None], new_rec_p[None],
            new_conv_s.reshape(1, ns, 2, d_conv), new_rec_s[None])
```

```python
import functools

import jax
import jax.numpy as jnp
from jax import lax
from jax.experimental import pallas as pl
from jax.experimental.pallas import tpu as pltpu
from jax.experimental.pallas import tpu_sc as plsc

F32 = jnp.float32
BF16 = jnp.bfloat16

N_HEADS = 4
HEAD = 128
N_META = 16
CHUNK = 64
SAFE_CHUNK = 16
DECAY_LIMIT = 80.0
TOP_K = 4
SWIGLU_LIMIT = 7.0
SWIGLU_ALPHA = 1.702
EPS = 1e-5

LANES = 128
MXU_N = 256
PROMPT_TILE = 256
TOKEN_TILE = 128
FINAL_TILE = 512
MOE_TILE = 512
SC_CORES = 2
SC_SUBCORES = 16
DISPATCH_CHUNK = 24
SC_RING = 4
VMEM_LIMIT = 56 * 1024 * 1024


def _dot(a, b):
    return jnp.dot(a, b, preferred_element_type=F32)


def _dot_nt(a, b):
    return lax.dot_general(a, b, (((1,), (1,)), ((), ())), preferred_element_type=F32)


def _dot_tn(a, b):
    return lax.dot_general(a, b, (((0,), (0,)), ((), ())), preferred_element_type=F32)


def _pack_halves(x):
    n = x.shape[-1] // 2
    lo = pltpu.bitcast(x[:, :n].astype(BF16).astype(F32), jnp.uint32)
    hi = pltpu.bitcast(x[:, n:].astype(BF16).astype(F32), jnp.uint32)
    return (lo >> 16) | (hi & jnp.uint32(0xFFFF0000))


def _unpack_halves(u):
    lo = pltpu.bitcast(u << 16, F32)
    hi = pltpu.bitcast(u & jnp.uint32(0xFFFF0000), F32)
    return jnp.concatenate([lo, hi], axis=-1)


def _sigmoid(x):
    return 1.0 / (1.0 + jnp.exp(-x))


def _rms(x, g):
    ms = jnp.mean(x * x, axis=-1, keepdims=True)
    return x * lax.rsqrt(ms + EPS) * g


def _project(x, nm_ref, win_ref):
    return _dot(_rms(x, nm_ref[...]).astype(BF16), win_ref[...])


def _split_u(u, d_conv, d_rec):
    pts = [0, d_conv, 2 * d_conv, 3 * d_conv, 3 * d_conv + d_rec, 3 * d_conv + 2 * d_rec,
           3 * d_conv + 3 * d_rec, 3 * d_conv + 4 * d_rec]
    return [u[:, pts[i]:pts[i + 1]] for i in range(7)]


def _lower_bound(rlb_ref, layer):
    r = rlb_ref[...]
    e = jnp.exp(r - jnp.max(r, axis=0, keepdims=True))
    return jnp.sum(e[0:layer + 1], axis=0, keepdims=True) / jnp.sum(e, axis=0, keepdims=True)


def _forget(fx, lb):
    f = lb + (1.0 - lb) * _sigmoid(fx)
    return f, 1.0 - f


def _rec_out(o, og, rg):
    parts = []
    for h in range(N_HEADS):
        oh = o[:, h * HEAD:(h + 1) * HEAD]
        parts.append(oh * lax.rsqrt(jnp.mean(oh * oh, axis=-1, keepdims=True) + EPS))
    return jnp.concatenate(parts, axis=-1) * rg * (og * _sigmoid(og))


def _route(logits):
    n = logits.shape[-1]
    lane = lax.broadcasted_iota(jnp.int32, logits.shape, 1).astype(F32)
    work = logits
    tops, hots = [], []
    for _ in range(TOP_K):
        m = jnp.max(work, axis=-1, keepdims=True)
        first = jnp.min(jnp.where(work == m, lane, float(n)), axis=-1, keepdims=True)
        hot = lane == first
        tops.append(m)
        hots.append(hot)
        work = jnp.where(hot, -jnp.inf, work)
    es = [jnp.exp(t - tops[0]) for t in tops]
    den = es[0]
    for e in es[1:]:
        den = den + e
    sel = jnp.zeros_like(logits)
    gm = jnp.zeros_like(logits)
    for hot, e in zip(hots, es):
        sel = sel + jnp.where(hot, 1.0, 0.0)
        gm = gm + jnp.where(hot, e / den, 0.0)
    return sel, gm


def _tail(x, y, wout_ref, nf_ref, rw_ref, rb_ref):
    h = x + _dot(y.astype(BF16), wout_ref[...])
    xn = _rms(h, nf_ref[...])
    xh = xn.astype(BF16)
    xl = (xn - xh.astype(F32)).astype(BF16)
    rw = rw_ref[...]
    wh = rw.astype(BF16)
    wl = (rw - wh.astype(F32)).astype(BF16)
    logits = _dot(xh, wh) + _dot(xh, wl) + _dot(xl, wh) + rb_ref[...]
    sel, gm = _route(logits)
    return h, xn, sel, gm


def _cumsum_rows(x):
    n = x.shape[0]
    row = lax.broadcasted_iota(jnp.int32, (n, 1), 0)
    shift = 1
    while shift < n:
        x = x + jnp.where(row >= shift, pltpu.roll(x, shift, axis=0), 0.0)
        shift *= 2
    return x


def _rec_chunk(q, kk, v, lf, st_ref, causal):
    c = q.shape[0]
    b = _cumsum_rows(lf)
    eb = jnp.exp(b)
    qe = (q * eb).astype(BF16)
    ke = (kk * jnp.exp(-b)).astype(BF16)
    vb = v.astype(BF16)
    eb_last = eb[c - 1:c]
    outs = []
    for h in range(N_HEADS):
        sl = slice(h * HEAD, (h + 1) * HEAD)
        st = st_ref[h]
        sc = jnp.where(causal, _dot_nt(qe[:, sl], ke[:, sl]), 0.0)
        outs.append(_dot(sc.astype(BF16), vb[:, sl]) + _dot_nt(qe[:, sl], st.astype(BF16)))
        st_ref[h] = (st + _dot_tn(vb[:, sl], ke[:, sl])) * eb_last[:, sl]
    return jnp.concatenate(outs, axis=-1)


def _rec_chunk_safe(q, kk, v, lf, st_ref):
    c = q.shape[0]
    row = lax.broadcasted_iota(jnp.int32, (c, 1), 0)
    b = _cumsum_rows(lf)
    eb = jnp.exp(b)
    qe = (q * eb).astype(BF16)
    kl = (kk * jnp.exp(b[c - 1:c] - b)).astype(BF16)
    vb = v.astype(BF16)
    outs = []
    for h in range(N_HEADS):
        sl = slice(h * HEAD, (h + 1) * HEAD)
        st = st_ref[h]
        bh, kh, qh, vh = b[:, sl], kk[:, sl], q[:, sl], v[:, sl]
        intra = jnp.zeros((c, HEAD), F32)
        for t in range(c):
            w = kh * jnp.exp(jnp.where(row <= t, bh[t:t + 1] - bh, -jnp.inf))
            score = jnp.sum(w * qh[t:t + 1], axis=-1, keepdims=True)
            intra = jnp.where(row == t, jnp.sum(score * vh, axis=0, keepdims=True), intra)
        outs.append(intra + _dot_nt(qe[:, sl], st.astype(BF16)))
        st_ref[h] = st * eb[c - 1:c, sl] + _dot_tn(vb[:, sl], kl[:, sl])
    return jnp.concatenate(outs, axis=-1)


def _mix_tile(u, layer, chunk, cw_ref, rlb_ref, convbuf, st, cv_out):
    tm = u.shape[0]
    d_conv = cw_ref.shape[-1]
    d_rec = rlb_ref.shape[-1]
    bg, cg, hv, q, fx, iv, og = _split_u(u, d_conv, d_rec)

    bx = bg * hv
    convbuf[8:8 + tm, :] = bx
    cw = cw_ref[...]
    conv = cw[0:1] * convbuf[6:6 + tm, :] + cw[1:2] * convbuf[7:7 + tm, :] + cw[2:3] * bx
    convbuf[6:8, :] = bx[tm - 2:tm]
    cv_out[...] = bx[tm - 2:tm]

    f, kk = _forget(fx, _lower_bound(rlb_ref, layer))
    lf = jnp.log(f)
    outs = []
    if chunk is None:
        for c in range(tm // SAFE_CHUNK):
            rs = slice(c * SAFE_CHUNK, (c + 1) * SAFE_CHUNK)
            outs.append(_rec_chunk_safe(q[rs], kk[rs], iv[rs], lf[rs], st))
    else:
        row = lax.broadcasted_iota(jnp.int32, (chunk, chunk), 0)
        col = lax.broadcasted_iota(jnp.int32, (chunk, chunk), 1)
        causal = row >= col
        for c in range(tm // chunk):
            rs = slice(c * chunk, (c + 1) * chunk)
            outs.append(_rec_chunk(q[rs], kk[rs], iv[rs], lf[rs], st, causal))
    return cg * conv, jnp.concatenate(outs, axis=0), og


def _load_state(st, convbuf, s0_ref, c0_ref):
    for h in range(N_HEADS):
        st[h] = s0_ref[h].T
    convbuf[6:8, :] = c0_ref[...]


def _meta_kernel(layer, chunk, x_ref, s0_ref, c0_ref, nm_ref, win_ref, cw_ref, rlb_ref, st_out, cv_out,
                 convbuf, st):
    _load_state(st, convbuf, s0_ref, c0_ref)
    _mix_tile(_project(x_ref[...], nm_ref, win_ref), layer, chunk, cw_ref, rlb_ref, convbuf, st, cv_out)
    for h in range(N_HEADS):
        st_out[h] = st[h].T


def _prompt_kernel(layer, tm, chunk, n_pairs, pairs_per_seq,
                   xr0_ref, x1_ref, xp2_ref, s0_ref, c0_ref, nm_ref, win_ref, cw_ref, rlb_ref,
                   rg_ref, wout_ref, nf_ref, rw_ref, rb_ref, hs_ref, xns_ref, sels_ref, gms_ref,
                   h_ref, xn_ref, eid_ref, rnk_ref, gat_ref, cnt_ref, st_out, cv_out,
                   convbuf, st, ua, ub, carry):
    step = pl.program_id(0)
    live = step < n_pairs

    @pl.when(step == 0)
    def _():
        ua[...] = _project(xr0_ref[...], nm_ref, win_ref)
        carry[...] = jnp.zeros_like(carry)

    @pl.when(jnp.logical_and(lax.rem(step, pairs_per_seq) == 0, live))
    def _():
        _load_state(st, convbuf, s0_ref, c0_ref)

    @pl.when(live)
    def _():
        def finish(x, u, rows):
            y_conv, o, og = _mix_tile(u, layer, chunk, cw_ref, rlb_ref, convbuf, st, cv_out)
            y = jnp.concatenate([y_conv, _rec_out(o, og, rg_ref[...])], axis=-1)
            h, xn, sel, gm = _tail(x, y, wout_ref, nf_ref, rw_ref, rb_ref)
            h_ref[rows, :] = h
            xn_ref[rows, :] = _pack_halves(xn)
            eid_ref[rows, :], rnk_ref[rows, :], gat_ref[rows, :] = _rank_block(sel, gm, carry)

        ub[...] = _project(x1_ref[...], nm_ref, win_ref)
        finish(xr0_ref[...], ua[...], slice(0, tm))
        ua[...] = _project(xp2_ref[...], nm_ref, win_ref)
        finish(x1_ref[...], ub[...], slice(tm, 2 * tm))

    @pl.when(jnp.logical_and(lax.rem(step, pairs_per_seq) == pairs_per_seq - 1, live))
    def _():
        for h in range(N_HEADS):
            st_out[h] = st[h].T

    @pl.when(step == n_pairs)
    def _():
        ns = hs_ref.shape[0]
        eid, rnk, gat = _rank_block(sels_ref[...], gms_ref[...], carry)
        for dst, val in ((h_ref, hs_ref[...]), (xn_ref, _pack_halves(xns_ref[...])),
                         (eid_ref, eid), (rnk_ref, rnk), (gat_ref, gat)):
            dst[0:ns, :] = val
            dst[ns:2 * tm, :] = jnp.zeros((2 * tm - ns, dst.shape[-1]), dst.dtype)
        cnt_ref[...] = carry[...]


def _const_spec(shape):
    return pl.BlockSpec(shape, lambda *_: (0,) * len(shape))


def _meta_call(x, weights, layer, chunk):
    assert x.shape[0] % (chunk or SAFE_CHUNK) == 0
    nm, win, cw, rlb = weights
    d_conv = cw.shape[-1]
    s0 = jnp.zeros((N_HEADS, HEAD, HEAD), F32)
    c0 = jnp.zeros((2, d_conv), F32)
    return pl.pallas_call(
        functools.partial(_meta_kernel, layer, chunk),
        out_shape=[jax.ShapeDtypeStruct(s0.shape, F32), jax.ShapeDtypeStruct(c0.shape, F32)],
        scratch_shapes=[pltpu.VMEM((x.shape[0] + 8, d_conv), F32), pltpu.VMEM(s0.shape, F32)],
        compiler_params=pltpu.CompilerParams(vmem_limit_bytes=VMEM_LIMIT),
        name="mixer_meta",
    )(x, s0, c0, nm, win, cw, rlb)


def _prompt_call(x, s0, c0, weights, tail_w, decode, layer, tm, chunk):
    nseq, length, d = x.shape
    nt = length // tm
    assert nt % 2 == 0 and decode[0].shape[0] <= 2 * tm and tm % (chunk or SAFE_CHUNK) == 0
    pairs_per_seq = nt // 2
    n_pairs = nseq * pairs_per_seq
    nm, win, cw, rlb = weights
    d_conv = cw.shape[-1]
    d_in = win.shape[-1]

    def tile_spec(offset):
        def index(s):
            tile = jnp.minimum(2 * s + offset, 2 * n_pairs - 1)
            return (tile // nt, lax.rem(tile, nt), 0)
        return pl.BlockSpec((None, tm, d), index)

    def seq_of(s):
        return jnp.minimum(s // pairs_per_seq, nseq - 1)

    consts = [s0, c0, nm, win, cw, rlb] + list(tail_w) + list(decode)
    n_exp = decode[2].shape[-1]
    tok = [(d, F32), (d // 2, jnp.uint32), (LANES, F32), (LANES, F32), (LANES, F32)]
    n_steps = n_pairs + 1
    return pl.pallas_call(
        functools.partial(_prompt_kernel, layer, tm, chunk, n_pairs, pairs_per_seq),
        grid=(n_steps,),
        in_specs=[tile_spec(0), tile_spec(1), tile_spec(2)] + [_const_spec(a.shape) for a in consts],
        out_specs=[pl.BlockSpec((2 * tm, w), lambda s: (s, 0)) for w, _ in tok] + [
            pl.BlockSpec((1, n_exp), lambda s: (0, 0)),
            pl.BlockSpec((None,) + s0.shape, lambda s: (seq_of(s), 0, 0, 0)),
            pl.BlockSpec((None,) + c0.shape, lambda s: (seq_of(s), 0, 0))],
        out_shape=[jax.ShapeDtypeStruct((n_steps * 2 * tm, w), t) for w, t in tok] + [
            jax.ShapeDtypeStruct((1, n_exp), F32),
            jax.ShapeDtypeStruct((nseq,) + s0.shape, F32), jax.ShapeDtypeStruct((nseq,) + c0.shape, F32)],
        scratch_shapes=[pltpu.VMEM((tm + 8, d_conv), F32), pltpu.VMEM(s0.shape, F32),
                        pltpu.VMEM((tm, d_in), F32), pltpu.VMEM((tm, d_in), F32),
                        pltpu.VMEM((1, n_exp), F32)],
        compiler_params=pltpu.CompilerParams(dimension_semantics=("arbitrary",),
                                             vmem_limit_bytes=VMEM_LIMIT),
        name="mixer_prompt",
    )(x, x, x, *consts)


def _sample_in_kernel(layer, x_ref, sc_ref, nm_ref, win_ref, cw_ref, rlb_ref,
                      yc_ref, nc_ref, f_ref, k_ref, q_ref, v_ref, og_ref):
    d_conv = cw_ref.shape[-1]
    d_rec = rlb_ref.shape[-1]
    u = _project(x_ref[...], nm_ref, win_ref)
    bg, cg, hv, q, fx, iv, og = _split_u(u, d_conv, d_rec)
    bx = bg * hv
    sc = sc_ref[...]
    s0, s1 = sc[:, :d_conv], sc[:, d_conv:]
    cw = cw_ref[...]
    yc_ref[...] = cg * (cw[0:1] * s0 + cw[1:2] * s1 + cw[2:3] * bx)
    nc_ref[...] = jnp.concatenate([s1, bx], axis=-1)
    f, kk = _forget(fx, _lower_bound(rlb_ref, layer))
    f_ref[...] = f
    k_ref[...] = kk
    q_ref[...] = q
    v_ref[...] = iv
    og_ref[...] = og


def _sample_state_kernel(group, f_ref, k_ref, q_ref, v_ref, s_ref, sn_ref, o_ref):
    for j in range(group):
        for h in range(N_HEADS):
            rs = slice(h * HEAD, (h + 1) * HEAD)
            fcol = f_ref[rs, j:j + 1]
            kcol = k_ref[rs, j:j + 1]
            qcol = q_ref[rs, j:j + 1]
            vrow = v_ref[j:j + 1, rs]
            sn = fcol * s_ref[j, h] + kcol * vrow
            sn_ref[j, h] = sn
            o_ref[j:j + 1, rs] = jnp.sum(qcol * sn, axis=0, keepdims=True)


def _sample_tail_kernel(x_ref, yc_ref, o_ref, og_ref, rg_ref, wout_ref, nf_ref, rw_ref, rb_ref,
                        h_ref, xn_ref, sel_ref, gm_ref):
    y = jnp.concatenate([yc_ref[...], _rec_out(o_ref[...], og_ref[...], rg_ref[...])], axis=-1)
    h, xn, sel, gm = _tail(x_ref[...], y, wout_ref, nf_ref, rw_ref, rb_ref)
    h_ref[...] = h
    xn_ref[...] = xn
    sel_ref[...] = sel
    gm_ref[...] = gm


def _rank_block(sel, gm, carry):
    tb, ne = sel.shape
    row = lax.broadcasted_iota(jnp.int32, (tb, tb), 0)
    col = lax.broadcasted_iota(jnp.int32, (tb, tb), 1)
    before = jnp.where(col < row, 1.0, 0.0).astype(BF16)
    selb = sel.astype(BF16)
    rank = _dot(before, selb) + carry[...]
    carry[...] = carry[...] + jnp.sum(sel, axis=0, keepdims=True)
    er = lax.broadcasted_iota(jnp.int32, (ne, ne), 0)
    ec = lax.broadcasted_iota(jnp.int32, (ne, ne), 1)
    lower = jnp.where(er < ec, 1.0, 0.0).astype(BF16)
    order = _dot(selb, lower)
    lane_e = lax.broadcasted_iota(jnp.int32, (tb, ne), 1).astype(F32)
    lane = lax.broadcasted_iota(jnp.int32, (tb, LANES), 1)
    eid = jnp.zeros((tb, LANES), F32)
    rnk = jnp.zeros((tb, LANES), F32)
    gat = jnp.zeros((tb, LANES), F32)
    for k in range(TOP_K):
        pick = jnp.where(order == float(k), sel, 0.0)
        eid = jnp.where(lane == k, jnp.sum(pick * lane_e, axis=-1, keepdims=True), eid)
        rnk = jnp.where(lane == k, jnp.sum(pick * rank, axis=-1, keepdims=True), rnk)
        gat = jnp.where(lane == k, jnp.sum(pick * gm, axis=-1, keepdims=True), gat)
    return eid, rnk, gat


def _dispatch(xn, pos_w, n_slots):
    n_workers, n_chunks, top_k, ch = pos_w.shape
    assert n_workers == SC_CORES * SC_SUBCORES and ch % 8 == 0 and ch <= LANES
    d = xn.shape[1]
    mesh = plsc.VectorSubcoreMesh(core_axis_name="c", subcore_axis_name="s")

    @functools.partial(
        pl.kernel, mesh=mesh,
        out_type=jax.ShapeDtypeStruct((n_slots, d), xn.dtype),
        scratch_types=[pltpu.VMEM((n_chunks, top_k, ch), jnp.int32), pltpu.VMEM((SC_RING, ch, d), xn.dtype),
                       pltpu.SemaphoreType.DMA((SC_RING,)), pltpu.SemaphoreType.DMA((SC_RING,))],
        name="dispatch",
    )
    def run(xn_hbm, pos_hbm, xs_hbm, idx_v, rows_v, sem_r, sem_w):
        wid = lax.axis_index("s") * SC_CORES + lax.axis_index("c")

        def read(c, b):
            src = xn_hbm.at[pl.ds((wid * n_chunks + c) * ch, ch)]
            return pltpu.make_async_copy(src, rows_v.at[b], sem_r.at[b])

        def write(c, b, k):
            return pltpu.make_async_copy(rows_v.at[b], xs_hbm.at[idx_v.at[c, k]], sem_w.at[b])

        pltpu.sync_copy(pos_hbm.at[wid], idx_v)
        ahead = SC_RING - 1
        for c0 in range(min(ahead, n_chunks)):
            read(c0, c0).start()

        @pl.loop(0, n_chunks)
        def _(c):
            b = lax.rem(c, SC_RING)
            read(c, b).wait()
            for k in range(top_k):
                write(c, b, k).start()

            @pl.when(c >= 1)
            def _():
                for k in range(top_k):
                    write(c - 1, lax.rem(c - 1, SC_RING), k).wait()

            @pl.when(c + ahead < n_chunks)
            def _():
                read(c + ahead, lax.rem(c + ahead, SC_RING)).start()

        for k in range(top_k):
            write(n_chunks - 1, (n_chunks - 1) % SC_RING, k).wait()

    return run(xn, pos_w)


def _collect(ys, pos_w):
    n_workers, n_chunks, top_k, ch = pos_w.shape
    assert n_workers == SC_CORES * SC_SUBCORES and ch % 8 == 0 and ch <= LANES
    d = ys.shape[1]
    mesh = plsc.VectorSubcoreMesh(core_axis_name="c", subcore_axis_name="s")

    @functools.partial(
        pl.kernel, mesh=mesh,
        out_type=jax.ShapeDtypeStruct((n_workers * n_chunks * ch, top_k * d), ys.dtype),
        scratch_types=[pltpu.VMEM((n_chunks, top_k, ch), jnp.int32), pltpu.VMEM((top_k, ch, d), ys.dtype),
                       pltpu.SemaphoreType.DMA((top_k,)), pltpu.SemaphoreType.DMA((top_k,))],
        name="collect",
    )
    def run(ys_hbm, pos_hbm, out_hbm, idx_v, rows_v, sem_r, sem_w):
        wid = lax.axis_index("s") * SC_CORES + lax.axis_index("c")

        def read(c, k):
            return pltpu.make_async_copy(ys_hbm.at[idx_v.at[c, k]], rows_v.at[k], sem_r.at[k])

        def write(c, k):
            dst = out_hbm.at[pl.ds((wid * n_chunks + c) * ch, ch), pl.ds(k * d, d)]
            return pltpu.make_async_copy(rows_v.at[k], dst, sem_w.at[k])

        pltpu.sync_copy(pos_hbm.at[wid], idx_v)
        for k in range(top_k - 1):
            read(0, k).start()

        @pl.loop(0, n_chunks)
        def _(c):
            for k in range(top_k):
                read(c, k).wait()
                write(c, k).start()
                if k >= 1:
                    write(c, k - 1).wait()

                    @pl.when(c + 1 < n_chunks)
                    def _():
                        read(c + 1, k - 1).start()
                else:
                    @pl.when(c >= 1)
                    def _():
                        write(c - 1, top_k - 1).wait()
                    read(c, top_k - 1).start()

        write(n_chunks - 1, top_k - 1).wait()

    return run(ys, pos_w)


def _moe_kernel(tm, te_ref, nu_ref, tn_ref, x_ref, w1_hbm, bg_ref, bl_ref, w2_hbm, b2_ref,
                ys_ref, w1f, w2f, sem, w1p, w2b, act):
    i = pl.program_id(0)
    n_used = nu_ref[0]
    d_ff2 = w1f.shape[-1]
    n_blk = d_ff2 // MXU_N
    expert = te_ref[i]

    def weight_copies(e):
        return (pltpu.make_async_copy(w1_hbm.at[e], w1f, sem.at[0]),
                pltpu.make_async_copy(w2_hbm.at[e], w2f, sem.at[1]))

    @pl.when(i == 0)
    def _():
        for cp in weight_copies(expert):
            cp.start()

    @pl.when(i >= n_used)
    def _():
        ys_ref[...] = jnp.zeros_like(ys_ref)

    prev = te_ref[jnp.maximum(i - 1, 0)]
    changed = jnp.logical_or(i == 0, expert != prev)

    @pl.when(jnp.logical_and(changed, i < n_used))
    def _():
        for cp in weight_copies(expert):
            cp.wait()
        r = lax.broadcasted_iota(jnp.int32, (MXU_N, MXU_N), 0)
        c = lax.broadcasted_iota(jnp.int32, (MXU_N, MXU_N), 1)
        src = jnp.where(c < MXU_N // 2, 2 * c, 2 * (c - MXU_N // 2) + 1)
        perm = jnp.where(r == src, 1.0, 0.0).astype(BF16)
        for blk in range(n_blk):
            cs = slice(blk * MXU_N, (blk + 1) * MXU_N)
            w1p[:, cs] = _dot(w1f[:, cs].astype(BF16), perm).astype(BF16)
        w2b[...] = w2f[...].astype(BF16)

        @pl.when(tn_ref[i] != expert)
        def _():
            for cp in weight_copies(tn_ref[i]):
                cp.start()

    @pl.when(i < n_used)
    def _():
        x = _unpack_halves(x_ref[...]).astype(BF16)
        for blk in range(n_blk):
            a = _dot(x, w1p[:, blk * MXU_N:(blk + 1) * MXU_N])
            half = MXU_N // 2
            hs = slice(blk * half, (blk + 1) * half)
            glu = jnp.minimum(a[:, :half] + bg_ref[:, hs], SWIGLU_LIMIT)
            lin = jnp.clip(a[:, half:] + bl_ref[:, hs], -SWIGLU_LIMIT, SWIGLU_LIMIT)
            act[:, hs] = (glu * _sigmoid(SWIGLU_ALPHA * glu) * (lin + 1.0)).astype(BF16)
        ys_ref[...] = _pack_halves(_dot(act[...], w2b[...]) + b2_ref[...])


def _final_kernel(h_ref, gate_ref, nfin_ref, z_ref, y_ref):
    w = z_ref.shape[-1] // TOP_K
    g = gate_ref[...]
    out = h_ref[...]
    for k in range(TOP_K):
        out = out + g[:, k:k + 1] * _unpack_halves(z_ref[:, k * w:(k + 1) * w])
    y_ref[...] = _rms(out, nfin_ref[...])


def _final_call(h_all, gate, nfin, z, first_row, n_rows, tile):
    d = h_all.shape[-1]
    off = first_row // tile
    assert first_row % tile == 0 and n_rows % tile == 0
    return pl.pallas_call(
        _final_kernel,
        grid=(n_rows // tile,),
        in_specs=[
            pl.BlockSpec((tile, d), lambda i: (i + off, 0)),
            pl.BlockSpec((tile, gate.shape[-1]), lambda i: (i + off, 0)),
            pl.BlockSpec((1, d), lambda i: (0, 0)),
            pl.BlockSpec((tile, z.shape[-1]), lambda i: (i + off, 0)),
        ],
        out_specs=pl.BlockSpec((tile, d), lambda i: (i, 0)),
        out_shape=jax.ShapeDtypeStruct((n_rows, d), F32),
        compiler_params=pltpu.CompilerParams(dimension_semantics=("arbitrary",),
                                             vmem_limit_bytes=VMEM_LIMIT),
        name="final",
    )(h_all, gate, nfin, z)


def kernel(x_prompt, x_sample, state_conv, state_rec, meta_tokens, norm_mix, w_in, conv_w,
           rec_lower_bound, rec_norm, w_out, norm_ffn, router_w, router_b, expert_w1, expert_b1,
           expert_w2, expert_b2, norm_final):
    depth = norm_mix.shape[0]
    assert depth == 1, "single-layer step"
    layer = 0
    bp, seq, d = x_prompt.shape
    ns = x_sample.shape[0]
    assert x_sample.shape[1] == 1 and ns == TOKEN_TILE
    d_conv = conv_w.shape[-1]
    d_rec = rec_lower_bound.shape[-1]
    assert state_conv.shape[2] == 2 and d_rec == N_HEADS * HEAD
    n_exp = router_w.shape[-1]
    d_ff = expert_w2.shape[2]
    n_prompt = bp * seq
    n_tok = n_prompt + ns
    assert seq % PROMPT_TILE == 0 and n_prompt % TOKEN_TILE == 0

    nm = norm_mix[layer][None]
    win = w_in[layer].astype(BF16)
    cw = conv_w[layer]
    rlb = rec_lower_bound
    rg = rec_norm[layer][None]
    wout = w_out[layer].astype(BF16)
    nf = norm_ffn[layer][None]
    rw = router_w[layer]
    rb = router_b[layer][None]
    mix_w = (nm, win, cw, rlb)
    tail_w = (rg, wout, nf, rw, rb)

    lb = jnp.sum(jax.nn.softmax(rlb, axis=0)[:layer + 1], axis=0)
    fast = CHUNK * jnp.max(-jnp.log(lb)) < DECAY_LIMIT

    st_meta, cv_meta = lax.cond(
        fast, lambda: _meta_call(meta_tokens, mix_w, layer, N_META),
        lambda: _meta_call(meta_tokens, mix_w, layer, None))

    xs = x_sample.reshape(ns, d)
    wide = jax.ShapeDtypeStruct((ns, d_rec), F32)
    y_conv_s, new_conv_s, f_s, k_s, q_s, v_s, og_s = pl.pallas_call(
        functools.partial(_sample_in_kernel, layer),
        out_shape=[jax.ShapeDtypeStruct((ns, d_conv), F32), jax.ShapeDtypeStruct((ns, 2 * d_conv), F32),
                   wide, wide, wide, wide, wide],
        compiler_params=pltpu.CompilerParams(vmem_limit_bytes=VMEM_LIMIT),
        name="sample_in",
    )(xs, state_conv[layer].reshape(ns, 2 * d_conv), nm, win, cw, rlb)

    group = 8
    n_grp = ns // group

    def cols(a):
        return a.T.reshape(d_rec, n_grp, group).transpose(1, 0, 2)

    col_spec = pl.BlockSpec((None, d_rec, group), lambda g: (g, 0, 0))
    st_spec = pl.BlockSpec((group, N_HEADS, HEAD, HEAD), lambda g: (g, 0, 0, 0))
    row_spec = pl.BlockSpec((group, d_rec), lambda g: (g, 0))
    new_rec_s, o_s = pl.pallas_call(
        functools.partial(_sample_state_kernel, group),
        grid=(n_grp,),
        in_specs=[col_spec, col_spec, col_spec, row_spec, st_spec],
        out_specs=[st_spec, row_spec],
        out_shape=[jax.ShapeDtypeStruct(state_rec.shape[1:], F32), wide],
        compiler_params=pltpu.CompilerParams(dimension_semantics=("arbitrary",),
                                             vmem_limit_bytes=VMEM_LIMIT),
        name="sample_state",
    )(cols(f_s), cols(k_s), cols(q_s), v_s, state_rec[layer])

    decode = pl.pallas_call(
        _sample_tail_kernel,
        out_shape=[jax.ShapeDtypeStruct((ns, d), F32), jax.ShapeDtypeStruct((ns, d), F32),
                   jax.ShapeDtypeStruct((ns, n_exp), F32), jax.ShapeDtypeStruct((ns, n_exp), F32)],
        compiler_params=pltpu.CompilerParams(vmem_limit_bytes=VMEM_LIMIT),
        name="sample_tail",
    )(xs, y_conv_s, o_s, og_s, rg, wout, nf, rw, rb)

    h_all, xn_all, eid, rnk, gate, counts, new_rec_p, new_conv_p = lax.cond(
        fast,
        lambda: _prompt_call(x_prompt, st_meta, cv_meta, mix_w, tail_w, decode, layer, PROMPT_TILE, CHUNK),
        lambda: _prompt_call(x_prompt, st_meta, cv_meta, mix_w, tail_w, decode, layer, PROMPT_TILE, None))

    tm = MOE_TILE
    n_tiles = (n_tok * TOP_K) // tm + n_exp
    counts = counts[0].astype(jnp.int32)
    tiles_e = (counts + tm - 1) // tm
    tile_end = jnp.cumsum(tiles_e)
    n_used = tile_end[-1]
    offs = (tile_end - tiles_e) * tm
    eid4 = eid[:n_tok, :TOP_K].astype(jnp.int32)
    experts = jnp.arange(n_exp, dtype=jnp.int32)
    off4 = jnp.sum(jnp.where(eid4[..., None] == experts, offs, 0), axis=-1)
    pos = off4 + rnk[:n_tok, :TOP_K].astype(jnp.int32)
    tile_ids = jnp.minimum(jnp.arange(n_tiles, dtype=jnp.int32), n_used - 1)
    tile_expert = jnp.minimum(jnp.sum((tile_end[None, :] <= tile_ids[:, None]).astype(jnp.int32), axis=1),
                              n_exp - 1)
    after = tile_end[tile_expert]
    next_expert = jnp.where(after < n_used, tile_expert[jnp.minimum(after, n_tiles - 1)], tile_expert)

    n_rows = xn_all.shape[0]
    n_workers = SC_CORES * SC_SUBCORES
    assert n_rows % (n_workers * DISPATCH_CHUNK) == 0
    n_spare = (n_rows - n_tok) * TOP_K
    n_slots = n_tiles * tm + n_spare
    spare = n_tiles * tm + jnp.arange(n_spare, dtype=jnp.int32).reshape(-1, TOP_K)
    pos_rows = jnp.concatenate([pos, spare], axis=0)

    def to_workers(p, ch):
        assert p.shape[0] % (n_workers * ch) == 0
        return p.reshape(n_workers, -1, ch, TOP_K).transpose(0, 1, 3, 2)

    xs = _dispatch(xn_all, to_workers(pos_rows, DISPATCH_CHUNK), n_slots)

    w1 = expert_w1[layer]
    w2 = expert_w2[layer]
    b1 = expert_b1[layer]
    b1g = b1[:, 0::2][:, None, :]
    b1l = b1[:, 1::2][:, None, :]
    b2 = expert_b2[layer][:, None, :]
    ys = pl.pallas_call(
        functools.partial(_moe_kernel, tm),
        grid_spec=pltpu.PrefetchScalarGridSpec(
            num_scalar_prefetch=3,
            grid=(n_tiles,),
            in_specs=[
                pl.BlockSpec((tm, d // 2), lambda i, te, nu, tn: (jnp.minimum(i, nu[0] - 1), 0)),
                pl.BlockSpec(memory_space=pl.ANY),
                pl.BlockSpec((None, 1, d_ff), lambda i, te, nu, tn: (te[i], 0, 0)),
                pl.BlockSpec((None, 1, d_ff), lambda i, te, nu, tn: (te[i], 0, 0)),
                pl.BlockSpec(memory_space=pl.ANY),
                pl.BlockSpec((None, 1, d), lambda i, te, nu, tn: (te[i], 0, 0)),
            ],
            out_specs=pl.BlockSpec((tm, d // 2), lambda i, te, nu, tn: (i, 0)),
            scratch_shapes=[pltpu.VMEM((d, 2 * d_ff), F32), pltpu.VMEM((d_ff, d), F32),
                            pltpu.SemaphoreType.DMA((2,)),
                            pltpu.VMEM((d, 2 * d_ff), BF16), pltpu.VMEM((d_ff, d), BF16),
                            pltpu.VMEM((tm, d_ff), BF16)],
        ),
        out_shape=jax.ShapeDtypeStruct((n_slots, d // 2), jnp.uint32),
        compiler_params=pltpu.CompilerParams(dimension_semantics=("arbitrary",),
                                             vmem_limit_bytes=VMEM_LIMIT),
        name="moe",
    )(tile_expert, n_used[None].astype(jnp.int32), next_expert, xs, w1, b1g, b1l, w2, b2)

    z = _collect(ys, to_workers(pos_rows, DISPATCH_CHUNK))
    y_p = _final_call(h_all, gate, norm_final[None], z, 0, n_prompt, FINAL_TILE)
    y_s = _final_call(h_all, gate, norm_final[None], z, n_prompt, ns, TOKEN_TILE)

    return (y_p.reshape(bp, seq, d), y_s.reshape(ns, 1, d),
            new_conv_p[None], new_rec_p[None],
            new_conv_s.reshape(1, ns, 2, d_conv), new_rec_s[None])
```

```python
import functools

import jax
import jax.numpy as jnp
from jax import lax
from jax.experimental import pallas as pl
from jax.experimental.pallas import tpu as pltpu
from jax.experimental.pallas import tpu_sc as plsc

F32 = jnp.float32
BF16 = jnp.bfloat16

N_HEADS = 4
HEAD = 128
N_META = 16
CHUNK = 64
SAFE_CHUNK = 16
DECAY_LIMIT = 80.0
TOP_K = 4
SWIGLU_LIMIT = 7.0
SWIGLU_ALPHA = 1.702
EPS = 1e-5

LANES = 128
MXU_N = 256
PROMPT_TILE = 256
TOKEN_TILE = 128
FINAL_TILE = 512
MOE_TILE = 512
SC_CORES = 2
SC_SUBCORES = 16
DISPATCH_CHUNK = 24
SC_RING = 4
VMEM_LIMIT = 56 * 1024 * 1024


def _dot(a, b):
    return jnp.dot(a, b, preferred_element_type=F32)


def _dot_nt(a, b):
    return lax.dot_general(a, b, (((1,), (1,)), ((), ())), preferred_element_type=F32)


def _dot_tn(a, b):
    return lax.dot_general(a, b, (((0,), (0,)), ((), ())), preferred_element_type=F32)


def _pack_halves(x):
    n = x.shape[-1] // 2
    lo = pltpu.bitcast(x[:, :n].astype(BF16).astype(F32), jnp.uint32)
    hi = pltpu.bitcast(x[:, n:].astype(BF16).astype(F32), jnp.uint32)
    return (lo >> 16) | (hi & jnp.uint32(0xFFFF0000))


def _unpack_halves(u):
    lo = pltpu.bitcast(u << 16, F32)
    hi = pltpu.bitcast(u & jnp.uint32(0xFFFF0000), F32)
    return jnp.concatenate([lo, hi], axis=-1)


def _sigmoid(x):
    return 1.0 / (1.0 + jnp.exp(-x))


def _rms(x, g):
    ms = jnp.mean(x * x, axis=-1, keepdims=True)
    return x * lax.rsqrt(ms + EPS) * g


def _project(x, nm_ref, win_ref):
    return _dot(_rms(x, nm_ref[...]).astype(BF16), win_ref[...])


def _split_u(u, d_conv, d_rec):
    pts = [0, d_conv, 2 * d_conv, 3 * d_conv, 3 * d_conv + d_rec, 3 * d_conv + 2 * d_rec,
           3 * d_conv + 3 * d_rec, 3 * d_conv + 4 * d_rec]
    return [u[:, pts[i]:pts[i + 1]] for i in range(7)]


def _lower_bound(rlb_ref, layer):
    r = rlb_ref[...]
    e = jnp.exp(r - jnp.max(r, axis=0, keepdims=True))
    return jnp.sum(e[0:layer + 1], axis=0, keepdims=True) / jnp.sum(e, axis=0, keepdims=True)


def _forget(fx, lb):
    f = lb + (1.0 - lb) * _sigmoid(fx)
    return f, 1.0 - f


def _rec_out(o, og, rg):
    parts = []
    for h in range(N_HEADS):
        oh = o[:, h * HEAD:(h + 1) * HEAD]
        parts.append(oh * lax.rsqrt(jnp.mean(oh * oh, axis=-1, keepdims=True) + EPS))
    return jnp.concatenate(parts, axis=-1) * rg * (og * _sigmoid(og))


def _route(logits):
    m_rows, n = logits.shape
    lane = lax.broadcasted_iota(jnp.int32, logits.shape, 1).astype(F32)
    work = logits
    tops, firsts = [], []
    sel = jnp.zeros_like(logits)
    for _ in range(TOP_K):
        m = jnp.max(work, axis=-1, keepdims=True)
        first = jnp.min(jnp.where(work == m, lane, float(n)), axis=-1, keepdims=True)
        hot = lane == first
        tops.append(m)
        firsts.append(first)
        sel = sel + jnp.where(hot, 1.0, 0.0)
        work = jnp.where(hot, -jnp.inf, work)
    es = [jnp.exp(t - tops[0]) for t in tops]
    den = es[0]
    for e in es[1:]:
        den = den + e
    wide = lax.broadcasted_iota(jnp.int32, (m_rows, LANES), 1)
    route = jnp.zeros((m_rows, LANES), F32)
    for k in range(TOP_K):
        route = jnp.where(wide == k, firsts[k], route)
        route = jnp.where(wide == TOP_K + k, es[k] / den, route)
    return sel, route


def _tail(x, y, wout_ref, nf_ref, rw_ref, rb_ref):
    h = x + _dot(y.astype(BF16), wout_ref[...])
    xn = _rms(h, nf_ref[...])
    xh = xn.astype(BF16)
    xl = (xn - xh.astype(F32)).astype(BF16)
    rw = rw_ref[...]
    wh = rw.astype(BF16)
    wl = (rw - wh.astype(F32)).astype(BF16)
    logits = _dot(xh, wh) + _dot(xh, wl) + _dot(xl, wh) + rb_ref[...]
    sel, route = _route(logits)
    return h, xn, sel, route


def _cumsum_rows(x):
    n = x.shape[0]
    row = lax.broadcasted_iota(jnp.int32, (n, 1), 0)
    shift = 1
    while shift < n:
        x = x + jnp.where(row >= shift, pltpu.roll(x, shift, axis=0), 0.0)
        shift *= 2
    return x


def _rec_chunk(q, kk, v, lf, st_ref, causal):
    c = q.shape[0]
    b = _cumsum_rows(lf)
    eb = jnp.exp(b)
    qe = (q * eb).astype(BF16)
    ke = (kk * jnp.exp(-b)).astype(BF16)
    vb = v.astype(BF16)
    eb_last = eb[c - 1:c]
    outs = []
    for h in range(N_HEADS):
        sl = slice(h * HEAD, (h + 1) * HEAD)
        st = st_ref[h]
        sc = jnp.where(causal, _dot_nt(qe[:, sl], ke[:, sl]), 0.0)
        outs.append(_dot(sc.astype(BF16), vb[:, sl]) + _dot_nt(qe[:, sl], st.astype(BF16)))
        st_ref[h] = (st + _dot_tn(vb[:, sl], ke[:, sl])) * eb_last[:, sl]
    return jnp.concatenate(outs, axis=-1)


def _rec_chunk_safe(q, kk, v, lf, st_ref):
    c = q.shape[0]
    row = lax.broadcasted_iota(jnp.int32, (c, 1), 0)
    b = _cumsum_rows(lf)
    eb = jnp.exp(b)
    qe = (q * eb).astype(BF16)
    kl = (kk * jnp.exp(b[c - 1:c] - b)).astype(BF16)
    vb = v.astype(BF16)
    outs = []
    for h in range(N_HEADS):
        sl = slice(h * HEAD, (h + 1) * HEAD)
        st = st_ref[h]
        bh, kh, qh, vh = b[:, sl], kk[:, sl], q[:, sl], v[:, sl]
        intra = jnp.zeros((c, HEAD), F32)
        for t in range(c):
            w = kh * jnp.exp(jnp.where(row <= t, bh[t:t + 1] - bh, -jnp.inf))
            score = jnp.sum(w * qh[t:t + 1], axis=-1, keepdims=True)
            intra = jnp.where(row == t, jnp.sum(score * vh, axis=0, keepdims=True), intra)
        outs.append(intra + _dot_nt(qe[:, sl], st.astype(BF16)))
        st_ref[h] = st * eb[c - 1:c, sl] + _dot_tn(vb[:, sl], kl[:, sl])
    return jnp.concatenate(outs, axis=-1)


def _mix_tile(u, layer, chunk, cw_ref, rlb_ref, convbuf, st, cv_out):
    tm = u.shape[0]
    d_conv = cw_ref.shape[-1]
    d_rec = rlb_ref.shape[-1]
    bg, cg, hv, q, fx, iv, og = _split_u(u, d_conv, d_rec)

    bx = bg * hv
    convbuf[8:8 + tm, :] = bx
    cw = cw_ref[...]
    conv = cw[0:1] * convbuf[6:6 + tm, :] + cw[1:2] * convbuf[7:7 + tm, :] + cw[2:3] * bx
    convbuf[6:8, :] = bx[tm - 2:tm]
    cv_out[...] = bx[tm - 2:tm]

    f, kk = _forget(fx, _lower_bound(rlb_ref, layer))
    lf = jnp.log(f)
    outs = []
    if chunk is None:
        for c in range(tm // SAFE_CHUNK):
            rs = slice(c * SAFE_CHUNK, (c + 1) * SAFE_CHUNK)
            outs.append(_rec_chunk_safe(q[rs], kk[rs], iv[rs], lf[rs], st))
    else:
        row = lax.broadcasted_iota(jnp.int32, (chunk, chunk), 0)
        col = lax.broadcasted_iota(jnp.int32, (chunk, chunk), 1)
        causal = row >= col
        for c in range(tm // chunk):
            rs = slice(c * chunk, (c + 1) * chunk)
            outs.append(_rec_chunk(q[rs], kk[rs], iv[rs], lf[rs], st, causal))
    return cg * conv, jnp.concatenate(outs, axis=0), og


def _load_state(st, convbuf, s0_ref, c0_ref):
    for h in range(N_HEADS):
        st[h] = s0_ref[h].T
    convbuf[6:8, :] = c0_ref[...]


def _meta_kernel(layer, chunk, x_ref, s0_ref, c0_ref, nm_ref, win_ref, cw_ref, rlb_ref, st_out, cv_out,
                 convbuf, st):
    _load_state(st, convbuf, s0_ref, c0_ref)
    _mix_tile(_project(x_ref[...], nm_ref, win_ref), layer, chunk, cw_ref, rlb_ref, convbuf, st, cv_out)
    for h in range(N_HEADS):
        st_out[h] = st[h].T


def _prompt_kernel(layer, tm, chunk, n_pairs, pairs_per_seq,
                   xr0_ref, x1_ref, xp2_ref, s0_ref, c0_ref, nm_ref, win_ref, cw_ref, rlb_ref,
                   rg_ref, wout_ref, nf_ref, rw_ref, rb_ref, hs_ref, xns_ref, sels_ref, routes_ref,
                   h_ref, xn_ref, route_ref, rnk_ref, cnt_ref, st_out, cv_out,
                   convbuf, st, ua, ub, carry):
    step = pl.program_id(0)
    live = step < n_pairs

    @pl.when(step == 0)
    def _():
        ua[...] = _project(xr0_ref[...], nm_ref, win_ref)
        carry[...] = jnp.zeros_like(carry)

    @pl.when(jnp.logical_and(lax.rem(step, pairs_per_seq) == 0, live))
    def _():
        _load_state(st, convbuf, s0_ref, c0_ref)

    @pl.when(live)
    def _():
        def finish(x, u, rows):
            y_conv, o, og = _mix_tile(u, layer, chunk, cw_ref, rlb_ref, convbuf, st, cv_out)
            y = jnp.concatenate([y_conv, _rec_out(o, og, rg_ref[...])], axis=-1)
            h, xn, sel, route = _tail(x, y, wout_ref, nf_ref, rw_ref, rb_ref)
            h_ref[rows, :] = h
            xn_ref[rows, :] = _pack_halves(xn)
            route_ref[rows, :] = route
            rnk_ref[rows, :] = _rank_block(sel, route, carry)

        ub[...] = _project(x1_ref[...], nm_ref, win_ref)
        finish(xr0_ref[...], ua[...], slice(0, tm))
        ua[...] = _project(xp2_ref[...], nm_ref, win_ref)
        finish(x1_ref[...], ub[...], slice(tm, 2 * tm))

    @pl.when(jnp.logical_and(lax.rem(step, pairs_per_seq) == pairs_per_seq - 1, live))
    def _():
        for h in range(N_HEADS):
            st_out[h] = st[h].T

    @pl.when(step == n_pairs)
    def _():
        ns = hs_ref.shape[0]
        rnk = _rank_block(sels_ref[...], routes_ref[...], carry)
        for dst, val in ((h_ref, hs_ref[...]), (xn_ref, _pack_halves(xns_ref[...])),
                         (route_ref, routes_ref[...]), (rnk_ref, rnk)):
            dst[0:ns, :] = val
            dst[ns:2 * tm, :] = jnp.zeros((2 * tm - ns, dst.shape[-1]), dst.dtype)
        cnt_ref[...] = carry[...]


def _const_spec(shape):
    return pl.BlockSpec(shape, lambda *_: (0,) * len(shape))


def _meta_call(x, weights, layer, chunk):
    assert x.shape[0] % (chunk or SAFE_CHUNK) == 0
    nm, win, cw, rlb = weights
    d_conv = cw.shape[-1]
    s0 = jnp.zeros((N_HEADS, HEAD, HEAD), F32)
    c0 = jnp.zeros((2, d_conv), F32)
    return pl.pallas_call(
        functools.partial(_meta_kernel, layer, chunk),
        out_shape=[jax.ShapeDtypeStruct(s0.shape, F32), jax.ShapeDtypeStruct(c0.shape, F32)],
        scratch_shapes=[pltpu.VMEM((x.shape[0] + 8, d_conv), F32), pltpu.VMEM(s0.shape, F32)],
        compiler_params=pltpu.CompilerParams(vmem_limit_bytes=VMEM_LIMIT),
        name="mixer_meta",
    )(x, s0, c0, nm, win, cw, rlb)


def _prompt_call(x, s0, c0, weights, tail_w, decode, layer, tm, chunk):
    nseq, length, d = x.shape
    nt = length // tm
    assert nt % 2 == 0 and decode[0].shape[0] <= 2 * tm and tm % (chunk or SAFE_CHUNK) == 0
    pairs_per_seq = nt // 2
    n_pairs = nseq * pairs_per_seq
    nm, win, cw, rlb = weights
    d_conv = cw.shape[-1]
    d_in = win.shape[-1]

    def tile_spec(offset):
        def index(s):
            tile = jnp.minimum(2 * s + offset, 2 * n_pairs - 1)
            return (tile // nt, lax.rem(tile, nt), 0)
        return pl.BlockSpec((None, tm, d), index)

    def seq_of(s):
        return jnp.minimum(s // pairs_per_seq, nseq - 1)

    consts = [s0, c0, nm, win, cw, rlb] + list(tail_w) + list(decode)
    n_exp = decode[2].shape[-1]
    tok = [(d, F32), (d // 2, jnp.uint32), (LANES, F32), (LANES, F32)]
    n_steps = n_pairs + 1
    return pl.pallas_call(
        functools.partial(_prompt_kernel, layer, tm, chunk, n_pairs, pairs_per_seq),
        grid=(n_steps,),
        in_specs=[tile_spec(0), tile_spec(1), tile_spec(2)] + [_const_spec(a.shape) for a in consts],
        out_specs=[pl.BlockSpec((2 * tm, w), lambda s: (s, 0)) for w, _ in tok] + [
            pl.BlockSpec((1, n_exp), lambda s: (0, 0)),
            pl.BlockSpec((None,) + s0.shape, lambda s: (seq_of(s), 0, 0, 0)),
            pl.BlockSpec((None,) + c0.shape, lambda s: (seq_of(s), 0, 0))],
        out_shape=[jax.ShapeDtypeStruct((n_steps * 2 * tm, w), t) for w, t in tok] + [
            jax.ShapeDtypeStruct((1, n_exp), F32),
            jax.ShapeDtypeStruct((nseq,) + s0.shape, F32), jax.ShapeDtypeStruct((nseq,) + c0.shape, F32)],
        scratch_shapes=[pltpu.VMEM((tm + 8, d_conv), F32), pltpu.VMEM(s0.shape, F32),
                        pltpu.VMEM((tm, d_in), F32), pltpu.VMEM((tm, d_in), F32),
                        pltpu.VMEM((1, n_exp), F32)],
        compiler_params=pltpu.CompilerParams(dimension_semantics=("arbitrary",),
                                             vmem_limit_bytes=VMEM_LIMIT),
        name="mixer_prompt",
    )(x, x, x, *consts)


def _sample_in_kernel(layer, x_ref, sc_ref, nm_ref, win_ref, cw_ref, rlb_ref,
                      yc_ref, nc_ref, f_ref, k_ref, q_ref, v_ref, og_ref):
    d_conv = cw_ref.shape[-1]
    d_rec = rlb_ref.shape[-1]
    u = _project(x_ref[...], nm_ref, win_ref)
    bg, cg, hv, q, fx, iv, og = _split_u(u, d_conv, d_rec)
    bx = bg * hv
    sc = sc_ref[...]
    s0, s1 = sc[:, :d_conv], sc[:, d_conv:]
    cw = cw_ref[...]
    yc_ref[...] = cg * (cw[0:1] * s0 + cw[1:2] * s1 + cw[2:3] * bx)
    nc_ref[...] = jnp.concatenate([s1, bx], axis=-1)
    f, kk = _forget(fx, _lower_bound(rlb_ref, layer))
    f_ref[...] = f
    k_ref[...] = kk
    q_ref[...] = q
    v_ref[...] = iv
    og_ref[...] = og


def _sample_state_kernel(group, f_ref, k_ref, q_ref, v_ref, s_ref, sn_ref, o_ref):
    for j in range(group):
        for h in range(N_HEADS):
            rs = slice(h * HEAD, (h + 1) * HEAD)
            fcol = f_ref[rs, j:j + 1]
            kcol = k_ref[rs, j:j + 1]
            qcol = q_ref[rs, j:j + 1]
            vrow = v_ref[j:j + 1, rs]
            sn = fcol * s_ref[j, h] + kcol * vrow
            sn_ref[j, h] = sn
            o_ref[j:j + 1, rs] = jnp.sum(qcol * sn, axis=0, keepdims=True)


def _sample_tail_kernel(x_ref, yc_ref, o_ref, og_ref, rg_ref, wout_ref, nf_ref, rw_ref, rb_ref,
                        h_ref, xn_ref, sel_ref, route_ref):
    y = jnp.concatenate([yc_ref[...], _rec_out(o_ref[...], og_ref[...], rg_ref[...])], axis=-1)
    h, xn, sel, route = _tail(x_ref[...], y, wout_ref, nf_ref, rw_ref, rb_ref)
    h_ref[...] = h
    xn_ref[...] = xn
    sel_ref[...] = sel
    route_ref[...] = route


def _rank_block(sel, route, carry):
    tb, ne = sel.shape
    row = lax.broadcasted_iota(jnp.int32, (tb, tb), 0)
    col = lax.broadcasted_iota(jnp.int32, (tb, tb), 1)
    before = jnp.where(col < row, 1.0, 0.0).astype(BF16)
    rank = _dot(before, sel.astype(BF16)) + carry[...]
    carry[...] = carry[...] + jnp.sum(sel, axis=0, keepdims=True)
    lane_e = lax.broadcasted_iota(jnp.int32, (tb, ne), 1).astype(F32)
    lane = lax.broadcasted_iota(jnp.int32, (tb, LANES), 1)
    rnk = jnp.zeros((tb, LANES), F32)
    for k in range(TOP_K):
        mine = jnp.where(lane_e == route[:, k:k + 1], rank, 0.0)
        rnk = jnp.where(lane == k, jnp.sum(mine, axis=-1, keepdims=True), rnk)
    return rnk


def _dispatch(xn, pos_w, n_slots):
    n_workers, n_chunks, top_k, ch = pos_w.shape
    assert n_workers == SC_CORES * SC_SUBCORES and ch % 8 == 0 and ch <= LANES
    d = xn.shape[1]
    mesh = plsc.VectorSubcoreMesh(core_axis_name="c", subcore_axis_name="s")

    @functools.partial(
        pl.kernel, mesh=mesh,
        out_type=jax.ShapeDtypeStruct((n_slots, d), xn.dtype),
        scratch_types=[pltpu.VMEM((n_chunks, top_k, ch), jnp.int32), pltpu.VMEM((SC_RING, ch, d), xn.dtype),
                       pltpu.SemaphoreType.DMA((SC_RING,)), pltpu.SemaphoreType.DMA((SC_RING,))],
        name="dispatch",
    )
    def run(xn_hbm, pos_hbm, xs_hbm, idx_v, rows_v, sem_r, sem_w):
        wid = lax.axis_index("s") * SC_CORES + lax.axis_index("c")

        def read(c, b):
            src = xn_hbm.at[pl.ds((wid * n_chunks + c) * ch, ch)]
            return pltpu.make_async_copy(src, rows_v.at[b], sem_r.at[b])

        def write(c, b, k):
            return pltpu.make_async_copy(rows_v.at[b], xs_hbm.at[idx_v.at[c, k]], sem_w.at[b])

        pltpu.sync_copy(pos_hbm.at[wid], idx_v)
        ahead = SC_RING - 1
        for c0 in range(min(ahead, n_chunks)):
            read(c0, c0).start()

        @pl.loop(0, n_chunks)
        def _(c):
            b = lax.rem(c, SC_RING)
            read(c, b).wait()
            for k in range(top_k):
                write(c, b, k).start()

            @pl.when(c >= 1)
            def _():
                for k in range(top_k):
                    write(c - 1, lax.rem(c - 1, SC_RING), k).wait()

            @pl.when(c + ahead < n_chunks)
            def _():
                read(c + ahead, lax.rem(c + ahead, SC_RING)).start()

        for k in range(top_k):
            write(n_chunks - 1, (n_chunks - 1) % SC_RING, k).wait()

    return run(xn, pos_w)


def _collect(ys, pos_w):
    n_workers, n_chunks, top_k, ch = pos_w.shape
    assert n_workers == SC_CORES * SC_SUBCORES and ch % 8 == 0 and ch <= LANES
    d = ys.shape[1]
    mesh = plsc.VectorSubcoreMesh(core_axis_name="c", subcore_axis_name="s")

    @functools.partial(
        pl.kernel, mesh=mesh,
        out_type=jax.ShapeDtypeStruct((n_workers * n_chunks * ch, top_k * d), ys.dtype),
        scratch_types=[pltpu.VMEM((n_chunks, top_k, ch), jnp.int32), pltpu.VMEM((top_k, ch, d), ys.dtype),
                       pltpu.SemaphoreType.DMA((top_k,)), pltpu.SemaphoreType.DMA((top_k,))],
        name="collect",
    )
    def run(ys_hbm, pos_hbm, out_hbm, idx_v, rows_v, sem_r, sem_w):
        wid = lax.axis_index("s") * SC_CORES + lax.axis_index("c")

        def read(c, k):
            return pltpu.make_async_copy(ys_hbm.at[idx_v.at[c, k]], rows_v.at[k], sem_r.at[k])

        def write(c, k):
            dst = out_hbm.at[pl.ds((wid * n_chunks + c) * ch, ch), pl.ds(k * d, d)]
            return pltpu.make_async_copy(rows_v.at[k], dst, sem_w.at[k])

        pltpu.sync_copy(pos_hbm.at[wid], idx_v)
        for k in range(top_k - 1):
            read(0, k).start()

        @pl.loop(0, n_chunks)
        def _(c):
            for k in range(top_k):
                read(c, k).wait()
                write(c, k).start()
                if k >= 1:
                    write(c, k - 1).wait()

                    @pl.when(c + 1 < n_chunks)
                    def _():
                        read(c + 1, k - 1).start()
                else:
                    @pl.when(c >= 1)
                    def _():
                        write(c - 1, top_k - 1).wait()
                    read(c, top_k - 1).start()

        write(n_chunks - 1, top_k - 1).wait()

    return run(ys, pos_w)


def _moe_kernel(tm, te_ref, nu_ref, tn_ref, x_ref, w1_hbm, bg_ref, bl_ref, w2_hbm, b2_ref,
                ys_ref, w1f, w2f, sem, w1p, w2b, act):
    i = pl.program_id(0)
    n_used = nu_ref[0]
    d_ff2 = w1f.shape[-1]
    n_blk = d_ff2 // MXU_N
    expert = te_ref[i]

    def weight_copies(e):
        return (pltpu.make_async_copy(w1_hbm.at[e], w1f, sem.at[0]),
                pltpu.make_async_copy(w2_hbm.at[e], w2f, sem.at[1]))

    @pl.when(i == 0)
    def _():
        for cp in weight_copies(expert):
            cp.start()

    @pl.when(i >= n_used)
    def _():
        ys_ref[...] = jnp.zeros_like(ys_ref)

    prev = te_ref[jnp.maximum(i - 1, 0)]
    changed = jnp.logical_or(i == 0, expert != prev)

    @pl.when(jnp.logical_and(changed, i < n_used))
    def _():
        for cp in weight_copies(expert):
            cp.wait()
        r = lax.broadcasted_iota(jnp.int32, (MXU_N, MXU_N), 0)
        c = lax.broadcasted_iota(jnp.int32, (MXU_N, MXU_N), 1)
        src = jnp.where(c < MXU_N // 2, 2 * c, 2 * (c - MXU_N // 2) + 1)
        perm = jnp.where(r == src, 1.0, 0.0).astype(BF16)
        for blk in range(n_blk):
            cs = slice(blk * MXU_N, (blk + 1) * MXU_N)
            w1p[:, cs] = _dot(w1f[:, cs].astype(BF16), perm).astype(BF16)
        w2b[...] = w2f[...].astype(BF16)

        @pl.when(tn_ref[i] != expert)
        def _():
            for cp in weight_copies(tn_ref[i]):
                cp.start()

    @pl.when(i < n_used)
    def _():
        x = _unpack_halves(x_ref[...]).astype(BF16)
        for blk in range(n_blk):
            a = _dot(x, w1p[:, blk * MXU_N:(blk + 1) * MXU_N])
            half = MXU_N // 2
            hs = slice(blk * half, (blk + 1) * half)
            glu = jnp.minimum(a[:, :half] + bg_ref[:, hs], SWIGLU_LIMIT)
            lin = jnp.clip(a[:, half:] + bl_ref[:, hs], -SWIGLU_LIMIT, SWIGLU_LIMIT)
            act[:, hs] = (glu * _sigmoid(SWIGLU_ALPHA * glu) * (lin + 1.0)).astype(BF16)
        ys_ref[...] = _pack_halves(_dot(act[...], w2b[...]) + b2_ref[...])


def _final_kernel(h_ref, gate_ref, nfin_ref, z_ref, y_ref):
    w = z_ref.shape[-1] // TOP_K
    g = gate_ref[...]
    out = h_ref[...]
    for k in range(TOP_K):
        out = out + g[:, TOP_K + k:TOP_K + k + 1] * _unpack_halves(z_ref[:, k * w:(k + 1) * w])
    y_ref[...] = _rms(out, nfin_ref[...])


def _final_call(h_all, gate, nfin, z, first_row, n_rows, tile):
    d = h_all.shape[-1]
    off = first_row // tile
    assert first_row % tile == 0 and n_rows % tile == 0
    return pl.pallas_call(
        _final_kernel,
        grid=(n_rows // tile,),
        in_specs=[
            pl.BlockSpec((tile, d), lambda i: (i + off, 0)),
            pl.BlockSpec((tile, gate.shape[-1]), lambda i: (i + off, 0)),
            pl.BlockSpec((1, d), lambda i: (0, 0)),
            pl.BlockSpec((tile, z.shape[-1]), lambda i: (i + off, 0)),
        ],
        out_specs=pl.BlockSpec((tile, d), lambda i: (i, 0)),
        out_shape=jax.ShapeDtypeStruct((n_rows, d), F32),
        compiler_params=pltpu.CompilerParams(dimension_semantics=("arbitrary",),
                                             vmem_limit_bytes=VMEM_LIMIT),
        name="final",
    )(h_all, gate, nfin, z)


def kernel(x_prompt, x_sample, state_conv, state_rec, meta_tokens, norm_mix, w_in, conv_w,
           rec_lower_bound, rec_norm, w_out, norm_ffn, router_w, router_b, expert_w1, expert_b1,
           expert_w2, expert_b2, norm_final):
    depth = norm_mix.shape[0]
    assert depth == 1, "single-layer step"
    layer = 0
    bp, seq, d = x_prompt.shape
    ns = x_sample.shape[0]
    assert x_sample.shape[1] == 1 and ns == TOKEN_TILE
    d_conv = conv_w.shape[-1]
    d_rec = rec_lower_bound.shape[-1]
    assert state_conv.shape[2] == 2 and d_rec == N_HEADS * HEAD
    n_exp = router_w.shape[-1]
    d_ff = expert_w2.shape[2]
    n_prompt = bp * seq
    n_tok = n_prompt + ns
    assert seq % PROMPT_TILE == 0 and n_prompt % TOKEN_TILE == 0

    nm = norm_mix[layer][None]
    win = w_in[layer].astype(BF16)
    cw = conv_w[layer]
    rlb = rec_lower_bound
    rg = rec_norm[layer][None]
    wout = w_out[layer].astype(BF16)
    nf = norm_ffn[layer][None]
    rw = router_w[layer]
    rb = router_b[layer][None]
    mix_w = (nm, win, cw, rlb)
    tail_w = (rg, wout, nf, rw, rb)

    lb = jnp.sum(jax.nn.softmax(rlb, axis=0)[:layer + 1], axis=0)
    fast = CHUNK * jnp.max(-jnp.log(lb)) < DECAY_LIMIT

    st_meta, cv_meta = lax.cond(
        fast, lambda: _meta_call(meta_tokens, mix_w, layer, N_META),
        lambda: _meta_call(meta_tokens, mix_w, layer, None))

    xs = x_sample.reshape(ns, d)
    wide = jax.ShapeDtypeStruct((ns, d_rec), F32)
    y_conv_s, new_conv_s, f_s, k_s, q_s, v_s, og_s = pl.pallas_call(
        functools.partial(_sample_in_kernel, layer),
        out_shape=[jax.ShapeDtypeStruct((ns, d_conv), F32), jax.ShapeDtypeStruct((ns, 2 * d_conv), F32),
                   wide, wide, wide, wide, wide],
        compiler_params=pltpu.CompilerParams(vmem_limit_bytes=VMEM_LIMIT),
        name="sample_in",
    )(xs, state_conv[layer].reshape(ns, 2 * d_conv), nm, win, cw, rlb)

    group = 8
    n_grp = ns // group

    def cols(a):
        return a.T.reshape(d_rec, n_grp, group).transpose(1, 0, 2)

    col_spec = pl.BlockSpec((None, d_rec, group), lambda g: (g, 0, 0))
    st_spec = pl.BlockSpec((group, N_HEADS, HEAD, HEAD), lambda g: (g, 0, 0, 0))
    row_spec = pl.BlockSpec((group, d_rec), lambda g: (g, 0))
    new_rec_s, o_s = pl.pallas_call(
        functools.partial(_sample_state_kernel, group),
        grid=(n_grp,),
        in_specs=[col_spec, col_spec, col_spec, row_spec, st_spec],
        out_specs=[st_spec, row_spec],
        out_shape=[jax.ShapeDtypeStruct(state_rec.shape[1:], F32), wide],
        compiler_params=pltpu.CompilerParams(dimension_semantics=("arbitrary",),
                                             vmem_limit_bytes=VMEM_LIMIT),
        name="sample_state",
    )(cols(f_s), cols(k_s), cols(q_s), v_s, state_rec[layer])

    decode = pl.pallas_call(
        _sample_tail_kernel,
        out_shape=[jax.ShapeDtypeStruct((ns, d), F32), jax.ShapeDtypeStruct((ns, d), F32),
                   jax.ShapeDtypeStruct((ns, n_exp), F32), jax.ShapeDtypeStruct((ns, LANES), F32)],
        compiler_params=pltpu.CompilerParams(vmem_limit_bytes=VMEM_LIMIT),
        name="sample_tail",
    )(xs, y_conv_s, o_s, og_s, rg, wout, nf, rw, rb)

    h_all, xn_all, route, rnk, counts, new_rec_p, new_conv_p = lax.cond(
        fast,
        lambda: _prompt_call(x_prompt, st_meta, cv_meta, mix_w, tail_w, decode, layer, PROMPT_TILE, CHUNK),
        lambda: _prompt_call(x_prompt, st_meta, cv_meta, mix_w, tail_w, decode, layer, PROMPT_TILE, None))

    tm = MOE_TILE
    n_tiles = (n_tok * TOP_K) // tm + n_exp
    counts = counts[0].astype(jnp.int32)
    tiles_e = (counts + tm - 1) // tm
    tile_end = jnp.cumsum(tiles_e)
    n_used = tile_end[-1]
    offs = (tile_end - tiles_e) * tm
    eid4 = route[:n_tok, :TOP_K].astype(jnp.int32)
    experts = jnp.arange(n_exp, dtype=jnp.int32)
    off4 = jnp.sum(jnp.where(eid4[..., None] == experts, offs, 0), axis=-1)
    pos = off4 + rnk[:n_tok, :TOP_K].astype(jnp.int32)
    tile_ids = jnp.minimum(jnp.arange(n_tiles, dtype=jnp.int32), n_used - 1)
    tile_expert = jnp.minimum(jnp.sum((tile_end[None, :] <= tile_ids[:, None]).astype(jnp.int32), axis=1),
                              n_exp - 1)
    after = tile_end[tile_expert]
    next_expert = jnp.where(after < n_used, tile_expert[jnp.minimum(after, n_tiles - 1)], tile_expert)

    n_rows = xn_all.shape[0]
    n_workers = SC_CORES * SC_SUBCORES
    assert n_rows % (n_workers * DISPATCH_CHUNK) == 0
    n_spare = (n_rows - n_tok) * TOP_K
    n_slots = n_tiles * tm + n_spare
    spare = n_tiles * tm + jnp.arange(n_spare, dtype=jnp.int32).reshape(-1, TOP_K)
    pos_rows = jnp.concatenate([pos, spare], axis=0)

    def to_workers(p, ch):
        assert p.shape[0] % (n_workers * ch) == 0
        return p.reshape(n_workers, -1, ch, TOP_K).transpose(0, 1, 3, 2)

    xs = _dispatch(xn_all, to_workers(pos_rows, DISPATCH_CHUNK), n_slots)

    w1 = expert_w1[layer]
    w2 = expert_w2[layer]
    b1 = expert_b1[layer]
    b1g = b1[:, 0::2][:, None, :]
    b1l = b1[:, 1::2][:, None, :]
    b2 = expert_b2[layer][:, None, :]
    ys = pl.pallas_call(
        functools.partial(_moe_kernel, tm),
        grid_spec=pltpu.PrefetchScalarGridSpec(
            num_scalar_prefetch=3,
            grid=(n_tiles,),
            in_specs=[
                pl.BlockSpec((tm, d // 2), lambda i, te, nu, tn: (jnp.minimum(i, nu[0] - 1), 0)),
                pl.BlockSpec(memory_space=pl.ANY),
                pl.BlockSpec((None, 1, d_ff), lambda i, te, nu, tn: (te[i], 0, 0)),
                pl.BlockSpec((None, 1, d_ff), lambda i, te, nu, tn: (te[i], 0, 0)),
                pl.BlockSpec(memory_space=pl.ANY),
                pl.BlockSpec((None, 1, d), lambda i, te, nu, tn: (te[i], 0, 0)),
            ],
            out_specs=pl.BlockSpec((tm, d // 2), lambda i, te, nu, tn: (i, 0)),
            scratch_shapes=[pltpu.VMEM((d, 2 * d_ff), F32), pltpu.VMEM((d_ff, d), F32),
                            pltpu.SemaphoreType.DMA((2,)),
                            pltpu.VMEM((d, 2 * d_ff), BF16), pltpu.VMEM((d_ff, d), BF16),
                            pltpu.VMEM((tm, d_ff), BF16)],
        ),
        out_shape=jax.ShapeDtypeStruct((n_slots, d // 2), jnp.uint32),
        compiler_params=pltpu.CompilerParams(dimension_semantics=("arbitrary",),
                                             vmem_limit_bytes=VMEM_LIMIT),
        name="moe",
    )(tile_expert, n_used[None].astype(jnp.int32), next_expert, xs, w1, b1g, b1l, w2, b2)

    z = _collect(ys, to_workers(pos_rows, DISPATCH_CHUNK))
    y_p = _final_call(h_all, route, norm_final[None], z, 0, n_prompt, FINAL_TILE)
    y_s = _final_call(h_all, route, norm_final[None], z, n_prompt, ns, TOKEN_TILE)

    return (y_p.reshape(bp, seq, d), y_s.reshape(ns, 1, d),
            new_conv_p[None], new_rec_p[None],
            new_conv_s.reshape(1, ns, 2, d_conv), new_rec_s[None])
```

```python
import functools

import jax
import jax.numpy as jnp
from jax import lax
from jax.experimental import pallas as pl
from jax.experimental.pallas import tpu as pltpu
from jax.experimental.pallas import tpu_sc as plsc

F32 = jnp.float32
BF16 = jnp.bfloat16

N_HEADS = 4
HEAD = 128
N_META = 16
CHUNK = 64
SAFE_CHUNK = 16
DECAY_LIMIT = 80.0
TOP_K = 4
SWIGLU_LIMIT = 7.0
SWIGLU_ALPHA = 1.702
EPS = 1e-5

LANES = 128
MXU_N = 256
PROMPT_TILE = 256
TOKEN_TILE = 128
FINAL_TILE = 512
MOE_TILE = 512
SC_CORES = 2
SC_SUBCORES = 16
DISPATCH_CHUNK = 24
SC_RING = 4
VMEM_LIMIT = 56 * 1024 * 1024


def _dot(a, b):
    return jnp.dot(a, b, preferred_element_type=F32)


def _dot_nt(a, b):
    return lax.dot_general(a, b, (((1,), (1,)), ((), ())), preferred_element_type=F32)


def _dot_tn(a, b):
    return lax.dot_general(a, b, (((0,), (0,)), ((), ())), preferred_element_type=F32)


def _pack_halves(x):
    n = x.shape[-1] // 2
    lo = pltpu.bitcast(x[:, :n].astype(BF16).astype(F32), jnp.uint32)
    hi = pltpu.bitcast(x[:, n:].astype(BF16).astype(F32), jnp.uint32)
    return (lo >> 16) | (hi & jnp.uint32(0xFFFF0000))


def _unpack_halves(u):
    lo = pltpu.bitcast(u << 16, F32)
    hi = pltpu.bitcast(u & jnp.uint32(0xFFFF0000), F32)
    return jnp.concatenate([lo, hi], axis=-1)


def _sigmoid(x):
    return 1.0 / (1.0 + jnp.exp(-x))


def _rms(x, g):
    ms = jnp.mean(x * x, axis=-1, keepdims=True)
    return x * lax.rsqrt(ms + EPS) * g


def _project(x, nm_ref, win_ref):
    return _dot(_rms(x, nm_ref[...]).astype(BF16), win_ref[...])


def _split_u(u, d_conv, d_rec):
    pts = [0, d_conv, 2 * d_conv, 3 * d_conv, 3 * d_conv + d_rec, 3 * d_conv + 2 * d_rec,
           3 * d_conv + 3 * d_rec, 3 * d_conv + 4 * d_rec]
    return [u[:, pts[i]:pts[i + 1]] for i in range(7)]


def _lower_bound(rlb_ref, layer):
    r = rlb_ref[...]
    e = jnp.exp(r - jnp.max(r, axis=0, keepdims=True))
    return jnp.sum(e[0:layer + 1], axis=0, keepdims=True) / jnp.sum(e, axis=0, keepdims=True)


def _forget(fx, lb):
    f = lb + (1.0 - lb) * _sigmoid(fx)
    return f, 1.0 - f


def _rec_out(o, og, rg):
    parts = []
    for h in range(N_HEADS):
        oh = o[:, h * HEAD:(h + 1) * HEAD]
        parts.append(oh * lax.rsqrt(jnp.mean(oh * oh, axis=-1, keepdims=True) + EPS))
    return jnp.concatenate(parts, axis=-1) * rg * (og * _sigmoid(og))


def _route(logits):
    m_rows, n = logits.shape
    lane = lax.broadcasted_iota(jnp.int32, logits.shape, 1).astype(F32)
    work = logits
    tops, firsts = [], []
    sel = jnp.zeros_like(logits)
    for _ in range(TOP_K):
        m = jnp.max(work, axis=-1, keepdims=True)
        first = jnp.min(jnp.where(work == m, lane, float(n)), axis=-1, keepdims=True)
        hot = lane == first
        tops.append(m)
        firsts.append(first)
        sel = sel + jnp.where(hot, 1.0, 0.0)
        work = jnp.where(hot, -jnp.inf, work)
    es = [jnp.exp(t - tops[0]) for t in tops]
    den = es[0]
    for e in es[1:]:
        den = den + e
    wide = lax.broadcasted_iota(jnp.int32, (m_rows, LANES), 1)
    route = jnp.zeros((m_rows, LANES), F32)
    for k in range(TOP_K):
        route = jnp.where(wide == k, firsts[k], route)
        route = jnp.where(wide == TOP_K + k, es[k] / den, route)
    return sel, route


def _tail(x, y, wout_ref, nf_ref, rw_ref, rb_ref):
    h = x + _dot(y.astype(BF16), wout_ref[...])
    xn = _rms(h, nf_ref[...])
    xh = xn.astype(BF16)
    xl = (xn - xh.astype(F32)).astype(BF16)
    rw = rw_ref[...]
    wh = rw.astype(BF16)
    wl = (rw - wh.astype(F32)).astype(BF16)
    logits = _dot(xh, wh) + _dot(xh, wl) + _dot(xl, wh) + rb_ref[...]
    sel, route = _route(logits)
    return h, xn, sel, route


def _cumsum_rows(x):
    n = x.shape[0]
    row = lax.broadcasted_iota(jnp.int32, (n, 1), 0)
    shift = 1
    while shift < n:
        x = x + jnp.where(row >= shift, pltpu.roll(x, shift, axis=0), 0.0)
        shift *= 2
    return x


def _rec_chunk(q, kk, v, lf, st_ref, causal):
    c = q.shape[0]
    b = _cumsum_rows(lf)
    eb = jnp.exp(b)
    qe = (q * eb).astype(BF16)
    ke = (kk * jnp.exp(-b)).astype(BF16)
    vb = v.astype(BF16)
    eb_last = eb[c - 1:c]
    outs = []
    for h in range(N_HEADS):
        sl = slice(h * HEAD, (h + 1) * HEAD)
        st = st_ref[h]
        sc = jnp.where(causal, _dot_nt(qe[:, sl], ke[:, sl]), 0.0)
        outs.append(_dot(sc.astype(BF16), vb[:, sl]) + _dot_nt(qe[:, sl], st.astype(BF16)))
        st_ref[h] = (st + _dot_tn(vb[:, sl], ke[:, sl])) * eb_last[:, sl]
    return jnp.concatenate(outs, axis=-1)


def _rec_chunk_safe(q, kk, v, lf, st_ref):
    c = q.shape[0]
    row = lax.broadcasted_iota(jnp.int32, (c, 1), 0)
    b = _cumsum_rows(lf)
    eb = jnp.exp(b)
    qe = (q * eb).astype(BF16)
    kl = (kk * jnp.exp(b[c - 1:c] - b)).astype(BF16)
    vb = v.astype(BF16)
    outs = []
    for h in range(N_HEADS):
        sl = slice(h * HEAD, (h + 1) * HEAD)
        st = st_ref[h]
        bh, kh, qh, vh = b[:, sl], kk[:, sl], q[:, sl], v[:, sl]
        intra = jnp.zeros((c, HEAD), F32)
        for t in range(c):
            w = kh * jnp.exp(jnp.where(row <= t, bh[t:t + 1] - bh, -jnp.inf))
            score = jnp.sum(w * qh[t:t + 1], axis=-1, keepdims=True)
            intra = jnp.where(row == t, jnp.sum(score * vh, axis=0, keepdims=True), intra)
        outs.append(intra + _dot_nt(qe[:, sl], st.astype(BF16)))
        st_ref[h] = st * eb[c - 1:c, sl] + _dot_tn(vb[:, sl], kl[:, sl])
    return jnp.concatenate(outs, axis=-1)


def _mix_tile(u, layer, chunk, cw_ref, rlb_ref, convbuf, st, cv_out):
    tm = u.shape[0]
    d_conv = cw_ref.shape[-1]
    d_rec = rlb_ref.shape[-1]
    bg, cg, hv, q, fx, iv, og = _split_u(u, d_conv, d_rec)

    bx = bg * hv
    convbuf[8:8 + tm, :] = bx
    cw = cw_ref[...]
    conv = cw[0:1] * convbuf[6:6 + tm, :] + cw[1:2] * convbuf[7:7 + tm, :] + cw[2:3] * bx
    convbuf[6:8, :] = bx[tm - 2:tm]
    cv_out[...] = bx[tm - 2:tm]

    f, kk = _forget(fx, _lower_bound(rlb_ref, layer))
    lf = jnp.log(f)
    outs = []
    if chunk is None:
        for c in range(tm // SAFE_CHUNK):
            rs = slice(c * SAFE_CHUNK, (c + 1) * SAFE_CHUNK)
            outs.append(_rec_chunk_safe(q[rs], kk[rs], iv[rs], lf[rs], st))
    else:
        row = lax.broadcasted_iota(jnp.int32, (chunk, chunk), 0)
        col = lax.broadcasted_iota(jnp.int32, (chunk, chunk), 1)
        causal = row >= col
        for c in range(tm // chunk):
            rs = slice(c * chunk, (c + 1) * chunk)
            outs.append(_rec_chunk(q[rs], kk[rs], iv[rs], lf[rs], st, causal))
    return cg * conv, jnp.concatenate(outs, axis=0), og


def _load_state(st, convbuf, s0_ref, c0_ref):
    for h in range(N_HEADS):
        st[h] = s0_ref[h].T
    convbuf[6:8, :] = c0_ref[...]


def _meta_kernel(layer, chunk, x_ref, s0_ref, c0_ref, nm_ref, win_ref, cw_ref, rlb_ref, st_out, cv_out,
                 convbuf, st):
    _load_state(st, convbuf, s0_ref, c0_ref)
    _mix_tile(_project(x_ref[...], nm_ref, win_ref), layer, chunk, cw_ref, rlb_ref, convbuf, st, cv_out)
    for h in range(N_HEADS):
        st_out[h] = st[h].T


def _prompt_kernel(layer, tm, chunk, n_pairs, pairs_per_seq, cap,
                   xr0_ref, x1_ref, xp2_ref, s0_ref, c0_ref, nm_ref, win_ref, cw_ref, rlb_ref,
                   rg_ref, wout_ref, nf_ref, rw_ref, rb_ref, hs_ref, xns_ref, sels_ref, routes_ref,
                   h_ref, xn_ref, route_ref, rnk_ref, cnt_ref, st_out, cv_out,
                   convbuf, st, ua, ub, carry):
    step = pl.program_id(0)
    live = step < n_pairs

    @pl.when(step == 0)
    def _():
        ua[...] = _project(xr0_ref[...], nm_ref, win_ref)
        carry[...] = jnp.zeros_like(carry)

    @pl.when(jnp.logical_and(lax.rem(step, pairs_per_seq) == 0, live))
    def _():
        _load_state(st, convbuf, s0_ref, c0_ref)

    @pl.when(live)
    def _():
        def finish(x, u, rows):
            y_conv, o, og = _mix_tile(u, layer, chunk, cw_ref, rlb_ref, convbuf, st, cv_out)
            y = jnp.concatenate([y_conv, _rec_out(o, og, rg_ref[...])], axis=-1)
            h, xn, sel, route = _tail(x, y, wout_ref, nf_ref, rw_ref, rb_ref)
            h_ref[rows, :] = h
            xn_ref[rows, :] = _pack_halves(xn)
            route_ref[rows, :] = route
            rnk_ref[rows, :] = _rank_block(sel, route, carry, cap)

        ub[...] = _project(x1_ref[...], nm_ref, win_ref)
        finish(xr0_ref[...], ua[...], slice(0, tm))
        ua[...] = _project(xp2_ref[...], nm_ref, win_ref)
        finish(x1_ref[...], ub[...], slice(tm, 2 * tm))

    @pl.when(jnp.logical_and(lax.rem(step, pairs_per_seq) == pairs_per_seq - 1, live))
    def _():
        for h in range(N_HEADS):
            st_out[h] = st[h].T

    @pl.when(step == n_pairs)
    def _():
        ns = hs_ref.shape[0]
        n_exp = sels_ref.shape[-1]
        pad = 2 * tm - ns
        spare = (lax.broadcasted_iota(jnp.int32, (pad, LANES), 0) * TOP_K
                 + lax.broadcasted_iota(jnp.int32, (pad, LANES), 1) + n_exp * cap).astype(F32)
        rnk_ref[0:ns, :] = _rank_block(sels_ref[...], routes_ref[...], carry, cap)
        rnk_ref[ns:2 * tm, :] = spare
        for dst, val in ((h_ref, hs_ref[...]), (xn_ref, _pack_halves(xns_ref[...])),
                         (route_ref, routes_ref[...])):
            dst[0:ns, :] = val
            dst[ns:2 * tm, :] = jnp.zeros((pad, dst.shape[-1]), dst.dtype)
        cnt_ref[...] = carry[...]


def _const_spec(shape):
    return pl.BlockSpec(shape, lambda *_: (0,) * len(shape))


def _meta_call(x, weights, layer, chunk):
    assert x.shape[0] % (chunk or SAFE_CHUNK) == 0
    nm, win, cw, rlb = weights
    d_conv = cw.shape[-1]
    s0 = jnp.zeros((N_HEADS, HEAD, HEAD), F32)
    c0 = jnp.zeros((2, d_conv), F32)
    return pl.pallas_call(
        functools.partial(_meta_kernel, layer, chunk),
        out_shape=[jax.ShapeDtypeStruct(s0.shape, F32), jax.ShapeDtypeStruct(c0.shape, F32)],
        scratch_shapes=[pltpu.VMEM((x.shape[0] + 8, d_conv), F32), pltpu.VMEM(s0.shape, F32)],
        compiler_params=pltpu.CompilerParams(vmem_limit_bytes=VMEM_LIMIT),
        name="mixer_meta",
    )(x, s0, c0, nm, win, cw, rlb)


def _prompt_call(x, s0, c0, weights, tail_w, decode, layer, tm, chunk):
    nseq, length, d = x.shape
    nt = length // tm
    assert nt % 2 == 0 and decode[0].shape[0] <= 2 * tm and tm % (chunk or SAFE_CHUNK) == 0
    pairs_per_seq = nt // 2
    n_pairs = nseq * pairs_per_seq
    nm, win, cw, rlb = weights
    d_conv = cw.shape[-1]
    d_in = win.shape[-1]

    def tile_spec(offset):
        def index(s):
            tile = jnp.minimum(2 * s + offset, 2 * n_pairs - 1)
            return (tile // nt, lax.rem(tile, nt), 0)
        return pl.BlockSpec((None, tm, d), index)

    def seq_of(s):
        return jnp.minimum(s // pairs_per_seq, nseq - 1)

    consts = [s0, c0, nm, win, cw, rlb] + list(tail_w) + list(decode)
    n_exp = decode[2].shape[-1]
    tok = [(d, F32), (d // 2, jnp.uint32), (LANES, F32), (LANES, F32)]
    n_steps = n_pairs + 1
    return pl.pallas_call(
        functools.partial(_prompt_kernel, layer, tm, chunk, n_pairs, pairs_per_seq, n_steps * 2 * tm),
        grid=(n_steps,),
        in_specs=[tile_spec(0), tile_spec(1), tile_spec(2)] + [_const_spec(a.shape) for a in consts],
        out_specs=[pl.BlockSpec((2 * tm, w), lambda s: (s, 0)) for w, _ in tok] + [
            pl.BlockSpec((1, n_exp), lambda s: (0, 0)),
            pl.BlockSpec((None,) + s0.shape, lambda s: (seq_of(s), 0, 0, 0)),
            pl.BlockSpec((None,) + c0.shape, lambda s: (seq_of(s), 0, 0))],
        out_shape=[jax.ShapeDtypeStruct((n_steps * 2 * tm, w), t) for w, t in tok] + [
            jax.ShapeDtypeStruct((1, n_exp), F32),
            jax.ShapeDtypeStruct((nseq,) + s0.shape, F32), jax.ShapeDtypeStruct((nseq,) + c0.shape, F32)],
        scratch_shapes=[pltpu.VMEM((tm + 8, d_conv), F32), pltpu.VMEM(s0.shape, F32),
                        pltpu.VMEM((tm, d_in), F32), pltpu.VMEM((tm, d_in), F32),
                        pltpu.VMEM((1, n_exp), F32)],
        compiler_params=pltpu.CompilerParams(dimension_semantics=("arbitrary",),
                                             vmem_limit_bytes=VMEM_LIMIT),
        name="mixer_prompt",
    )(x, x, x, *consts)


def _sample_in_kernel(layer, x_ref, sc_ref, nm_ref, win_ref, cw_ref, rlb_ref,
                      yc_ref, nc_ref, f_ref, k_ref, q_ref, v_ref, og_ref):
    d_conv = cw_ref.shape[-1]
    d_rec = rlb_ref.shape[-1]
    u = _project(x_ref[...], nm_ref, win_ref)
    bg, cg, hv, q, fx, iv, og = _split_u(u, d_conv, d_rec)
    bx = bg * hv
    sc = sc_ref[...]
    s0, s1 = sc[:, :d_conv], sc[:, d_conv:]
    cw = cw_ref[...]
    yc_ref[...] = cg * (cw[0:1] * s0 + cw[1:2] * s1 + cw[2:3] * bx)
    nc_ref[...] = jnp.concatenate([s1, bx], axis=-1)
    f, kk = _forget(fx, _lower_bound(rlb_ref, layer))
    f_ref[...] = f
    k_ref[...] = kk
    q_ref[...] = q
    v_ref[...] = iv
    og_ref[...] = og


def _sample_state_kernel(group, f_ref, k_ref, q_ref, v_ref, s_ref, sn_ref, o_ref):
    for j in range(group):
        for h in range(N_HEADS):
            rs = slice(h * HEAD, (h + 1) * HEAD)
            fcol = f_ref[rs, j:j + 1]
            kcol = k_ref[rs, j:j + 1]
            qcol = q_ref[rs, j:j + 1]
            vrow = v_ref[j:j + 1, rs]
            sn = fcol * s_ref[j, h] + kcol * vrow
            sn_ref[j, h] = sn
            o_ref[j:j + 1, rs] = jnp.sum(qcol * sn, axis=0, keepdims=True)


def _sample_tail_kernel(x_ref, yc_ref, o_ref, og_ref, rg_ref, wout_ref, nf_ref, rw_ref, rb_ref,
                        h_ref, xn_ref, sel_ref, route_ref):
    y = jnp.concatenate([yc_ref[...], _rec_out(o_ref[...], og_ref[...], rg_ref[...])], axis=-1)
    h, xn, sel, route = _tail(x_ref[...], y, wout_ref, nf_ref, rw_ref, rb_ref)
    h_ref[...] = h
    xn_ref[...] = xn
    sel_ref[...] = sel
    route_ref[...] = route


def _rank_block(sel, route, carry, cap):
    tb, ne = sel.shape
    row = lax.broadcasted_iota(jnp.int32, (tb, tb), 0)
    col = lax.broadcasted_iota(jnp.int32, (tb, tb), 1)
    before = jnp.where(col < row, 1.0, 0.0).astype(BF16)
    rank = _dot(before, sel.astype(BF16)) + carry[...]
    carry[...] = carry[...] + jnp.sum(sel, axis=0, keepdims=True)
    lane_e = lax.broadcasted_iota(jnp.int32, (tb, ne), 1).astype(F32)
    lane = lax.broadcasted_iota(jnp.int32, (tb, LANES), 1)
    rnk = jnp.zeros((tb, LANES), F32)
    for k in range(TOP_K):
        expert = route[:, k:k + 1]
        mine = jnp.where(lane_e == expert, rank, 0.0)
        rnk = jnp.where(lane == k, expert * float(cap) + jnp.sum(mine, axis=-1, keepdims=True), rnk)
    return rnk


def _dispatch(xn, pos_w, n_slots):
    n_workers, n_chunks, top_k, ch = pos_w.shape
    assert n_workers == SC_CORES * SC_SUBCORES and ch % 8 == 0 and ch <= LANES
    d = xn.shape[1]
    mesh = plsc.VectorSubcoreMesh(core_axis_name="c", subcore_axis_name="s")

    @functools.partial(
        pl.kernel, mesh=mesh,
        out_type=jax.ShapeDtypeStruct((n_slots, d), xn.dtype),
        scratch_types=[pltpu.VMEM((n_chunks, top_k, ch), jnp.int32), pltpu.VMEM((SC_RING, ch, d), xn.dtype),
                       pltpu.SemaphoreType.DMA((SC_RING,)), pltpu.SemaphoreType.DMA((SC_RING,))],
        name="dispatch",
    )
    def run(xn_hbm, pos_hbm, xs_hbm, idx_v, rows_v, sem_r, sem_w):
        wid = lax.axis_index("s") * SC_CORES + lax.axis_index("c")

        def read(c, b):
            src = xn_hbm.at[pl.ds((wid * n_chunks + c) * ch, ch)]
            return pltpu.make_async_copy(src, rows_v.at[b], sem_r.at[b])

        def write(c, b, k):
            return pltpu.make_async_copy(rows_v.at[b], xs_hbm.at[idx_v.at[c, k]], sem_w.at[b])

        pltpu.sync_copy(pos_hbm.at[wid], idx_v)
        ahead = SC_RING - 1
        for c0 in range(min(ahead, n_chunks)):
            read(c0, c0).start()

        @pl.loop(0, n_chunks)
        def _(c):
            b = lax.rem(c, SC_RING)
            read(c, b).wait()
            for k in range(top_k):
                write(c, b, k).start()

            @pl.when(c >= 1)
            def _():
                for k in range(top_k):
                    write(c - 1, lax.rem(c - 1, SC_RING), k).wait()

            @pl.when(c + ahead < n_chunks)
            def _():
                read(c + ahead, lax.rem(c + ahead, SC_RING)).start()

        for k in range(top_k):
            write(n_chunks - 1, (n_chunks - 1) % SC_RING, k).wait()

    return run(xn, pos_w)


def _collect(ys, pos_w):
    n_workers, n_chunks, top_k, ch = pos_w.shape
    assert n_workers == SC_CORES * SC_SUBCORES and ch % 8 == 0 and ch <= LANES
    d = ys.shape[1]
    mesh = plsc.VectorSubcoreMesh(core_axis_name="c", subcore_axis_name="s")

    @functools.partial(
        pl.kernel, mesh=mesh,
        out_type=jax.ShapeDtypeStruct((n_workers * n_chunks * ch, top_k * d), ys.dtype),
        scratch_types=[pltpu.VMEM((n_chunks, top_k, ch), jnp.int32), pltpu.VMEM((top_k, ch, d), ys.dtype),
                       pltpu.SemaphoreType.DMA((top_k,)), pltpu.SemaphoreType.DMA((top_k,))],
        name="collect",
    )
    def run(ys_hbm, pos_hbm, out_hbm, idx_v, rows_v, sem_r, sem_w):
        wid = lax.axis_index("s") * SC_CORES + lax.axis_index("c")

        def read(c, k):
            return pltpu.make_async_copy(ys_hbm.at[idx_v.at[c, k]], rows_v.at[k], sem_r.at[k])

        def write(c, k):
            dst = out_hbm.at[pl.ds((wid * n_chunks + c) * ch, ch), pl.ds(k * d, d)]
            return pltpu.make_async_copy(rows_v.at[k], dst, sem_w.at[k])

        pltpu.sync_copy(pos_hbm.at[wid], idx_v)
        for k in range(top_k - 1):
            read(0, k).start()

        @pl.loop(0, n_chunks)
        def _(c):
            for k in range(top_k):
                read(c, k).wait()
                write(c, k).start()
                if k >= 1:
                    write(c, k - 1).wait()

                    @pl.when(c + 1 < n_chunks)
                    def _():
                        read(c + 1, k - 1).start()
                else:
                    @pl.when(c >= 1)
                    def _():
                        write(c - 1, top_k - 1).wait()
                    read(c, top_k - 1).start()

        write(n_chunks - 1, top_k - 1).wait()

    return run(ys, pos_w)


def _moe_kernel(tm, te_ref, nu_ref, tn_ref, tb_ref, x_ref, w1_hbm, bg_ref, bl_ref, w2_hbm, b2_ref,
                ys_ref, w1f, w2f, sem, w1p, w2b, act):
    i = pl.program_id(0)
    n_used = nu_ref[0]
    d_ff2 = w1f.shape[-1]
    n_blk = d_ff2 // MXU_N
    expert = te_ref[i]

    def weight_copies(e):
        return (pltpu.make_async_copy(w1_hbm.at[e], w1f, sem.at[0]),
                pltpu.make_async_copy(w2_hbm.at[e], w2f, sem.at[1]))

    @pl.when(i == 0)
    def _():
        for cp in weight_copies(expert):
            cp.start()

    @pl.when(i >= n_used)
    def _():
        ys_ref[...] = jnp.zeros_like(ys_ref)

    prev = te_ref[jnp.maximum(i - 1, 0)]
    changed = jnp.logical_or(i == 0, expert != prev)

    @pl.when(jnp.logical_and(changed, i < n_used))
    def _():
        for cp in weight_copies(expert):
            cp.wait()
        r = lax.broadcasted_iota(jnp.int32, (MXU_N, MXU_N), 0)
        c = lax.broadcasted_iota(jnp.int32, (MXU_N, MXU_N), 1)
        src = jnp.where(c < MXU_N // 2, 2 * c, 2 * (c - MXU_N // 2) + 1)
        perm = jnp.where(r == src, 1.0, 0.0).astype(BF16)
        for blk in range(n_blk):
            cs = slice(blk * MXU_N, (blk + 1) * MXU_N)
            w1p[:, cs] = _dot(w1f[:, cs].astype(BF16), perm).astype(BF16)
        w2b[...] = w2f[...].astype(BF16)

        @pl.when(tn_ref[i] != expert)
        def _():
            for cp in weight_copies(tn_ref[i]):
                cp.start()

    @pl.when(i < n_used)
    def _():
        x = _unpack_halves(x_ref[...]).astype(BF16)
        bg, bl = bg_ref[expert], bl_ref[expert]
        for blk in range(n_blk):
            a = _dot(x, w1p[:, blk * MXU_N:(blk + 1) * MXU_N])
            half = MXU_N // 2
            hs = slice(blk * half, (blk + 1) * half)
            glu = jnp.minimum(a[:, :half] + bg[:, hs], SWIGLU_LIMIT)
            lin = jnp.clip(a[:, half:] + bl[:, hs], -SWIGLU_LIMIT, SWIGLU_LIMIT)
            act[:, hs] = (glu * _sigmoid(SWIGLU_ALPHA * glu) * (lin + 1.0)).astype(BF16)
        ys_ref[...] = _pack_halves(_dot(act[...], w2b[...]) + b2_ref[expert])


def _final_kernel(h_ref, gate_ref, nfin_ref, z_ref, y_ref):
    w = z_ref.shape[-1] // TOP_K
    g = gate_ref[...]
    out = h_ref[...]
    for k in range(TOP_K):
        out = out + g[:, TOP_K + k:TOP_K + k + 1] * _unpack_halves(z_ref[:, k * w:(k + 1) * w])
    y_ref[...] = _rms(out, nfin_ref[...])


def _final_call(h_all, gate, nfin, z, first_row, n_rows, tile):
    d = h_all.shape[-1]
    off = first_row // tile
    assert first_row % tile == 0 and n_rows % tile == 0
    return pl.pallas_call(
        _final_kernel,
        grid=(n_rows // tile,),
        in_specs=[
            pl.BlockSpec((tile, d), lambda i: (i + off, 0)),
            pl.BlockSpec((tile, gate.shape[-1]), lambda i: (i + off, 0)),
            pl.BlockSpec((1, d), lambda i: (0, 0)),
            pl.BlockSpec((tile, z.shape[-1]), lambda i: (i + off, 0)),
        ],
        out_specs=pl.BlockSpec((tile, d), lambda i: (i, 0)),
        out_shape=jax.ShapeDtypeStruct((n_rows, d), F32),
        compiler_params=pltpu.CompilerParams(dimension_semantics=("arbitrary",),
                                             vmem_limit_bytes=VMEM_LIMIT),
        name="final",
    )(h_all, gate, nfin, z)


def kernel(x_prompt, x_sample, state_conv, state_rec, meta_tokens, norm_mix, w_in, conv_w,
           rec_lower_bound, rec_norm, w_out, norm_ffn, router_w, router_b, expert_w1, expert_b1,
           expert_w2, expert_b2, norm_final):
    depth = norm_mix.shape[0]
    assert depth == 1, "single-layer step"
    layer = 0
    bp, seq, d = x_prompt.shape
    ns = x_sample.shape[0]
    assert x_sample.shape[1] == 1 and ns == TOKEN_TILE
    d_conv = conv_w.shape[-1]
    d_rec = rec_lower_bound.shape[-1]
    assert state_conv.shape[2] == 2 and d_rec == N_HEADS * HEAD
    n_exp = router_w.shape[-1]
    d_ff = expert_w2.shape[2]
    n_prompt = bp * seq
    n_tok = n_prompt + ns
    assert seq % PROMPT_TILE == 0 and n_prompt % TOKEN_TILE == 0

    nm = norm_mix[layer][None]
    win = w_in[layer].astype(BF16)
    cw = conv_w[layer]
    rlb = rec_lower_bound
    rg = rec_norm[layer][None]
    wout = w_out[layer].astype(BF16)
    nf = norm_ffn[layer][None]
    rw = router_w[layer]
    rb = router_b[layer][None]
    mix_w = (nm, win, cw, rlb)
    tail_w = (rg, wout, nf, rw, rb)

    lb = jnp.sum(jax.nn.softmax(rlb, axis=0)[:layer + 1], axis=0)
    fast = CHUNK * jnp.max(-jnp.log(lb)) < DECAY_LIMIT

    st_meta, cv_meta = lax.cond(
        fast, lambda: _meta_call(meta_tokens, mix_w, layer, N_META),
        lambda: _meta_call(meta_tokens, mix_w, layer, None))

    xs = x_sample.reshape(ns, d)
    wide = jax.ShapeDtypeStruct((ns, d_rec), F32)
    y_conv_s, new_conv_s, f_s, k_s, q_s, v_s, og_s = pl.pallas_call(
        functools.partial(_sample_in_kernel, layer),
        out_shape=[jax.ShapeDtypeStruct((ns, d_conv), F32), jax.ShapeDtypeStruct((ns, 2 * d_conv), F32),
                   wide, wide, wide, wide, wide],
        compiler_params=pltpu.CompilerParams(vmem_limit_bytes=VMEM_LIMIT),
        name="sample_in",
    )(xs, state_conv[layer].reshape(ns, 2 * d_conv), nm, win, cw, rlb)

    group = 8
    n_grp = ns // group

    def cols(a):
        return a.T.reshape(d_rec, n_grp, group).transpose(1, 0, 2)

    col_spec = pl.BlockSpec((None, d_rec, group), lambda g: (g, 0, 0))
    st_spec = pl.BlockSpec((group, N_HEADS, HEAD, HEAD), lambda g: (g, 0, 0, 0))
    row_spec = pl.BlockSpec((group, d_rec), lambda g: (g, 0))
    new_rec_s, o_s = pl.pallas_call(
        functools.partial(_sample_state_kernel, group),
        grid=(n_grp,),
        in_specs=[col_spec, col_spec, col_spec, row_spec, st_spec],
        out_specs=[st_spec, row_spec],
        out_shape=[jax.ShapeDtypeStruct(state_rec.shape[1:], F32), wide],
        compiler_params=pltpu.CompilerParams(dimension_semantics=("arbitrary",),
                                             vmem_limit_bytes=VMEM_LIMIT),
        name="sample_state",
    )(cols(f_s), cols(k_s), cols(q_s), v_s, state_rec[layer])

    decode = pl.pallas_call(
        _sample_tail_kernel,
        out_shape=[jax.ShapeDtypeStruct((ns, d), F32), jax.ShapeDtypeStruct((ns, d), F32),
                   jax.ShapeDtypeStruct((ns, n_exp), F32), jax.ShapeDtypeStruct((ns, LANES), F32)],
        compiler_params=pltpu.CompilerParams(vmem_limit_bytes=VMEM_LIMIT),
        name="sample_tail",
    )(xs, y_conv_s, o_s, og_s, rg, wout, nf, rw, rb)

    h_all, xn_all, route, rnk, counts, new_rec_p, new_conv_p = lax.cond(
        fast,
        lambda: _prompt_call(x_prompt, st_meta, cv_meta, mix_w, tail_w, decode, layer, PROMPT_TILE, CHUNK),
        lambda: _prompt_call(x_prompt, st_meta, cv_meta, mix_w, tail_w, decode, layer, PROMPT_TILE, None))

    tm = MOE_TILE
    cap = h_all.shape[0]
    assert cap % tm == 0
    n_tiles = (n_tok * TOP_K) // tm + n_exp
    counts = counts[0].astype(jnp.int32)
    tiles_e = (counts + tm - 1) // tm
    tile_end = jnp.cumsum(tiles_e)
    n_used = tile_end[-1]
    tile_ids = jnp.arange(n_tiles, dtype=jnp.int32)
    tile_expert = jnp.minimum(
        jnp.sum((tile_end[None, :] <= jnp.minimum(tile_ids, n_used - 1)[:, None]).astype(jnp.int32), axis=1),
        n_exp - 1)
    after = tile_end[tile_expert]
    next_expert = jnp.where(after < n_used, tile_expert[jnp.minimum(after, n_tiles - 1)], tile_expert)
    n_rows = cap
    n_spare = (n_rows - n_tok) * TOP_K
    spare_blocks = -(-n_spare // tm)
    dummy_block = n_exp * (cap // tm) + spare_blocks
    n_slots = (dummy_block + 1) * tm
    tile_block = jnp.where(tile_ids < n_used,
                           tile_expert * (cap // tm) + tile_ids - (tile_end - tiles_e)[tile_expert],
                           dummy_block)

    n_workers = SC_CORES * SC_SUBCORES
    assert n_rows % (n_workers * DISPATCH_CHUNK) == 0
    pos_rows = rnk[:, :TOP_K].astype(jnp.int32)

    def to_workers(p, ch):
        assert p.shape[0] % (n_workers * ch) == 0
        return p.reshape(n_workers, -1, ch, TOP_K).transpose(0, 1, 3, 2)

    xs = _dispatch(xn_all, to_workers(pos_rows, DISPATCH_CHUNK), n_slots)

    w1 = expert_w1[layer]
    w2 = expert_w2[layer]
    b1 = expert_b1[layer]
    b1g = b1[:, 0::2][:, None, :]
    b1l = b1[:, 1::2][:, None, :]
    b2 = expert_b2[layer][:, None, :]
    ys = pl.pallas_call(
        functools.partial(_moe_kernel, tm),
        grid_spec=pltpu.PrefetchScalarGridSpec(
            num_scalar_prefetch=4,
            grid=(n_tiles,),
            in_specs=[
                pl.BlockSpec((tm, d // 2), lambda i, te, nu, tn, tb: (tb[i], 0)),
                pl.BlockSpec(memory_space=pl.ANY),
                _const_spec(b1g.shape), _const_spec(b1l.shape),
                pl.BlockSpec(memory_space=pl.ANY),
                _const_spec(b2.shape),
            ],
            out_specs=pl.BlockSpec((tm, d // 2), lambda i, te, nu, tn, tb: (tb[i], 0)),
            scratch_shapes=[pltpu.VMEM((d, 2 * d_ff), F32), pltpu.VMEM((d_ff, d), F32),
                            pltpu.SemaphoreType.DMA((2,)),
                            pltpu.VMEM((d, 2 * d_ff), BF16), pltpu.VMEM((d_ff, d), BF16),
                            pltpu.VMEM((tm, d_ff), BF16)],
        ),
        out_shape=jax.ShapeDtypeStruct((n_slots, d // 2), jnp.uint32),
        compiler_params=pltpu.CompilerParams(dimension_semantics=("arbitrary",),
                                             vmem_limit_bytes=VMEM_LIMIT),
        name="moe",
    )(tile_expert, n_used[None].astype(jnp.int32), next_expert, tile_block, xs, w1, b1g, b1l, w2, b2)

    z = _collect(ys, to_workers(pos_rows, DISPATCH_CHUNK))
    y_p = _final_call(h_all, route, norm_final[None], z, 0, n_prompt, FINAL_TILE)
    y_s = _final_call(h_all, route, norm_final[None], z, n_prompt, ns, TOKEN_TILE)

    return (y_p.reshape(bp, seq, d), y_s.reshape(ns, 1, d),
            new_conv_p[None], new_rec_p[None],
            new_conv_s.reshape(1, ns, 2, d_conv), new_rec_s[None])
```

```python
import functools

import jax
import jax.numpy as jnp
from jax import lax
from jax.experimental import pallas as pl
from jax.experimental.pallas import tpu as pltpu
from jax.experimental.pallas import tpu_sc as plsc

F32 = jnp.float32
BF16 = jnp.bfloat16

N_HEADS = 4
HEAD = 128
N_META = 16
CHUNK = 64
SAFE_CHUNK = 16
DECAY_LIMIT = 80.0
TOP_K = 4
SWIGLU_LIMIT = 7.0
SWIGLU_ALPHA = 1.702
EPS = 1e-5

LANES = 128
MXU_N = 256
PROMPT_TILE = 256
TOKEN_TILE = 128
FINAL_TILE = 512
MOE_TILE = 512
SC_CORES = 2
SC_SUBCORES = 16
DISPATCH_CHUNK = 24
SC_RING = 4
VMEM_LIMIT = 56 * 1024 * 1024


def _dot(a, b):
    return jnp.dot(a, b, preferred_element_type=F32)


def _dot_nt(a, b):
    return lax.dot_general(a, b, (((1,), (1,)), ((), ())), preferred_element_type=F32)


def _dot_tn(a, b):
    return lax.dot_general(a, b, (((0,), (0,)), ((), ())), preferred_element_type=F32)


def _pack_halves(x):
    n = x.shape[-1] // 2
    lo = pltpu.bitcast(x[:, :n].astype(BF16).astype(F32), jnp.uint32)
    hi = pltpu.bitcast(x[:, n:].astype(BF16).astype(F32), jnp.uint32)
    return (lo >> 16) | (hi & jnp.uint32(0xFFFF0000))


def _unpack_halves(u):
    lo = pltpu.bitcast(u << 16, F32)
    hi = pltpu.bitcast(u & jnp.uint32(0xFFFF0000), F32)
    return jnp.concatenate([lo, hi], axis=-1)


def _sigmoid(x):
    return 1.0 / (1.0 + jnp.exp(-x))


def _rms(x, g):
    ms = jnp.mean(x * x, axis=-1, keepdims=True)
    return x * lax.rsqrt(ms + EPS) * g


def _project(x, nm_ref, win_ref):
    return _dot(_rms(x, nm_ref[...]).astype(BF16), win_ref[...])


def _split_u(u, d_conv, d_rec):
    pts = [0, d_conv, 2 * d_conv, 3 * d_conv, 3 * d_conv + d_rec, 3 * d_conv + 2 * d_rec,
           3 * d_conv + 3 * d_rec, 3 * d_conv + 4 * d_rec]
    return [u[:, pts[i]:pts[i + 1]] for i in range(7)]


def _lower_bound(rlb_ref, layer):
    r = rlb_ref[...]
    e = jnp.exp(r - jnp.max(r, axis=0, keepdims=True))
    return jnp.sum(e[0:layer + 1], axis=0, keepdims=True) / jnp.sum(e, axis=0, keepdims=True)


def _forget(fx, lb):
    f = lb + (1.0 - lb) * _sigmoid(fx)
    return f, 1.0 - f


def _rec_out(o, og, rg):
    parts = []
    for h in range(N_HEADS):
        oh = o[:, h * HEAD:(h + 1) * HEAD]
        parts.append(oh * lax.rsqrt(jnp.mean(oh * oh, axis=-1, keepdims=True) + EPS))
    return jnp.concatenate(parts, axis=-1) * rg * (og * _sigmoid(og))


def _route(logits):
    m_rows, n = logits.shape
    lane = lax.broadcasted_iota(jnp.int32, logits.shape, 1).astype(F32)
    work = logits
    tops, firsts = [], []
    sel = jnp.zeros_like(logits)
    for _ in range(TOP_K):
        m = jnp.max(work, axis=-1, keepdims=True)
        first = jnp.min(jnp.where(work == m, lane, float(n)), axis=-1, keepdims=True)
        hot = lane == first
        tops.append(m)
        firsts.append(first)
        sel = sel + jnp.where(hot, 1.0, 0.0)
        work = jnp.where(hot, -jnp.inf, work)
    es = [jnp.exp(t - tops[0]) for t in tops]
    den = es[0]
    for e in es[1:]:
        den = den + e
    wide = lax.broadcasted_iota(jnp.int32, (m_rows, LANES), 1)
    route = jnp.zeros((m_rows, LANES), F32)
    for k in range(TOP_K):
        route = jnp.where(wide == k, firsts[k], route)
        route = jnp.where(wide == TOP_K + k, es[k] / den, route)
    return sel, route


def _tail(x, y, wout_ref, nf_ref, rw_ref, rb_ref):
    h = x + _dot(y.astype(BF16), wout_ref[...])
    xn = _rms(h, nf_ref[...])
    xh = xn.astype(BF16)
    xl = (xn - xh.astype(F32)).astype(BF16)
    rw = rw_ref[...]
    wh = rw.astype(BF16)
    wl = (rw - wh.astype(F32)).astype(BF16)
    logits = _dot(xh, wh) + _dot(xh, wl) + _dot(xl, wh) + rb_ref[...]
    sel, route = _route(logits)
    return h, xn, sel, route


def _cumsum_rows(x):
    n = x.shape[0]
    row = lax.broadcasted_iota(jnp.int32, (n, 1), 0)
    shift = 1
    while shift < n:
        x = x + jnp.where(row >= shift, pltpu.roll(x, shift, axis=0), 0.0)
        shift *= 2
    return x


def _rec_chunk(q, kk, v, lf, st_ref, causal):
    c = q.shape[0]
    b = _cumsum_rows(lf)
    eb = jnp.exp(b)
    qe = (q * eb).astype(BF16)
    ke = (kk * jnp.exp(-b)).astype(BF16)
    vb = v.astype(BF16)
    eb_last = eb[c - 1:c]
    outs = []
    for h in range(N_HEADS):
        sl = slice(h * HEAD, (h + 1) * HEAD)
        st = st_ref[h]
        sc = jnp.where(causal, _dot_nt(qe[:, sl], ke[:, sl]), 0.0)
        outs.append(_dot(sc.astype(BF16), vb[:, sl]) + _dot_nt(qe[:, sl], st.astype(BF16)))
        st_ref[h] = (st + _dot_tn(vb[:, sl], ke[:, sl])) * eb_last[:, sl]
    return jnp.concatenate(outs, axis=-1)


def _rec_chunk_safe(q, kk, v, lf, st_ref):
    c = q.shape[0]
    row = lax.broadcasted_iota(jnp.int32, (c, 1), 0)
    b = _cumsum_rows(lf)
    eb = jnp.exp(b)
    qe = (q * eb).astype(BF16)
    kl = (kk * jnp.exp(b[c - 1:c] - b)).astype(BF16)
    vb = v.astype(BF16)
    outs = []
    for h in range(N_HEADS):
        sl = slice(h * HEAD, (h + 1) * HEAD)
        st = st_ref[h]
        bh, kh, qh, vh = b[:, sl], kk[:, sl], q[:, sl], v[:, sl]
        intra = jnp.zeros((c, HEAD), F32)
        for t in range(c):
            w = kh * jnp.exp(jnp.where(row <= t, bh[t:t + 1] - bh, -jnp.inf))
            score = jnp.sum(w * qh[t:t + 1], axis=-1, keepdims=True)
            intra = jnp.where(row == t, jnp.sum(score * vh, axis=0, keepdims=True), intra)
        outs.append(intra + _dot_nt(qe[:, sl], st.astype(BF16)))
        st_ref[h] = st * eb[c - 1:c, sl] + _dot_tn(vb[:, sl], kl[:, sl])
    return jnp.concatenate(outs, axis=-1)


def _mix_tile(u, layer, chunk, cw_ref, rlb_ref, convbuf, st, cv_out):
    tm = u.shape[0]
    d_conv = cw_ref.shape[-1]
    d_rec = rlb_ref.shape[-1]
    bg, cg, hv, q, fx, iv, og = _split_u(u, d_conv, d_rec)

    bx = bg * hv
    convbuf[8:8 + tm, :] = bx
    cw = cw_ref[...]
    conv = cw[0:1] * convbuf[6:6 + tm, :] + cw[1:2] * convbuf[7:7 + tm, :] + cw[2:3] * bx
    convbuf[6:8, :] = bx[tm - 2:tm]
    cv_out[...] = bx[tm - 2:tm]

    f, kk = _forget(fx, _lower_bound(rlb_ref, layer))
    lf = jnp.log(f)
    outs = []
    if chunk is None:
        for c in range(tm // SAFE_CHUNK):
            rs = slice(c * SAFE_CHUNK, (c + 1) * SAFE_CHUNK)
            outs.append(_rec_chunk_safe(q[rs], kk[rs], iv[rs], lf[rs], st))
    else:
        row = lax.broadcasted_iota(jnp.int32, (chunk, chunk), 0)
        col = lax.broadcasted_iota(jnp.int32, (chunk, chunk), 1)
        causal = row >= col
        for c in range(tm // chunk):
            rs = slice(c * chunk, (c + 1) * chunk)
            outs.append(_rec_chunk(q[rs], kk[rs], iv[rs], lf[rs], st, causal))
    return cg * conv, jnp.concatenate(outs, axis=0), og


def _load_state(st, convbuf, s0_ref, c0_ref):
    for h in range(N_HEADS):
        st[h] = s0_ref[h].T
    convbuf[6:8, :] = c0_ref[...]


def _meta_kernel(layer, chunk, x_ref, s0_ref, c0_ref, nm_ref, win_ref, cw_ref, rlb_ref, st_out, cv_out,
                 convbuf, st):
    _load_state(st, convbuf, s0_ref, c0_ref)
    _mix_tile(_project(x_ref[...], nm_ref, win_ref), layer, chunk, cw_ref, rlb_ref, convbuf, st, cv_out)
    for h in range(N_HEADS):
        st_out[h] = st[h].T


def _prompt_kernel(layer, tm, chunk, n_pairs, pairs_per_seq, cap,
                   xr0_ref, x1_ref, xp2_ref, s0_ref, c0_ref, nm_ref, win_ref, cw_ref, rlb_ref,
                   rg_ref, wout_ref, nf_ref, rw_ref, rb_ref, hs_ref, xns_ref, sels_ref, routes_ref,
                   h_ref, xn_ref, route_ref, rnk_ref, cnt_ref, st_out, cv_out,
                   convbuf, st, ua, ub, carry):
    step = pl.program_id(0)
    live = step < n_pairs

    @pl.when(step == 0)
    def _():
        ua[...] = _project(xr0_ref[...], nm_ref, win_ref)
        carry[...] = jnp.zeros_like(carry)

    @pl.when(jnp.logical_and(lax.rem(step, pairs_per_seq) == 0, live))
    def _():
        _load_state(st, convbuf, s0_ref, c0_ref)

    @pl.when(live)
    def _():
        def finish(x, u, rows):
            y_conv, o, og = _mix_tile(u, layer, chunk, cw_ref, rlb_ref, convbuf, st, cv_out)
            y = jnp.concatenate([y_conv, _rec_out(o, og, rg_ref[...])], axis=-1)
            h, xn, sel, route = _tail(x, y, wout_ref, nf_ref, rw_ref, rb_ref)
            h_ref[rows, :] = h
            xn_ref[rows, :] = _pack_halves(xn)
            route_ref[rows, :] = route
            rnk_ref[rows, :] = _rank_block(sel, route, carry, cap)

        ub[...] = _project(x1_ref[...], nm_ref, win_ref)
        finish(xr0_ref[...], ua[...], slice(0, tm))
        ua[...] = _project(xp2_ref[...], nm_ref, win_ref)
        finish(x1_ref[...], ub[...], slice(tm, 2 * tm))

    @pl.when(jnp.logical_and(lax.rem(step, pairs_per_seq) == pairs_per_seq - 1, live))
    def _():
        for h in range(N_HEADS):
            st_out[h] = st[h].T

    @pl.when(step == n_pairs)
    def _():
        ns = hs_ref.shape[0]
        n_exp = sels_ref.shape[-1]
        pad = 2 * tm - ns
        spare = (lax.broadcasted_iota(jnp.int32, (pad, LANES), 0) * TOP_K
                 + lax.broadcasted_iota(jnp.int32, (pad, LANES), 1) + n_exp * cap).astype(F32)
        rnk_ref[0:ns, :] = _rank_block(sels_ref[...], routes_ref[...], carry, cap)
        rnk_ref[ns:2 * tm, :] = spare
        for dst, val in ((h_ref, hs_ref[...]), (xn_ref, _pack_halves(xns_ref[...])),
                         (route_ref, routes_ref[...])):
            dst[0:ns, :] = val
            dst[ns:2 * tm, :] = jnp.zeros((pad, dst.shape[-1]), dst.dtype)
        cnt_ref[...] = carry[...]


def _const_spec(shape):
    return pl.BlockSpec(shape, lambda *_: (0,) * len(shape))


def _meta_call(x, weights, layer, chunk):
    assert x.shape[0] % (chunk or SAFE_CHUNK) == 0
    nm, win, cw, rlb = weights
    d_conv = cw.shape[-1]
    s0 = jnp.zeros((N_HEADS, HEAD, HEAD), F32)
    c0 = jnp.zeros((2, d_conv), F32)
    return pl.pallas_call(
        functools.partial(_meta_kernel, layer, chunk),
        out_shape=[jax.ShapeDtypeStruct(s0.shape, F32), jax.ShapeDtypeStruct(c0.shape, F32)],
        scratch_shapes=[pltpu.VMEM((x.shape[0] + 8, d_conv), F32), pltpu.VMEM(s0.shape, F32)],
        compiler_params=pltpu.CompilerParams(vmem_limit_bytes=VMEM_LIMIT),
        name="mixer_meta",
    )(x, s0, c0, nm, win, cw, rlb)


def _prompt_call(x, s0, c0, weights, tail_w, decode, layer, tm, chunk):
    nseq, length, d = x.shape
    nt = length // tm
    assert nt % 2 == 0 and decode[0].shape[0] <= 2 * tm and tm % (chunk or SAFE_CHUNK) == 0
    pairs_per_seq = nt // 2
    n_pairs = nseq * pairs_per_seq
    nm, win, cw, rlb = weights
    d_conv = cw.shape[-1]
    d_in = win.shape[-1]

    def tile_spec(offset):
        def index(s):
            tile = jnp.minimum(2 * s + offset, 2 * n_pairs - 1)
            return (tile // nt, lax.rem(tile, nt), 0)
        return pl.BlockSpec((None, tm, d), index)

    def seq_of(s):
        return jnp.minimum(s // pairs_per_seq, nseq - 1)

    consts = [s0, c0, nm, win, cw, rlb] + list(tail_w) + list(decode)
    n_exp = decode[2].shape[-1]
    tok = [(d, F32), (d // 2, jnp.uint32), (LANES, F32), (LANES, F32)]
    n_steps = n_pairs + 1
    return pl.pallas_call(
        functools.partial(_prompt_kernel, layer, tm, chunk, n_pairs, pairs_per_seq, n_steps * 2 * tm),
        grid=(n_steps,),
        in_specs=[tile_spec(0), tile_spec(1), tile_spec(2)] + [_const_spec(a.shape) for a in consts],
        out_specs=[pl.BlockSpec((2 * tm, w), lambda s: (s, 0)) for w, _ in tok] + [
            pl.BlockSpec((1, n_exp), lambda s: (0, 0)),
            pl.BlockSpec((None,) + s0.shape, lambda s: (seq_of(s), 0, 0, 0)),
            pl.BlockSpec((None,) + c0.shape, lambda s: (seq_of(s), 0, 0))],
        out_shape=[jax.ShapeDtypeStruct((n_steps * 2 * tm, w), t) for w, t in tok] + [
            jax.ShapeDtypeStruct((1, n_exp), F32),
            jax.ShapeDtypeStruct((nseq,) + s0.shape, F32), jax.ShapeDtypeStruct((nseq,) + c0.shape, F32)],
        scratch_shapes=[pltpu.VMEM((tm + 8, d_conv), F32), pltpu.VMEM(s0.shape, F32),
                        pltpu.VMEM((tm, d_in), F32), pltpu.VMEM((tm, d_in), F32),
                        pltpu.VMEM((1, n_exp), F32)],
        compiler_params=pltpu.CompilerParams(dimension_semantics=("arbitrary",),
                                             vmem_limit_bytes=VMEM_LIMIT),
        name="mixer_prompt",
    )(x, x, x, *consts)


def _sample_in_kernel(layer, x_ref, sc_ref, nm_ref, win_ref, cw_ref, rlb_ref,
                      yc_ref, nc_ref, f_ref, k_ref, q_ref, v_ref, og_ref):
    d_conv = cw_ref.shape[-1]
    d_rec = rlb_ref.shape[-1]
    u = _project(x_ref[...], nm_ref, win_ref)
    bg, cg, hv, q, fx, iv, og = _split_u(u, d_conv, d_rec)
    bx = bg * hv
    sc = sc_ref[...]
    s0, s1 = sc[:, :d_conv], sc[:, d_conv:]
    cw = cw_ref[...]
    yc_ref[...] = cg * (cw[0:1] * s0 + cw[1:2] * s1 + cw[2:3] * bx)
    nc_ref[...] = jnp.concatenate([s1, bx], axis=-1)
    f, kk = _forget(fx, _lower_bound(rlb_ref, layer))
    f_ref[...] = f
    k_ref[...] = kk
    q_ref[...] = q
    v_ref[...] = iv
    og_ref[...] = og


def _sample_state_kernel(group, f_ref, k_ref, q_ref, v_ref, s_ref, sn_ref, o_ref):
    for j in range(group):
        for h in range(N_HEADS):
            rs = slice(h * HEAD, (h + 1) * HEAD)
            fcol = f_ref[rs, j:j + 1]
            kcol = k_ref[rs, j:j + 1]
            qcol = q_ref[rs, j:j + 1]
            vrow = v_ref[j:j + 1, rs]
            sn = fcol * s_ref[j, h] + kcol * vrow
            sn_ref[j, h] = sn
            o_ref[j:j + 1, rs] = jnp.sum(qcol * sn, axis=0, keepdims=True)


def _sample_tail_kernel(x_ref, yc_ref, o_ref, og_ref, rg_ref, wout_ref, nf_ref, rw_ref, rb_ref,
                        h_ref, xn_ref, sel_ref, route_ref):
    y = jnp.concatenate([yc_ref[...], _rec_out(o_ref[...], og_ref[...], rg_ref[...])], axis=-1)
    h, xn, sel, route = _tail(x_ref[...], y, wout_ref, nf_ref, rw_ref, rb_ref)
    h_ref[...] = h
    xn_ref[...] = xn
    sel_ref[...] = sel
    route_ref[...] = route


def _rank_block(sel, route, carry, cap):
    tb, ne = sel.shape
    row = lax.broadcasted_iota(jnp.int32, (tb, tb), 0)
    col = lax.broadcasted_iota(jnp.int32, (tb, tb), 1)
    before = jnp.where(col < row, 1.0, 0.0).astype(BF16)
    rank = _dot(before, sel.astype(BF16)) + carry[...]
    carry[...] = carry[...] + jnp.sum(sel, axis=0, keepdims=True)
    lane_e = lax.broadcasted_iota(jnp.int32, (tb, ne), 1).astype(F32)
    lane = lax.broadcasted_iota(jnp.int32, (tb, LANES), 1)
    rnk = jnp.zeros((tb, LANES), F32)
    for k in range(TOP_K):
        expert = route[:, k:k + 1]
        mine = jnp.where(lane_e == expert, rank, 0.0)
        rnk = jnp.where(lane == k, expert * float(cap) + jnp.sum(mine, axis=-1, keepdims=True), rnk)
    return rnk


def _dispatch(xn, pos_w, n_slots):
    n_workers, n_chunks, top_k, ch = pos_w.shape
    assert n_workers == SC_CORES * SC_SUBCORES and ch % 8 == 0 and ch <= LANES
    d = xn.shape[1]
    mesh = plsc.VectorSubcoreMesh(core_axis_name="c", subcore_axis_name="s")

    @functools.partial(
        pl.kernel, mesh=mesh,
        out_type=jax.ShapeDtypeStruct((n_slots, d), xn.dtype),
        scratch_types=[pltpu.VMEM((n_chunks, top_k, ch), jnp.int32), pltpu.VMEM((SC_RING, ch, d), xn.dtype),
                       pltpu.SemaphoreType.DMA((SC_RING,)), pltpu.SemaphoreType.DMA((SC_RING,))],
        name="dispatch",
    )
    def run(xn_hbm, pos_hbm, xs_hbm, idx_v, rows_v, sem_r, sem_w):
        wid = lax.axis_index("s") * SC_CORES + lax.axis_index("c")

        def read(c, b):
            src = xn_hbm.at[pl.ds((wid * n_chunks + c) * ch, ch)]
            return pltpu.make_async_copy(src, rows_v.at[b], sem_r.at[b])

        def write(c, b, k):
            return pltpu.make_async_copy(rows_v.at[b], xs_hbm.at[idx_v.at[c, k]], sem_w.at[b])

        pltpu.sync_copy(pos_hbm.at[wid], idx_v)
        ahead = SC_RING - 1
        for c0 in range(min(ahead, n_chunks)):
            read(c0, c0).start()

        @pl.loop(0, n_chunks)
        def _(c):
            b = lax.rem(c, SC_RING)
            read(c, b).wait()
            for k in range(top_k):
                write(c, b, k).start()

            @pl.when(c >= 1)
            def _():
                for k in range(top_k):
                    write(c - 1, lax.rem(c - 1, SC_RING), k).wait()

            @pl.when(c + ahead < n_chunks)
            def _():
                read(c + ahead, lax.rem(c + ahead, SC_RING)).start()

        for k in range(top_k):
            write(n_chunks - 1, (n_chunks - 1) % SC_RING, k).wait()

    return run(xn, pos_w)


def _collect(ys, pos_w):
    n_workers, n_chunks, top_k, ch = pos_w.shape
    assert n_workers == SC_CORES * SC_SUBCORES and ch % 8 == 0 and ch <= LANES
    d = ys.shape[1]
    mesh = plsc.VectorSubcoreMesh(core_axis_name="c", subcore_axis_name="s")

    @functools.partial(
        pl.kernel, mesh=mesh,
        out_type=jax.ShapeDtypeStruct((n_workers * n_chunks * ch, top_k * d), ys.dtype),
        scratch_types=[pltpu.VMEM((n_chunks, top_k, ch), jnp.int32), pltpu.VMEM((top_k, ch, d), ys.dtype),
                       pltpu.SemaphoreType.DMA((top_k,)), pltpu.SemaphoreType.DMA((top_k,))],
        name="collect",
    )
    def run(ys_hbm, pos_hbm, out_hbm, idx_v, rows_v, sem_r, sem_w):
        wid = lax.axis_index("s") * SC_CORES + lax.axis_index("c")

        def read(c, k):
            return pltpu.make_async_copy(ys_hbm.at[idx_v.at[c, k]], rows_v.at[k], sem_r.at[k])

        def write(c, k):
            dst = out_hbm.at[pl.ds((wid * n_chunks + c) * ch, ch), pl.ds(k * d, d)]
            return pltpu.make_async_copy(rows_v.at[k], dst, sem_w.at[k])

        pltpu.sync_copy(pos_hbm.at[wid], idx_v)
        for k in range(top_k - 1):
            read(0, k).start()

        @pl.loop(0, n_chunks)
        def _(c):
            for k in range(top_k):
                read(c, k).wait()
                write(c, k).start()
                if k >= 1:
                    write(c, k - 1).wait()

                    @pl.when(c + 1 < n_chunks)
                    def _():
                        read(c + 1, k - 1).start()
                else:
                    @pl.when(c >= 1)
                    def _():
                        write(c - 1, top_k - 1).wait()
                    read(c, top_k - 1).start()

        write(n_chunks - 1, top_k - 1).wait()

    return run(ys, pos_w)


def _moe_kernel(tm, te_ref, nu_ref, tn_ref, tb_ref, x_ref, w1_hbm, bg_ref, bl_ref, w2_hbm, b2_ref,
                ys_ref, w1f, w2f, sem, w1p, w2b, act):
    i = pl.program_id(0)
    n_used = nu_ref[0]
    d_ff2 = w1f.shape[-1]
    n_blk = d_ff2 // MXU_N
    expert = te_ref[i]

    def weight_copies(e):
        return (pltpu.make_async_copy(w1_hbm.at[e], w1f, sem.at[0]),
                pltpu.make_async_copy(w2_hbm.at[e], w2f, sem.at[1]))

    @pl.when(i == 0)
    def _():
        for cp in weight_copies(expert):
            cp.start()

    @pl.when(i >= n_used)
    def _():
        ys_ref[...] = jnp.zeros_like(ys_ref)

    prev = te_ref[jnp.maximum(i - 1, 0)]
    changed = jnp.logical_or(i == 0, expert != prev)

    @pl.when(jnp.logical_and(changed, i < n_used))
    def _():
        for cp in weight_copies(expert):
            cp.wait()
        r = lax.broadcasted_iota(jnp.int32, (MXU_N, MXU_N), 0)
        c = lax.broadcasted_iota(jnp.int32, (MXU_N, MXU_N), 1)
        src = jnp.where(c < MXU_N // 2, 2 * c, 2 * (c - MXU_N // 2) + 1)
        perm = jnp.where(r == src, 1.0, 0.0).astype(BF16)
        for blk in range(n_blk):
            cs = slice(blk * MXU_N, (blk + 1) * MXU_N)
            w1p[:, cs] = _dot(w1f[:, cs].astype(BF16), perm).astype(BF16)
        w2b[...] = w2f[...].astype(BF16)

        @pl.when(tn_ref[i] != expert)
        def _():
            for cp in weight_copies(tn_ref[i]):
                cp.start(priority=1)

    @pl.when(i < n_used)
    def _():
        x = _unpack_halves(x_ref[...]).astype(BF16)
        bg, bl = bg_ref[expert], bl_ref[expert]
        for blk in range(n_blk):
            a = _dot(x, w1p[:, blk * MXU_N:(blk + 1) * MXU_N])
            half = MXU_N // 2
            hs = slice(blk * half, (blk + 1) * half)
            glu = jnp.minimum(a[:, :half] + bg[:, hs], SWIGLU_LIMIT)
            lin = jnp.clip(a[:, half:] + bl[:, hs], -SWIGLU_LIMIT, SWIGLU_LIMIT)
            act[:, hs] = (glu * _sigmoid(SWIGLU_ALPHA * glu) * (lin + 1.0)).astype(BF16)
        ys_ref[...] = _pack_halves(_dot(act[...], w2b[...]) + b2_ref[expert])


def _final_kernel(h_ref, gate_ref, nfin_ref, z_ref, y_ref):
    w = z_ref.shape[-1] // TOP_K
    g = gate_ref[...]
    out = h_ref[...]
    for k in range(TOP_K):
        out = out + g[:, TOP_K + k:TOP_K + k + 1] * _unpack_halves(z_ref[:, k * w:(k + 1) * w])
    y_ref[...] = _rms(out, nfin_ref[...])


def _final_call(h_all, gate, nfin, z, first_row, n_rows, tile):
    d = h_all.shape[-1]
    off = first_row // tile
    assert first_row % tile == 0 and n_rows % tile == 0
    return pl.pallas_call(
        _final_kernel,
        grid=(n_rows // tile,),
        in_specs=[
            pl.BlockSpec((tile, d), lambda i: (i + off, 0)),
            pl.BlockSpec((tile, gate.shape[-1]), lambda i: (i + off, 0)),
            pl.BlockSpec((1, d), lambda i: (0, 0)),
            pl.BlockSpec((tile, z.shape[-1]), lambda i: (i + off, 0)),
        ],
        out_specs=pl.BlockSpec((tile, d), lambda i: (i, 0)),
        out_shape=jax.ShapeDtypeStruct((n_rows, d), F32),
        compiler_params=pltpu.CompilerParams(dimension_semantics=("arbitrary",),
                                             vmem_limit_bytes=VMEM_LIMIT),
        name="final",
    )(h_all, gate, nfin, z)


def kernel(x_prompt, x_sample, state_conv, state_rec, meta_tokens, norm_mix, w_in, conv_w,
           rec_lower_bound, rec_norm, w_out, norm_ffn, router_w, router_b, expert_w1, expert_b1,
           expert_w2, expert_b2, norm_final):
    depth = norm_mix.shape[0]
    assert depth == 1, "single-layer step"
    layer = 0
    bp, seq, d = x_prompt.shape
    ns = x_sample.shape[0]
    assert x_sample.shape[1] == 1 and ns == TOKEN_TILE
    d_conv = conv_w.shape[-1]
    d_rec = rec_lower_bound.shape[-1]
    assert state_conv.shape[2] == 2 and d_rec == N_HEADS * HEAD
    n_exp = router_w.shape[-1]
    d_ff = expert_w2.shape[2]
    n_prompt = bp * seq
    n_tok = n_prompt + ns
    assert seq % PROMPT_TILE == 0 and n_prompt % TOKEN_TILE == 0

    nm = norm_mix[layer][None]
    win = w_in[layer].astype(BF16)
    cw = conv_w[layer]
    rlb = rec_lower_bound
    rg = rec_norm[layer][None]
    wout = w_out[layer].astype(BF16)
    nf = norm_ffn[layer][None]
    rw = router_w[layer]
    rb = router_b[layer][None]
    mix_w = (nm, win, cw, rlb)
    tail_w = (rg, wout, nf, rw, rb)

    lb = jnp.sum(jax.nn.softmax(rlb, axis=0)[:layer + 1], axis=0)
    fast = CHUNK * jnp.max(-jnp.log(lb)) < DECAY_LIMIT

    st_meta, cv_meta = lax.cond(
        fast, lambda: _meta_call(meta_tokens, mix_w, layer, N_META),
        lambda: _meta_call(meta_tokens, mix_w, layer, None))

    xs = x_sample.reshape(ns, d)
    wide = jax.ShapeDtypeStruct((ns, d_rec), F32)
    y_conv_s, new_conv_s, f_s, k_s, q_s, v_s, og_s = pl.pallas_call(
        functools.partial(_sample_in_kernel, layer),
        out_shape=[jax.ShapeDtypeStruct((ns, d_conv), F32), jax.ShapeDtypeStruct((ns, 2 * d_conv), F32),
                   wide, wide, wide, wide, wide],
        compiler_params=pltpu.CompilerParams(vmem_limit_bytes=VMEM_LIMIT),
        name="sample_in",
    )(xs, state_conv[layer].reshape(ns, 2 * d_conv), nm, win, cw, rlb)

    group = 8
    n_grp = ns // group

    def cols(a):
        return a.T.reshape(d_rec, n_grp, group).transpose(1, 0, 2)

    col_spec = pl.BlockSpec((None, d_rec, group), lambda g: (g, 0, 0))
    st_spec = pl.BlockSpec((group, N_HEADS, HEAD, HEAD), lambda g: (g, 0, 0, 0))
    row_spec = pl.BlockSpec((group, d_rec), lambda g: (g, 0))
    new_rec_s, o_s = pl.pallas_call(
        functools.partial(_sample_state_kernel, group),
        grid=(n_grp,),
        in_specs=[col_spec, col_spec, col_spec, row_spec, st_spec],
        out_specs=[st_spec, row_spec],
        out_shape=[jax.ShapeDtypeStruct(state_rec.shape[1:], F32), wide],
        compiler_params=pltpu.CompilerParams(dimension_semantics=("arbitrary",),
                                             vmem_limit_bytes=VMEM_LIMIT),
        name="sample_state",
    )(cols(f_s), cols(k_s), cols(q_s), v_s, state_rec[layer])

    decode = pl.pallas_call(
        _sample_tail_kernel,
        out_shape=[jax.ShapeDtypeStruct((ns, d), F32), jax.ShapeDtypeStruct((ns, d), F32),
                   jax.ShapeDtypeStruct((ns, n_exp), F32), jax.ShapeDtypeStruct((ns, LANES), F32)],
        compiler_params=pltpu.CompilerParams(vmem_limit_bytes=VMEM_LIMIT),
        name="sample_tail",
    )(xs, y_conv_s, o_s, og_s, rg, wout, nf, rw, rb)

    h_all, xn_all, route, rnk, counts, new_rec_p, new_conv_p = lax.cond(
        fast,
        lambda: _prompt_call(x_prompt, st_meta, cv_meta, mix_w, tail_w, decode, layer, PROMPT_TILE, CHUNK),
        lambda: _prompt_call(x_prompt, st_meta, cv_meta, mix_w, tail_w, decode, layer, PROMPT_TILE, None))

    tm = MOE_TILE
    cap = h_all.shape[0]
    assert cap % tm == 0
    n_tiles = (n_tok * TOP_K) // tm + n_exp
    counts = counts[0].astype(jnp.int32)
    tiles_e = (counts + tm - 1) // tm
    tile_end = jnp.cumsum(tiles_e)
    n_used = tile_end[-1]
    tile_ids = jnp.arange(n_tiles, dtype=jnp.int32)

    def expert_of(tile):
        return jnp.minimum(jnp.sum((tile_end[None, :] <= tile[:, None]).astype(jnp.int32), axis=1), n_exp - 1)

    tile_expert = expert_of(jnp.minimum(tile_ids, n_used - 1))
    mine = tile_expert[:, None] == jnp.arange(n_exp, dtype=jnp.int32)[None, :]
    after = jnp.sum(jnp.where(mine, tile_end[None, :], 0), axis=1)
    first = jnp.sum(jnp.where(mine, (tile_end - tiles_e)[None, :], 0), axis=1)
    next_expert = jnp.where(after < n_used, expert_of(after), tile_expert)
    n_rows = cap
    n_spare = (n_rows - n_tok) * TOP_K
    spare_blocks = -(-n_spare // tm)
    dummy_block = n_exp * (cap // tm) + spare_blocks
    n_slots = (dummy_block + 1) * tm
    tile_block = jnp.where(tile_ids < n_used, tile_expert * (cap // tm) + tile_ids - first, dummy_block)

    n_workers = SC_CORES * SC_SUBCORES
    assert n_rows % (n_workers * DISPATCH_CHUNK) == 0
    pos_rows = rnk[:, :TOP_K].astype(jnp.int32)

    def to_workers(p, ch):
        assert p.shape[0] % (n_workers * ch) == 0
        return p.reshape(n_workers, -1, ch, TOP_K).transpose(0, 1, 3, 2)

    xs = _dispatch(xn_all, to_workers(pos_rows, DISPATCH_CHUNK), n_slots)

    w1 = expert_w1[layer]
    w2 = expert_w2[layer]
    b1 = expert_b1[layer]
    b1g = b1[:, 0::2][:, None, :]
    b1l = b1[:, 1::2][:, None, :]
    b2 = expert_b2[layer][:, None, :]
    ys = pl.pallas_call(
        functools.partial(_moe_kernel, tm),
        grid_spec=pltpu.PrefetchScalarGridSpec(
            num_scalar_prefetch=4,
            grid=(n_tiles,),
            in_specs=[
                pl.BlockSpec((tm, d // 2), lambda i, te, nu, tn, tb: (tb[i], 0)),
                pl.BlockSpec(memory_space=pl.ANY),
                _const_spec(b1g.shape), _const_spec(b1l.shape),
                pl.BlockSpec(memory_space=pl.ANY),
                _const_spec(b2.shape),
            ],
            out_specs=pl.BlockSpec((tm, d // 2), lambda i, te, nu, tn, tb: (tb[i], 0)),
            scratch_shapes=[pltpu.VMEM((d, 2 * d_ff), F32), pltpu.VMEM((d_ff, d), F32),
                            pltpu.SemaphoreType.DMA((2,)),
                            pltpu.VMEM((d, 2 * d_ff), BF16), pltpu.VMEM((d_ff, d), BF16),
                            pltpu.VMEM((tm, d_ff), BF16)],
        ),
        out_shape=jax.ShapeDtypeStruct((n_slots, d // 2), jnp.uint32),
        compiler_params=pltpu.CompilerParams(dimension_semantics=("arbitrary",),
                                             vmem_limit_bytes=VMEM_LIMIT),
        name="moe",
    )(tile_expert, n_used[None].astype(jnp.int32), next_expert, tile_block, xs, w1, b1g, b1l, w2, b2)

    z = _collect(ys, to_workers(pos_rows, DISPATCH_CHUNK))
    y_p = _final_call(h_all, route, norm_final[None], z, 0, n_prompt, FINAL_TILE)
    y_s = _final_call(h_all, route, norm_final[None], z, n_prompt, ns, TOKEN_TILE)

    return (y_p.reshape(bp, seq, d), y_s.reshape(ns, 1, d),
            new_conv_p[None], new_rec_p[None],
            new_conv_s.reshape(1, ns, 2, d_conv), new_rec_s[None])
```

```python
import functools

import jax
import jax.numpy as jnp
from jax import lax
from jax.experimental import pallas as pl
from jax.experimental.pallas import tpu as pltpu
from jax.experimental.pallas import tpu_sc as plsc

F32 = jnp.float32
BF16 = jnp.bfloat16

N_HEADS = 4
HEAD = 128
N_META = 16
CHUNK = 64
SAFE_CHUNK = 16
DECAY_LIMIT = 80.0
TOP_K = 4
SWIGLU_LIMIT = 7.0
SWIGLU_ALPHA = 1.702
EPS = 1e-5

LANES = 128
MXU_N = 256
PROMPT_TILE = 256
TOKEN_TILE = 128
FINAL_TILE = 512
MOE_TILE = 512
SC_CORES = 2
SC_SUBCORES = 16
DISPATCH_CHUNK = 48
SC_RING = 4
VMEM_LIMIT = 56 * 1024 * 1024


def _dot(a, b):
    return jnp.dot(a, b, preferred_element_type=F32)


def _dot_nt(a, b):
    return lax.dot_general(a, b, (((1,), (1,)), ((), ())), preferred_element_type=F32)


def _dot_tn(a, b):
    return lax.dot_general(a, b, (((0,), (0,)), ((), ())), preferred_element_type=F32)


def _pack_halves(x):
    n = x.shape[-1] // 2
    lo = pltpu.bitcast(x[:, :n].astype(BF16).astype(F32), jnp.uint32)
    hi = pltpu.bitcast(x[:, n:].astype(BF16).astype(F32), jnp.uint32)
    return (lo >> 16) | (hi & jnp.uint32(0xFFFF0000))


def _unpack_halves(u):
    lo = pltpu.bitcast(u << 16, F32)
    hi = pltpu.bitcast(u & jnp.uint32(0xFFFF0000), F32)
    return jnp.concatenate([lo, hi], axis=-1)


def _sigmoid(x):
    return 1.0 / (1.0 + jnp.exp(-x))


def _rms(x, g):
    ms = jnp.mean(x * x, axis=-1, keepdims=True)
    return x * lax.rsqrt(ms + EPS) * g


def _project(x, nm_ref, win_ref):
    return _dot(_rms(x, nm_ref[...]).astype(BF16), win_ref[...])


def _split_u(u, d_conv, d_rec):
    pts = [0, d_conv, 2 * d_conv, 3 * d_conv, 3 * d_conv + d_rec, 3 * d_conv + 2 * d_rec,
           3 * d_conv + 3 * d_rec, 3 * d_conv + 4 * d_rec]
    return [u[:, pts[i]:pts[i + 1]] for i in range(7)]


def _lower_bound(rlb_ref, layer):
    r = rlb_ref[...]
    e = jnp.exp(r - jnp.max(r, axis=0, keepdims=True))
    return jnp.sum(e[0:layer + 1], axis=0, keepdims=True) / jnp.sum(e, axis=0, keepdims=True)


def _forget(fx, lb):
    f = lb + (1.0 - lb) * _sigmoid(fx)
    return f, 1.0 - f


def _rec_out(o, og, rg):
    parts = []
    for h in range(N_HEADS):
        oh = o[:, h * HEAD:(h + 1) * HEAD]
        parts.append(oh * lax.rsqrt(jnp.mean(oh * oh, axis=-1, keepdims=True) + EPS))
    return jnp.concatenate(parts, axis=-1) * rg * (og * _sigmoid(og))


def _route(logits):
    m_rows, n = logits.shape
    lane = lax.broadcasted_iota(jnp.int32, logits.shape, 1).astype(F32)
    work = logits
    tops, firsts = [], []
    sel = jnp.zeros_like(logits)
    for _ in range(TOP_K):
        m = jnp.max(work, axis=-1, keepdims=True)
        first = jnp.min(jnp.where(work == m, lane, float(n)), axis=-1, keepdims=True)
        hot = lane == first
        tops.append(m)
        firsts.append(first)
        sel = sel + jnp.where(hot, 1.0, 0.0)
        work = jnp.where(hot, -jnp.inf, work)
    es = [jnp.exp(t - tops[0]) for t in tops]
    den = es[0]
    for e in es[1:]:
        den = den + e
    wide = lax.broadcasted_iota(jnp.int32, (m_rows, LANES), 1)
    route = jnp.zeros((m_rows, LANES), F32)
    for k in range(TOP_K):
        route = jnp.where(wide == k, firsts[k], route)
        route = jnp.where(wide == TOP_K + k, es[k] / den, route)
    return sel, route


def _tail(x, y, wout_ref, nf_ref, rw_ref, rb_ref):
    h = x + _dot(y.astype(BF16), wout_ref[...])
    xn = _rms(h, nf_ref[...])
    xh = xn.astype(BF16)
    xl = (xn - xh.astype(F32)).astype(BF16)
    rw = rw_ref[...]
    wh = rw.astype(BF16)
    wl = (rw - wh.astype(F32)).astype(BF16)
    logits = _dot(xh, wh) + _dot(xh, wl) + _dot(xl, wh) + rb_ref[...]
    sel, route = _route(logits)
    return h, xn, sel, route


def _cumsum_rows(x):
    n = x.shape[0]
    row = lax.broadcasted_iota(jnp.int32, (n, 1), 0)
    shift = 1
    while shift < n:
        x = x + jnp.where(row >= shift, pltpu.roll(x, shift, axis=0), 0.0)
        shift *= 2
    return x


def _rec_chunk(q, kk, v, lf, st_ref, causal):
    c = q.shape[0]
    b = _cumsum_rows(lf)
    eb = jnp.exp(b)
    qe = (q * eb).astype(BF16)
    ke = (kk * jnp.exp(-b)).astype(BF16)
    vb = v.astype(BF16)
    eb_last = eb[c - 1:c]
    outs = []
    for h in range(N_HEADS):
        sl = slice(h * HEAD, (h + 1) * HEAD)
        st = st_ref[h]
        sc = jnp.where(causal, _dot_nt(qe[:, sl], ke[:, sl]), 0.0)
        outs.append(_dot(sc.astype(BF16), vb[:, sl]) + _dot_nt(qe[:, sl], st.astype(BF16)))
        st_ref[h] = (st + _dot_tn(vb[:, sl], ke[:, sl])) * eb_last[:, sl]
    return jnp.concatenate(outs, axis=-1)


def _rec_chunk_safe(q, kk, v, lf, st_ref):
    c = q.shape[0]
    row = lax.broadcasted_iota(jnp.int32, (c, 1), 0)
    b = _cumsum_rows(lf)
    eb = jnp.exp(b)
    qe = (q * eb).astype(BF16)
    kl = (kk * jnp.exp(b[c - 1:c] - b)).astype(BF16)
    vb = v.astype(BF16)
    outs = []
    for h in range(N_HEADS):
        sl = slice(h * HEAD, (h + 1) * HEAD)
        st = st_ref[h]
        bh, kh, qh, vh = b[:, sl], kk[:, sl], q[:, sl], v[:, sl]
        intra = jnp.zeros((c, HEAD), F32)
        for t in range(c):
            w = kh * jnp.exp(jnp.where(row <= t, bh[t:t + 1] - bh, -jnp.inf))
            score = jnp.sum(w * qh[t:t + 1], axis=-1, keepdims=True)
            intra = jnp.where(row == t, jnp.sum(score * vh, axis=0, keepdims=True), intra)
        outs.append(intra + _dot_nt(qe[:, sl], st.astype(BF16)))
        st_ref[h] = st * eb[c - 1:c, sl] + _dot_tn(vb[:, sl], kl[:, sl])
    return jnp.concatenate(outs, axis=-1)


def _mix_tile(u, layer, chunk, cw_ref, rlb_ref, convbuf, st, cv_out):
    tm = u.shape[0]
    d_conv = cw_ref.shape[-1]
    d_rec = rlb_ref.shape[-1]
    bg, cg, hv, q, fx, iv, og = _split_u(u, d_conv, d_rec)

    bx = bg * hv
    convbuf[8:8 + tm, :] = bx
    cw = cw_ref[...]
    conv = cw[0:1] * convbuf[6:6 + tm, :] + cw[1:2] * convbuf[7:7 + tm, :] + cw[2:3] * bx
    convbuf[6:8, :] = bx[tm - 2:tm]
    cv_out[...] = bx[tm - 2:tm]

    f, kk = _forget(fx, _lower_bound(rlb_ref, layer))
    lf = jnp.log(f)
    outs = []
    if chunk is None:
        for c in range(tm // SAFE_CHUNK):
            rs = slice(c * SAFE_CHUNK, (c + 1) * SAFE_CHUNK)
            outs.append(_rec_chunk_safe(q[rs], kk[rs], iv[rs], lf[rs], st))
    else:
        row = lax.broadcasted_iota(jnp.int32, (chunk, chunk), 0)
        col = lax.broadcasted_iota(jnp.int32, (chunk, chunk), 1)
        causal = row >= col
        for c in range(tm // chunk):
            rs = slice(c * chunk, (c + 1) * chunk)
            outs.append(_rec_chunk(q[rs], kk[rs], iv[rs], lf[rs], st, causal))
    return cg * conv, jnp.concatenate(outs, axis=0), og


def _load_state(st, convbuf, s0_ref, c0_ref):
    for h in range(N_HEADS):
        st[h] = s0_ref[h].T
    convbuf[6:8, :] = c0_ref[...]


def _meta_kernel(layer, chunk, x_ref, s0_ref, c0_ref, nm_ref, win_ref, cw_ref, rlb_ref, st_out, cv_out,
                 convbuf, st):
    _load_state(st, convbuf, s0_ref, c0_ref)
    _mix_tile(_project(x_ref[...], nm_ref, win_ref), layer, chunk, cw_ref, rlb_ref, convbuf, st, cv_out)
    for h in range(N_HEADS):
        st_out[h] = st[h].T


def _prompt_kernel(layer, tm, chunk, n_pairs, pairs_per_seq, cap,
                   xr0_ref, x1_ref, xp2_ref, s0_ref, c0_ref, nm_ref, win_ref, cw_ref, rlb_ref,
                   rg_ref, wout_ref, nf_ref, rw_ref, rb_ref, hs_ref, xns_ref, sels_ref, routes_ref,
                   h_ref, xn_ref, route_ref, rnk_ref, cnt_ref, st_out, cv_out,
                   convbuf, st, ua, ub, carry):
    step = pl.program_id(0)
    live = step < n_pairs

    @pl.when(step == 0)
    def _():
        ua[...] = _project(xr0_ref[...], nm_ref, win_ref)
        carry[...] = jnp.zeros_like(carry)

    @pl.when(jnp.logical_and(lax.rem(step, pairs_per_seq) == 0, live))
    def _():
        _load_state(st, convbuf, s0_ref, c0_ref)

    @pl.when(live)
    def _():
        def finish(x, u, rows):
            y_conv, o, og = _mix_tile(u, layer, chunk, cw_ref, rlb_ref, convbuf, st, cv_out)
            y = jnp.concatenate([y_conv, _rec_out(o, og, rg_ref[...])], axis=-1)
            h, xn, sel, route = _tail(x, y, wout_ref, nf_ref, rw_ref, rb_ref)
            h_ref[rows, :] = h
            xn_ref[rows, :] = _pack_halves(xn)
            route_ref[rows, :] = route
            rnk_ref[rows, :] = _rank_block(sel, route, carry, cap)

        ub[...] = _project(x1_ref[...], nm_ref, win_ref)
        finish(xr0_ref[...], ua[...], slice(0, tm))
        ua[...] = _project(xp2_ref[...], nm_ref, win_ref)
        finish(x1_ref[...], ub[...], slice(tm, 2 * tm))

    @pl.when(jnp.logical_and(lax.rem(step, pairs_per_seq) == pairs_per_seq - 1, live))
    def _():
        for h in range(N_HEADS):
            st_out[h] = st[h].T

    @pl.when(step == n_pairs)
    def _():
        ns = hs_ref.shape[0]
        n_exp = sels_ref.shape[-1]
        pad = 2 * tm - ns
        spare = (lax.broadcasted_iota(jnp.int32, (pad, LANES), 0) * TOP_K
                 + lax.broadcasted_iota(jnp.int32, (pad, LANES), 1) + n_exp * cap).astype(F32)
        rnk_ref[0:ns, :] = _rank_block(sels_ref[...], routes_ref[...], carry, cap)
        rnk_ref[ns:2 * tm, :] = spare
        for dst, val in ((h_ref, hs_ref[...]), (xn_ref, _pack_halves(xns_ref[...])),
                         (route_ref, routes_ref[...])):
            dst[0:ns, :] = val
            dst[ns:2 * tm, :] = jnp.zeros((pad, dst.shape[-1]), dst.dtype)
        cnt_ref[...] = carry[...]


def _const_spec(shape):
    return pl.BlockSpec(shape, lambda *_: (0,) * len(shape))


def _meta_call(x, weights, layer, chunk):
    assert x.shape[0] % (chunk or SAFE_CHUNK) == 0
    nm, win, cw, rlb = weights
    d_conv = cw.shape[-1]
    s0 = jnp.zeros((N_HEADS, HEAD, HEAD), F32)
    c0 = jnp.zeros((2, d_conv), F32)
    return pl.pallas_call(
        functools.partial(_meta_kernel, layer, chunk),
        out_shape=[jax.ShapeDtypeStruct(s0.shape, F32), jax.ShapeDtypeStruct(c0.shape, F32)],
        scratch_shapes=[pltpu.VMEM((x.shape[0] + 8, d_conv), F32), pltpu.VMEM(s0.shape, F32)],
        compiler_params=pltpu.CompilerParams(vmem_limit_bytes=VMEM_LIMIT),
        name="mixer_meta",
    )(x, s0, c0, nm, win, cw, rlb)


def _prompt_call(x, s0, c0, weights, tail_w, decode, layer, tm, chunk):
    nseq, length, d = x.shape
    nt = length // tm
    assert nt % 2 == 0 and decode[0].shape[0] <= 2 * tm and tm % (chunk or SAFE_CHUNK) == 0
    pairs_per_seq = nt // 2
    n_pairs = nseq * pairs_per_seq
    nm, win, cw, rlb = weights
    d_conv = cw.shape[-1]
    d_in = win.shape[-1]

    def tile_spec(offset):
        def index(s):
            tile = jnp.minimum(2 * s + offset, 2 * n_pairs - 1)
            return (tile // nt, lax.rem(tile, nt), 0)
        return pl.BlockSpec((None, tm, d), index)

    def seq_of(s):
        return jnp.minimum(s // pairs_per_seq, nseq - 1)

    consts = [s0, c0, nm, win, cw, rlb] + list(tail_w) + list(decode)
    n_exp = decode[2].shape[-1]
    tok = [(d, F32), (d // 2, jnp.uint32), (LANES, F32), (LANES, F32)]
    n_steps = n_pairs + 1
    return pl.pallas_call(
        functools.partial(_prompt_kernel, layer, tm, chunk, n_pairs, pairs_per_seq, n_steps * 2 * tm),
        grid=(n_steps,),
        in_specs=[tile_spec(0), tile_spec(1), tile_spec(2)] + [_const_spec(a.shape) for a in consts],
        out_specs=[pl.BlockSpec((2 * tm, w), lambda s: (s, 0)) for w, _ in tok] + [
            pl.BlockSpec((1, n_exp), lambda s: (0, 0)),
            pl.BlockSpec((None,) + s0.shape, lambda s: (seq_of(s), 0, 0, 0)),
            pl.BlockSpec((None,) + c0.shape, lambda s: (seq_of(s), 0, 0))],
        out_shape=[jax.ShapeDtypeStruct((n_steps * 2 * tm, w), t) for w, t in tok] + [
            jax.ShapeDtypeStruct((1, n_exp), F32),
            jax.ShapeDtypeStruct((nseq,) + s0.shape, F32), jax.ShapeDtypeStruct((nseq,) + c0.shape, F32)],
        scratch_shapes=[pltpu.VMEM((tm + 8, d_conv), F32), pltpu.VMEM(s0.shape, F32),
                        pltpu.VMEM((tm, d_in), F32), pltpu.VMEM((tm, d_in), F32),
                        pltpu.VMEM((1, n_exp), F32)],
        compiler_params=pltpu.CompilerParams(dimension_semantics=("arbitrary",),
                                             vmem_limit_bytes=VMEM_LIMIT),
        name="mixer_prompt",
    )(x, x, x, *consts)


def _sample_in_kernel(layer, x_ref, sc_ref, nm_ref, win_ref, cw_ref, rlb_ref,
                      yc_ref, nc_ref, f_ref, k_ref, q_ref, v_ref, og_ref):
    d_conv = cw_ref.shape[-1]
    d_rec = rlb_ref.shape[-1]
    u = _project(x_ref[...], nm_ref, win_ref)
    bg, cg, hv, q, fx, iv, og = _split_u(u, d_conv, d_rec)
    bx = bg * hv
    sc = sc_ref[...]
    s0, s1 = sc[:, :d_conv], sc[:, d_conv:]
    cw = cw_ref[...]
    yc_ref[...] = cg * (cw[0:1] * s0 + cw[1:2] * s1 + cw[2:3] * bx)
    nc_ref[...] = jnp.concatenate([s1, bx], axis=-1)
    f, kk = _forget(fx, _lower_bound(rlb_ref, layer))
    f_ref[...] = f
    k_ref[...] = kk
    q_ref[...] = q
    v_ref[...] = iv
    og_ref[...] = og


def _sample_state_kernel(group, f_ref, k_ref, q_ref, v_ref, s_ref, sn_ref, o_ref):
    for j in range(group):
        for h in range(N_HEADS):
            rs = slice(h * HEAD, (h + 1) * HEAD)
            fcol = f_ref[rs, j:j + 1]
            kcol = k_ref[rs, j:j + 1]
            qcol = q_ref[rs, j:j + 1]
            vrow = v_ref[j:j + 1, rs]
            sn = fcol * s_ref[j, h] + kcol * vrow
            sn_ref[j, h] = sn
            o_ref[j:j + 1, rs] = jnp.sum(qcol * sn, axis=0, keepdims=True)


def _sample_tail_kernel(x_ref, yc_ref, o_ref, og_ref, rg_ref, wout_ref, nf_ref, rw_ref, rb_ref,
                        h_ref, xn_ref, sel_ref, route_ref):
    y = jnp.concatenate([yc_ref[...], _rec_out(o_ref[...], og_ref[...], rg_ref[...])], axis=-1)
    h, xn, sel, route = _tail(x_ref[...], y, wout_ref, nf_ref, rw_ref, rb_ref)
    h_ref[...] = h
    xn_ref[...] = xn
    sel_ref[...] = sel
    route_ref[...] = route


def _rank_block(sel, route, carry, cap):
    tb, ne = sel.shape
    row = lax.broadcasted_iota(jnp.int32, (tb, tb), 0)
    col = lax.broadcasted_iota(jnp.int32, (tb, tb), 1)
    before = jnp.where(col < row, 1.0, 0.0).astype(BF16)
    rank = _dot(before, sel.astype(BF16)) + carry[...]
    carry[...] = carry[...] + jnp.sum(sel, axis=0, keepdims=True)
    lane_e = lax.broadcasted_iota(jnp.int32, (tb, ne), 1).astype(F32)
    lane = lax.broadcasted_iota(jnp.int32, (tb, LANES), 1)
    rnk = jnp.zeros((tb, LANES), F32)
    for k in range(TOP_K):
        expert = route[:, k:k + 1]
        mine = jnp.where(lane_e == expert, rank, 0.0)
        rnk = jnp.where(lane == k, expert * float(cap) + jnp.sum(mine, axis=-1, keepdims=True), rnk)
    return rnk


def _dispatch(xn, pos_w, n_slots):
    n_workers, n_chunks, top_k, ch = pos_w.shape
    assert n_workers == SC_CORES * SC_SUBCORES and ch % 8 == 0 and ch <= LANES
    d = xn.shape[1]
    mesh = plsc.VectorSubcoreMesh(core_axis_name="c", subcore_axis_name="s")

    @functools.partial(
        pl.kernel, mesh=mesh,
        out_type=jax.ShapeDtypeStruct((n_slots, d), xn.dtype),
        scratch_types=[pltpu.VMEM((n_chunks, top_k, ch), jnp.int32), pltpu.VMEM((SC_RING, ch, d), xn.dtype),
                       pltpu.SemaphoreType.DMA((SC_RING,)), pltpu.SemaphoreType.DMA((SC_RING,))],
        name="dispatch",
    )
    def run(xn_hbm, pos_hbm, xs_hbm, idx_v, rows_v, sem_r, sem_w):
        wid = lax.axis_index("s") * SC_CORES + lax.axis_index("c")

        def read(c, b):
            src = xn_hbm.at[pl.ds((wid * n_chunks + c) * ch, ch)]
            return pltpu.make_async_copy(src, rows_v.at[b], sem_r.at[b])

        def write(c, b, k):
            return pltpu.make_async_copy(rows_v.at[b], xs_hbm.at[idx_v.at[c, k]], sem_w.at[b])

        pltpu.sync_copy(pos_hbm.at[wid], idx_v)
        ahead = SC_RING - 1
        for c0 in range(min(ahead, n_chunks)):
            read(c0, c0).start()

        @pl.loop(0, n_chunks)
        def _(c):
            b = lax.rem(c, SC_RING)
            read(c, b).wait()
            for k in range(top_k):
                write(c, b, k).start()

            @pl.when(c >= 1)
            def _():
                for k in range(top_k):
                    write(c - 1, lax.rem(c - 1, SC_RING), k).wait()

            @pl.when(c + ahead < n_chunks)
            def _():
                read(c + ahead, lax.rem(c + ahead, SC_RING)).start()

        for k in range(top_k):
            write(n_chunks - 1, (n_chunks - 1) % SC_RING, k).wait()

    return run(xn, pos_w)


def _collect(ys, pos_w):
    n_workers, n_chunks, top_k, ch = pos_w.shape
    assert n_workers == SC_CORES * SC_SUBCORES and ch % 8 == 0 and ch <= LANES
    d = ys.shape[1]
    mesh = plsc.VectorSubcoreMesh(core_axis_name="c", subcore_axis_name="s")

    @functools.partial(
        pl.kernel, mesh=mesh,
        out_type=jax.ShapeDtypeStruct((n_workers * n_chunks * ch, top_k * d), ys.dtype),
        scratch_types=[pltpu.VMEM((n_chunks, top_k, ch), jnp.int32), pltpu.VMEM((top_k, ch, d), ys.dtype),
                       pltpu.SemaphoreType.DMA((top_k,)), pltpu.SemaphoreType.DMA((top_k,))],
        name="collect",
    )
    def run(ys_hbm, pos_hbm, out_hbm, idx_v, rows_v, sem_r, sem_w):
        wid = lax.axis_index("s") * SC_CORES + lax.axis_index("c")

        def read(c, k):
            return pltpu.make_async_copy(ys_hbm.at[idx_v.at[c, k]], rows_v.at[k], sem_r.at[k])

        def write(c, k):
            dst = out_hbm.at[pl.ds((wid * n_chunks + c) * ch, ch), pl.ds(k * d, d)]
            return pltpu.make_async_copy(rows_v.at[k], dst, sem_w.at[k])

        pltpu.sync_copy(pos_hbm.at[wid], idx_v)
        for k in range(top_k - 1):
            read(0, k).start()

        @pl.loop(0, n_chunks)
        def _(c):
            for k in range(top_k):
                read(c, k).wait()
                write(c, k).start()
                if k >= 1:
                    write(c, k - 1).wait()

                    @pl.when(c + 1 < n_chunks)
                    def _():
                        read(c + 1, k - 1).start()
                else:
                    @pl.when(c >= 1)
                    def _():
                        write(c - 1, top_k - 1).wait()
                    read(c, top_k - 1).start()

        write(n_chunks - 1, top_k - 1).wait()

    return run(ys, pos_w)


def _moe_kernel(tm, te_ref, nu_ref, tn_ref, tb_ref, x_ref, w1_hbm, bg_ref, bl_ref, w2_hbm, b2_ref,
                ys_ref, w1f, w2f, sem, w1p, w2b, act):
    i = pl.program_id(0)
    n_used = nu_ref[0]
    d_ff2 = w1f.shape[-1]
    n_blk = d_ff2 // MXU_N
    expert = te_ref[i]

    def weight_copies(e):
        return (pltpu.make_async_copy(w1_hbm.at[e], w1f, sem.at[0]),
                pltpu.make_async_copy(w2_hbm.at[e], w2f, sem.at[1]))

    @pl.when(i == 0)
    def _():
        for cp in weight_copies(expert):
            cp.start()

    @pl.when(i >= n_used)
    def _():
        ys_ref[...] = jnp.zeros_like(ys_ref)

    prev = te_ref[jnp.maximum(i - 1, 0)]
    changed = jnp.logical_or(i == 0, expert != prev)

    @pl.when(jnp.logical_and(changed, i < n_used))
    def _():
        for cp in weight_copies(expert):
            cp.wait()
        r = lax.broadcasted_iota(jnp.int32, (MXU_N, MXU_N), 0)
        c = lax.broadcasted_iota(jnp.int32, (MXU_N, MXU_N), 1)
        src = jnp.where(c < MXU_N // 2, 2 * c, 2 * (c - MXU_N // 2) + 1)
        perm = jnp.where(r == src, 1.0, 0.0).astype(BF16)
        for blk in range(n_blk):
            cs = slice(blk * MXU_N, (blk + 1) * MXU_N)
            w1p[:, cs] = _dot(w1f[:, cs].astype(BF16), perm).astype(BF16)
        w2b[...] = w2f[...].astype(BF16)

        @pl.when(tn_ref[i] != expert)
        def _():
            for cp in weight_copies(tn_ref[i]):
                cp.start(priority=1)

    @pl.when(i < n_used)
    def _():
        x = _unpack_halves(x_ref[...]).astype(BF16)
        bg, bl = bg_ref[expert], bl_ref[expert]
        for blk in range(n_blk):
            a = _dot(x, w1p[:, blk * MXU_N:(blk + 1) * MXU_N])
            half = MXU_N // 2
            hs = slice(blk * half, (blk + 1) * half)
            glu = jnp.minimum(a[:, :half] + bg[:, hs], SWIGLU_LIMIT)
            lin = jnp.clip(a[:, half:] + bl[:, hs], -SWIGLU_LIMIT, SWIGLU_LIMIT)
            act[:, hs] = (glu * _sigmoid(SWIGLU_ALPHA * glu) * (lin + 1.0)).astype(BF16)
        ys_ref[...] = _pack_halves(_dot(act[...], w2b[...]) + b2_ref[expert])


def _final_kernel(h_ref, gate_ref, nfin_ref, z_ref, y_ref):
    w = z_ref.shape[-1] // TOP_K
    g = gate_ref[...]
    out = h_ref[...]
    for k in range(TOP_K):
        out = out + g[:, TOP_K + k:TOP_K + k + 1] * _unpack_halves(z_ref[:, k * w:(k + 1) * w])
    y_ref[...] = _rms(out, nfin_ref[...])


def _final_call(h_all, gate, nfin, z, first_row, n_rows, tile):
    d = h_all.shape[-1]
    off = first_row // tile
    assert first_row % tile == 0 and n_rows % tile == 0
    return pl.pallas_call(
        _final_kernel,
        grid=(n_rows // tile,),
        in_specs=[
            pl.BlockSpec((tile, d), lambda i: (i + off, 0)),
            pl.BlockSpec((tile, gate.shape[-1]), lambda i: (i + off, 0)),
            pl.BlockSpec((1, d), lambda i: (0, 0)),
            pl.BlockSpec((tile, z.shape[-1]), lambda i: (i + off, 0)),
        ],
        out_specs=pl.BlockSpec((tile, d), lambda i: (i, 0)),
        out_shape=jax.ShapeDtypeStruct((n_rows, d), F32),
        compiler_params=pltpu.CompilerParams(dimension_semantics=("arbitrary",),
                                             vmem_limit_bytes=VMEM_LIMIT),
        name="final",
    )(h_all, gate, nfin, z)


def kernel(x_prompt, x_sample, state_conv, state_rec, meta_tokens, norm_mix, w_in, conv_w,
           rec_lower_bound, rec_norm, w_out, norm_ffn, router_w, router_b, expert_w1, expert_b1,
           expert_w2, expert_b2, norm_final):
    depth = norm_mix.shape[0]
    assert depth == 1, "single-layer step"
    layer = 0
    bp, seq, d = x_prompt.shape
    ns = x_sample.shape[0]
    assert x_sample.shape[1] == 1 and ns == TOKEN_TILE
    d_conv = conv_w.shape[-1]
    d_rec = rec_lower_bound.shape[-1]
    assert state_conv.shape[2] == 2 and d_rec == N_HEADS * HEAD
    n_exp = router_w.shape[-1]
    d_ff = expert_w2.shape[2]
    n_prompt = bp * seq
    n_tok = n_prompt + ns
    assert seq % PROMPT_TILE == 0 and n_prompt % TOKEN_TILE == 0

    nm = norm_mix[layer][None]
    win = w_in[layer].astype(BF16)
    cw = conv_w[layer]
    rlb = rec_lower_bound
    rg = rec_norm[layer][None]
    wout = w_out[layer].astype(BF16)
    nf = norm_ffn[layer][None]
    rw = router_w[layer]
    rb = router_b[layer][None]
    mix_w = (nm, win, cw, rlb)
    tail_w = (rg, wout, nf, rw, rb)

    lb = jnp.sum(jax.nn.softmax(rlb, axis=0)[:layer + 1], axis=0)
    fast = CHUNK * jnp.max(-jnp.log(lb)) < DECAY_LIMIT

    st_meta, cv_meta = lax.cond(
        fast, lambda: _meta_call(meta_tokens, mix_w, layer, N_META),
        lambda: _meta_call(meta_tokens, mix_w, layer, None))

    xs = x_sample.reshape(ns, d)
    wide = jax.ShapeDtypeStruct((ns, d_rec), F32)
    y_conv_s, new_conv_s, f_s, k_s, q_s, v_s, og_s = pl.pallas_call(
        functools.partial(_sample_in_kernel, layer),
        out_shape=[jax.ShapeDtypeStruct((ns, d_conv), F32), jax.ShapeDtypeStruct((ns, 2 * d_conv), F32),
                   wide, wide, wide, wide, wide],
        compiler_params=pltpu.CompilerParams(vmem_limit_bytes=VMEM_LIMIT),
        name="sample_in",
    )(xs, state_conv[layer].reshape(ns, 2 * d_conv), nm, win, cw, rlb)

    group = 8
    n_grp = ns // group

    def cols(a):
        return a.T.reshape(d_rec, n_grp, group).transpose(1, 0, 2)

    col_spec = pl.BlockSpec((None, d_rec, group), lambda g: (g, 0, 0))
    st_spec = pl.BlockSpec((group, N_HEADS, HEAD, HEAD), lambda g: (g, 0, 0, 0))
    row_spec = pl.BlockSpec((group, d_rec), lambda g: (g, 0))
    new_rec_s, o_s = pl.pallas_call(
        functools.partial(_sample_state_kernel, group),
        grid=(n_grp,),
        in_specs=[col_spec, col_spec, col_spec, row_spec, st_spec],
        out_specs=[st_spec, row_spec],
        out_shape=[jax.ShapeDtypeStruct(state_rec.shape[1:], F32), wide],
        compiler_params=pltpu.CompilerParams(dimension_semantics=("arbitrary",),
                                             vmem_limit_bytes=VMEM_LIMIT),
        name="sample_state",
    )(cols(f_s), cols(k_s), cols(q_s), v_s, state_rec[layer])

    decode = pl.pallas_call(
        _sample_tail_kernel,
        out_shape=[jax.ShapeDtypeStruct((ns, d), F32), jax.ShapeDtypeStruct((ns, d), F32),
                   jax.ShapeDtypeStruct((ns, n_exp), F32), jax.ShapeDtypeStruct((ns, LANES), F32)],
        compiler_params=pltpu.CompilerParams(vmem_limit_bytes=VMEM_LIMIT),
        name="sample_tail",
    )(xs, y_conv_s, o_s, og_s, rg, wout, nf, rw, rb)

    h_all, xn_all, route, rnk, counts, new_rec_p, new_conv_p = lax.cond(
        fast,
        lambda: _prompt_call(x_prompt, st_meta, cv_meta, mix_w, tail_w, decode, layer, PROMPT_TILE, CHUNK),
        lambda: _prompt_call(x_prompt, st_meta, cv_meta, mix_w, tail_w, decode, layer, PROMPT_TILE, None))

    tm = MOE_TILE
    cap = h_all.shape[0]
    assert cap % tm == 0
    n_tiles = (n_tok * TOP_K) // tm + n_exp
    counts = counts[0].astype(jnp.int32)
    tiles_e = (counts + tm - 1) // tm
    tile_end = jnp.cumsum(tiles_e)
    n_used = tile_end[-1]
    tile_ids = jnp.arange(n_tiles, dtype=jnp.int32)

    def expert_of(tile):
        return jnp.minimum(jnp.sum((tile_end[None, :] <= tile[:, None]).astype(jnp.int32), axis=1), n_exp - 1)

    tile_expert = expert_of(jnp.minimum(tile_ids, n_used - 1))
    mine = tile_expert[:, None] == jnp.arange(n_exp, dtype=jnp.int32)[None, :]
    after = jnp.sum(jnp.where(mine, tile_end[None, :], 0), axis=1)
    first = jnp.sum(jnp.where(mine, (tile_end - tiles_e)[None, :], 0), axis=1)
    next_expert = jnp.where(after < n_used, expert_of(after), tile_expert)
    n_rows = cap
    n_spare = (n_rows - n_tok) * TOP_K
    spare_blocks = -(-n_spare // tm)
    dummy_block = n_exp * (cap // tm) + spare_blocks
    n_slots = (dummy_block + 1) * tm
    tile_block = jnp.where(tile_ids < n_used, tile_expert * (cap // tm) + tile_ids - first, dummy_block)

    n_workers = SC_CORES * SC_SUBCORES
    assert n_rows % (n_workers * DISPATCH_CHUNK) == 0
    pos_rows = rnk[:, :TOP_K].astype(jnp.int32)

    def to_workers(p, ch):
        assert p.shape[0] % (n_workers * ch) == 0
        return p.reshape(n_workers, -1, ch, TOP_K).transpose(0, 1, 3, 2)

    xs = _dispatch(xn_all, to_workers(pos_rows, DISPATCH_CHUNK), n_slots)

    w1 = expert_w1[layer]
    w2 = expert_w2[layer]
    b1 = expert_b1[layer]
    b1g = b1[:, 0::2][:, None, :]
    b1l = b1[:, 1::2][:, None, :]
    b2 = expert_b2[layer][:, None, :]
    ys = pl.pallas_call(
        functools.partial(_moe_kernel, tm),
        grid_spec=pltpu.PrefetchScalarGridSpec(
            num_scalar_prefetch=4,
            grid=(n_tiles,),
            in_specs=[
                pl.BlockSpec((tm, d // 2), lambda i, te, nu, tn, tb: (tb[i], 0)),
                pl.BlockSpec(memory_space=pl.ANY),
                _const_spec(b1g.shape), _const_spec(b1l.shape),
                pl.BlockSpec(memory_space=pl.ANY),
                _const_spec(b2.shape),
            ],
            out_specs=pl.BlockSpec((tm, d // 2), lambda i, te, nu, tn, tb: (tb[i], 0)),
            scratch_shapes=[pltpu.VMEM((d, 2 * d_ff), F32), pltpu.VMEM((d_ff, d), F32),
                            pltpu.SemaphoreType.DMA((2,)),
                            pltpu.VMEM((d, 2 * d_ff), BF16), pltpu.VMEM((d_ff, d), BF16),
                            pltpu.VMEM((tm, d_ff), BF16)],
        ),
        out_shape=jax.ShapeDtypeStruct((n_slots, d // 2), jnp.uint32),
        compiler_params=pltpu.CompilerParams(dimension_semantics=("arbitrary",),
                                             vmem_limit_bytes=VMEM_LIMIT),
        name="moe",
    )(tile_expert, n_used[None].astype(jnp.int32), next_expert, tile_block, xs, w1, b1g, b1l, w2, b2)

    z = _collect(ys, to_workers(pos_rows, DISPATCH_CHUNK))
    y_p = _final_call(h_all, route, norm_final[None], z, 0, n_prompt, FINAL_TILE)
    y_s = _final_call(h_all, route, norm_final[None], z, n_prompt, ns, TOKEN_TILE)

    return (y_p.reshape(bp, seq, d), y_s.reshape(ns, 1, d),
            new_conv_p[None], new_rec_p[None],
            new_conv_s.reshape(1, ns, 2, d_conv), new_rec_s[None])
```

```python
import functools

import jax
import jax.numpy as jnp
from jax import lax
from jax.experimental import pallas as pl
from jax.experimental.pallas import tpu as pltpu
from jax.experimental.pallas import tpu_sc as plsc

F32 = jnp.float32
BF16 = jnp.bfloat16

N_HEADS = 4
HEAD = 128
N_META = 16
CHUNK = 64
SAFE_CHUNK = 16
DECAY_LIMIT = 80.0
TOP_K = 4
SWIGLU_LIMIT = 7.0
SWIGLU_ALPHA = 1.702
EPS = 1e-5

LANES = 128
MXU_N = 256
PROMPT_TILE = 256
TOKEN_TILE = 128
FINAL_TILE = 512
MOE_TILE = 512
SC_CORES = 2
SC_SUBCORES = 16
DISPATCH_CHUNK = 24
SC_RING = 4
VMEM_LIMIT = 56 * 1024 * 1024


def _dot(a, b):
    return jnp.dot(a, b, preferred_element_type=F32)


def _dot_nt(a, b):
    return lax.dot_general(a, b, (((1,), (1,)), ((), ())), preferred_element_type=F32)


def _dot_tn(a, b):
    return lax.dot_general(a, b, (((0,), (0,)), ((), ())), preferred_element_type=F32)


def _pack_halves(x):
    n = x.shape[-1] // 2
    lo = pltpu.bitcast(x[:, :n].astype(BF16).astype(F32), jnp.uint32)
    hi = pltpu.bitcast(x[:, n:].astype(BF16).astype(F32), jnp.uint32)
    return (lo >> 16) | (hi & jnp.uint32(0xFFFF0000))


def _unpack_halves(u):
    lo = pltpu.bitcast(u << 16, F32)
    hi = pltpu.bitcast(u & jnp.uint32(0xFFFF0000), F32)
    return jnp.concatenate([lo, hi], axis=-1)


def _sigmoid(x):
    return 1.0 / (1.0 + jnp.exp(-x))


def _rms(x, g):
    ms = jnp.mean(x * x, axis=-1, keepdims=True)
    return x * lax.rsqrt(ms + EPS) * g


def _project(x, nm_ref, win_ref):
    return _dot(_rms(x, nm_ref[...]).astype(BF16), win_ref[...])


def _split_u(u, d_conv, d_rec):
    pts = [0, d_conv, 2 * d_conv, 3 * d_conv, 3 * d_conv + d_rec, 3 * d_conv + 2 * d_rec,
           3 * d_conv + 3 * d_rec, 3 * d_conv + 4 * d_rec]
    return [u[:, pts[i]:pts[i + 1]] for i in range(7)]


def _lower_bound(rlb_ref, layer):
    r = rlb_ref[...]
    e = jnp.exp(r - jnp.max(r, axis=0, keepdims=True))
    return jnp.sum(e[0:layer + 1], axis=0, keepdims=True) / jnp.sum(e, axis=0, keepdims=True)


def _factorised_ok(rlb_ref, layer):
    return CHUNK * jnp.max(-jnp.log(_lower_bound(rlb_ref, layer))) < DECAY_LIMIT


def _forget(fx, lb):
    f = lb + (1.0 - lb) * _sigmoid(fx)
    return f, 1.0 - f


def _rec_out(o, og, rg):
    parts = []
    for h in range(N_HEADS):
        oh = o[:, h * HEAD:(h + 1) * HEAD]
        parts.append(oh * lax.rsqrt(jnp.mean(oh * oh, axis=-1, keepdims=True) + EPS))
    return jnp.concatenate(parts, axis=-1) * rg * (og * _sigmoid(og))


def _route(logits):
    m_rows, n = logits.shape
    lane = lax.broadcasted_iota(jnp.int32, logits.shape, 1).astype(F32)
    work = logits
    tops, firsts = [], []
    sel = jnp.zeros_like(logits)
    for _ in range(TOP_K):
        m = jnp.max(work, axis=-1, keepdims=True)
        first = jnp.min(jnp.where(work == m, lane, float(n)), axis=-1, keepdims=True)
        hot = lane == first
        tops.append(m)
        firsts.append(first)
        sel = sel + jnp.where(hot, 1.0, 0.0)
        work = jnp.where(hot, -jnp.inf, work)
    es = [jnp.exp(t - tops[0]) for t in tops]
    den = es[0]
    for e in es[1:]:
        den = den + e
    wide = lax.broadcasted_iota(jnp.int32, (m_rows, LANES), 1)
    route = jnp.zeros((m_rows, LANES), F32)
    for k in range(TOP_K):
        route = jnp.where(wide == k, firsts[k], route)
        route = jnp.where(wide == TOP_K + k, es[k] / den, route)
    return sel, route


def _tail(x, y, wout_ref, nf_ref, rw_ref, rb_ref):
    h = x + _dot(y.astype(BF16), wout_ref[...])
    xn = _rms(h, nf_ref[...])
    xh = xn.astype(BF16)
    xl = (xn - xh.astype(F32)).astype(BF16)
    rw = rw_ref[...]
    wh = rw.astype(BF16)
    wl = (rw - wh.astype(F32)).astype(BF16)
    logits = _dot(xh, wh) + _dot(xh, wl) + _dot(xl, wh) + rb_ref[...]
    sel, route = _route(logits)
    return h, xn, sel, route


def _cumsum_rows(x):
    n = x.shape[0]
    row = lax.broadcasted_iota(jnp.int32, (n, 1), 0)
    shift = 1
    while shift < n:
        x = x + jnp.where(row >= shift, pltpu.roll(x, shift, axis=0), 0.0)
        shift *= 2
    return x


def _rec_chunk(q, kk, v, lf, st_ref, causal):
    c = q.shape[0]
    b = _cumsum_rows(lf)
    eb = jnp.exp(b)
    qe = (q * eb).astype(BF16)
    ke = (kk * jnp.exp(-b)).astype(BF16)
    vb = v.astype(BF16)
    eb_last = eb[c - 1:c]
    outs = []
    for h in range(N_HEADS):
        sl = slice(h * HEAD, (h + 1) * HEAD)
        st = st_ref[h]
        sc = jnp.where(causal, _dot_nt(qe[:, sl], ke[:, sl]), 0.0)
        outs.append(_dot(sc.astype(BF16), vb[:, sl]) + _dot_nt(qe[:, sl], st.astype(BF16)))
        st_ref[h] = (st + _dot_tn(vb[:, sl], ke[:, sl])) * eb_last[:, sl]
    return jnp.concatenate(outs, axis=-1)


def _rec_chunk_safe(q, kk, v, lf, st_ref):
    c = q.shape[0]
    row = lax.broadcasted_iota(jnp.int32, (c, 1), 0)
    b = _cumsum_rows(lf)
    eb = jnp.exp(b)
    qe = (q * eb).astype(BF16)
    kl = (kk * jnp.exp(b[c - 1:c] - b)).astype(BF16)
    vb = v.astype(BF16)
    outs = []
    for h in range(N_HEADS):
        sl = slice(h * HEAD, (h + 1) * HEAD)
        st = st_ref[h]
        bh, kh, qh, vh = b[:, sl], kk[:, sl], q[:, sl], v[:, sl]
        intra = jnp.zeros((c, HEAD), F32)
        for t in range(c):
            w = kh * jnp.exp(jnp.where(row <= t, bh[t:t + 1] - bh, -jnp.inf))
            score = jnp.sum(w * qh[t:t + 1], axis=-1, keepdims=True)
            intra = jnp.where(row == t, jnp.sum(score * vh, axis=0, keepdims=True), intra)
        outs.append(intra + _dot_nt(qe[:, sl], st.astype(BF16)))
        st_ref[h] = st * eb[c - 1:c, sl] + _dot_tn(vb[:, sl], kl[:, sl])
    return jnp.concatenate(outs, axis=-1)


def _mix_tile(u, layer, chunk, cw_ref, rlb_ref, convbuf, st, cv_out):
    tm = u.shape[0]
    d_conv = cw_ref.shape[-1]
    d_rec = rlb_ref.shape[-1]
    bg, cg, hv, q, fx, iv, og = _split_u(u, d_conv, d_rec)

    bx = bg * hv
    convbuf[8:8 + tm, :] = bx
    cw = cw_ref[...]
    conv = cw[0:1] * convbuf[6:6 + tm, :] + cw[1:2] * convbuf[7:7 + tm, :] + cw[2:3] * bx
    convbuf[6:8, :] = bx[tm - 2:tm]
    cv_out[...] = bx[tm - 2:tm]

    f, kk = _forget(fx, _lower_bound(rlb_ref, layer))
    lf = jnp.log(f)
    outs = []
    if chunk is None:
        for c in range(tm // SAFE_CHUNK):
            rs = slice(c * SAFE_CHUNK, (c + 1) * SAFE_CHUNK)
            outs.append(_rec_chunk_safe(q[rs], kk[rs], iv[rs], lf[rs], st))
    else:
        row = lax.broadcasted_iota(jnp.int32, (chunk, chunk), 0)
        col = lax.broadcasted_iota(jnp.int32, (chunk, chunk), 1)
        causal = row >= col
        for c in range(tm // chunk):
            rs = slice(c * chunk, (c + 1) * chunk)
            outs.append(_rec_chunk(q[rs], kk[rs], iv[rs], lf[rs], st, causal))
    return cg * conv, jnp.concatenate(outs, axis=0), og


def _load_state(st, convbuf, s0_ref, c0_ref):
    for h in range(N_HEADS):
        st[h] = s0_ref[h].T
    convbuf[6:8, :] = c0_ref[...]


def _meta_kernel(layer, chunk, x_ref, s0_ref, c0_ref, nm_ref, win_ref, cw_ref, rlb_ref, st_out, cv_out,
                 convbuf, st):
    _load_state(st, convbuf, s0_ref, c0_ref)
    u = _project(x_ref[...], nm_ref, win_ref)
    fast = _factorised_ok(rlb_ref, layer)

    @pl.when(fast)
    def _():
        _mix_tile(u, layer, chunk, cw_ref, rlb_ref, convbuf, st, cv_out)

    @pl.when(jnp.logical_not(fast))
    def _():
        _mix_tile(u, layer, None, cw_ref, rlb_ref, convbuf, st, cv_out)

    for h in range(N_HEADS):
        st_out[h] = st[h].T


def _prompt_kernel(layer, tm, chunk, n_pairs, pairs_per_seq, cap,
                   xr0_ref, x1_ref, xp2_ref, s0_ref, c0_ref, nm_ref, win_ref, cw_ref, rlb_ref,
                   rg_ref, wout_ref, nf_ref, rw_ref, rb_ref, hs_ref, xns_ref, sels_ref, routes_ref,
                   h_ref, xn_ref, route_ref, rnk_ref, cnt_ref, st_out, cv_out,
                   convbuf, st, ua, ub, carry):
    step = pl.program_id(0)
    live = step < n_pairs
    fast = _factorised_ok(rlb_ref, layer)

    @pl.when(step == 0)
    def _():
        ua[...] = _project(xr0_ref[...], nm_ref, win_ref)
        carry[...] = jnp.zeros_like(carry)

    @pl.when(jnp.logical_and(lax.rem(step, pairs_per_seq) == 0, live))
    def _():
        _load_state(st, convbuf, s0_ref, c0_ref)

    def two_tiles(rec_chunk):
        def finish(x, u, rows):
            y_conv, o, og = _mix_tile(u, layer, rec_chunk, cw_ref, rlb_ref, convbuf, st, cv_out)
            y = jnp.concatenate([y_conv, _rec_out(o, og, rg_ref[...])], axis=-1)
            h, xn, sel, route = _tail(x, y, wout_ref, nf_ref, rw_ref, rb_ref)
            h_ref[rows, :] = h
            xn_ref[rows, :] = _pack_halves(xn)
            route_ref[rows, :] = route
            rnk_ref[rows, :] = _rank_block(sel, route, carry, cap)

        ub[...] = _project(x1_ref[...], nm_ref, win_ref)
        finish(xr0_ref[...], ua[...], slice(0, tm))
        ua[...] = _project(xp2_ref[...], nm_ref, win_ref)
        finish(x1_ref[...], ub[...], slice(tm, 2 * tm))

    @pl.when(jnp.logical_and(live, fast))
    def _():
        two_tiles(chunk)

    @pl.when(jnp.logical_and(live, jnp.logical_not(fast)))
    def _():
        two_tiles(None)

    @pl.when(jnp.logical_and(lax.rem(step, pairs_per_seq) == pairs_per_seq - 1, live))
    def _():
        for h in range(N_HEADS):
            st_out[h] = st[h].T

    @pl.when(step == n_pairs)
    def _():
        ns = hs_ref.shape[0]
        n_exp = sels_ref.shape[-1]
        pad = 2 * tm - ns
        spare = (lax.broadcasted_iota(jnp.int32, (pad, LANES), 0) * TOP_K
                 + lax.broadcasted_iota(jnp.int32, (pad, LANES), 1) + n_exp * cap).astype(F32)
        rnk_ref[0:ns, :] = _rank_block(sels_ref[...], routes_ref[...], carry, cap)
        rnk_ref[ns:2 * tm, :] = spare
        for dst, val in ((h_ref, hs_ref[...]), (xn_ref, _pack_halves(xns_ref[...])),
                         (route_ref, routes_ref[...])):
            dst[0:ns, :] = val
            dst[ns:2 * tm, :] = jnp.zeros((pad, dst.shape[-1]), dst.dtype)
        cnt_ref[...] = carry[...]


def _const_spec(shape):
    return pl.BlockSpec(shape, lambda *_: (0,) * len(shape))


def _meta_call(x, weights, layer, chunk):
    assert x.shape[0] % chunk == 0 and x.shape[0] % SAFE_CHUNK == 0
    nm, win, cw, rlb = weights
    d_conv = cw.shape[-1]
    s0 = jnp.zeros((N_HEADS, HEAD, HEAD), F32)
    c0 = jnp.zeros((2, d_conv), F32)
    return pl.pallas_call(
        functools.partial(_meta_kernel, layer, chunk),
        out_shape=[jax.ShapeDtypeStruct(s0.shape, F32), jax.ShapeDtypeStruct(c0.shape, F32)],
        scratch_shapes=[pltpu.VMEM((x.shape[0] + 8, d_conv), F32), pltpu.VMEM(s0.shape, F32)],
        compiler_params=pltpu.CompilerParams(vmem_limit_bytes=VMEM_LIMIT),
        name="mixer_meta",
    )(x, s0, c0, nm, win, cw, rlb)


def _prompt_call(x, s0, c0, weights, tail_w, decode, layer, tm, chunk):
    nseq, length, d = x.shape
    nt = length // tm
    assert nt % 2 == 0 and decode[0].shape[0] <= 2 * tm and tm % chunk == 0 and tm % SAFE_CHUNK == 0
    pairs_per_seq = nt // 2
    n_pairs = nseq * pairs_per_seq
    nm, win, cw, rlb = weights
    d_conv = cw.shape[-1]
    d_in = win.shape[-1]

    def tile_spec(offset):
        def index(s):
            tile = jnp.minimum(2 * s + offset, 2 * n_pairs - 1)
            return (tile // nt, lax.rem(tile, nt), 0)
        return pl.BlockSpec((None, tm, d), index)

    def seq_of(s):
        return jnp.minimum(s // pairs_per_seq, nseq - 1)

    consts = [s0, c0, nm, win, cw, rlb] + list(tail_w) + list(decode)
    n_exp = decode[2].shape[-1]
    tok = [(d, F32), (d // 2, jnp.uint32), (LANES, F32), (LANES, F32)]
    n_steps = n_pairs + 1
    return pl.pallas_call(
        functools.partial(_prompt_kernel, layer, tm, chunk, n_pairs, pairs_per_seq, n_steps * 2 * tm),
        grid=(n_steps,),
        in_specs=[tile_spec(0), tile_spec(1), tile_spec(2)] + [_const_spec(a.shape) for a in consts],
        out_specs=[pl.BlockSpec((2 * tm, w), lambda s: (s, 0)) for w, _ in tok] + [
            pl.BlockSpec((1, n_exp), lambda s: (0, 0)),
            pl.BlockSpec((None,) + s0.shape, lambda s: (seq_of(s), 0, 0, 0)),
            pl.BlockSpec((None,) + c0.shape, lambda s: (seq_of(s), 0, 0))],
        out_shape=[jax.ShapeDtypeStruct((n_steps * 2 * tm, w), t) for w, t in tok] + [
            jax.ShapeDtypeStruct((1, n_exp), F32),
            jax.ShapeDtypeStruct((nseq,) + s0.shape, F32), jax.ShapeDtypeStruct((nseq,) + c0.shape, F32)],
        scratch_shapes=[pltpu.VMEM((tm + 8, d_conv), F32), pltpu.VMEM(s0.shape, F32),
                        pltpu.VMEM((tm, d_in), F32), pltpu.VMEM((tm, d_in), F32),
                        pltpu.VMEM((1, n_exp), F32)],
        compiler_params=pltpu.CompilerParams(dimension_semantics=("arbitrary",),
                                             vmem_limit_bytes=VMEM_LIMIT),
        name="mixer_prompt",
    )(x, x, x, *consts)


def _sample_in_kernel(layer, x_ref, sc_ref, nm_ref, win_ref, cw_ref, rlb_ref,
                      yc_ref, nc_ref, f_ref, k_ref, q_ref, v_ref, og_ref):
    d_conv = cw_ref.shape[-1]
    d_rec = rlb_ref.shape[-1]
    u = _project(x_ref[...], nm_ref, win_ref)
    bg, cg, hv, q, fx, iv, og = _split_u(u, d_conv, d_rec)
    bx = bg * hv
    sc = sc_ref[...]
    s0, s1 = sc[:, :d_conv], sc[:, d_conv:]
    cw = cw_ref[...]
    yc_ref[...] = cg * (cw[0:1] * s0 + cw[1:2] * s1 + cw[2:3] * bx)
    nc_ref[...] = jnp.concatenate([s1, bx], axis=-1)
    f, kk = _forget(fx, _lower_bound(rlb_ref, layer))
    f_ref[...] = f
    k_ref[...] = kk
    q_ref[...] = q
    v_ref[...] = iv
    og_ref[...] = og


def _sample_state_kernel(group, f_ref, k_ref, q_ref, v_ref, s_ref, sn_ref, o_ref):
    for j in range(group):
        for h in range(N_HEADS):
            rs = slice(h * HEAD, (h + 1) * HEAD)
            fcol = f_ref[rs, j:j + 1]
            kcol = k_ref[rs, j:j + 1]
            qcol = q_ref[rs, j:j + 1]
            vrow = v_ref[j:j + 1, rs]
            sn = fcol * s_ref[j, h] + kcol * vrow
            sn_ref[j, h] = sn
            o_ref[j:j + 1, rs] = jnp.sum(qcol * sn, axis=0, keepdims=True)


def _sample_tail_kernel(x_ref, yc_ref, o_ref, og_ref, rg_ref, wout_ref, nf_ref, rw_ref, rb_ref,
                        h_ref, xn_ref, sel_ref, route_ref):
    y = jnp.concatenate([yc_ref[...], _rec_out(o_ref[...], og_ref[...], rg_ref[...])], axis=-1)
    h, xn, sel, route = _tail(x_ref[...], y, wout_ref, nf_ref, rw_ref, rb_ref)
    h_ref[...] = h
    xn_ref[...] = xn
    sel_ref[...] = sel
    route_ref[...] = route


def _rank_block(sel, route, carry, cap):
    tb, ne = sel.shape
    row = lax.broadcasted_iota(jnp.int32, (tb, tb), 0)
    col = lax.broadcasted_iota(jnp.int32, (tb, tb), 1)
    before = jnp.where(col < row, 1.0, 0.0).astype(BF16)
    rank = _dot(before, sel.astype(BF16)) + carry[...]
    carry[...] = carry[...] + jnp.sum(sel, axis=0, keepdims=True)
    lane_e = lax.broadcasted_iota(jnp.int32, (tb, ne), 1).astype(F32)
    lane = lax.broadcasted_iota(jnp.int32, (tb, LANES), 1)
    rnk = jnp.zeros((tb, LANES), F32)
    for k in range(TOP_K):
        expert = route[:, k:k + 1]
        mine = jnp.where(lane_e == expert, rank, 0.0)
        rnk = jnp.where(lane == k, expert * float(cap) + jnp.sum(mine, axis=-1, keepdims=True), rnk)
    return rnk


def _dispatch(xn, pos_w, n_slots):
    n_workers, n_chunks, top_k, ch = pos_w.shape
    assert n_workers == SC_CORES * SC_SUBCORES and ch % 8 == 0 and ch <= LANES
    d = xn.shape[1]
    mesh = plsc.VectorSubcoreMesh(core_axis_name="c", subcore_axis_name="s")

    @functools.partial(
        pl.kernel, mesh=mesh,
        out_type=jax.ShapeDtypeStruct((n_slots, d), xn.dtype),
        scratch_types=[pltpu.VMEM((n_chunks, top_k, ch), jnp.int32), pltpu.VMEM((SC_RING, ch, d), xn.dtype),
                       pltpu.SemaphoreType.DMA((SC_RING,)), pltpu.SemaphoreType.DMA((SC_RING,))],
        name="dispatch",
    )
    def run(xn_hbm, pos_hbm, xs_hbm, idx_v, rows_v, sem_r, sem_w):
        wid = lax.axis_index("s") * SC_CORES + lax.axis_index("c")

        def read(c, b):
            src = xn_hbm.at[pl.ds((wid * n_chunks + c) * ch, ch)]
            return pltpu.make_async_copy(src, rows_v.at[b], sem_r.at[b])

        def write(c, b, k):
            return pltpu.make_async_copy(rows_v.at[b], xs_hbm.at[idx_v.at[c, k]], sem_w.at[b])

        pltpu.sync_copy(pos_hbm.at[wid], idx_v)
        ahead = SC_RING - 1
        for c0 in range(min(ahead, n_chunks)):
            read(c0, c0).start()

        @pl.loop(0, n_chunks)
        def _(c):
            b = lax.rem(c, SC_RING)
            read(c, b).wait()
            for k in range(top_k):
                write(c, b, k).start()

            @pl.when(c >= 1)
            def _():
                for k in range(top_k):
                    write(c - 1, lax.rem(c - 1, SC_RING), k).wait()

            @pl.when(c + ahead < n_chunks)
            def _():
                read(c + ahead, lax.rem(c + ahead, SC_RING)).start()

        for k in range(top_k):
            write(n_chunks - 1, (n_chunks - 1) % SC_RING, k).wait()

    return run(xn, pos_w)


def _collect(ys, pos_w):
    n_workers, n_chunks, top_k, ch = pos_w.shape
    assert n_workers == SC_CORES * SC_SUBCORES and ch % 8 == 0 and ch <= LANES
    d = ys.shape[1]
    mesh = plsc.VectorSubcoreMesh(core_axis_name="c", subcore_axis_name="s")

    @functools.partial(
        pl.kernel, mesh=mesh,
        out_type=jax.ShapeDtypeStruct((n_workers * n_chunks * ch, top_k * d), ys.dtype),
        scratch_types=[pltpu.VMEM((n_chunks, top_k, ch), jnp.int32), pltpu.VMEM((top_k, ch, d), ys.dtype),
                       pltpu.SemaphoreType.DMA((top_k,)), pltpu.SemaphoreType.DMA((top_k,))],
        name="collect",
    )
    def run(ys_hbm, pos_hbm, out_hbm, idx_v, rows_v, sem_r, sem_w):
        wid = lax.axis_index("s") * SC_CORES + lax.axis_index("c")

        def read(c, k):
            return pltpu.make_async_copy(ys_hbm.at[idx_v.at[c, k]], rows_v.at[k], sem_r.at[k])

        def write(c, k):
            dst = out_hbm.at[pl.ds((wid * n_chunks + c) * ch, ch), pl.ds(k * d, d)]
            return pltpu.make_async_copy(rows_v.at[k], dst, sem_w.at[k])

        pltpu.sync_copy(pos_hbm.at[wid], idx_v)
        for k in range(top_k - 1):
            read(0, k).start()

        @pl.loop(0, n_chunks)
        def _(c):
            for k in range(top_k):
                read(c, k).wait()
                write(c, k).start()
                if k >= 1:
                    write(c, k - 1).wait()

                    @pl.when(c + 1 < n_chunks)
                    def _():
                        read(c + 1, k - 1).start()
                else:
                    @pl.when(c >= 1)
                    def _():
                        write(c - 1, top_k - 1).wait()
                    read(c, top_k - 1).start()

        write(n_chunks - 1, top_k - 1).wait()

    return run(ys, pos_w)


def _moe_kernel(tm, te_ref, nu_ref, tn_ref, tb_ref, x_ref, w1_hbm, bg_ref, bl_ref, w2_hbm, b2_ref,
                ys_ref, w1f, w2f, sem, w1p, w2b, act):
    i = pl.program_id(0)
    n_used = nu_ref[0]
    d_ff2 = w1f.shape[-1]
    n_blk = d_ff2 // MXU_N
    expert = te_ref[i]

    def weight_copies(e):
        return (pltpu.make_async_copy(w1_hbm.at[e], w1f, sem.at[0]),
                pltpu.make_async_copy(w2_hbm.at[e], w2f, sem.at[1]))

    @pl.when(i == 0)
    def _():
        for cp in weight_copies(expert):
            cp.start()

    @pl.when(i >= n_used)
    def _():
        ys_ref[...] = jnp.zeros_like(ys_ref)

    prev = te_ref[jnp.maximum(i - 1, 0)]
    changed = jnp.logical_or(i == 0, expert != prev)

    @pl.when(jnp.logical_and(changed, i < n_used))
    def _():
        for cp in weight_copies(expert):
            cp.wait()
        r = lax.broadcasted_iota(jnp.int32, (MXU_N, MXU_N), 0)
        c = lax.broadcasted_iota(jnp.int32, (MXU_N, MXU_N), 1)
        src = jnp.where(c < MXU_N // 2, 2 * c, 2 * (c - MXU_N // 2) + 1)
        perm = jnp.where(r == src, 1.0, 0.0).astype(BF16)
        for blk in range(n_blk):
            cs = slice(blk * MXU_N, (blk + 1) * MXU_N)
            w1p[:, cs] = _dot(w1f[:, cs].astype(BF16), perm).astype(BF16)
        w2b[...] = w2f[...].astype(BF16)

        @pl.when(tn_ref[i] != expert)
        def _():
            for cp in weight_copies(tn_ref[i]):
                cp.start(priority=1)

    @pl.when(i < n_used)
    def _():
        x = _unpack_halves(x_ref[...]).astype(BF16)
        bg, bl = bg_ref[expert], bl_ref[expert]
        for blk in range(n_blk):
            a = _dot(x, w1p[:, blk * MXU_N:(blk + 1) * MXU_N])
            half = MXU_N // 2
            hs = slice(blk * half, (blk + 1) * half)
            glu = jnp.minimum(a[:, :half] + bg[:, hs], SWIGLU_LIMIT)
            lin = jnp.clip(a[:, half:] + bl[:, hs], -SWIGLU_LIMIT, SWIGLU_LIMIT)
            act[:, hs] = (glu * _sigmoid(SWIGLU_ALPHA * glu) * (lin + 1.0)).astype(BF16)
        ys_ref[...] = _pack_halves(_dot(act[...], w2b[...]) + b2_ref[expert])


def _final_kernel(h_ref, gate_ref, nfin_ref, z_ref, y_ref):
    w = z_ref.shape[-1] // TOP_K
    g = gate_ref[...]
    out = h_ref[...]
    for k in range(TOP_K):
        out = out + g[:, TOP_K + k:TOP_K + k + 1] * _unpack_halves(z_ref[:, k * w:(k + 1) * w])
    y_ref[...] = _rms(out, nfin_ref[...])


def _final_call(h_all, gate, nfin, z, first_row, n_rows, tile):
    d = h_all.shape[-1]
    off = first_row // tile
    assert first_row % tile == 0 and n_rows % tile == 0
    return pl.pallas_call(
        _final_kernel,
        grid=(n_rows // tile,),
        in_specs=[
            pl.BlockSpec((tile, d), lambda i: (i + off, 0)),
            pl.BlockSpec((tile, gate.shape[-1]), lambda i: (i + off, 0)),
            pl.BlockSpec((1, d), lambda i: (0, 0)),
            pl.BlockSpec((tile, z.shape[-1]), lambda i: (i + off, 0)),
        ],
        out_specs=pl.BlockSpec((tile, d), lambda i: (i, 0)),
        out_shape=jax.ShapeDtypeStruct((n_rows, d), F32),
        compiler_params=pltpu.CompilerParams(dimension_semantics=("arbitrary",),
                                             vmem_limit_bytes=VMEM_LIMIT),
        name="final",
    )(h_all, gate, nfin, z)


def kernel(x_prompt, x_sample, state_conv, state_rec, meta_tokens, norm_mix, w_in, conv_w,
           rec_lower_bound, rec_norm, w_out, norm_ffn, router_w, router_b, expert_w1, expert_b1,
           expert_w2, expert_b2, norm_final):
    depth = norm_mix.shape[0]
    assert depth == 1, "single-layer step"
    layer = 0
    bp, seq, d = x_prompt.shape
    ns = x_sample.shape[0]
    assert x_sample.shape[1] == 1 and ns == TOKEN_TILE
    d_conv = conv_w.shape[-1]
    d_rec = rec_lower_bound.shape[-1]
    assert state_conv.shape[2] == 2 and d_rec == N_HEADS * HEAD
    n_exp = router_w.shape[-1]
    d_ff = expert_w2.shape[2]
    n_prompt = bp * seq
    n_tok = n_prompt + ns
    assert seq % PROMPT_TILE == 0 and n_prompt % TOKEN_TILE == 0

    nm = norm_mix[layer][None]
    win = w_in[layer].astype(BF16)
    cw = conv_w[layer]
    rlb = rec_lower_bound
    rg = rec_norm[layer][None]
    wout = w_out[layer].astype(BF16)
    nf = norm_ffn[layer][None]
    rw = router_w[layer]
    rb = router_b[layer][None]
    mix_w = (nm, win, cw, rlb)
    tail_w = (rg, wout, nf, rw, rb)

    st_meta, cv_meta = _meta_call(meta_tokens, mix_w, layer, N_META)

    xs = x_sample.reshape(ns, d)
    wide = jax.ShapeDtypeStruct((ns, d_rec), F32)
    y_conv_s, new_conv_s, f_s, k_s, q_s, v_s, og_s = pl.pallas_call(
        functools.partial(_sample_in_kernel, layer),
        out_shape=[jax.ShapeDtypeStruct((ns, d_conv), F32), jax.ShapeDtypeStruct((ns, 2 * d_conv), F32),
                   wide, wide, wide, wide, wide],
        compiler_params=pltpu.CompilerParams(vmem_limit_bytes=VMEM_LIMIT),
        name="sample_in",
    )(xs, state_conv[layer].reshape(ns, 2 * d_conv), nm, win, cw, rlb)

    group = 8
    n_grp = ns // group

    def cols(a):
        return a.T.reshape(d_rec, n_grp, group).transpose(1, 0, 2)

    col_spec = pl.BlockSpec((None, d_rec, group), lambda g: (g, 0, 0))
    st_spec = pl.BlockSpec((group, N_HEADS, HEAD, HEAD), lambda g: (g, 0, 0, 0))
    row_spec = pl.BlockSpec((group, d_rec), lambda g: (g, 0))
    new_rec_s, o_s = pl.pallas_call(
        functools.partial(_sample_state_kernel, group),
        grid=(n_grp,),
        in_specs=[col_spec, col_spec, col_spec, row_spec, st_spec],
        out_specs=[st_spec, row_spec],
        out_shape=[jax.ShapeDtypeStruct(state_rec.shape[1:], F32), wide],
        compiler_params=pltpu.CompilerParams(dimension_semantics=("arbitrary",),
                                             vmem_limit_bytes=VMEM_LIMIT),
        name="sample_state",
    )(cols(f_s), cols(k_s), cols(q_s), v_s, state_rec[layer])

    decode = pl.pallas_call(
        _sample_tail_kernel,
        out_shape=[jax.ShapeDtypeStruct((ns, d), F32), jax.ShapeDtypeStruct((ns, d), F32),
                   jax.ShapeDtypeStruct((ns, n_exp), F32), jax.ShapeDtypeStruct((ns, LANES), F32)],
        compiler_params=pltpu.CompilerParams(vmem_limit_bytes=VMEM_LIMIT),
        name="sample_tail",
    )(xs, y_conv_s, o_s, og_s, rg, wout, nf, rw, rb)

    h_all, xn_all, route, rnk, counts, new_rec_p, new_conv_p = _prompt_call(
        x_prompt, st_meta, cv_meta, mix_w, tail_w, decode, layer, PROMPT_TILE, CHUNK)

    tm = MOE_TILE
    cap = h_all.shape[0]
    assert cap % tm == 0
    n_tiles = (n_tok * TOP_K) // tm + n_exp
    counts = counts[0].astype(jnp.int32)
    tiles_e = (counts + tm - 1) // tm
    tile_end = jnp.cumsum(tiles_e)
    n_used = tile_end[-1]
    tile_ids = jnp.arange(n_tiles, dtype=jnp.int32)

    def expert_of(tile):
        return jnp.minimum(jnp.sum((tile_end[None, :] <= tile[:, None]).astype(jnp.int32), axis=1), n_exp - 1)

    tile_expert = expert_of(jnp.minimum(tile_ids, n_used - 1))
    mine = tile_expert[:, None] == jnp.arange(n_exp, dtype=jnp.int32)[None, :]
    after = jnp.sum(jnp.where(mine, tile_end[None, :], 0), axis=1)
    first = jnp.sum(jnp.where(mine, (tile_end - tiles_e)[None, :], 0), axis=1)
    next_expert = jnp.where(after < n_used, expert_of(after), tile_expert)
    n_rows = cap
    n_spare = (n_rows - n_tok) * TOP_K
    spare_blocks = -(-n_spare // tm)
    dummy_block = n_exp * (cap // tm) + spare_blocks
    n_slots = (dummy_block + 1) * tm
    tile_block = jnp.where(tile_ids < n_used, tile_expert * (cap // tm) + tile_ids - first, dummy_block)

    n_workers = SC_CORES * SC_SUBCORES
    assert n_rows % (n_workers * DISPATCH_CHUNK) == 0
    pos_rows = rnk[:, :TOP_K].astype(jnp.int32)

    def to_workers(p, ch):
        assert p.shape[0] % (n_workers * ch) == 0
        return p.reshape(n_workers, -1, ch, TOP_K).transpose(0, 1, 3, 2)

    xs = _dispatch(xn_all, to_workers(pos_rows, DISPATCH_CHUNK), n_slots)

    w1 = expert_w1[layer]
    w2 = expert_w2[layer]
    b1 = expert_b1[layer]
    b1g = b1[:, 0::2][:, None, :]
    b1l = b1[:, 1::2][:, None, :]
    b2 = expert_b2[layer][:, None, :]
    ys = pl.pallas_call(
        functools.partial(_moe_kernel, tm),
        grid_spec=pltpu.PrefetchScalarGridSpec(
            num_scalar_prefetch=4,
            grid=(n_tiles,),
            in_specs=[
                pl.BlockSpec((tm, d // 2), lambda i, te, nu, tn, tb: (tb[i], 0)),
                pl.BlockSpec(memory_space=pl.ANY),
                _const_spec(b1g.shape), _const_spec(b1l.shape),
                pl.BlockSpec(memory_space=pl.ANY),
                _const_spec(b2.shape),
            ],
            out_specs=pl.BlockSpec((tm, d // 2), lambda i, te, nu, tn, tb: (tb[i], 0)),
            scratch_shapes=[pltpu.VMEM((d, 2 * d_ff), F32), pltpu.VMEM((d_ff, d), F32),
                            pltpu.SemaphoreType.DMA((2,)),
                            pltpu.VMEM((d, 2 * d_ff), BF16), pltpu.VMEM((d_ff, d), BF16),
                            pltpu.VMEM((tm, d_ff), BF16)],
        ),
        out_shape=jax.ShapeDtypeStruct((n_slots, d // 2), jnp.uint32),
        compiler_params=pltpu.CompilerParams(dimension_semantics=("arbitrary",),
                                             vmem_limit_bytes=VMEM_LIMIT),
        name="moe",
    )(tile_expert, n_used[None].astype(jnp.int32), next_expert, tile_block, xs, w1, b1g, b1l, w2, b2)

    z = _collect(ys, to_workers(pos_rows, DISPATCH_CHUNK))
    y_p = _final_call(h_all, route, norm_final[None], z, 0, n_prompt, FINAL_TILE)
    y_s = _final_call(h_all, route, norm_final[None], z, n_prompt, ns, TOKEN_TILE)

    return (y_p.reshape(bp, seq, d), y_s.reshape(ns, 1, d),
            new_conv_p[None], new_rec_p[None],
            new_conv_s.reshape(1, ns, 2, d_conv), new_rec_s[None])
```

```python
import functools

import jax
import jax.numpy as jnp
from jax import lax
from jax.experimental import pallas as pl
from jax.experimental.pallas import tpu as pltpu
from jax.experimental.pallas import tpu_sc as plsc

F32 = jnp.float32
BF16 = jnp.bfloat16

N_HEADS = 4
HEAD = 128
N_META = 16
CHUNK = 64
SAFE_CHUNK = 16
DECAY_LIMIT = 80.0
TOP_K = 4
SWIGLU_LIMIT = 7.0
SWIGLU_ALPHA = 1.702
EPS = 1e-5

LANES = 128
MXU_N = 256
PROMPT_TILE = 256
TOKEN_TILE = 128
FINAL_TILE = 512
MOE_TILE = 512
SC_CORES = 2
SC_SUBCORES = 16
DISPATCH_CHUNK = 24
SC_RING = 4
VMEM_LIMIT = 56 * 1024 * 1024


def _dot(a, b):
    return jnp.dot(a, b, preferred_element_type=F32)


def _dot_nt(a, b):
    return lax.dot_general(a, b, (((1,), (1,)), ((), ())), preferred_element_type=F32)


def _dot_tn(a, b):
    return lax.dot_general(a, b, (((0,), (0,)), ((), ())), preferred_element_type=F32)


def _pack_halves(x):
    n = x.shape[-1] // 2
    lo = pltpu.bitcast(x[:, :n].astype(BF16).astype(F32), jnp.uint32)
    hi = pltpu.bitcast(x[:, n:].astype(BF16).astype(F32), jnp.uint32)
    return (lo >> 16) | (hi & jnp.uint32(0xFFFF0000))


def _unpack_halves(u):
    lo = pltpu.bitcast(u << 16, F32)
    hi = pltpu.bitcast(u & jnp.uint32(0xFFFF0000), F32)
    return jnp.concatenate([lo, hi], axis=-1)


def _sigmoid(x):
    return 1.0 / (1.0 + jnp.exp(-x))


def _rms(x, g):
    ms = jnp.mean(x * x, axis=-1, keepdims=True)
    return x * lax.rsqrt(ms + EPS) * g


def _project(x, nm_ref, win_ref):
    return _dot(_rms(x, nm_ref[...]).astype(BF16), win_ref[...])


def _split_u(u, d_conv, d_rec):
    pts = [0, d_conv, 2 * d_conv, 3 * d_conv, 3 * d_conv + d_rec, 3 * d_conv + 2 * d_rec,
           3 * d_conv + 3 * d_rec, 3 * d_conv + 4 * d_rec]
    return [u[:, pts[i]:pts[i + 1]] for i in range(7)]


def _lower_bound(rlb_ref, layer):
    r = rlb_ref[...]
    e = jnp.exp(r - jnp.max(r, axis=0, keepdims=True))
    return jnp.sum(e[0:layer + 1], axis=0, keepdims=True) / jnp.sum(e, axis=0, keepdims=True)


def _forget(fx, lb):
    f = lb + (1.0 - lb) * _sigmoid(fx)
    return f, 1.0 - f


def _rec_out(o, og, rg):
    parts = []
    for h in range(N_HEADS):
        oh = o[:, h * HEAD:(h + 1) * HEAD]
        parts.append(oh * lax.rsqrt(jnp.mean(oh * oh, axis=-1, keepdims=True) + EPS))
    return jnp.concatenate(parts, axis=-1) * rg * (og * _sigmoid(og))


def _route(logits):
    m_rows, n = logits.shape
    lane = lax.broadcasted_iota(jnp.int32, logits.shape, 1).astype(F32)
    work = logits
    tops, firsts = [], []
    sel = jnp.zeros_like(logits)
    for _ in range(TOP_K):
        m = jnp.max(work, axis=-1, keepdims=True)
        first = jnp.min(jnp.where(work == m, lane, float(n)), axis=-1, keepdims=True)
        hot = lane == first
        tops.append(m)
        firsts.append(first)
        sel = sel + jnp.where(hot, 1.0, 0.0)
        work = jnp.where(hot, -jnp.inf, work)
    es = [jnp.exp(t - tops[0]) for t in tops]
    den = es[0]
    for e in es[1:]:
        den = den + e
    wide = lax.broadcasted_iota(jnp.int32, (m_rows, LANES), 1)
    route = jnp.zeros((m_rows, LANES), F32)
    for k in range(TOP_K):
        route = jnp.where(wide == k, firsts[k], route)
        route = jnp.where(wide == TOP_K + k, es[k] / den, route)
    return sel, route


def _tail(x, y, wout_ref, nf_ref, rw_ref, rb_ref):
    h = x + _dot(y.astype(BF16), wout_ref[...])
    xn = _rms(h, nf_ref[...])
    xh = xn.astype(BF16)
    xl = (xn - xh.astype(F32)).astype(BF16)
    rw = rw_ref[...]
    wh = rw.astype(BF16)
    wl = (rw - wh.astype(F32)).astype(BF16)
    logits = _dot(xh, wh) + _dot(xh, wl) + _dot(xl, wh) + rb_ref[...]
    sel, route = _route(logits)
    return h, xn, sel, route


def _cumsum_rows(x):
    n = x.shape[0]
    row = lax.broadcasted_iota(jnp.int32, (n, 1), 0)
    shift = 1
    while shift < n:
        x = x + jnp.where(row >= shift, pltpu.roll(x, shift, axis=0), 0.0)
        shift *= 2
    return x


def _rec_chunk(q, kk, v, lf, st_ref, causal):
    c = q.shape[0]
    b = _cumsum_rows(lf)
    eb = jnp.exp(b)
    qe = (q * eb).astype(BF16)
    ke = (kk * jnp.exp(-b)).astype(BF16)
    vb = v.astype(BF16)
    eb_last = eb[c - 1:c]
    outs = []
    for h in range(N_HEADS):
        sl = slice(h * HEAD, (h + 1) * HEAD)
        st = st_ref[h]
        sc = jnp.where(causal, _dot_nt(qe[:, sl], ke[:, sl]), 0.0)
        outs.append(_dot(sc.astype(BF16), vb[:, sl]) + _dot_nt(qe[:, sl], st.astype(BF16)))
        st_ref[h] = (st + _dot_tn(vb[:, sl], ke[:, sl])) * eb_last[:, sl]
    return jnp.concatenate(outs, axis=-1)


def _rec_chunk_safe(q, kk, v, lf, st_ref):
    c = q.shape[0]
    row = lax.broadcasted_iota(jnp.int32, (c, 1), 0)
    b = _cumsum_rows(lf)
    eb = jnp.exp(b)
    qe = (q * eb).astype(BF16)
    kl = (kk * jnp.exp(b[c - 1:c] - b)).astype(BF16)
    vb = v.astype(BF16)
    outs = []
    for h in range(N_HEADS):
        sl = slice(h * HEAD, (h + 1) * HEAD)
        st = st_ref[h]
        bh, kh, qh, vh = b[:, sl], kk[:, sl], q[:, sl], v[:, sl]
        intra = jnp.zeros((c, HEAD), F32)
        for t in range(c):
            w = kh * jnp.exp(jnp.where(row <= t, bh[t:t + 1] - bh, -jnp.inf))
            score = jnp.sum(w * qh[t:t + 1], axis=-1, keepdims=True)
            intra = jnp.where(row == t, jnp.sum(score * vh, axis=0, keepdims=True), intra)
        outs.append(intra + _dot_nt(qe[:, sl], st.astype(BF16)))
        st_ref[h] = st * eb[c - 1:c, sl] + _dot_tn(vb[:, sl], kl[:, sl])
    return jnp.concatenate(outs, axis=-1)


def _mix_tile(u, layer, chunk, cw_ref, rlb_ref, convbuf, st, cv_out):
    tm = u.shape[0]
    d_conv = cw_ref.shape[-1]
    d_rec = rlb_ref.shape[-1]
    bg, cg, hv, q, fx, iv, og = _split_u(u, d_conv, d_rec)

    bx = bg * hv
    convbuf[8:8 + tm, :] = bx
    cw = cw_ref[...]
    conv = cw[0:1] * convbuf[6:6 + tm, :] + cw[1:2] * convbuf[7:7 + tm, :] + cw[2:3] * bx
    convbuf[6:8, :] = bx[tm - 2:tm]
    cv_out[...] = bx[tm - 2:tm]

    f, kk = _forget(fx, _lower_bound(rlb_ref, layer))
    lf = jnp.log(f)
    outs = []
    if chunk is None:
        for c in range(tm // SAFE_CHUNK):
            rs = slice(c * SAFE_CHUNK, (c + 1) * SAFE_CHUNK)
            outs.append(_rec_chunk_safe(q[rs], kk[rs], iv[rs], lf[rs], st))
    else:
        row = lax.broadcasted_iota(jnp.int32, (chunk, chunk), 0)
        col = lax.broadcasted_iota(jnp.int32, (chunk, chunk), 1)
        causal = row >= col
        for c in range(tm // chunk):
            rs = slice(c * chunk, (c + 1) * chunk)
            outs.append(_rec_chunk(q[rs], kk[rs], iv[rs], lf[rs], st, causal))
    return cg * conv, jnp.concatenate(outs, axis=0), og


def _load_state(st, convbuf, s0_ref, c0_ref):
    for h in range(N_HEADS):
        st[h] = s0_ref[h].T
    convbuf[6:8, :] = c0_ref[...]


def _meta_kernel(layer, chunk, x_ref, s0_ref, c0_ref, nm_ref, win_ref, cw_ref, rlb_ref, st_out, cv_out,
                 convbuf, st):
    _load_state(st, convbuf, s0_ref, c0_ref)
    _mix_tile(_project(x_ref[...], nm_ref, win_ref), layer, chunk, cw_ref, rlb_ref, convbuf, st, cv_out)
    for h in range(N_HEADS):
        st_out[h] = st[h].T


def _prompt_kernel(layer, tm, chunk, n_pairs, pairs_per_seq, cap,
                   xr0_ref, x1_ref, xp2_ref, s0_ref, c0_ref, nm_ref, win_ref, cw_ref, rlb_ref,
                   rg_ref, wout_ref, nf_ref, rw_ref, rb_ref, hs_ref, xns_ref, sels_ref, routes_ref,
                   h_ref, xn_ref, route_ref, rnk_ref, cnt_ref, st_out, cv_out,
                   convbuf, st, ua, ub, carry):
    step = pl.program_id(0)
    live = step < n_pairs

    @pl.when(step == 0)
    def _():
        ua[...] = _project(xr0_ref[...], nm_ref, win_ref)
        carry[...] = jnp.zeros_like(carry)

    @pl.when(jnp.logical_and(lax.rem(step, pairs_per_seq) == 0, live))
    def _():
        _load_state(st, convbuf, s0_ref, c0_ref)

    @pl.when(live)
    def _():
        def finish(x, u, rows):
            y_conv, o, og = _mix_tile(u, layer, chunk, cw_ref, rlb_ref, convbuf, st, cv_out)
            y = jnp.concatenate([y_conv, _rec_out(o, og, rg_ref[...])], axis=-1)
            h, xn, sel, route = _tail(x, y, wout_ref, nf_ref, rw_ref, rb_ref)
            h_ref[rows, :] = h
            xn_ref[rows, :] = _pack_halves(xn)
            route_ref[rows, :] = route
            rnk_ref[rows, :] = _rank_block(sel, route, carry, cap)

        ub[...] = _project(x1_ref[...], nm_ref, win_ref)
        finish(xr0_ref[...], ua[...], slice(0, tm))
        ua[...] = _project(xp2_ref[...], nm_ref, win_ref)
        finish(x1_ref[...], ub[...], slice(tm, 2 * tm))

    @pl.when(jnp.logical_and(lax.rem(step, pairs_per_seq) == pairs_per_seq - 1, live))
    def _():
        for h in range(N_HEADS):
            st_out[h] = st[h].T

    @pl.when(step == n_pairs)
    def _():
        ns = hs_ref.shape[0]
        n_exp = sels_ref.shape[-1]
        pad = 2 * tm - ns
        spare = (lax.broadcasted_iota(jnp.int32, (pad, LANES), 0) * TOP_K
                 + lax.broadcasted_iota(jnp.int32, (pad, LANES), 1) + n_exp * cap).astype(F32)
        rnk_ref[0:ns, :] = _rank_block(sels_ref[...], routes_ref[...], carry, cap)
        rnk_ref[ns:2 * tm, :] = spare
        for dst, val in ((h_ref, hs_ref[...]), (xn_ref, _pack_halves(xns_ref[...])),
                         (route_ref, routes_ref[...])):
            dst[0:ns, :] = val
            dst[ns:2 * tm, :] = jnp.zeros((pad, dst.shape[-1]), dst.dtype)
        cnt_ref[...] = carry[...]


def _const_spec(shape):
    return pl.BlockSpec(shape, lambda *_: (0,) * len(shape))


def _meta_call(x, weights, layer, chunk):
    assert x.shape[0] % (chunk or SAFE_CHUNK) == 0
    nm, win, cw, rlb = weights
    d_conv = cw.shape[-1]
    s0 = jnp.zeros((N_HEADS, HEAD, HEAD), F32)
    c0 = jnp.zeros((2, d_conv), F32)
    return pl.pallas_call(
        functools.partial(_meta_kernel, layer, chunk),
        out_shape=[jax.ShapeDtypeStruct(s0.shape, F32), jax.ShapeDtypeStruct(c0.shape, F32)],
        scratch_shapes=[pltpu.VMEM((x.shape[0] + 8, d_conv), F32), pltpu.VMEM(s0.shape, F32)],
        compiler_params=pltpu.CompilerParams(vmem_limit_bytes=VMEM_LIMIT),
        name="mixer_meta",
    )(x, s0, c0, nm, win, cw, rlb)


def _prompt_call(x, s0, c0, weights, tail_w, decode, layer, tm, chunk):
    nseq, length, d = x.shape
    nt = length // tm
    assert nt % 2 == 0 and decode[0].shape[0] <= 2 * tm and tm % (chunk or SAFE_CHUNK) == 0
    pairs_per_seq = nt // 2
    n_pairs = nseq * pairs_per_seq
    nm, win, cw, rlb = weights
    d_conv = cw.shape[-1]
    d_in = win.shape[-1]

    def tile_spec(offset):
        def index(s):
            tile = jnp.minimum(2 * s + offset, 2 * n_pairs - 1)
            return (tile // nt, lax.rem(tile, nt), 0)
        return pl.BlockSpec((None, tm, d), index)

    def seq_of(s):
        return jnp.minimum(s // pairs_per_seq, nseq - 1)

    consts = [s0, c0, nm, win, cw, rlb] + list(tail_w) + list(decode)
    n_exp = decode[2].shape[-1]
    tok = [(d, F32), (d // 2, jnp.uint32), (LANES, F32), (LANES, F32)]
    n_steps = n_pairs + 1
    return pl.pallas_call(
        functools.partial(_prompt_kernel, layer, tm, chunk, n_pairs, pairs_per_seq, n_steps * 2 * tm),
        grid=(n_steps,),
        in_specs=[tile_spec(0), tile_spec(1), tile_spec(2)] + [_const_spec(a.shape) for a in consts],
        out_specs=[pl.BlockSpec((2 * tm, w), lambda s: (s, 0)) for w, _ in tok] + [
            pl.BlockSpec((1, n_exp), lambda s: (0, 0)),
            pl.BlockSpec((None,) + s0.shape, lambda s: (seq_of(s), 0, 0, 0)),
            pl.BlockSpec((None,) + c0.shape, lambda s: (seq_of(s), 0, 0))],
        out_shape=[jax.ShapeDtypeStruct((n_steps * 2 * tm, w), t) for w, t in tok] + [
            jax.ShapeDtypeStruct((1, n_exp), F32),
            jax.ShapeDtypeStruct((nseq,) + s0.shape, F32), jax.ShapeDtypeStruct((nseq,) + c0.shape, F32)],
        scratch_shapes=[pltpu.VMEM((tm + 8, d_conv), F32), pltpu.VMEM(s0.shape, F32),
                        pltpu.VMEM((tm, d_in), F32), pltpu.VMEM((tm, d_in), F32),
                        pltpu.VMEM((1, n_exp), F32)],
        compiler_params=pltpu.CompilerParams(dimension_semantics=("arbitrary",),
                                             vmem_limit_bytes=VMEM_LIMIT),
        name="mixer_prompt",
    )(x, x, x, *consts)


def _sample_in_kernel(layer, x_ref, sc_ref, nm_ref, win_ref, cw_ref, rlb_ref,
                      yc_ref, nc_ref, f_ref, k_ref, q_ref, v_ref, og_ref):
    d_conv = cw_ref.shape[-1]
    d_rec = rlb_ref.shape[-1]
    u = _project(x_ref[...], nm_ref, win_ref)
    bg, cg, hv, q, fx, iv, og = _split_u(u, d_conv, d_rec)
    bx = bg * hv
    sc = sc_ref[...]
    s0, s1 = sc[:, :d_conv], sc[:, d_conv:]
    cw = cw_ref[...]
    yc_ref[...] = cg * (cw[0:1] * s0 + cw[1:2] * s1 + cw[2:3] * bx)
    nc_ref[...] = jnp.concatenate([s1, bx], axis=-1)
    f, kk = _forget(fx, _lower_bound(rlb_ref, layer))
    f_ref[...] = f
    k_ref[...] = kk
    q_ref[...] = q
    v_ref[...] = iv
    og_ref[...] = og


def _sample_state_kernel(group, f_ref, k_ref, q_ref, v_ref, s_ref, sn_ref, o_ref):
    for j in range(group):
        for h in range(N_HEADS):
            rs = slice(h * HEAD, (h + 1) * HEAD)
            fcol = f_ref[rs, j:j + 1]
            kcol = k_ref[rs, j:j + 1]
            qcol = q_ref[rs, j:j + 1]
            vrow = v_ref[j:j + 1, rs]
            sn = fcol * s_ref[j, h] + kcol * vrow
            sn_ref[j, h] = sn
            o_ref[j:j + 1, rs] = jnp.sum(qcol * sn, axis=0, keepdims=True)


def _sample_tail_kernel(x_ref, yc_ref, o_ref, og_ref, rg_ref, wout_ref, nf_ref, rw_ref, rb_ref,
                        h_ref, xn_ref, sel_ref, route_ref):
    y = jnp.concatenate([yc_ref[...], _rec_out(o_ref[...], og_ref[...], rg_ref[...])], axis=-1)
    h, xn, sel, route = _tail(x_ref[...], y, wout_ref, nf_ref, rw_ref, rb_ref)
    h_ref[...] = h
    xn_ref[...] = xn
    sel_ref[...] = sel
    route_ref[...] = route


def _rank_block(sel, route, carry, cap):
    tb, ne = sel.shape
    row = lax.broadcasted_iota(jnp.int32, (tb, tb), 0)
    col = lax.broadcasted_iota(jnp.int32, (tb, tb), 1)
    before = jnp.where(col < row, 1.0, 0.0).astype(BF16)
    rank = _dot(before, sel.astype(BF16)) + carry[...]
    carry[...] = carry[...] + jnp.sum(sel, axis=0, keepdims=True)
    lane_e = lax.broadcasted_iota(jnp.int32, (tb, ne), 1).astype(F32)
    lane = lax.broadcasted_iota(jnp.int32, (tb, LANES), 1)
    rnk = jnp.zeros((tb, LANES), F32)
    for k in range(TOP_K):
        expert = route[:, k:k + 1]
        mine = jnp.where(lane_e == expert, rank, 0.0)
        rnk = jnp.where(lane == k, expert * float(cap) + jnp.sum(mine, axis=-1, keepdims=True), rnk)
    return rnk


def _dispatch(xn, pos_w, n_slots):
    n_workers, n_chunks, top_k, ch = pos_w.shape
    assert n_workers == SC_CORES * SC_SUBCORES and ch % 8 == 0 and ch <= LANES
    d = xn.shape[1]
    mesh = plsc.VectorSubcoreMesh(core_axis_name="c", subcore_axis_name="s")

    @functools.partial(
        pl.kernel, mesh=mesh,
        out_type=jax.ShapeDtypeStruct((n_slots, d), xn.dtype),
        scratch_types=[pltpu.VMEM((n_chunks, top_k, ch), jnp.int32), pltpu.VMEM((SC_RING, ch, d), xn.dtype),
                       pltpu.SemaphoreType.DMA((SC_RING,)), pltpu.SemaphoreType.DMA((SC_RING,))],
        name="dispatch",
    )
    def run(xn_hbm, pos_hbm, xs_hbm, idx_v, rows_v, sem_r, sem_w):
        wid = lax.axis_index("s") * SC_CORES + lax.axis_index("c")

        def read(c, b):
            src = xn_hbm.at[pl.ds((wid * n_chunks + c) * ch, ch)]
            return pltpu.make_async_copy(src, rows_v.at[b], sem_r.at[b])

        def write(c, b, k):
            return pltpu.make_async_copy(rows_v.at[b], xs_hbm.at[idx_v.at[c, k]], sem_w.at[b])

        pltpu.sync_copy(pos_hbm.at[wid], idx_v)
        ahead = SC_RING - 1
        for c0 in range(min(ahead, n_chunks)):
            read(c0, c0).start()

        @pl.loop(0, n_chunks)
        def _(c):
            b = lax.rem(c, SC_RING)
            read(c, b).wait()
            for k in range(top_k):
                write(c, b, k).start()

            @pl.when(c >= 1)
            def _():
                for k in range(top_k):
                    write(c - 1, lax.rem(c - 1, SC_RING), k).wait()

            @pl.when(c + ahead < n_chunks)
            def _():
                read(c + ahead, lax.rem(c + ahead, SC_RING)).start()

        for k in range(top_k):
            write(n_chunks - 1, (n_chunks - 1) % SC_RING, k).wait()

    return run(xn, pos_w)


def _collect(ys, pos_w):
    n_workers, n_chunks, top_k, ch = pos_w.shape
    assert n_workers == SC_CORES * SC_SUBCORES and ch % 8 == 0 and ch <= LANES
    d = ys.shape[1]
    mesh = plsc.VectorSubcoreMesh(core_axis_name="c", subcore_axis_name="s")

    @functools.partial(
        pl.kernel, mesh=mesh,
        out_type=jax.ShapeDtypeStruct((n_workers * n_chunks * ch, top_k * d), ys.dtype),
        scratch_types=[pltpu.VMEM((n_chunks, top_k, ch), jnp.int32), pltpu.VMEM((top_k, ch, d), ys.dtype),
                       pltpu.SemaphoreType.DMA((top_k,)), pltpu.SemaphoreType.DMA((top_k,))],
        name="collect",
    )
    def run(ys_hbm, pos_hbm, out_hbm, idx_v, rows_v, sem_r, sem_w):
        wid = lax.axis_index("s") * SC_CORES + lax.axis_index("c")

        def read(c, k):
            return pltpu.make_async_copy(ys_hbm.at[idx_v.at[c, k]], rows_v.at[k], sem_r.at[k])

        def write(c, k):
            dst = out_hbm.at[pl.ds((wid * n_chunks + c) * ch, ch), pl.ds(k * d, d)]
            return pltpu.make_async_copy(rows_v.at[k], dst, sem_w.at[k])

        pltpu.sync_copy(pos_hbm.at[wid], idx_v)
        for k in range(top_k - 1):
            read(0, k).start()

        @pl.loop(0, n_chunks)
        def _(c):
            for k in range(top_k):
                read(c, k).wait()
                write(c, k).start()
                if k >= 1:
                    write(c, k - 1).wait()

                    @pl.when(c + 1 < n_chunks)
                    def _():
                        read(c + 1, k - 1).start()
                else:
                    @pl.when(c >= 1)
                    def _():
                        write(c - 1, top_k - 1).wait()
                    read(c, top_k - 1).start()

        write(n_chunks - 1, top_k - 1).wait()

    return run(ys, pos_w)


def _moe_kernel(tm, te_ref, nu_ref, tn_ref, tb_ref, tf_ref, x_ref, w1_hbm, bg_ref, bl_ref, w2_hbm, b2_ref,
                ys_ref, w1f, w2f, sem, w1p, w2b, act):
    i = pl.program_id(0)
    n_used = nu_ref[0]
    d_ff2 = w1f.shape[-1]
    n_blk = d_ff2 // MXU_N
    expert = te_ref[i]

    def weight_copies(e):
        return (pltpu.make_async_copy(w1_hbm.at[e], w1f, sem.at[0]),
                pltpu.make_async_copy(w2_hbm.at[e], w2f, sem.at[1]))

    @pl.when(i == 0)
    def _():
        for cp in weight_copies(expert):
            cp.start()

    @pl.when(i >= n_used)
    def _():
        ys_ref[...] = jnp.zeros_like(ys_ref)

    prev = te_ref[jnp.maximum(i - 1, 0)]
    changed = jnp.logical_or(i == 0, expert != prev)

    @pl.when(jnp.logical_and(changed, i < n_used))
    def _():
        for cp in weight_copies(expert):
            cp.wait()
        r = lax.broadcasted_iota(jnp.int32, (MXU_N, MXU_N), 0)
        c = lax.broadcasted_iota(jnp.int32, (MXU_N, MXU_N), 1)
        src = jnp.where(c < MXU_N // 2, 2 * c, 2 * (c - MXU_N // 2) + 1)
        perm = jnp.where(r == src, 1.0, 0.0).astype(BF16)
        for blk in range(n_blk):
            cs = slice(blk * MXU_N, (blk + 1) * MXU_N)
            w1p[:, cs] = _dot(w1f[:, cs].astype(BF16), perm).astype(BF16)
        w2b[...] = w2f[...].astype(BF16)

        @pl.when(tn_ref[i] != expert)
        def _():
            for cp in weight_copies(tn_ref[i]):
                cp.start(priority=1)

    def expert_mlp(rows):
        x = _unpack_halves(x_ref[0:rows, :]).astype(BF16)
        bg, bl = bg_ref[expert], bl_ref[expert]
        for blk in range(n_blk):
            a = _dot(x, w1p[:, blk * MXU_N:(blk + 1) * MXU_N])
            half = MXU_N // 2
            hs = slice(blk * half, (blk + 1) * half)
            glu = jnp.minimum(a[:, :half] + bg[:, hs], SWIGLU_LIMIT)
            lin = jnp.clip(a[:, half:] + bl[:, hs], -SWIGLU_LIMIT, SWIGLU_LIMIT)
            act[0:rows, hs] = (glu * _sigmoid(SWIGLU_ALPHA * glu) * (lin + 1.0)).astype(BF16)
        ys_ref[0:rows, :] = _pack_halves(_dot(act[0:rows, :], w2b[...]) + b2_ref[expert])

    both = tf_ref[i] == 1

    @pl.when(jnp.logical_and(i < n_used, both))
    def _():
        expert_mlp(tm)

    @pl.when(jnp.logical_and(i < n_used, jnp.logical_not(both)))
    def _():
        expert_mlp(tm // 2)
        ys_ref[tm // 2:tm, :] = jnp.zeros((tm - tm // 2, ys_ref.shape[-1]), ys_ref.dtype)


def _final_kernel(h_ref, gate_ref, nfin_ref, z_ref, y_ref):
    w = z_ref.shape[-1] // TOP_K
    g = gate_ref[...]
    out = h_ref[...]
    for k in range(TOP_K):
        out = out + g[:, TOP_K + k:TOP_K + k + 1] * _unpack_halves(z_ref[:, k * w:(k + 1) * w])
    y_ref[...] = _rms(out, nfin_ref[...])


def _final_call(h_all, gate, nfin, z, first_row, n_rows, tile):
    d = h_all.shape[-1]
    off = first_row // tile
    assert first_row % tile == 0 and n_rows % tile == 0
    return pl.pallas_call(
        _final_kernel,
        grid=(n_rows // tile,),
        in_specs=[
            pl.BlockSpec((tile, d), lambda i: (i + off, 0)),
            pl.BlockSpec((tile, gate.shape[-1]), lambda i: (i + off, 0)),
            pl.BlockSpec((1, d), lambda i: (0, 0)),
            pl.BlockSpec((tile, z.shape[-1]), lambda i: (i + off, 0)),
        ],
        out_specs=pl.BlockSpec((tile, d), lambda i: (i, 0)),
        out_shape=jax.ShapeDtypeStruct((n_rows, d), F32),
        compiler_params=pltpu.CompilerParams(dimension_semantics=("arbitrary",),
                                             vmem_limit_bytes=VMEM_LIMIT),
        name="final",
    )(h_all, gate, nfin, z)


def kernel(x_prompt, x_sample, state_conv, state_rec, meta_tokens, norm_mix, w_in, conv_w,
           rec_lower_bound, rec_norm, w_out, norm_ffn, router_w, router_b, expert_w1, expert_b1,
           expert_w2, expert_b2, norm_final):
    depth = norm_mix.shape[0]
    assert depth == 1, "single-layer step"
    layer = 0
    bp, seq, d = x_prompt.shape
    ns = x_sample.shape[0]
    assert x_sample.shape[1] == 1 and ns == TOKEN_TILE
    d_conv = conv_w.shape[-1]
    d_rec = rec_lower_bound.shape[-1]
    assert state_conv.shape[2] == 2 and d_rec == N_HEADS * HEAD
    n_exp = router_w.shape[-1]
    d_ff = expert_w2.shape[2]
    n_prompt = bp * seq
    n_tok = n_prompt + ns
    assert seq % PROMPT_TILE == 0 and n_prompt % TOKEN_TILE == 0

    nm = norm_mix[layer][None]
    win = w_in[layer].astype(BF16)
    cw = conv_w[layer]
    rlb = rec_lower_bound
    rg = rec_norm[layer][None]
    wout = w_out[layer].astype(BF16)
    nf = norm_ffn[layer][None]
    rw = router_w[layer]
    rb = router_b[layer][None]
    mix_w = (nm, win, cw, rlb)
    tail_w = (rg, wout, nf, rw, rb)

    lb = jnp.sum(jax.nn.softmax(rlb, axis=0)[:layer + 1], axis=0)
    fast = CHUNK * jnp.max(-jnp.log(lb)) < DECAY_LIMIT

    st_meta, cv_meta = lax.cond(
        fast, lambda: _meta_call(meta_tokens, mix_w, layer, N_META),
        lambda: _meta_call(meta_tokens, mix_w, layer, None))

    xs = x_sample.reshape(ns, d)
    wide = jax.ShapeDtypeStruct((ns, d_rec), F32)
    y_conv_s, new_conv_s, f_s, k_s, q_s, v_s, og_s = pl.pallas_call(
        functools.partial(_sample_in_kernel, layer),
        out_shape=[jax.ShapeDtypeStruct((ns, d_conv), F32), jax.ShapeDtypeStruct((ns, 2 * d_conv), F32),
                   wide, wide, wide, wide, wide],
        compiler_params=pltpu.CompilerParams(vmem_limit_bytes=VMEM_LIMIT),
        name="sample_in",
    )(xs, state_conv[layer].reshape(ns, 2 * d_conv), nm, win, cw, rlb)

    group = 8
    n_grp = ns // group

    def cols(a):
        return a.T.reshape(d_rec, n_grp, group).transpose(1, 0, 2)

    col_spec = pl.BlockSpec((None, d_rec, group), lambda g: (g, 0, 0))
    st_spec = pl.BlockSpec((group, N_HEADS, HEAD, HEAD), lambda g: (g, 0, 0, 0))
    row_spec = pl.BlockSpec((group, d_rec), lambda g: (g, 0))
    new_rec_s, o_s = pl.pallas_call(
        functools.partial(_sample_state_kernel, group),
        grid=(n_grp,),
        in_specs=[col_spec, col_spec, col_spec, row_spec, st_spec],
        out_specs=[st_spec, row_spec],
        out_shape=[jax.ShapeDtypeStruct(state_rec.shape[1:], F32), wide],
        compiler_params=pltpu.CompilerParams(dimension_semantics=("arbitrary",),
                                             vmem_limit_bytes=VMEM_LIMIT),
        name="sample_state",
    )(cols(f_s), cols(k_s), cols(q_s), v_s, state_rec[layer])

    decode = pl.pallas_call(
        _sample_tail_kernel,
        out_shape=[jax.ShapeDtypeStruct((ns, d), F32), jax.ShapeDtypeStruct((ns, d), F32),
                   jax.ShapeDtypeStruct((ns, n_exp), F32), jax.ShapeDtypeStruct((ns, LANES), F32)],
        compiler_params=pltpu.CompilerParams(vmem_limit_bytes=VMEM_LIMIT),
        name="sample_tail",
    )(xs, y_conv_s, o_s, og_s, rg, wout, nf, rw, rb)

    h_all, xn_all, route, rnk, counts, new_rec_p, new_conv_p = lax.cond(
        fast,
        lambda: _prompt_call(x_prompt, st_meta, cv_meta, mix_w, tail_w, decode, layer, PROMPT_TILE, CHUNK),
        lambda: _prompt_call(x_prompt, st_meta, cv_meta, mix_w, tail_w, decode, layer, PROMPT_TILE, None))

    tm = MOE_TILE
    cap = h_all.shape[0]
    assert cap % tm == 0
    n_tiles = (n_tok * TOP_K) // tm + n_exp
    counts = counts[0].astype(jnp.int32)
    tiles_e = (counts + tm - 1) // tm
    tile_end = jnp.cumsum(tiles_e)
    n_used = tile_end[-1]
    tile_ids = jnp.arange(n_tiles, dtype=jnp.int32)

    def expert_of(tile):
        return jnp.minimum(jnp.sum((tile_end[None, :] <= tile[:, None]).astype(jnp.int32), axis=1), n_exp - 1)

    tile_expert = expert_of(jnp.minimum(tile_ids, n_used - 1))
    mine = tile_expert[:, None] == jnp.arange(n_exp, dtype=jnp.int32)[None, :]
    after = jnp.sum(jnp.where(mine, tile_end[None, :], 0), axis=1)
    first = jnp.sum(jnp.where(mine, (tile_end - tiles_e)[None, :], 0), axis=1)
    next_expert = jnp.where(after < n_used, expert_of(after), tile_expert)
    n_rows = cap
    n_spare = (n_rows - n_tok) * TOP_K
    spare_blocks = -(-n_spare // tm)
    dummy_block = n_exp * (cap // tm) + spare_blocks
    n_slots = (dummy_block + 1) * tm
    tile_block = jnp.where(tile_ids < n_used, tile_expert * (cap // tm) + tile_ids - first, dummy_block)
    rows_left = jnp.sum(jnp.where(mine, counts[None, :], 0), axis=1) - (tile_ids - first) * tm
    tile_full = (rows_left > tm // 2).astype(jnp.int32)

    n_workers = SC_CORES * SC_SUBCORES
    assert n_rows % (n_workers * DISPATCH_CHUNK) == 0
    pos_rows = rnk[:, :TOP_K].astype(jnp.int32)

    def to_workers(p, ch):
        assert p.shape[0] % (n_workers * ch) == 0
        return p.reshape(n_workers, -1, ch, TOP_K).transpose(0, 1, 3, 2)

    xs = _dispatch(xn_all, to_workers(pos_rows, DISPATCH_CHUNK), n_slots)

    w1 = expert_w1[layer]
    w2 = expert_w2[layer]
    b1 = expert_b1[layer]
    b1g = b1[:, 0::2][:, None, :]
    b1l = b1[:, 1::2][:, None, :]
    b2 = expert_b2[layer][:, None, :]
    ys = pl.pallas_call(
        functools.partial(_moe_kernel, tm),
        grid_spec=pltpu.PrefetchScalarGridSpec(
            num_scalar_prefetch=5,
            grid=(n_tiles,),
            in_specs=[
                pl.BlockSpec((tm, d // 2), lambda i, te, nu, tn, tb, tf: (tb[i], 0)),
                pl.BlockSpec(memory_space=pl.ANY),
                _const_spec(b1g.shape), _const_spec(b1l.shape),
                pl.BlockSpec(memory_space=pl.ANY),
                _const_spec(b2.shape),
            ],
            out_specs=pl.BlockSpec((tm, d // 2), lambda i, te, nu, tn, tb, tf: (tb[i], 0)),
            scratch_shapes=[pltpu.VMEM((d, 2 * d_ff), F32), pltpu.VMEM((d_ff, d), F32),
                            pltpu.SemaphoreType.DMA((2,)),
                            pltpu.VMEM((d, 2 * d_ff), BF16), pltpu.VMEM((d_ff, d), BF16),
                            pltpu.VMEM((tm, d_ff), BF16)],
        ),
        out_shape=jax.ShapeDtypeStruct((n_slots, d // 2), jnp.uint32),
        compiler_params=pltpu.CompilerParams(dimension_semantics=("arbitrary",),
                                             vmem_limit_bytes=VMEM_LIMIT),
        name="moe",
    )(tile_expert, n_used[None].astype(jnp.int32), next_expert, tile_block, tile_full,
      xs, w1, b1g, b1l, w2, b2)

    z = _collect(ys, to_workers(pos_rows, DISPATCH_CHUNK))
    y_p = _final_call(h_all, route, norm_final[None], z, 0, n_prompt, FINAL_TILE)
    y_s = _final_call(h_all, route, norm_final[None], z, n_prompt, ns, TOKEN_TILE)

    return (y_p.reshape(bp, seq, d), y_s.reshape(ns, 1, d),
            new_conv_p[None], new_rec_p[None],
            new_conv_s.reshape(1, ns, 2, d_conv), new_rec_s[None])
```

```python
import functools

import jax
import jax.numpy as jnp
from jax import lax
from jax.experimental import pallas as pl
from jax.experimental.pallas import tpu as pltpu
from jax.experimental.pallas import tpu_sc as plsc

F32 = jnp.float32
BF16 = jnp.bfloat16

N_HEADS = 4
HEAD = 128
N_META = 16
CHUNK = 64
SAFE_CHUNK = 16
DECAY_LIMIT = 80.0
TOP_K = 4
SWIGLU_LIMIT = 7.0
SWIGLU_ALPHA = 1.702
EPS = 1e-5

LANES = 128
MXU_N = 256
PROMPT_TILE = 256
TOKEN_TILE = 128
FINAL_TILE = 512
MOE_TILE = 512
MOE_PARTS = 4
SC_CORES = 2
SC_SUBCORES = 16
DISPATCH_CHUNK = 24
SC_RING = 4
VMEM_LIMIT = 56 * 1024 * 1024


def _dot(a, b):
    return jnp.dot(a, b, preferred_element_type=F32)


def _dot_nt(a, b):
    return lax.dot_general(a, b, (((1,), (1,)), ((), ())), preferred_element_type=F32)


def _dot_tn(a, b):
    return lax.dot_general(a, b, (((0,), (0,)), ((), ())), preferred_element_type=F32)


def _pack_halves(x):
    n = x.shape[-1] // 2
    lo = pltpu.bitcast(x[:, :n].astype(BF16).astype(F32), jnp.uint32)
    hi = pltpu.bitcast(x[:, n:].astype(BF16).astype(F32), jnp.uint32)
    return (lo >> 16) | (hi & jnp.uint32(0xFFFF0000))


def _unpack_halves(u):
    lo = pltpu.bitcast(u << 16, F32)
    hi = pltpu.bitcast(u & jnp.uint32(0xFFFF0000), F32)
    return jnp.concatenate([lo, hi], axis=-1)


def _sigmoid(x):
    return 1.0 / (1.0 + jnp.exp(-x))


def _rms(x, g):
    ms = jnp.mean(x * x, axis=-1, keepdims=True)
    return x * lax.rsqrt(ms + EPS) * g


def _project(x, nm_ref, win_ref):
    return _dot(_rms(x, nm_ref[...]).astype(BF16), win_ref[...])


def _split_u(u, d_conv, d_rec):
    pts = [0, d_conv, 2 * d_conv, 3 * d_conv, 3 * d_conv + d_rec, 3 * d_conv + 2 * d_rec,
           3 * d_conv + 3 * d_rec, 3 * d_conv + 4 * d_rec]
    return [u[:, pts[i]:pts[i + 1]] for i in range(7)]


def _lower_bound(rlb_ref, layer):
    r = rlb_ref[...]
    e = jnp.exp(r - jnp.max(r, axis=0, keepdims=True))
    return jnp.sum(e[0:layer + 1], axis=0, keepdims=True) / jnp.sum(e, axis=0, keepdims=True)


def _forget(fx, lb):
    f = lb + (1.0 - lb) * _sigmoid(fx)
    return f, 1.0 - f


def _rec_out(o, og, rg):
    parts = []
    for h in range(N_HEADS):
        oh = o[:, h * HEAD:(h + 1) * HEAD]
        parts.append(oh * lax.rsqrt(jnp.mean(oh * oh, axis=-1, keepdims=True) + EPS))
    return jnp.concatenate(parts, axis=-1) * rg * (og * _sigmoid(og))


def _route(logits):
    m_rows, n = logits.shape
    lane = lax.broadcasted_iota(jnp.int32, logits.shape, 1).astype(F32)
    work = logits
    tops, firsts = [], []
    sel = jnp.zeros_like(logits)
    for _ in range(TOP_K):
        m = jnp.max(work, axis=-1, keepdims=True)
        first = jnp.min(jnp.where(work == m, lane, float(n)), axis=-1, keepdims=True)
        hot = lane == first
        tops.append(m)
        firsts.append(first)
        sel = sel + jnp.where(hot, 1.0, 0.0)
        work = jnp.where(hot, -jnp.inf, work)
    es = [jnp.exp(t - tops[0]) for t in tops]
    den = es[0]
    for e in es[1:]:
        den = den + e
    wide = lax.broadcasted_iota(jnp.int32, (m_rows, LANES), 1)
    route = jnp.zeros((m_rows, LANES), F32)
    for k in range(TOP_K):
        route = jnp.where(wide == k, firsts[k], route)
        route = jnp.where(wide == TOP_K + k, es[k] / den, route)
    return sel, route


def _tail(x, y, wout_ref, nf_ref, rw_ref, rb_ref):
    h = x + _dot(y.astype(BF16), wout_ref[...])
    xn = _rms(h, nf_ref[...])
    xh = xn.astype(BF16)
    xl = (xn - xh.astype(F32)).astype(BF16)
    rw = rw_ref[...]
    wh = rw.astype(BF16)
    wl = (rw - wh.astype(F32)).astype(BF16)
    logits = _dot(xh, wh) + _dot(xh, wl) + _dot(xl, wh) + rb_ref[...]
    sel, route = _route(logits)
    return h, xn, sel, route


def _cumsum_rows(x):
    n = x.shape[0]
    row = lax.broadcasted_iota(jnp.int32, (n, 1), 0)
    shift = 1
    while shift < n:
        x = x + jnp.where(row >= shift, pltpu.roll(x, shift, axis=0), 0.0)
        shift *= 2
    return x


def _rec_chunk(q, kk, v, lf, st_ref, causal):
    c = q.shape[0]
    b = _cumsum_rows(lf)
    eb = jnp.exp(b)
    qe = (q * eb).astype(BF16)
    ke = (kk * jnp.exp(-b)).astype(BF16)
    vb = v.astype(BF16)
    eb_last = eb[c - 1:c]
    outs = []
    for h in range(N_HEADS):
        sl = slice(h * HEAD, (h + 1) * HEAD)
        st = st_ref[h]
        sc = jnp.where(causal, _dot_nt(qe[:, sl], ke[:, sl]), 0.0)
        outs.append(_dot(sc.astype(BF16), vb[:, sl]) + _dot_nt(qe[:, sl], st.astype(BF16)))
        st_ref[h] = (st + _dot_tn(vb[:, sl], ke[:, sl])) * eb_last[:, sl]
    return jnp.concatenate(outs, axis=-1)


def _rec_chunk_safe(q, kk, v, lf, st_ref):
    c = q.shape[0]
    row = lax.broadcasted_iota(jnp.int32, (c, 1), 0)
    b = _cumsum_rows(lf)
    eb = jnp.exp(b)
    qe = (q * eb).astype(BF16)
    kl = (kk * jnp.exp(b[c - 1:c] - b)).astype(BF16)
    vb = v.astype(BF16)
    outs = []
    for h in range(N_HEADS):
        sl = slice(h * HEAD, (h + 1) * HEAD)
        st = st_ref[h]
        bh, kh, qh, vh = b[:, sl], kk[:, sl], q[:, sl], v[:, sl]
        intra = jnp.zeros((c, HEAD), F32)
        for t in range(c):
            w = kh * jnp.exp(jnp.where(row <= t, bh[t:t + 1] - bh, -jnp.inf))
            score = jnp.sum(w * qh[t:t + 1], axis=-1, keepdims=True)
            intra = jnp.where(row == t, jnp.sum(score * vh, axis=0, keepdims=True), intra)
        outs.append(intra + _dot_nt(qe[:, sl], st.astype(BF16)))
        st_ref[h] = st * eb[c - 1:c, sl] + _dot_tn(vb[:, sl], kl[:, sl])
    return jnp.concatenate(outs, axis=-1)


def _mix_tile(u, layer, chunk, cw_ref, rlb_ref, convbuf, st, cv_out):
    tm = u.shape[0]
    d_conv = cw_ref.shape[-1]
    d_rec = rlb_ref.shape[-1]
    bg, cg, hv, q, fx, iv, og = _split_u(u, d_conv, d_rec)

    bx = bg * hv
    convbuf[8:8 + tm, :] = bx
    cw = cw_ref[...]
    conv = cw[0:1] * convbuf[6:6 + tm, :] + cw[1:2] * convbuf[7:7 + tm, :] + cw[2:3] * bx
    convbuf[6:8, :] = bx[tm - 2:tm]
    cv_out[...] = bx[tm - 2:tm]

    f, kk = _forget(fx, _lower_bound(rlb_ref, layer))
    lf = jnp.log(f)
    outs = []
    if chunk is None:
        for c in range(tm // SAFE_CHUNK):
            rs = slice(c * SAFE_CHUNK, (c + 1) * SAFE_CHUNK)
            outs.append(_rec_chunk_safe(q[rs], kk[rs], iv[rs], lf[rs], st))
    else:
        row = lax.broadcasted_iota(jnp.int32, (chunk, chunk), 0)
        col = lax.broadcasted_iota(jnp.int32, (chunk, chunk), 1)
        causal = row >= col
        for c in range(tm // chunk):
            rs = slice(c * chunk, (c + 1) * chunk)
            outs.append(_rec_chunk(q[rs], kk[rs], iv[rs], lf[rs], st, causal))
    return cg * conv, jnp.concatenate(outs, axis=0), og


def _load_state(st, convbuf, s0_ref, c0_ref):
    for h in range(N_HEADS):
        st[h] = s0_ref[h].T
    convbuf[6:8, :] = c0_ref[...]


def _meta_kernel(layer, chunk, x_ref, s0_ref, c0_ref, nm_ref, win_ref, cw_ref, rlb_ref, st_out, cv_out,
                 convbuf, st):
    _load_state(st, convbuf, s0_ref, c0_ref)
    _mix_tile(_project(x_ref[...], nm_ref, win_ref), layer, chunk, cw_ref, rlb_ref, convbuf, st, cv_out)
    for h in range(N_HEADS):
        st_out[h] = st[h].T


def _prompt_kernel(layer, tm, chunk, n_pairs, pairs_per_seq, cap,
                   xr0_ref, x1_ref, xp2_ref, s0_ref, c0_ref, nm_ref, win_ref, cw_ref, rlb_ref,
                   rg_ref, wout_ref, nf_ref, rw_ref, rb_ref, hs_ref, xns_ref, sels_ref, routes_ref,
                   h_ref, xn_ref, route_ref, rnk_ref, cnt_ref, st_out, cv_out,
                   convbuf, st, ua, ub, carry):
    step = pl.program_id(0)
    live = step < n_pairs

    @pl.when(step == 0)
    def _():
        ua[...] = _project(xr0_ref[...], nm_ref, win_ref)
        carry[...] = jnp.zeros_like(carry)

    @pl.when(jnp.logical_and(lax.rem(step, pairs_per_seq) == 0, live))
    def _():
        _load_state(st, convbuf, s0_ref, c0_ref)

    @pl.when(live)
    def _():
        def finish(x, u, rows):
            y_conv, o, og = _mix_tile(u, layer, chunk, cw_ref, rlb_ref, convbuf, st, cv_out)
            y = jnp.concatenate([y_conv, _rec_out(o, og, rg_ref[...])], axis=-1)
            h, xn, sel, route = _tail(x, y, wout_ref, nf_ref, rw_ref, rb_ref)
            h_ref[rows, :] = h
            xn_ref[rows, :] = _pack_halves(xn)
            route_ref[rows, :] = route
            rnk_ref[rows, :] = _rank_block(sel, route, carry, cap)

        ub[...] = _project(x1_ref[...], nm_ref, win_ref)
        finish(xr0_ref[...], ua[...], slice(0, tm))
        ua[...] = _project(xp2_ref[...], nm_ref, win_ref)
        finish(x1_ref[...], ub[...], slice(tm, 2 * tm))

    @pl.when(jnp.logical_and(lax.rem(step, pairs_per_seq) == pairs_per_seq - 1, live))
    def _():
        for h in range(N_HEADS):
            st_out[h] = st[h].T

    @pl.when(step == n_pairs)
    def _():
        ns = hs_ref.shape[0]
        n_exp = sels_ref.shape[-1]
        pad = 2 * tm - ns
        spare = (lax.broadcasted_iota(jnp.int32, (pad, LANES), 0) * TOP_K
                 + lax.broadcasted_iota(jnp.int32, (pad, LANES), 1) + n_exp * cap).astype(F32)
        rnk_ref[0:ns, :] = _rank_block(sels_ref[...], routes_ref[...], carry, cap)
        rnk_ref[ns:2 * tm, :] = spare
        for dst, val in ((h_ref, hs_ref[...]), (xn_ref, _pack_halves(xns_ref[...])),
                         (route_ref, routes_ref[...])):
            dst[0:ns, :] = val
            dst[ns:2 * tm, :] = jnp.zeros((pad, dst.shape[-1]), dst.dtype)
        cnt_ref[...] = carry[...]


def _const_spec(shape):
    return pl.BlockSpec(shape, lambda *_: (0,) * len(shape))


def _meta_call(x, weights, layer, chunk):
    assert x.shape[0] % (chunk or SAFE_CHUNK) == 0
    nm, win, cw, rlb = weights
    d_conv = cw.shape[-1]
    s0 = jnp.zeros((N_HEADS, HEAD, HEAD), F32)
    c0 = jnp.zeros((2, d_conv), F32)
    return pl.pallas_call(
        functools.partial(_meta_kernel, layer, chunk),
        out_shape=[jax.ShapeDtypeStruct(s0.shape, F32), jax.ShapeDtypeStruct(c0.shape, F32)],
        scratch_shapes=[pltpu.VMEM((x.shape[0] + 8, d_conv), F32), pltpu.VMEM(s0.shape, F32)],
        compiler_params=pltpu.CompilerParams(vmem_limit_bytes=VMEM_LIMIT),
        name="mixer_meta",
    )(x, s0, c0, nm, win, cw, rlb)


def _prompt_call(x, s0, c0, weights, tail_w, decode, layer, tm, chunk):
    nseq, length, d = x.shape
    nt = length // tm
    assert nt % 2 == 0 and decode[0].shape[0] <= 2 * tm and tm % (chunk or SAFE_CHUNK) == 0
    pairs_per_seq = nt // 2
    n_pairs = nseq * pairs_per_seq
    nm, win, cw, rlb = weights
    d_conv = cw.shape[-1]
    d_in = win.shape[-1]

    def tile_spec(offset):
        def index(s):
            tile = jnp.minimum(2 * s + offset, 2 * n_pairs - 1)
            return (tile // nt, lax.rem(tile, nt), 0)
        return pl.BlockSpec((None, tm, d), index)

    def seq_of(s):
        return jnp.minimum(s // pairs_per_seq, nseq - 1)

    consts = [s0, c0, nm, win, cw, rlb] + list(tail_w) + list(decode)
    n_exp = decode[2].shape[-1]
    tok = [(d, F32), (d // 2, jnp.uint32), (LANES, F32), (LANES, F32)]
    n_steps = n_pairs + 1
    return pl.pallas_call(
        functools.partial(_prompt_kernel, layer, tm, chunk, n_pairs, pairs_per_seq, n_steps * 2 * tm),
        grid=(n_steps,),
        in_specs=[tile_spec(0), tile_spec(1), tile_spec(2)] + [_const_spec(a.shape) for a in consts],
        out_specs=[pl.BlockSpec((2 * tm, w), lambda s: (s, 0)) for w, _ in tok] + [
            pl.BlockSpec((1, n_exp), lambda s: (0, 0)),
            pl.BlockSpec((None,) + s0.shape, lambda s: (seq_of(s), 0, 0, 0)),
            pl.BlockSpec((None,) + c0.shape, lambda s: (seq_of(s), 0, 0))],
        out_shape=[jax.ShapeDtypeStruct((n_steps * 2 * tm, w), t) for w, t in tok] + [
            jax.ShapeDtypeStruct((1, n_exp), F32),
            jax.ShapeDtypeStruct((nseq,) + s0.shape, F32), jax.ShapeDtypeStruct((nseq,) + c0.shape, F32)],
        scratch_shapes=[pltpu.VMEM((tm + 8, d_conv), F32), pltpu.VMEM(s0.shape, F32),
                        pltpu.VMEM((tm, d_in), F32), pltpu.VMEM((tm, d_in), F32),
                        pltpu.VMEM((1, n_exp), F32)],
        compiler_params=pltpu.CompilerParams(dimension_semantics=("arbitrary",),
                                             vmem_limit_bytes=VMEM_LIMIT),
        name="mixer_prompt",
    )(x, x, x, *consts)


def _sample_in_kernel(layer, x_ref, sc_ref, nm_ref, win_ref, cw_ref, rlb_ref,
                      yc_ref, nc_ref, f_ref, k_ref, q_ref, v_ref, og_ref):
    d_conv = cw_ref.shape[-1]
    d_rec = rlb_ref.shape[-1]
    u = _project(x_ref[...], nm_ref, win_ref)
    bg, cg, hv, q, fx, iv, og = _split_u(u, d_conv, d_rec)
    bx = bg * hv
    sc = sc_ref[...]
    s0, s1 = sc[:, :d_conv], sc[:, d_conv:]
    cw = cw_ref[...]
    yc_ref[...] = cg * (cw[0:1] * s0 + cw[1:2] * s1 + cw[2:3] * bx)
    nc_ref[...] = jnp.concatenate([s1, bx], axis=-1)
    f, kk = _forget(fx, _lower_bound(rlb_ref, layer))
    f_ref[...] = f
    k_ref[...] = kk
    q_ref[...] = q
    v_ref[...] = iv
    og_ref[...] = og


def _sample_state_kernel(group, f_ref, k_ref, q_ref, v_ref, s_ref, sn_ref, o_ref):
    for j in range(group):
        for h in range(N_HEADS):
            rs = slice(h * HEAD, (h + 1) * HEAD)
            fcol = f_ref[rs, j:j + 1]
            kcol = k_ref[rs, j:j + 1]
            qcol = q_ref[rs, j:j + 1]
            vrow = v_ref[j:j + 1, rs]
            sn = fcol * s_ref[j, h] + kcol * vrow
            sn_ref[j, h] = sn
            o_ref[j:j + 1, rs] = jnp.sum(qcol * sn, axis=0, keepdims=True)


def _sample_tail_kernel(x_ref, yc_ref, o_ref, og_ref, rg_ref, wout_ref, nf_ref, rw_ref, rb_ref,
                        h_ref, xn_ref, sel_ref, route_ref):
    y = jnp.concatenate([yc_ref[...], _rec_out(o_ref[...], og_ref[...], rg_ref[...])], axis=-1)
    h, xn, sel, route = _tail(x_ref[...], y, wout_ref, nf_ref, rw_ref, rb_ref)
    h_ref[...] = h
    xn_ref[...] = xn
    sel_ref[...] = sel
    route_ref[...] = route


def _rank_block(sel, route, carry, cap):
    tb, ne = sel.shape
    row = lax.broadcasted_iota(jnp.int32, (tb, tb), 0)
    col = lax.broadcasted_iota(jnp.int32, (tb, tb), 1)
    before = jnp.where(col < row, 1.0, 0.0).astype(BF16)
    rank = _dot(before, sel.astype(BF16)) + carry[...]
    carry[...] = carry[...] + jnp.sum(sel, axis=0, keepdims=True)
    lane_e = lax.broadcasted_iota(jnp.int32, (tb, ne), 1).astype(F32)
    lane = lax.broadcasted_iota(jnp.int32, (tb, LANES), 1)
    rnk = jnp.zeros((tb, LANES), F32)
    for k in range(TOP_K):
        expert = route[:, k:k + 1]
        mine = jnp.where(lane_e == expert, rank, 0.0)
        rnk = jnp.where(lane == k, expert * float(cap) + jnp.sum(mine, axis=-1, keepdims=True), rnk)
    return rnk


def _dispatch(xn, pos_w, n_slots):
    n_workers, n_chunks, top_k, ch = pos_w.shape
    assert n_workers == SC_CORES * SC_SUBCORES and ch % 8 == 0 and ch <= LANES
    d = xn.shape[1]
    mesh = plsc.VectorSubcoreMesh(core_axis_name="c", subcore_axis_name="s")

    @functools.partial(
        pl.kernel, mesh=mesh,
        out_type=jax.ShapeDtypeStruct((n_slots, d), xn.dtype),
        scratch_types=[pltpu.VMEM((n_chunks, top_k, ch), jnp.int32), pltpu.VMEM((SC_RING, ch, d), xn.dtype),
                       pltpu.SemaphoreType.DMA((SC_RING,)), pltpu.SemaphoreType.DMA((SC_RING,))],
        name="dispatch",
    )
    def run(xn_hbm, pos_hbm, xs_hbm, idx_v, rows_v, sem_r, sem_w):
        wid = lax.axis_index("s") * SC_CORES + lax.axis_index("c")

        def read(c, b):
            src = xn_hbm.at[pl.ds((wid * n_chunks + c) * ch, ch)]
            return pltpu.make_async_copy(src, rows_v.at[b], sem_r.at[b])

        def write(c, b, k):
            return pltpu.make_async_copy(rows_v.at[b], xs_hbm.at[idx_v.at[c, k]], sem_w.at[b])

        pltpu.sync_copy(pos_hbm.at[wid], idx_v)
        ahead = SC_RING - 1
        for c0 in range(min(ahead, n_chunks)):
            read(c0, c0).start()

        @pl.loop(0, n_chunks)
        def _(c):
            b = lax.rem(c, SC_RING)
            read(c, b).wait()
            for k in range(top_k):
                write(c, b, k).start()

            @pl.when(c >= 1)
            def _():
                for k in range(top_k):
                    write(c - 1, lax.rem(c - 1, SC_RING), k).wait()

            @pl.when(c + ahead < n_chunks)
            def _():
                read(c + ahead, lax.rem(c + ahead, SC_RING)).start()

        for k in range(top_k):
            write(n_chunks - 1, (n_chunks - 1) % SC_RING, k).wait()

    return run(xn, pos_w)


def _collect(ys, pos_w):
    n_workers, n_chunks, top_k, ch = pos_w.shape
    assert n_workers == SC_CORES * SC_SUBCORES and ch % 8 == 0 and ch <= LANES
    d = ys.shape[1]
    mesh = plsc.VectorSubcoreMesh(core_axis_name="c", subcore_axis_name="s")

    @functools.partial(
        pl.kernel, mesh=mesh,
        out_type=jax.ShapeDtypeStruct((n_workers * n_chunks * ch, top_k * d), ys.dtype),
        scratch_types=[pltpu.VMEM((n_chunks, top_k, ch), jnp.int32), pltpu.VMEM((top_k, ch, d), ys.dtype),
                       pltpu.SemaphoreType.DMA((top_k,)), pltpu.SemaphoreType.DMA((top_k,))],
        name="collect",
    )
    def run(ys_hbm, pos_hbm, out_hbm, idx_v, rows_v, sem_r, sem_w):
        wid = lax.axis_index("s") * SC_CORES + lax.axis_index("c")

        def read(c, k):
            return pltpu.make_async_copy(ys_hbm.at[idx_v.at[c, k]], rows_v.at[k], sem_r.at[k])

        def write(c, k):
            dst = out_hbm.at[pl.ds((wid * n_chunks + c) * ch, ch), pl.ds(k * d, d)]
            return pltpu.make_async_copy(rows_v.at[k], dst, sem_w.at[k])

        pltpu.sync_copy(pos_hbm.at[wid], idx_v)
        for k in range(top_k - 1):
            read(0, k).start()

        @pl.loop(0, n_chunks)
        def _(c):
            for k in range(top_k):
                read(c, k).wait()
                write(c, k).start()
                if k >= 1:
                    write(c, k - 1).wait()

                    @pl.when(c + 1 < n_chunks)
                    def _():
                        read(c + 1, k - 1).start()
                else:
                    @pl.when(c >= 1)
                    def _():
                        write(c - 1, top_k - 1).wait()
                    read(c, top_k - 1).start()

        write(n_chunks - 1, top_k - 1).wait()

    return run(ys, pos_w)


def _moe_kernel(tm, te_ref, nu_ref, tn_ref, tb_ref, tf_ref, x_ref, w1_hbm, bg_ref, bl_ref, w2_hbm, b2_ref,
                ys_ref, w1f, w2f, sem, w1p, w2b, act):
    i = pl.program_id(0)
    n_used = nu_ref[0]
    d_ff2 = w1f.shape[-1]
    n_blk = d_ff2 // MXU_N
    expert = te_ref[i]

    def weight_copies(e):
        return (pltpu.make_async_copy(w1_hbm.at[e], w1f, sem.at[0]),
                pltpu.make_async_copy(w2_hbm.at[e], w2f, sem.at[1]))

    @pl.when(i == 0)
    def _():
        for cp in weight_copies(expert):
            cp.start()

    @pl.when(i >= n_used)
    def _():
        ys_ref[...] = jnp.zeros_like(ys_ref)

    prev = te_ref[jnp.maximum(i - 1, 0)]
    changed = jnp.logical_or(i == 0, expert != prev)

    @pl.when(jnp.logical_and(changed, i < n_used))
    def _():
        for cp in weight_copies(expert):
            cp.wait()
        r = lax.broadcasted_iota(jnp.int32, (MXU_N, MXU_N), 0)
        c = lax.broadcasted_iota(jnp.int32, (MXU_N, MXU_N), 1)
        src = jnp.where(c < MXU_N // 2, 2 * c, 2 * (c - MXU_N // 2) + 1)
        perm = jnp.where(r == src, 1.0, 0.0).astype(BF16)
        for blk in range(n_blk):
            cs = slice(blk * MXU_N, (blk + 1) * MXU_N)
            w1p[:, cs] = _dot(w1f[:, cs].astype(BF16), perm).astype(BF16)
        w2b[...] = w2f[...].astype(BF16)

        @pl.when(tn_ref[i] != expert)
        def _():
            for cp in weight_copies(tn_ref[i]):
                cp.start(priority=1)

    def expert_mlp(rows):
        x = _unpack_halves(x_ref[0:rows, :]).astype(BF16)
        bg, bl = bg_ref[expert], bl_ref[expert]
        for blk in range(n_blk):
            a = _dot(x, w1p[:, blk * MXU_N:(blk + 1) * MXU_N])
            half = MXU_N // 2
            hs = slice(blk * half, (blk + 1) * half)
            glu = jnp.minimum(a[:, :half] + bg[:, hs], SWIGLU_LIMIT)
            lin = jnp.clip(a[:, half:] + bl[:, hs], -SWIGLU_LIMIT, SWIGLU_LIMIT)
            act[0:rows, hs] = (glu * _sigmoid(SWIGLU_ALPHA * glu) * (lin + 1.0)).astype(BF16)
        ys_ref[0:rows, :] = _pack_halves(_dot(act[0:rows, :], w2b[...]) + b2_ref[expert])

    part = tm // MOE_PARTS
    for parts in range(1, MOE_PARTS + 1):
        @pl.when(jnp.logical_and(i < n_used, tf_ref[i] == parts))
        def _(rows=parts * part):
            expert_mlp(rows)
            if rows < tm:
                ys_ref[rows:tm, :] = jnp.zeros((tm - rows, ys_ref.shape[-1]), ys_ref.dtype)


def _final_kernel(h_ref, gate_ref, nfin_ref, z_ref, y_ref):
    w = z_ref.shape[-1] // TOP_K
    g = gate_ref[...]
    out = h_ref[...]
    for k in range(TOP_K):
        out = out + g[:, TOP_K + k:TOP_K + k + 1] * _unpack_halves(z_ref[:, k * w:(k + 1) * w])
    y_ref[...] = _rms(out, nfin_ref[...])


def _final_call(h_all, gate, nfin, z, first_row, n_rows, tile):
    d = h_all.shape[-1]
    off = first_row // tile
    assert first_row % tile == 0 and n_rows % tile == 0
    return pl.pallas_call(
        _final_kernel,
        grid=(n_rows // tile,),
        in_specs=[
            pl.BlockSpec((tile, d), lambda i: (i + off, 0)),
            pl.BlockSpec((tile, gate.shape[-1]), lambda i: (i + off, 0)),
            pl.BlockSpec((1, d), lambda i: (0, 0)),
            pl.BlockSpec((tile, z.shape[-1]), lambda i: (i + off, 0)),
        ],
        out_specs=pl.BlockSpec((tile, d), lambda i: (i, 0)),
        out_shape=jax.ShapeDtypeStruct((n_rows, d), F32),
        compiler_params=pltpu.CompilerParams(dimension_semantics=("arbitrary",),
                                             vmem_limit_bytes=VMEM_LIMIT),
        name="final",
    )(h_all, gate, nfin, z)


def kernel(x_prompt, x_sample, state_conv, state_rec, meta_tokens, norm_mix, w_in, conv_w,
           rec_lower_bound, rec_norm, w_out, norm_ffn, router_w, router_b, expert_w1, expert_b1,
           expert_w2, expert_b2, norm_final):
    depth = norm_mix.shape[0]
    assert depth == 1, "single-layer step"
    layer = 0
    bp, seq, d = x_prompt.shape
    ns = x_sample.shape[0]
    assert x_sample.shape[1] == 1 and ns == TOKEN_TILE
    d_conv = conv_w.shape[-1]
    d_rec = rec_lower_bound.shape[-1]
    assert state_conv.shape[2] == 2 and d_rec == N_HEADS * HEAD
    n_exp = router_w.shape[-1]
    d_ff = expert_w2.shape[2]
    n_prompt = bp * seq
    n_tok = n_prompt + ns
    assert seq % PROMPT_TILE == 0 and n_prompt % TOKEN_TILE == 0

    nm = norm_mix[layer][None]
    win = w_in[layer].astype(BF16)
    cw = conv_w[layer]
    rlb = rec_lower_bound
    rg = rec_norm[layer][None]
    wout = w_out[layer].astype(BF16)
    nf = norm_ffn[layer][None]
    rw = router_w[layer]
    rb = router_b[layer][None]
    mix_w = (nm, win, cw, rlb)
    tail_w = (rg, wout, nf, rw, rb)

    lb = jnp.sum(jax.nn.softmax(rlb, axis=0)[:layer + 1], axis=0)
    fast = CHUNK * jnp.max(-jnp.log(lb)) < DECAY_LIMIT

    st_meta, cv_meta = lax.cond(
        fast, lambda: _meta_call(meta_tokens, mix_w, layer, N_META),
        lambda: _meta_call(meta_tokens, mix_w, layer, None))

    xs = x_sample.reshape(ns, d)
    wide = jax.ShapeDtypeStruct((ns, d_rec), F32)
    y_conv_s, new_conv_s, f_s, k_s, q_s, v_s, og_s = pl.pallas_call(
        functools.partial(_sample_in_kernel, layer),
        out_shape=[jax.ShapeDtypeStruct((ns, d_conv), F32), jax.ShapeDtypeStruct((ns, 2 * d_conv), F32),
                   wide, wide, wide, wide, wide],
        compiler_params=pltpu.CompilerParams(vmem_limit_bytes=VMEM_LIMIT),
        name="sample_in",
    )(xs, state_conv[layer].reshape(ns, 2 * d_conv), nm, win, cw, rlb)

    group = 8
    n_grp = ns // group

    def cols(a):
        return a.T.reshape(d_rec, n_grp, group).transpose(1, 0, 2)

    col_spec = pl.BlockSpec((None, d_rec, group), lambda g: (g, 0, 0))
    st_spec = pl.BlockSpec((group, N_HEADS, HEAD, HEAD), lambda g: (g, 0, 0, 0))
    row_spec = pl.BlockSpec((group, d_rec), lambda g: (g, 0))
    new_rec_s, o_s = pl.pallas_call(
        functools.partial(_sample_state_kernel, group),
        grid=(n_grp,),
        in_specs=[col_spec, col_spec, col_spec, row_spec, st_spec],
        out_specs=[st_spec, row_spec],
        out_shape=[jax.ShapeDtypeStruct(state_rec.shape[1:], F32), wide],
        compiler_params=pltpu.CompilerParams(dimension_semantics=("arbitrary",),
                                             vmem_limit_bytes=VMEM_LIMIT),
        name="sample_state",
    )(cols(f_s), cols(k_s), cols(q_s), v_s, state_rec[layer])

    decode = pl.pallas_call(
        _sample_tail_kernel,
        out_shape=[jax.ShapeDtypeStruct((ns, d), F32), jax.ShapeDtypeStruct((ns, d), F32),
                   jax.ShapeDtypeStruct((ns, n_exp), F32), jax.ShapeDtypeStruct((ns, LANES), F32)],
        compiler_params=pltpu.CompilerParams(vmem_limit_bytes=VMEM_LIMIT),
        name="sample_tail",
    )(xs, y_conv_s, o_s, og_s, rg, wout, nf, rw, rb)

    h_all, xn_all, route, rnk, counts, new_rec_p, new_conv_p = lax.cond(
        fast,
        lambda: _prompt_call(x_prompt, st_meta, cv_meta, mix_w, tail_w, decode, layer, PROMPT_TILE, CHUNK),
        lambda: _prompt_call(x_prompt, st_meta, cv_meta, mix_w, tail_w, decode, layer, PROMPT_TILE, None))

    tm = MOE_TILE
    cap = h_all.shape[0]
    assert cap % tm == 0
    n_tiles = (n_tok * TOP_K) // tm + n_exp
    counts = counts[0].astype(jnp.int32)
    tiles_e = (counts + tm - 1) // tm
    tile_end = jnp.cumsum(tiles_e)
    n_used = tile_end[-1]
    tile_ids = jnp.arange(n_tiles, dtype=jnp.int32)

    def expert_of(tile):
        return jnp.minimum(jnp.sum((tile_end[None, :] <= tile[:, None]).astype(jnp.int32), axis=1), n_exp - 1)

    tile_expert = expert_of(jnp.minimum(tile_ids, n_used - 1))
    mine = tile_expert[:, None] == jnp.arange(n_exp, dtype=jnp.int32)[None, :]
    after = jnp.sum(jnp.where(mine, tile_end[None, :], 0), axis=1)
    first = jnp.sum(jnp.where(mine, (tile_end - tiles_e)[None, :], 0), axis=1)
    next_expert = jnp.where(after < n_used, expert_of(after), tile_expert)
    n_rows = cap
    n_spare = (n_rows - n_tok) * TOP_K
    spare_blocks = -(-n_spare // tm)
    dummy_block = n_exp * (cap // tm) + spare_blocks
    n_slots = (dummy_block + 1) * tm
    tile_block = jnp.where(tile_ids < n_used, tile_expert * (cap // tm) + tile_ids - first, dummy_block)
    rows_left = jnp.sum(jnp.where(mine, counts[None, :], 0), axis=1) - (tile_ids - first) * tm
    part = tm // MOE_PARTS
    tile_full = jnp.clip((rows_left + part - 1) // part, 1, MOE_PARTS).astype(jnp.int32)

    n_workers = SC_CORES * SC_SUBCORES
    assert n_rows % (n_workers * DISPATCH_CHUNK) == 0
    pos_rows = rnk[:, :TOP_K].astype(jnp.int32)

    def to_workers(p, ch):
        assert p.shape[0] % (n_workers * ch) == 0
        return p.reshape(n_workers, -1, ch, TOP_K).transpose(0, 1, 3, 2)

    xs = _dispatch(xn_all, to_workers(pos_rows, DISPATCH_CHUNK), n_slots)

    w1 = expert_w1[layer]
    w2 = expert_w2[layer]
    b1 = expert_b1[layer]
    b1g = b1[:, 0::2][:, None, :]
    b1l = b1[:, 1::2][:, None, :]
    b2 = expert_b2[layer][:, None, :]
    ys = pl.pallas_call(
        functools.partial(_moe_kernel, tm),
        grid_spec=pltpu.PrefetchScalarGridSpec(
            num_scalar_prefetch=5,
            grid=(n_tiles,),
            in_specs=[
                pl.BlockSpec((tm, d // 2), lambda i, te, nu, tn, tb, tf: (tb[i], 0)),
                pl.BlockSpec(memory_space=pl.ANY),
                _const_spec(b1g.shape), _const_spec(b1l.shape),
                pl.BlockSpec(memory_space=pl.ANY),
                _const_spec(b2.shape),
            ],
            out_specs=pl.BlockSpec((tm, d // 2), lambda i, te, nu, tn, tb, tf: (tb[i], 0)),
            scratch_shapes=[pltpu.VMEM((d, 2 * d_ff), F32), pltpu.VMEM((d_ff, d), F32),
                            pltpu.SemaphoreType.DMA((2,)),
                            pltpu.VMEM((d, 2 * d_ff), BF16), pltpu.VMEM((d_ff, d), BF16),
                            pltpu.VMEM((tm, d_ff), BF16)],
        ),
        out_shape=jax.ShapeDtypeStruct((n_slots, d // 2), jnp.uint32),
        compiler_params=pltpu.CompilerParams(dimension_semantics=("arbitrary",),
                                             vmem_limit_bytes=VMEM_LIMIT),
        name="moe",
    )(tile_expert, n_used[None].astype(jnp.int32), next_expert, tile_block, tile_full,
      xs, w1, b1g, b1l, w2, b2)

    z = _collect(ys, to_workers(pos_rows, DISPATCH_CHUNK))
    y_p = _final_call(h_all, route, norm_final[None], z, 0, n_prompt, FINAL_TILE)
    y_s = _final_call(h_all, route, norm_final[None], z, n_prompt, ns, TOKEN_TILE)

    return (y_p.reshape(bp, seq, d), y_s.reshape(ns, 1, d),
            new_conv_p[None], new_rec_p[None],
            new_conv_s.reshape(1, ns, 2, d_conv), new_rec_s[None])
```

```python
import functools

import jax
import jax.numpy as jnp
from jax import lax
from jax.experimental import pallas as pl
from jax.experimental.pallas import tpu as pltpu
from jax.experimental.pallas import tpu_sc as plsc

F32 = jnp.float32
BF16 = jnp.bfloat16

N_HEADS = 4
HEAD = 128
N_META = 16
CHUNK = 64
SAFE_CHUNK = 16
DECAY_LIMIT = 80.0
TOP_K = 4
SWIGLU_LIMIT = 7.0
SWIGLU_ALPHA = 1.702
EPS = 1e-5

LANES = 128
MXU_N = 256
PROMPT_TILE = 256
TOKEN_TILE = 128
FINAL_TILE = 1024
MOE_TILE = 1024
MOE_PARTS = 4
SC_CORES = 2
SC_SUBCORES = 16
DISPATCH_CHUNK = 24
SC_RING = 4
VMEM_LIMIT = 56 * 1024 * 1024


def _dot(a, b):
    return jnp.dot(a, b, preferred_element_type=F32)


def _dot_nt(a, b):
    return lax.dot_general(a, b, (((1,), (1,)), ((), ())), preferred_element_type=F32)


def _dot_tn(a, b):
    return lax.dot_general(a, b, (((0,), (0,)), ((), ())), preferred_element_type=F32)


def _pack_halves(x):
    n = x.shape[-1] // 2
    lo = pltpu.bitcast(x[:, :n].astype(BF16).astype(F32), jnp.uint32)
    hi = pltpu.bitcast(x[:, n:].astype(BF16).astype(F32), jnp.uint32)
    return (lo >> 16) | (hi & jnp.uint32(0xFFFF0000))


def _unpack_halves(u):
    lo = pltpu.bitcast(u << 16, F32)
    hi = pltpu.bitcast(u & jnp.uint32(0xFFFF0000), F32)
    return jnp.concatenate([lo, hi], axis=-1)


def _sigmoid(x):
    return 1.0 / (1.0 + jnp.exp(-x))


def _rms(x, g):
    ms = jnp.mean(x * x, axis=-1, keepdims=True)
    return x * lax.rsqrt(ms + EPS) * g


def _project(x, nm_ref, win_ref):
    return _dot(_rms(x, nm_ref[...]).astype(BF16), win_ref[...])


def _split_u(u, d_conv, d_rec):
    pts = [0, d_conv, 2 * d_conv, 3 * d_conv, 3 * d_conv + d_rec, 3 * d_conv + 2 * d_rec,
           3 * d_conv + 3 * d_rec, 3 * d_conv + 4 * d_rec]
    return [u[:, pts[i]:pts[i + 1]] for i in range(7)]


def _lower_bound(rlb_ref, layer):
    r = rlb_ref[...]
    e = jnp.exp(r - jnp.max(r, axis=0, keepdims=True))
    return jnp.sum(e[0:layer + 1], axis=0, keepdims=True) / jnp.sum(e, axis=0, keepdims=True)


def _forget(fx, lb):
    f = lb + (1.0 - lb) * _sigmoid(fx)
    return f, 1.0 - f


def _rec_out(o, og, rg):
    parts = []
    for h in range(N_HEADS):
        oh = o[:, h * HEAD:(h + 1) * HEAD]
        parts.append(oh * lax.rsqrt(jnp.mean(oh * oh, axis=-1, keepdims=True) + EPS))
    return jnp.concatenate(parts, axis=-1) * rg * (og * _sigmoid(og))


def _route(logits):
    m_rows, n = logits.shape
    lane = lax.broadcasted_iota(jnp.int32, logits.shape, 1).astype(F32)
    work = logits
    tops, firsts = [], []
    sel = jnp.zeros_like(logits)
    for _ in range(TOP_K):
        m = jnp.max(work, axis=-1, keepdims=True)
        first = jnp.min(jnp.where(work == m, lane, float(n)), axis=-1, keepdims=True)
        hot = lane == first
        tops.append(m)
        firsts.append(first)
        sel = sel + jnp.where(hot, 1.0, 0.0)
        work = jnp.where(hot, -jnp.inf, work)
    es = [jnp.exp(t - tops[0]) for t in tops]
    den = es[0]
    for e in es[1:]:
        den = den + e
    wide = lax.broadcasted_iota(jnp.int32, (m_rows, LANES), 1)
    route = jnp.zeros((m_rows, LANES), F32)
    for k in range(TOP_K):
        route = jnp.where(wide == k, firsts[k], route)
        route = jnp.where(wide == TOP_K + k, es[k] / den, route)
    return sel, route


def _tail(x, y, wout_ref, nf_ref, rw_ref, rb_ref):
    h = x + _dot(y.astype(BF16), wout_ref[...])
    xn = _rms(h, nf_ref[...])
    xh = xn.astype(BF16)
    xl = (xn - xh.astype(F32)).astype(BF16)
    rw = rw_ref[...]
    wh = rw.astype(BF16)
    wl = (rw - wh.astype(F32)).astype(BF16)
    logits = _dot(xh, wh) + _dot(xh, wl) + _dot(xl, wh) + rb_ref[...]
    sel, route = _route(logits)
    return h, xn, sel, route


def _cumsum_rows(x):
    n = x.shape[0]
    row = lax.broadcasted_iota(jnp.int32, (n, 1), 0)
    shift = 1
    while shift < n:
        x = x + jnp.where(row >= shift, pltpu.roll(x, shift, axis=0), 0.0)
        shift *= 2
    return x


def _rec_chunk(q, kk, v, lf, st_ref, causal):
    c = q.shape[0]
    b = _cumsum_rows(lf)
    eb = jnp.exp(b)
    qe = (q * eb).astype(BF16)
    ke = (kk * jnp.exp(-b)).astype(BF16)
    vb = v.astype(BF16)
    eb_last = eb[c - 1:c]
    outs = []
    for h in range(N_HEADS):
        sl = slice(h * HEAD, (h + 1) * HEAD)
        st = st_ref[h]
        sc = jnp.where(causal, _dot_nt(qe[:, sl], ke[:, sl]), 0.0)
        outs.append(_dot(sc.astype(BF16), vb[:, sl]) + _dot_nt(qe[:, sl], st.astype(BF16)))
        st_ref[h] = (st + _dot_tn(vb[:, sl], ke[:, sl])) * eb_last[:, sl]
    return jnp.concatenate(outs, axis=-1)


def _rec_chunk_safe(q, kk, v, lf, st_ref):
    c = q.shape[0]
    row = lax.broadcasted_iota(jnp.int32, (c, 1), 0)
    b = _cumsum_rows(lf)
    eb = jnp.exp(b)
    qe = (q * eb).astype(BF16)
    kl = (kk * jnp.exp(b[c - 1:c] - b)).astype(BF16)
    vb = v.astype(BF16)
    outs = []
    for h in range(N_HEADS):
        sl = slice(h * HEAD, (h + 1) * HEAD)
        st = st_ref[h]
        bh, kh, qh, vh = b[:, sl], kk[:, sl], q[:, sl], v[:, sl]
        intra = jnp.zeros((c, HEAD), F32)
        for t in range(c):
            w = kh * jnp.exp(jnp.where(row <= t, bh[t:t + 1] - bh, -jnp.inf))
            score = jnp.sum(w * qh[t:t + 1], axis=-1, keepdims=True)
            intra = jnp.where(row == t, jnp.sum(score * vh, axis=0, keepdims=True), intra)
        outs.append(intra + _dot_nt(qe[:, sl], st.astype(BF16)))
        st_ref[h] = st * eb[c - 1:c, sl] + _dot_tn(vb[:, sl], kl[:, sl])
    return jnp.concatenate(outs, axis=-1)


def _mix_tile(u, layer, chunk, cw_ref, rlb_ref, convbuf, st, cv_out):
    tm = u.shape[0]
    d_conv = cw_ref.shape[-1]
    d_rec = rlb_ref.shape[-1]
    bg, cg, hv, q, fx, iv, og = _split_u(u, d_conv, d_rec)

    bx = bg * hv
    convbuf[8:8 + tm, :] = bx
    cw = cw_ref[...]
    conv = cw[0:1] * convbuf[6:6 + tm, :] + cw[1:2] * convbuf[7:7 + tm, :] + cw[2:3] * bx
    convbuf[6:8, :] = bx[tm - 2:tm]
    cv_out[...] = bx[tm - 2:tm]

    f, kk = _forget(fx, _lower_bound(rlb_ref, layer))
    lf = jnp.log(f)
    outs = []
    if chunk is None:
        for c in range(tm // SAFE_CHUNK):
            rs = slice(c * SAFE_CHUNK, (c + 1) * SAFE_CHUNK)
            outs.append(_rec_chunk_safe(q[rs], kk[rs], iv[rs], lf[rs], st))
    else:
        row = lax.broadcasted_iota(jnp.int32, (chunk, chunk), 0)
        col = lax.broadcasted_iota(jnp.int32, (chunk, chunk), 1)
        causal = row >= col
        for c in range(tm // chunk):
            rs = slice(c * chunk, (c + 1) * chunk)
            outs.append(_rec_chunk(q[rs], kk[rs], iv[rs], lf[rs], st, causal))
    return cg * conv, jnp.concatenate(outs, axis=0), og


def _load_state(st, convbuf, s0_ref, c0_ref):
    for h in range(N_HEADS):
        st[h] = s0_ref[h].T
    convbuf[6:8, :] = c0_ref[...]


def _meta_kernel(layer, chunk, x_ref, s0_ref, c0_ref, nm_ref, win_ref, cw_ref, rlb_ref, st_out, cv_out,
                 convbuf, st):
    _load_state(st, convbuf, s0_ref, c0_ref)
    _mix_tile(_project(x_ref[...], nm_ref, win_ref), layer, chunk, cw_ref, rlb_ref, convbuf, st, cv_out)
    for h in range(N_HEADS):
        st_out[h] = st[h].T


def _prompt_kernel(layer, tm, chunk, n_pairs, pairs_per_seq, cap,
                   xr0_ref, x1_ref, xp2_ref, s0_ref, c0_ref, nm_ref, win_ref, cw_ref, rlb_ref,
                   rg_ref, wout_ref, nf_ref, rw_ref, rb_ref, hs_ref, xns_ref, sels_ref, routes_ref,
                   h_ref, xn_ref, route_ref, rnk_ref, cnt_ref, st_out, cv_out,
                   convbuf, st, ua, ub, carry):
    step = pl.program_id(0)
    live = step < n_pairs

    @pl.when(step == 0)
    def _():
        ua[...] = _project(xr0_ref[...], nm_ref, win_ref)
        carry[...] = jnp.zeros_like(carry)

    @pl.when(jnp.logical_and(lax.rem(step, pairs_per_seq) == 0, live))
    def _():
        _load_state(st, convbuf, s0_ref, c0_ref)

    @pl.when(live)
    def _():
        def finish(x, u, rows):
            y_conv, o, og = _mix_tile(u, layer, chunk, cw_ref, rlb_ref, convbuf, st, cv_out)
            y = jnp.concatenate([y_conv, _rec_out(o, og, rg_ref[...])], axis=-1)
            h, xn, sel, route = _tail(x, y, wout_ref, nf_ref, rw_ref, rb_ref)
            h_ref[rows, :] = h
            xn_ref[rows, :] = _pack_halves(xn)
            route_ref[rows, :] = route
            rnk_ref[rows, :] = _rank_block(sel, route, carry, cap)

        ub[...] = _project(x1_ref[...], nm_ref, win_ref)
        finish(xr0_ref[...], ua[...], slice(0, tm))
        ua[...] = _project(xp2_ref[...], nm_ref, win_ref)
        finish(x1_ref[...], ub[...], slice(tm, 2 * tm))

    @pl.when(jnp.logical_and(lax.rem(step, pairs_per_seq) == pairs_per_seq - 1, live))
    def _():
        for h in range(N_HEADS):
            st_out[h] = st[h].T

    @pl.when(step == n_pairs)
    def _():
        ns = hs_ref.shape[0]
        n_exp = sels_ref.shape[-1]
        pad = 2 * tm - ns
        spare = (lax.broadcasted_iota(jnp.int32, (pad, LANES), 0) * TOP_K
                 + lax.broadcasted_iota(jnp.int32, (pad, LANES), 1) + n_exp * cap).astype(F32)
        rnk_ref[0:ns, :] = _rank_block(sels_ref[...], routes_ref[...], carry, cap)
        rnk_ref[ns:2 * tm, :] = spare
        for dst, val in ((h_ref, hs_ref[...]), (xn_ref, _pack_halves(xns_ref[...])),
                         (route_ref, routes_ref[...])):
            dst[0:ns, :] = val
            dst[ns:2 * tm, :] = jnp.zeros((pad, dst.shape[-1]), dst.dtype)
        cnt_ref[...] = carry[...]


def _slot_cap(n_rows):
    return -(-n_rows // MOE_TILE) * MOE_TILE


def _const_spec(shape):
    return pl.BlockSpec(shape, lambda *_: (0,) * len(shape))


def _meta_call(x, weights, layer, chunk):
    assert x.shape[0] % (chunk or SAFE_CHUNK) == 0
    nm, win, cw, rlb = weights
    d_conv = cw.shape[-1]
    s0 = jnp.zeros((N_HEADS, HEAD, HEAD), F32)
    c0 = jnp.zeros((2, d_conv), F32)
    return pl.pallas_call(
        functools.partial(_meta_kernel, layer, chunk),
        out_shape=[jax.ShapeDtypeStruct(s0.shape, F32), jax.ShapeDtypeStruct(c0.shape, F32)],
        scratch_shapes=[pltpu.VMEM((x.shape[0] + 8, d_conv), F32), pltpu.VMEM(s0.shape, F32)],
        compiler_params=pltpu.CompilerParams(vmem_limit_bytes=VMEM_LIMIT),
        name="mixer_meta",
    )(x, s0, c0, nm, win, cw, rlb)


def _prompt_call(x, s0, c0, weights, tail_w, decode, layer, tm, chunk):
    nseq, length, d = x.shape
    nt = length // tm
    assert nt % 2 == 0 and decode[0].shape[0] <= 2 * tm and tm % (chunk or SAFE_CHUNK) == 0
    pairs_per_seq = nt // 2
    n_pairs = nseq * pairs_per_seq
    nm, win, cw, rlb = weights
    d_conv = cw.shape[-1]
    d_in = win.shape[-1]

    def tile_spec(offset):
        def index(s):
            tile = jnp.minimum(2 * s + offset, 2 * n_pairs - 1)
            return (tile // nt, lax.rem(tile, nt), 0)
        return pl.BlockSpec((None, tm, d), index)

    def seq_of(s):
        return jnp.minimum(s // pairs_per_seq, nseq - 1)

    consts = [s0, c0, nm, win, cw, rlb] + list(tail_w) + list(decode)
    n_exp = decode[2].shape[-1]
    tok = [(d, F32), (d // 2, jnp.uint32), (LANES, F32), (LANES, F32)]
    n_steps = n_pairs + 1
    return pl.pallas_call(
        functools.partial(_prompt_kernel, layer, tm, chunk, n_pairs, pairs_per_seq, _slot_cap(n_steps * 2 * tm)),
        grid=(n_steps,),
        in_specs=[tile_spec(0), tile_spec(1), tile_spec(2)] + [_const_spec(a.shape) for a in consts],
        out_specs=[pl.BlockSpec((2 * tm, w), lambda s: (s, 0)) for w, _ in tok] + [
            pl.BlockSpec((1, n_exp), lambda s: (0, 0)),
            pl.BlockSpec((None,) + s0.shape, lambda s: (seq_of(s), 0, 0, 0)),
            pl.BlockSpec((None,) + c0.shape, lambda s: (seq_of(s), 0, 0))],
        out_shape=[jax.ShapeDtypeStruct((n_steps * 2 * tm, w), t) for w, t in tok] + [
            jax.ShapeDtypeStruct((1, n_exp), F32),
            jax.ShapeDtypeStruct((nseq,) + s0.shape, F32), jax.ShapeDtypeStruct((nseq,) + c0.shape, F32)],
        scratch_shapes=[pltpu.VMEM((tm + 8, d_conv), F32), pltpu.VMEM(s0.shape, F32),
                        pltpu.VMEM((tm, d_in), F32), pltpu.VMEM((tm, d_in), F32),
                        pltpu.VMEM((1, n_exp), F32)],
        compiler_params=pltpu.CompilerParams(dimension_semantics=("arbitrary",),
                                             vmem_limit_bytes=VMEM_LIMIT),
        name="mixer_prompt",
    )(x, x, x, *consts)


def _sample_in_kernel(layer, x_ref, sc_ref, nm_ref, win_ref, cw_ref, rlb_ref,
                      yc_ref, nc_ref, f_ref, k_ref, q_ref, v_ref, og_ref):
    d_conv = cw_ref.shape[-1]
    d_rec = rlb_ref.shape[-1]
    u = _project(x_ref[...], nm_ref, win_ref)
    bg, cg, hv, q, fx, iv, og = _split_u(u, d_conv, d_rec)
    bx = bg * hv
    sc = sc_ref[...]
    s0, s1 = sc[:, :d_conv], sc[:, d_conv:]
    cw = cw_ref[...]
    yc_ref[...] = cg * (cw[0:1] * s0 + cw[1:2] * s1 + cw[2:3] * bx)
    nc_ref[...] = jnp.concatenate([s1, bx], axis=-1)
    f, kk = _forget(fx, _lower_bound(rlb_ref, layer))
    f_ref[...] = f
    k_ref[...] = kk
    q_ref[...] = q
    v_ref[...] = iv
    og_ref[...] = og


def _sample_state_kernel(group, f_ref, k_ref, q_ref, v_ref, s_ref, sn_ref, o_ref):
    for j in range(group):
        for h in range(N_HEADS):
            rs = slice(h * HEAD, (h + 1) * HEAD)
            fcol = f_ref[rs, j:j + 1]
            kcol = k_ref[rs, j:j + 1]
            qcol = q_ref[rs, j:j + 1]
            vrow = v_ref[j:j + 1, rs]
            sn = fcol * s_ref[j, h] + kcol * vrow
            sn_ref[j, h] = sn
            o_ref[j:j + 1, rs] = jnp.sum(qcol * sn, axis=0, keepdims=True)


def _sample_tail_kernel(x_ref, yc_ref, o_ref, og_ref, rg_ref, wout_ref, nf_ref, rw_ref, rb_ref,
                        h_ref, xn_ref, sel_ref, route_ref):
    y = jnp.concatenate([yc_ref[...], _rec_out(o_ref[...], og_ref[...], rg_ref[...])], axis=-1)
    h, xn, sel, route = _tail(x_ref[...], y, wout_ref, nf_ref, rw_ref, rb_ref)
    h_ref[...] = h
    xn_ref[...] = xn
    sel_ref[...] = sel
    route_ref[...] = route


def _rank_block(sel, route, carry, cap):
    tb, ne = sel.shape
    row = lax.broadcasted_iota(jnp.int32, (tb, tb), 0)
    col = lax.broadcasted_iota(jnp.int32, (tb, tb), 1)
    before = jnp.where(col < row, 1.0, 0.0).astype(BF16)
    rank = _dot(before, sel.astype(BF16)) + carry[...]
    carry[...] = carry[...] + jnp.sum(sel, axis=0, keepdims=True)
    lane_e = lax.broadcasted_iota(jnp.int32, (tb, ne), 1).astype(F32)
    lane = lax.broadcasted_iota(jnp.int32, (tb, LANES), 1)
    rnk = jnp.zeros((tb, LANES), F32)
    for k in range(TOP_K):
        expert = route[:, k:k + 1]
        mine = jnp.where(lane_e == expert, rank, 0.0)
        rnk = jnp.where(lane == k, expert * float(cap) + jnp.sum(mine, axis=-1, keepdims=True), rnk)
    return rnk


def _dispatch(xn, pos_w, n_slots):
    n_workers, n_chunks, top_k, ch = pos_w.shape
    assert n_workers == SC_CORES * SC_SUBCORES and ch % 8 == 0 and ch <= LANES
    d = xn.shape[1]
    mesh = plsc.VectorSubcoreMesh(core_axis_name="c", subcore_axis_name="s")

    @functools.partial(
        pl.kernel, mesh=mesh,
        out_type=jax.ShapeDtypeStruct((n_slots, d), xn.dtype),
        scratch_types=[pltpu.VMEM((n_chunks, top_k, ch), jnp.int32), pltpu.VMEM((SC_RING, ch, d), xn.dtype),
                       pltpu.SemaphoreType.DMA((SC_RING,)), pltpu.SemaphoreType.DMA((SC_RING,))],
        name="dispatch",
    )
    def run(xn_hbm, pos_hbm, xs_hbm, idx_v, rows_v, sem_r, sem_w):
        wid = lax.axis_index("s") * SC_CORES + lax.axis_index("c")

        def read(c, b):
            src = xn_hbm.at[pl.ds((wid * n_chunks + c) * ch, ch)]
            return pltpu.make_async_copy(src, rows_v.at[b], sem_r.at[b])

        def write(c, b, k):
            return pltpu.make_async_copy(rows_v.at[b], xs_hbm.at[idx_v.at[c, k]], sem_w.at[b])

        pltpu.sync_copy(pos_hbm.at[wid], idx_v)
        ahead = SC_RING - 1
        for c0 in range(min(ahead, n_chunks)):
            read(c0, c0).start()

        @pl.loop(0, n_chunks)
        def _(c):
            b = lax.rem(c, SC_RING)
            read(c, b).wait()
            for k in range(top_k):
                write(c, b, k).start()

            @pl.when(c >= 1)
            def _():
                for k in range(top_k):
                    write(c - 1, lax.rem(c - 1, SC_RING), k).wait()

            @pl.when(c + ahead < n_chunks)
            def _():
                read(c + ahead, lax.rem(c + ahead, SC_RING)).start()

        for k in range(top_k):
            write(n_chunks - 1, (n_chunks - 1) % SC_RING, k).wait()

    return run(xn, pos_w)


def _collect(ys, pos_w):
    n_workers, n_chunks, top_k, ch = pos_w.shape
    assert n_workers == SC_CORES * SC_SUBCORES and ch % 8 == 0 and ch <= LANES
    d = ys.shape[1]
    mesh = plsc.VectorSubcoreMesh(core_axis_name="c", subcore_axis_name="s")

    @functools.partial(
        pl.kernel, mesh=mesh,
        out_type=jax.ShapeDtypeStruct((n_workers * n_chunks * ch, top_k * d), ys.dtype),
        scratch_types=[pltpu.VMEM((n_chunks, top_k, ch), jnp.int32), pltpu.VMEM((top_k, ch, d), ys.dtype),
                       pltpu.SemaphoreType.DMA((top_k,)), pltpu.SemaphoreType.DMA((top_k,))],
        name="collect",
    )
    def run(ys_hbm, pos_hbm, out_hbm, idx_v, rows_v, sem_r, sem_w):
        wid = lax.axis_index("s") * SC_CORES + lax.axis_index("c")

        def read(c, k):
            return pltpu.make_async_copy(ys_hbm.at[idx_v.at[c, k]], rows_v.at[k], sem_r.at[k])

        def write(c, k):
            dst = out_hbm.at[pl.ds((wid * n_chunks + c) * ch, ch), pl.ds(k * d, d)]
            return pltpu.make_async_copy(rows_v.at[k], dst, sem_w.at[k])

        pltpu.sync_copy(pos_hbm.at[wid], idx_v)
        for k in range(top_k - 1):
            read(0, k).start()

        @pl.loop(0, n_chunks)
        def _(c):
            for k in range(top_k):
                read(c, k).wait()
                write(c, k).start()
                if k >= 1:
                    write(c, k - 1).wait()

                    @pl.when(c + 1 < n_chunks)
                    def _():
                        read(c + 1, k - 1).start()
                else:
                    @pl.when(c >= 1)
                    def _():
                        write(c - 1, top_k - 1).wait()
                    read(c, top_k - 1).start()

        write(n_chunks - 1, top_k - 1).wait()

    return run(ys, pos_w)


def _moe_kernel(tm, te_ref, nu_ref, tn_ref, tb_ref, tf_ref, x_ref, w1_hbm, bg_ref, bl_ref, w2_hbm, b2_ref,
                ys_ref, w1f, w2f, sem, w1p, w2b, act):
    i = pl.program_id(0)
    n_used = nu_ref[0]
    d_ff2 = w1f.shape[-1]
    n_blk = d_ff2 // MXU_N
    expert = te_ref[i]

    def weight_copies(e):
        return (pltpu.make_async_copy(w1_hbm.at[e], w1f, sem.at[0]),
                pltpu.make_async_copy(w2_hbm.at[e], w2f, sem.at[1]))

    @pl.when(i == 0)
    def _():
        for cp in weight_copies(expert):
            cp.start()

    @pl.when(i >= n_used)
    def _():
        ys_ref[...] = jnp.zeros_like(ys_ref)

    prev = te_ref[jnp.maximum(i - 1, 0)]
    changed = jnp.logical_or(i == 0, expert != prev)

    @pl.when(jnp.logical_and(changed, i < n_used))
    def _():
        for cp in weight_copies(expert):
            cp.wait()
        r = lax.broadcasted_iota(jnp.int32, (MXU_N, MXU_N), 0)
        c = lax.broadcasted_iota(jnp.int32, (MXU_N, MXU_N), 1)
        src = jnp.where(c < MXU_N // 2, 2 * c, 2 * (c - MXU_N // 2) + 1)
        perm = jnp.where(r == src, 1.0, 0.0).astype(BF16)
        for blk in range(n_blk):
            cs = slice(blk * MXU_N, (blk + 1) * MXU_N)
            w1p[:, cs] = _dot(w1f[:, cs].astype(BF16), perm).astype(BF16)
        w2b[...] = w2f[...].astype(BF16)

        @pl.when(tn_ref[i] != expert)
        def _():
            for cp in weight_copies(tn_ref[i]):
                cp.start(priority=1)

    def expert_mlp(rows):
        x = _unpack_halves(x_ref[0:rows, :]).astype(BF16)
        bg, bl = bg_ref[expert], bl_ref[expert]
        for blk in range(n_blk):
            a = _dot(x, w1p[:, blk * MXU_N:(blk + 1) * MXU_N])
            half = MXU_N // 2
            hs = slice(blk * half, (blk + 1) * half)
            glu = jnp.minimum(a[:, :half] + bg[:, hs], SWIGLU_LIMIT)
            lin = jnp.clip(a[:, half:] + bl[:, hs], -SWIGLU_LIMIT, SWIGLU_LIMIT)
            act[0:rows, hs] = (glu * _sigmoid(SWIGLU_ALPHA * glu) * (lin + 1.0)).astype(BF16)
        ys_ref[0:rows, :] = _pack_halves(_dot(act[0:rows, :], w2b[...]) + b2_ref[expert])

    part = tm // MOE_PARTS
    for parts in range(1, MOE_PARTS + 1):
        @pl.when(jnp.logical_and(i < n_used, tf_ref[i] == parts))
        def _(rows=parts * part):
            expert_mlp(rows)
            if rows < tm:
                ys_ref[rows:tm, :] = jnp.zeros((tm - rows, ys_ref.shape[-1]), ys_ref.dtype)


def _final_kernel(h_ref, gate_ref, nfin_ref, z_ref, y_ref):
    w = z_ref.shape[-1] // TOP_K
    g = gate_ref[...]
    out = h_ref[...]
    for k in range(TOP_K):
        out = out + g[:, TOP_K + k:TOP_K + k + 1] * _unpack_halves(z_ref[:, k * w:(k + 1) * w])
    y_ref[...] = _rms(out, nfin_ref[...])


def _final_call(h_all, gate, nfin, z, first_row, n_rows, tile):
    d = h_all.shape[-1]
    off = first_row // tile
    assert first_row % tile == 0 and n_rows % tile == 0
    return pl.pallas_call(
        _final_kernel,
        grid=(n_rows // tile,),
        in_specs=[
            pl.BlockSpec((tile, d), lambda i: (i + off, 0)),
            pl.BlockSpec((tile, gate.shape[-1]), lambda i: (i + off, 0)),
            pl.BlockSpec((1, d), lambda i: (0, 0)),
            pl.BlockSpec((tile, z.shape[-1]), lambda i: (i + off, 0)),
        ],
        out_specs=pl.BlockSpec((tile, d), lambda i: (i, 0)),
        out_shape=jax.ShapeDtypeStruct((n_rows, d), F32),
        compiler_params=pltpu.CompilerParams(dimension_semantics=("arbitrary",),
                                             vmem_limit_bytes=VMEM_LIMIT),
        name="final",
    )(h_all, gate, nfin, z)


def kernel(x_prompt, x_sample, state_conv, state_rec, meta_tokens, norm_mix, w_in, conv_w,
           rec_lower_bound, rec_norm, w_out, norm_ffn, router_w, router_b, expert_w1, expert_b1,
           expert_w2, expert_b2, norm_final):
    depth = norm_mix.shape[0]
    assert depth == 1, "single-layer step"
    layer = 0
    bp, seq, d = x_prompt.shape
    ns = x_sample.shape[0]
    assert x_sample.shape[1] == 1 and ns == TOKEN_TILE
    d_conv = conv_w.shape[-1]
    d_rec = rec_lower_bound.shape[-1]
    assert state_conv.shape[2] == 2 and d_rec == N_HEADS * HEAD
    n_exp = router_w.shape[-1]
    d_ff = expert_w2.shape[2]
    n_prompt = bp * seq
    n_tok = n_prompt + ns
    assert seq % PROMPT_TILE == 0 and n_prompt % TOKEN_TILE == 0

    nm = norm_mix[layer][None]
    win = w_in[layer].astype(BF16)
    cw = conv_w[layer]
    rlb = rec_lower_bound
    rg = rec_norm[layer][None]
    wout = w_out[layer].astype(BF16)
    nf = norm_ffn[layer][None]
    rw = router_w[layer]
    rb = router_b[layer][None]
    mix_w = (nm, win, cw, rlb)
    tail_w = (rg, wout, nf, rw, rb)

    lb = jnp.sum(jax.nn.softmax(rlb, axis=0)[:layer + 1], axis=0)
    fast = CHUNK * jnp.max(-jnp.log(lb)) < DECAY_LIMIT

    xs = x_sample.reshape(ns, d)
    wide = jax.ShapeDtypeStruct((ns, d_rec), F32)
    y_conv_s, new_conv_s, f_s, k_s, q_s, v_s, og_s = pl.pallas_call(
        functools.partial(_sample_in_kernel, layer),
        out_shape=[jax.ShapeDtypeStruct((ns, d_conv), F32), jax.ShapeDtypeStruct((ns, 2 * d_conv), F32),
                   wide, wide, wide, wide, wide],
        compiler_params=pltpu.CompilerParams(vmem_limit_bytes=VMEM_LIMIT),
        name="sample_in",
    )(xs, state_conv[layer].reshape(ns, 2 * d_conv), nm, win, cw, rlb)

    group = 8
    n_grp = ns // group

    def cols(a):
        return a.T.reshape(d_rec, n_grp, group).transpose(1, 0, 2)

    col_spec = pl.BlockSpec((None, d_rec, group), lambda g: (g, 0, 0))
    st_spec = pl.BlockSpec((group, N_HEADS, HEAD, HEAD), lambda g: (g, 0, 0, 0))
    row_spec = pl.BlockSpec((group, d_rec), lambda g: (g, 0))
    new_rec_s, o_s = pl.pallas_call(
        functools.partial(_sample_state_kernel, group),
        grid=(n_grp,),
        in_specs=[col_spec, col_spec, col_spec, row_spec, st_spec],
        out_specs=[st_spec, row_spec],
        out_shape=[jax.ShapeDtypeStruct(state_rec.shape[1:], F32), wide],
        compiler_params=pltpu.CompilerParams(dimension_semantics=("arbitrary",),
                                             vmem_limit_bytes=VMEM_LIMIT),
        name="sample_state",
    )(cols(f_s), cols(k_s), cols(q_s), v_s, state_rec[layer])

    decode = pl.pallas_call(
        _sample_tail_kernel,
        out_shape=[jax.ShapeDtypeStruct((ns, d), F32), jax.ShapeDtypeStruct((ns, d), F32),
                   jax.ShapeDtypeStruct((ns, n_exp), F32), jax.ShapeDtypeStruct((ns, LANES), F32)],
        compiler_params=pltpu.CompilerParams(vmem_limit_bytes=VMEM_LIMIT),
        name="sample_tail",
    )(xs, y_conv_s, o_s, og_s, rg, wout, nf, rw, rb)

    def prompt_path(meta_chunk, chunk):
        st_meta, cv_meta = _meta_call(meta_tokens, mix_w, layer, meta_chunk)
        return _prompt_call(x_prompt, st_meta, cv_meta, mix_w, tail_w, decode, layer, PROMPT_TILE, chunk)

    h_all, xn_all, route, rnk, counts, new_rec_p, new_conv_p = lax.cond(
        fast, lambda: prompt_path(N_META, CHUNK), lambda: prompt_path(None, None))

    tm = MOE_TILE
    cap = _slot_cap(h_all.shape[0])
    n_tiles = (n_tok * TOP_K) // tm + n_exp
    counts = counts[0].astype(jnp.int32)
    tiles_e = (counts + tm - 1) // tm
    tile_end = jnp.cumsum(tiles_e)
    n_used = tile_end[-1]
    tile_ids = jnp.arange(n_tiles, dtype=jnp.int32)

    def expert_of(tile):
        return jnp.minimum(jnp.sum((tile_end[None, :] <= tile[:, None]).astype(jnp.int32), axis=1), n_exp - 1)

    tile_expert = expert_of(jnp.minimum(tile_ids, n_used - 1))
    mine = tile_expert[:, None] == jnp.arange(n_exp, dtype=jnp.int32)[None, :]
    after = jnp.sum(jnp.where(mine, tile_end[None, :], 0), axis=1)
    first = jnp.sum(jnp.where(mine, (tile_end - tiles_e)[None, :], 0), axis=1)
    next_expert = jnp.where(after < n_used, expert_of(after), tile_expert)
    n_rows = h_all.shape[0]
    n_spare = (n_rows - n_tok) * TOP_K
    spare_blocks = -(-n_spare // tm)
    dummy_block = n_exp * (cap // tm) + spare_blocks
    n_slots = (dummy_block + 1) * tm
    tile_block = jnp.where(tile_ids < n_used, tile_expert * (cap // tm) + tile_ids - first, dummy_block)
    rows_left = jnp.sum(jnp.where(mine, counts[None, :], 0), axis=1) - (tile_ids - first) * tm
    part = tm // MOE_PARTS
    tile_full = jnp.clip((rows_left + part - 1) // part, 1, MOE_PARTS).astype(jnp.int32)

    n_workers = SC_CORES * SC_SUBCORES
    assert n_rows % (n_workers * DISPATCH_CHUNK) == 0
    pos_rows = rnk[:, :TOP_K].astype(jnp.int32)

    def to_workers(p, ch):
        assert p.shape[0] % (n_workers * ch) == 0
        return p.reshape(n_workers, -1, ch, TOP_K).transpose(0, 1, 3, 2)

    xs = _dispatch(xn_all, to_workers(pos_rows, DISPATCH_CHUNK), n_slots)

    w1 = expert_w1[layer]
    w2 = expert_w2[layer]
    b1 = expert_b1[layer]
    b1g = b1[:, 0::2][:, None, :]
    b1l = b1[:, 1::2][:, None, :]
    b2 = expert_b2[layer][:, None, :]
    ys = pl.pallas_call(
        functools.partial(_moe_kernel, tm),
        grid_spec=pltpu.PrefetchScalarGridSpec(
            num_scalar_prefetch=5,
            grid=(n_tiles,),
            in_specs=[
                pl.BlockSpec((tm, d // 2), lambda i, te, nu, tn, tb, tf: (tb[i], 0)),
                pl.BlockSpec(memory_space=pl.ANY),
                _const_spec(b1g.shape), _const_spec(b1l.shape),
                pl.BlockSpec(memory_space=pl.ANY),
                _const_spec(b2.shape),
            ],
            out_specs=pl.BlockSpec((tm, d // 2), lambda i, te, nu, tn, tb, tf: (tb[i], 0)),
            scratch_shapes=[pltpu.VMEM((d, 2 * d_ff), F32), pltpu.VMEM((d_ff, d), F32),
                            pltpu.SemaphoreType.DMA((2,)),
                            pltpu.VMEM((d, 2 * d_ff), BF16), pltpu.VMEM((d_ff, d), BF16),
                            pltpu.VMEM((tm, d_ff), BF16)],
        ),
        out_shape=jax.ShapeDtypeStruct((n_slots, d // 2), jnp.uint32),
        compiler_params=pltpu.CompilerParams(dimension_semantics=("arbitrary",),
                                             vmem_limit_bytes=VMEM_LIMIT),
        name="moe",
    )(tile_expert, n_used[None].astype(jnp.int32), next_expert, tile_block, tile_full,
      xs, w1, b1g, b1l, w2, b2)

    z = _collect(ys, to_workers(pos_rows, DISPATCH_CHUNK))
    y_p = _final_call(h_all, route, norm_final[None], z, 0, n_prompt, FINAL_TILE)
    y_s = _final_call(h_all, route, norm_final[None], z, n_prompt, ns, TOKEN_TILE)

    return (y_p.reshape(bp, seq, d), y_s.reshape(ns, 1, d),
            new_conv_p[None], new_rec_p[None],
            new_conv_s.reshape(1, ns, 2, d_conv), new_rec_s[None])
```
